```python
import jax, jax.numpy as jnp
from jax import lax
import numpy as np

D_MODEL = 1024
BATCH = 16
SEQ = 256
DEPTH = 2
DEC_BATCH = 2
DEC_SEQ = 1024
PAST_LEN = 512

GRID_W = 64
N_MIXERS = 2
N_REC = (DEPTH + 1) // 2
N_ATT = DEPTH // 2
HEAD_DIM = 128
N_HEADS = D_MODEL // HEAD_DIM
N_KV_HEADS = 2
GROUP = N_HEADS // N_KV_HEADS
QKV_DIM = (N_HEADS + 2 * N_KV_HEADS) * HEAD_DIM
ROPE_THETA = 10000.0
Q_BLOCK = 128
D_RNN = D_MODEL
LRU_BLOCKS = 16
LRU_BLOCK_W = D_RNN // LRU_BLOCKS
CONV_W = 4
LRU_C = 8.0
D_FF = 2816
N_SUB = 3
EPS = 1e-6

kernel_name = "hybrid_rglru_gqa_prefix_diffusion_step"

F32 = jnp.float32


def rmsnorm(x, g):
    xf = x.astype(F32)
    y = xf * lax.rsqrt(jnp.mean(xf * xf, axis=-1, keepdims=True) + EPS)
    return (y * g.astype(F32)).astype(x.dtype)


def modulation(cvec, w, b):
    m = jax.nn.silu(cvec) @ w + b
    return m.reshape(cvec.shape[0], N_SUB, 3, D_MODEL)


def sub_in(x, g, m, sidx):
    shift, scale = m[:, sidx, 0], m[:, sidx, 1]
    return rmsnorm(x, g[sidx]) * (1 + scale[:, None, :]) + shift[:, None, :]


def sub_gate(m, sidx):
    return m[:, sidx, 2][:, None, :]


def swiglu(h, w_gu, w_down):
    gt, up = jnp.split(h @ w_gu, 2, axis=-1)
    return (jax.nn.silu(gt) * up) @ w_down


def centred_dwconv(x, w, b):
    T = x.shape[1]
    left = (CONV_W - 1) // 2
    right = CONV_W - 1 - left
    xp = jnp.pad(x, ((0, 0), (left, right), (0, 0)))
    out = b
    for k in range(CONV_W):
        out = out + xp[:, k:k + T] * w[k]
    return out


def linear_scan(a, u, h0, reverse):
    def step(h, au):
        a_t, u_t = au
        h = a_t * h + u_t
        return h, h
    h_last, hs = lax.scan(step, h0, (a.swapaxes(0, 1), u.swapaxes(0, 1)), reverse=reverse)
    return hs.swapaxes(0, 1), h_last


def rglru_block(xn, w_in, conv_w, conv_b, gate_w, gate_b, lam, w_out, h0):
    B, T, _ = xn.shape
    xb, yb = jnp.split(xn @ w_in, 2, axis=-1)
    xc = centred_dwconv(xb, conv_w, conv_b)
    xblk = xc.reshape(B, T, LRU_BLOCKS, LRU_BLOCK_W)
    gl = jnp.einsum('btnk,dgnkj->dgbtnj', xblk, gate_w).reshape(2, 2, B, T, D_RNN)
    gates = jax.nn.sigmoid(gl.astype(F32) + gate_b.astype(F32)[:, :, None, None, :])
    r, i = gates[:, 0], gates[:, 1]
    log_a = LRU_C * r * jax.nn.log_sigmoid(lam.astype(F32))[:, None, None, :]
    a = jnp.exp(log_a)
    u = jnp.sqrt(-jnp.expm1(2.0 * log_a)) * (i * xc.astype(F32)[None])
    h0f = h0.astype(F32)
    hf, hf_last = linear_scan(a[0], u[0], h0f[:, 0], False)
    hb, hb_last = linear_scan(a[1], u[1], h0f[:, 1], True)
    y = (hf + hb).astype(xn.dtype) * jax.nn.gelu(yb)
    return y @ w_out, jnp.stack([hf_last, hb_last], axis=1).astype(xn.dtype)


def axial_rope_angles(n_tok):
    rows = n_tok // GRID_W
    r_idx = jnp.broadcast_to(jnp.arange(rows)[:, None], (rows, GRID_W)).reshape(n_tok).astype(F32)
    c_idx = jnp.broadcast_to(jnp.arange(GRID_W)[None, :], (rows, GRID_W)).reshape(n_tok).astype(F32)
    n_freq = HEAD_DIM // 4
    inv = ROPE_THETA ** (-jnp.arange(n_freq, dtype=F32) / n_freq)
    ang = jnp.stack([r_idx[:, None] * inv, c_idx[:, None] * inv], axis=1)
    return jnp.cos(ang), jnp.sin(ang)


def apply_axial_rope(x, cos, sin):
    B, T, H, _ = x.shape
    xr = x.astype(F32).reshape(B, T, H, 2, 2, HEAD_DIM // 4)
    x0, x1 = xr[..., 0, :], xr[..., 1, :]
    c = cos[None, :, None]
    s = sin[None, :, None]
    out = jnp.stack([x0 * c - x1 * s, x0 * s + x1 * c], axis=-2)
    return out.reshape(B, T, H, HEAD_DIM).astype(x.dtype)


def qkv_heads(xn, w_qkv, q_g, k_g):
    B, T, _ = xn.shape
    q, k, v = jnp.split(xn @ w_qkv, [N_HEADS * HEAD_DIM, (N_HEADS + N_KV_HEADS) * HEAD_DIM], axis=-1)
    q = rmsnorm(q.reshape(B, T, N_HEADS, HEAD_DIM), q_g)
    k = rmsnorm(k.reshape(B, T, N_KV_HEADS, HEAD_DIM), k_g)
    v = v.reshape(B, T, N_KV_HEADS, HEAD_DIM)
    return q, k, v


def block_attention(q, k, v):
    B, S = q.shape[0], q.shape[1]
    nb = S // Q_BLOCK
    qb = q.reshape(B, nb, Q_BLOCK, N_KV_HEADS, GROUP, HEAD_DIM).transpose(1, 0, 2, 3, 4, 5)
    scale = HEAD_DIM ** -0.5

    def one_block(qblk):
        s = jnp.einsum('bqkgd,btkd->bkgqt', qblk, k).astype(F32) * scale
        p = jax.nn.softmax(s, axis=-1).astype(v.dtype)
        return jnp.einsum('bkgqt,btkd->bqkgd', p, v)

    o = lax.map(one_block, qb)
    return o.transpose(1, 0, 2, 3, 4, 5).reshape(B, S, N_HEADS * HEAD_DIM)


def setup_inputs(seed: int = 0) -> dict:
    key = jax.random.key(seed)
    ks = jax.random.split(key, 26)

    def nrm(k, shape, s):
        return jax.random.normal(k, shape, F32) * s

    u = jax.random.uniform(ks[15], (N_REC, 2, D_RNN), F32, minval=0.9, maxval=0.999)
    a0 = u ** (1.0 / LRU_C)
    lam = jnp.log(a0) - jnp.log1p(-a0)
    return {
        "x_prompt": nrm(ks[0], (BATCH, SEQ, D_MODEL), 1.0),
        "x_sample": nrm(ks[1], (DEC_BATCH, DEC_SEQ, D_MODEL), 1.0),
        "c": nrm(ks[2], (DEC_BATCH, D_MODEL), 1.0),
        "state_lru": nrm(ks[3], (DEC_BATCH, N_REC, 2, D_RNN), 0.5),
        "cache_k": nrm(ks[4], (DEC_BATCH, N_ATT, PAST_LEN, N_KV_HEADS, HEAD_DIM), 1.0),
        "cache_v": nrm(ks[5], (DEC_BATCH, N_ATT, PAST_LEN, N_KV_HEADS, HEAD_DIM), 1.0),
        "c_ctx": nrm(ks[6], (D_MODEL,), 1.0),
        "mod_w": nrm(ks[7], (DEPTH, D_MODEL, N_SUB * 3 * D_MODEL), 0.5 * D_MODEL ** -0.5),
        "mod_b": nrm(ks[8], (DEPTH, N_SUB * 3 * D_MODEL), 0.02),
        "norm_g": 1.0 + nrm(ks[9], (DEPTH, N_SUB, D_MODEL), 0.02),
        "ffn_w_gu": nrm(ks[10], (DEPTH, 2, D_MODEL, 2 * D_FF), D_MODEL ** -0.5),
        "ffn_w_down": nrm(ks[11], (DEPTH, 2, D_FF, D_MODEL), D_FF ** -0.5),
        "lru_w_in": nrm(ks[12], (N_REC, D_MODEL, 2 * D_RNN), D_MODEL ** -0.5),
        "lru_conv_w": nrm(ks[13], (N_REC, CONV_W, D_RNN), CONV_W ** -0.5),
        "lru_conv_b": nrm(ks[14], (N_REC, D_RNN), 0.01),
        "lru_gate_w": nrm(ks[16], (N_REC, 2, 2, LRU_BLOCKS, LRU_BLOCK_W, LRU_BLOCK_W), LRU_BLOCK_W ** -0.5),
        "lru_gate_b": nrm(ks[17], (N_REC, 2, 2, D_RNN), 0.01),
        "lru_lambda": lam,
        "lru_w_out": nrm(ks[18], (N_REC, D_RNN, D_MODEL), D_RNN ** -0.5),
        "att_w_qkv": nrm(ks[19], (N_ATT, D_MODEL, QKV_DIM), D_MODEL ** -0.5),
        "att_q_g": 1.0 + nrm(ks[20], (N_ATT, HEAD_DIM), 0.02),
        "att_k_g": 1.0 + nrm(ks[21], (N_ATT, HEAD_DIM), 0.02),
        "att_w_o": nrm(ks[22], (N_ATT, N_HEADS * HEAD_DIM, D_MODEL), (N_HEADS * HEAD_DIM) ** -0.5),
        "final_g": 1.0 + nrm(ks[23], (D_MODEL,), 0.02),
    }


def reference(x_prompt, x_sample, c, state_lru, cache_k, cache_v, c_ctx, mod_w, mod_b, norm_g,
              ffn_w_gu, ffn_w_down, lru_w_in, lru_conv_w, lru_conv_b, lru_gate_w, lru_gate_b,
              lru_lambda, lru_w_out, att_w_qkv, att_q_g, att_k_g, att_w_o, final_g):
    xp, xs = x_prompt, x_sample
    cos, sin = axial_rope_angles(xs.shape[1])
    new_states, new_k, new_v = [], [], []
    for layer in range(DEPTH):
        j = layer // N_MIXERS
        g = norm_g[layer]
        mp = modulation(c_ctx[None], mod_w[layer], mod_b[layer])
        ms = modulation(c, mod_w[layer], mod_b[layer])

        xp = xp + 0.5 * sub_gate(mp, 0) * swiglu(sub_in(xp, g, mp, 0), ffn_w_gu[layer, 0], ffn_w_down[layer, 0])
        xs = xs + 0.5 * sub_gate(ms, 0) * swiglu(sub_in(xs, g, ms, 0), ffn_w_gu[layer, 0], ffn_w_down[layer, 0])

        hp = sub_in(xp, g, mp, 1)
        hs = sub_in(xs, g, ms, 1)
        if layer % N_MIXERS == 0:
            lru_p = (lru_w_in[j], lru_conv_w[j], lru_conv_b[j], lru_gate_w[j], lru_gate_b[j],
                     lru_lambda[j], lru_w_out[j])
            h0 = jnp.zeros((xp.shape[0], 2, D_RNN), xp.dtype)
            op, st = rglru_block(hp, *lru_p, h0)
            new_states.append(st)
            os_, _ = rglru_block(hs, *lru_p, state_lru[:, j])
        else:
            qp, kp, vp = qkv_heads(hp, att_w_qkv[j], att_q_g[j], att_k_g[j])
            new_k.append(kp)
            new_v.append(vp)
            op = block_attention(qp, kp, vp) @ att_w_o[j]
            qs, ks_, vs = qkv_heads(hs, att_w_qkv[j], att_q_g[j], att_k_g[j])
            qs = apply_axial_rope(qs, cos, sin)
            ks_ = apply_axial_rope(ks_, cos, sin)
            k_all = jnp.concatenate([cache_k[:, j].astype(ks_.dtype), ks_], axis=1)
            v_all = jnp.concatenate([cache_v[:, j].astype(vs.dtype), vs], axis=1)
            os_ = block_attention(qs, k_all, v_all) @ att_w_o[j]
        xp = xp + sub_gate(mp, 1) * op
        xs = xs + sub_gate(ms, 1) * os_

        xp = xp + 0.5 * sub_gate(mp, 2) * swiglu(sub_in(xp, g, mp, 2), ffn_w_gu[layer, 1], ffn_w_down[layer, 1])
        xs = xs + 0.5 * sub_gate(ms, 2) * swiglu(sub_in(xs, g, ms, 2), ffn_w_gu[layer, 1], ffn_w_down[layer, 1])

    y_prompt = rmsnorm(xp, final_g)
    y_sample = rmsnorm(xs, final_g)
    new_state_lru = jnp.stack(new_states, axis=1)
    new_cache_k = jnp.stack(new_k, axis=1)
    new_cache_v = jnp.stack(new_v, axis=1)
    return (y_prompt, y_sample, new_state_lru, new_cache_k, new_cache_v)
```

```python
import functools

import jax
import jax.numpy as jnp
import numpy as np
from jax import lax
from jax.experimental import pallas as pl
from jax.experimental.pallas import tpu as pltpu

F32 = jnp.float32
BF16 = jnp.bfloat16

EPS = 1e-6
LRU_C = 8.0
GRID_W = 64
ROPE_THETA = 10000.0
N_SUB = 3
HEAD_DIM = 128
N_KV_HEADS = 2
LRU_BLOCKS = 16

V7X_VMEM_LIMIT_BYTES = 56 * 1024 * 1024
SUBLANES = 8
TOKEN_TILE = 1024
FF_CHUNK = 256
LRU_CHUNK = 256
MOD_CHUNK = 1536


def _params(*semantics):
    return pltpu.CompilerParams(dimension_semantics=semantics,
                                vmem_limit_bytes=V7X_VMEM_LIMIT_BYTES)


def _mm(a_bf16, w_f32):
    return jnp.dot(a_bf16, w_f32.astype(BF16), preferred_element_type=F32)


def _rms(x):
    return x * lax.rsqrt(jnp.mean(x * x, axis=-1, keepdims=True) + EPS)


def _sub_in(x, m_ref, g_ref, sidx):
    shift = m_ref[3 * sidx:3 * sidx + 1, :]
    scale = m_ref[3 * sidx + 1:3 * sidx + 2, :]
    return (_rms(x) * g_ref[sidx:sidx + 1, :]) * (1.0 + scale) + shift


def _gate(m_ref, sidx):
    return m_ref[3 * sidx + 2:3 * sidx + 3, :]


def _mod_kernel(cv_ref, w_ref, b_ref, o_ref):
    cv = cv_ref[...]
    act = (cv * jax.nn.sigmoid(cv)).astype(BF16)
    o_ref[...] = _mm(act, w_ref[...]) + b_ref[...]


def _modulation(cvecs, mod_w, mod_b):
    depth, d, n = mod_w.shape
    rows = cvecs.shape[0]
    tn = MOD_CHUNK
    assert n % tn == 0
    return pl.pallas_call(
        _mod_kernel,
        grid=(depth, n // tn),
        in_specs=[
            pl.BlockSpec((rows, d), lambda l, j: (0, 0)),
            pl.BlockSpec((None, d, tn), lambda l, j: (l, 0, j)),
            pl.BlockSpec((None, 1, tn), lambda l, j: (l, 0, j)),
        ],
        out_specs=pl.BlockSpec((None, rows, tn), lambda l, j: (l, 0, j)),
        out_shape=jax.ShapeDtypeStruct((depth, rows, n), F32),
        compiler_params=_params("arbitrary", "arbitrary"),
        name="modulation",
    )(cvecs, mod_w, mod_b.reshape(depth, 1, n))


def _ffn_kernel(*refs, sidx, final):
    if final:
        x_ref, m_ref, g_ref, wg_ref, wu_ref, wd_ref, fg_ref, o_ref, h_ref, acc_ref = refs
    else:
        x_ref, m_ref, g_ref, wg_ref, wu_ref, wd_ref, o_ref, h_ref, acc_ref = refs
    k = pl.program_id(1)

    @pl.when(k == 0)
    def _():
        h_ref[...] = _sub_in(x_ref[...], m_ref, g_ref, sidx).astype(BF16)
        acc_ref[...] = jnp.zeros_like(acc_ref)

    h = h_ref[...]
    gt = _mm(h, wg_ref[...])
    up = _mm(h, wu_ref[...])
    act = ((gt * jax.nn.sigmoid(gt)) * up).astype(BF16)
    acc_ref[...] += _mm(act, wd_ref[...])

    @pl.when(k == pl.num_programs(1) - 1)
    def _():
        y = x_ref[...] + (0.5 * _gate(m_ref, sidx)) * acc_ref[...]
        if final:
            y = _rms(y) * fg_ref[...]
        o_ref[...] = y


def _ffn(x, m, row_of_tile, norm_g, w_gu, w_down, layer, s, sidx, final_g=None):
    t, d = x.shape
    f = w_down.shape[2]
    tm, tf = TOKEN_TILE, FF_CHUNK
    assert t % tm == 0 and f % tf == 0
    nk = f // tf
    final = final_g is not None
    in_specs = [
        pl.BlockSpec((tm, d), lambda i, k: (i, 0)),
        pl.BlockSpec((None, None, 3 * N_SUB, d), lambda i, k: (layer, row_of_tile(i), 0, 0)),
        pl.BlockSpec((None, N_SUB, d), lambda i, k: (layer, 0, 0)),
        pl.BlockSpec((None, None, d, tf), lambda i, k: (layer, s, 0, k)),
        pl.BlockSpec((None, None, d, tf), lambda i, k: (layer, s, 0, nk + k)),
        pl.BlockSpec((None, None, tf, d), lambda i, k: (layer, s, k, 0)),
    ]
    args = [x, m, norm_g, w_gu, w_gu, w_down]
    if final:
        in_specs.append(pl.BlockSpec((1, d), lambda i, k: (0, 0)))
        args.append(final_g.reshape(1, d))
    return pl.pallas_call(
        functools.partial(_ffn_kernel, sidx=sidx, final=final),
        grid=(t // tm, nk),
        in_specs=in_specs,
        out_specs=pl.BlockSpec((tm, d), lambda i, k: (i, 0)),
        out_shape=jax.ShapeDtypeStruct((t, d), F32),
        scratch_shapes=[pltpu.VMEM((tm, d), BF16), pltpu.VMEM((tm, d), F32)],
        compiler_params=_params("arbitrary", "arbitrary"),
        name="ffn",
    )(*args)


def _gelu_tanh(x):
    c = np.float32(np.sqrt(2.0 / np.pi))
    return x * (0.5 * (1.0 + jnp.tanh(c * (x + 0.044715 * (x * x * x)))))


def _group_scan(a, u, row8, reverse):
    rows = a.shape[0]
    for s in (1, 2, 4):
        if reverse:
            shift, valid = rows - s, row8 < SUBLANES - s
        else:
            shift, valid = s, row8 >= s
        u = jnp.where(valid, u + a * pltpu.roll(u, shift, 0), u)
        a = jnp.where(valid, a * pltpu.roll(a, shift, 0), a)
    return a, u


def _lru_kernel(*refs, seq_len, has_h0, emit_state, conv_left):
    refs = list(refs)
    (x_ref, m_ref, g_ref, wx_ref, wy_ref, cw_ref, cb_ref, gw_ref, gb_ref, lam_ref,
     wo_ref) = refs[:11]
    pos = 11
    h0_ref = None
    if has_h0:
        h0_ref = refs[pos]
        pos += 1
    o_ref = refs[pos]
    pos += 1
    st_ref = None
    if emit_state:
        st_ref = refs[pos]
        pos += 1
    h_ref, acc_ref, af_ref, uf_ref, ab_ref, ub_ref = refs[pos:]

    c = pl.program_id(1)
    tm, cw_cols = af_ref.shape
    nseq = tm // seq_len
    ngroups = seq_len // SUBLANES

    @pl.when(c == 0)
    def _():
        h_ref[...] = _sub_in(x_ref[...], m_ref, g_ref, 1).astype(BF16)
        acc_ref[...] = jnp.zeros_like(acc_ref)

    h = h_ref[...]
    xb = _mm(h, wx_ref[...])
    yb = _mm(h, wy_ref[...])

    row = lax.broadcasted_iota(jnp.int32, (tm, cw_cols), 0)
    t = row & (seq_len - 1)
    cw = cw_ref[...]
    xc = cb_ref[...]
    for k in range(cw.shape[0]):
        off = k - conv_left
        if off == 0:
            tap = xb
        else:
            valid = (t + off >= 0) & (t + off <= seq_len - 1)
            tap = jnp.where(valid, pltpu.roll(xb, (-off) % tm, 0), 0.0)
        xc = xc + tap * cw[k:k + 1, :]

    gl = jnp.dot(xc.astype(BF16), gw_ref[...], preferred_element_type=F32)
    gb = gb_ref[...]
    lam = lam_ref[...]
    row8 = row & (SUBLANES - 1)
    for d, (a_ref, u_ref) in enumerate(((af_ref, uf_ref), (ab_ref, ub_ref))):
        r = jax.nn.sigmoid(gl[:, (2 * d) * cw_cols:(2 * d + 1) * cw_cols] + gb[2 * d:2 * d + 1, :])
        ig = jax.nn.sigmoid(gl[:, (2 * d + 1) * cw_cols:(2 * d + 2) * cw_cols]
                            + gb[2 * d + 1:2 * d + 2, :])
        lm = lam[d:d + 1, :]
        log_sig = jnp.minimum(lm, 0.0) - jnp.log1p(jnp.exp(-jnp.abs(lm)))
        a = jnp.exp((LRU_C * r) * log_sig)
        u = jnp.sqrt(1.0 - a * a) * (ig * xc)
        a, u = _group_scan(a, u, row8, reverse=(d == 1))
        a_ref[...] = a
        u_ref[...] = u

    if has_h0:
        init = tuple(h0_ref[s, dd:dd + 1, :] for s in range(nseq) for dd in range(2))
    else:
        init = tuple(jnp.zeros((1, cw_cols), F32) for _ in range(2 * nseq))

    def body(j, carry):
        out = []
        for s in range(nseq):
            rf = pl.multiple_of(s * seq_len + j * SUBLANES, SUBLANES)
            rb = pl.multiple_of(s * seq_len + (ngroups - 1 - j) * SUBLANES, SUBLANES)
            hf = uf_ref[pl.ds(rf, SUBLANES), :] + af_ref[pl.ds(rf, SUBLANES), :] * carry[2 * s]
            uf_ref[pl.ds(rf, SUBLANES), :] = hf
            hb = ub_ref[pl.ds(rb, SUBLANES), :] + ab_ref[pl.ds(rb, SUBLANES), :] * carry[2 * s + 1]
            ub_ref[pl.ds(rb, SUBLANES), :] = hb
            out += [hf[SUBLANES - 1:SUBLANES, :], hb[0:1, :]]
        return tuple(out)

    last = lax.fori_loop(0, ngroups, body, init)
    if emit_state:
        for s in range(nseq):
            st_ref[s, 0:1, :] = last[2 * s]
            st_ref[s, 1:2, :] = last[2 * s + 1]

    y = (uf_ref[...] + ub_ref[...]) * _gelu_tanh(yb)
    acc_ref[...] += _mm(y.astype(BF16), wo_ref[...])

    @pl.when(c == pl.num_programs(1) - 1)
    def _():
        o_ref[...] = x_ref[...] + _gate(m_ref, 1) * acc_ref[...]


def _lru_gate_weights(gate_w):
    nd, ng, nb, bw, _ = gate_w.shape
    per = LRU_CHUNK // bw
    nc = nb // per
    w = gate_w.reshape(nd * ng, nc, per, bw, bw)
    eye = jnp.eye(per, dtype=gate_w.dtype)
    w = w[:, :, :, :, None, :] * eye[None, None, :, None, :, None]
    w = jnp.transpose(w, (1, 2, 3, 0, 4, 5))
    return w.reshape(nc, per * bw, nd * ng * per * bw).astype(BF16)


def _lru(x, m, row_of_tile, norm_g, layer, w_in, conv_w, conv_b, gate_w, gate_b, lam, w_out,
         seq_len, h0, emit_state):
    t, d = x.shape
    r = w_out.shape[0]
    tm, cb = TOKEN_TILE, LRU_CHUNK
    assert t % tm == 0 and tm % seq_len == 0 and r % cb == 0 and seq_len % SUBLANES == 0
    assert seq_len & (seq_len - 1) == 0, "in-sequence position is taken with a bit mask"
    nc = r // cb
    nseq = tm // seq_len
    has_h0 = h0 is not None
    in_specs = [
        pl.BlockSpec((tm, d), lambda i, c: (i, 0)),
        pl.BlockSpec((None, None, 3 * N_SUB, d), lambda i, c: (layer, row_of_tile(i), 0, 0)),
        pl.BlockSpec((None, N_SUB, d), lambda i, c: (layer, 0, 0)),
        pl.BlockSpec((d, cb), lambda i, c: (0, c)),
        pl.BlockSpec((d, cb), lambda i, c: (0, nc + c)),
        pl.BlockSpec((conv_w.shape[0], cb), lambda i, c: (0, c)),
        pl.BlockSpec((1, cb), lambda i, c: (0, c)),
        pl.BlockSpec((None, cb, 4 * cb), lambda i, c: (c, 0, 0)),
        pl.BlockSpec((4, cb), lambda i, c: (0, c)),
        pl.BlockSpec((2, cb), lambda i, c: (0, c)),
        pl.BlockSpec((cb, d), lambda i, c: (c, 0)),
    ]
    args = [x, m, norm_g, w_in, w_in, conv_w, conv_b.reshape(1, r), _lru_gate_weights(gate_w),
            gate_b.reshape(4, r), lam, w_out]
    if has_h0:
        in_specs.append(pl.BlockSpec((nseq, 2, cb), lambda i, c: (i, 0, c)))
        args.append(h0)
    out_specs = [pl.BlockSpec((tm, d), lambda i, c: (i, 0))]
    out_shape = [jax.ShapeDtypeStruct((t, d), F32)]
    if emit_state:
        out_specs.append(pl.BlockSpec((nseq, 2, cb), lambda i, c: (i, 0, c)))
        out_shape.append(jax.ShapeDtypeStruct((t // seq_len, 2, r), F32))
    outs = pl.pallas_call(
        functools.partial(_lru_kernel, seq_len=seq_len, has_h0=has_h0, emit_state=emit_state,
                          conv_left=(conv_w.shape[0] - 1) // 2),
        grid=(t // tm, nc),
        in_specs=in_specs,
        out_specs=out_specs,
        out_shape=out_shape,
        scratch_shapes=[pltpu.VMEM((tm, d), BF16), pltpu.VMEM((tm, d), F32)]
        + [pltpu.VMEM((tm, cb), F32)] * 4,
        compiler_params=_params("arbitrary", "arbitrary"),
        name="rglru",
    )(*args)
    return outs if emit_state else (outs[0], None)


def _rope(x, cos, sin_signed, lane):
    hd = x.shape[1]
    partner = jnp.where((lane & 32) == 0, pltpu.roll(x, hd - 32, 1), pltpu.roll(x, 32, 1))
    return x * cos + partner * sin_signed


def _attn_kernel(*refs, seq_len, q_block, past_len, rope, emit_kv, group):
    refs = list(refs)
    x_ref, m_ref, g_ref, wq_ref, wk_ref, wv_ref, qg_ref, kg_ref, wo_ref = refs[:9]
    pos = 9
    if rope:
        cos_ref, sin_ref = refs[pos:pos + 2]
        pos += 2
    if past_len:
        ck_ref, cv_ref = refs[pos:pos + 2]
        pos += 2
    o_ref = refs[pos]
    pos += 1
    if emit_kv:
        kn_ref, vn_ref = refs[pos:pos + 2]
        pos += 2
    h_ref, acc_ref, q_s, k_s, v_s, o_s = refs[pos:]

    gi = pl.program_id(1)
    tm = x_ref.shape[0]
    hd = k_s.shape[1]
    nqb = seq_len // q_block
    nchunks = (tm // seq_len) * nqb
    nk = past_len + seq_len
    scale = hd ** -0.5

    @pl.when(gi == 0)
    def _():
        h_ref[...] = _sub_in(x_ref[...], m_ref, g_ref, 1).astype(BF16)
        acc_ref[...] = jnp.zeros_like(acc_ref)

    h = h_ref[...]
    q = _mm(h, wq_ref[...])
    k = _rms(_mm(h, wk_ref[...])) * kg_ref[...]
    v = _mm(h, wv_ref[...])
    if emit_kv:
        kn_ref[...] = k
        vn_ref[...] = v
    if rope:
        lane = lax.broadcasted_iota(jnp.int32, (tm, hd), 1)
        cos, sin = cos_ref[...], sin_ref[...]
        k = _rope(k, cos, sin, lane)
    k_s[past_len:past_len + tm, :] = k.astype(BF16)
    v_s[past_len:past_len + tm, :] = v.astype(BF16)
    if past_len:
        k_s[0:past_len, :] = ck_ref[...].astype(BF16)
        v_s[0:past_len, :] = cv_ref[...].astype(BF16)
    for j in range(group):
        qh = _rms(q[:, j * hd:(j + 1) * hd]) * qg_ref[...]
        if rope:
            qh = _rope(qh, cos, sin, lane)
        q_s[:, j * hd:(j + 1) * hd] = qh.astype(BF16)

    def chunk(ci, carry):
        r0 = pl.multiple_of(ci * q_block, q_block)
        if past_len:
            keys, vals = k_s[...], v_s[...]
        else:
            koff = pl.multiple_of((ci // nqb) * seq_len, seq_len)
            keys, vals = k_s[pl.ds(koff, nk), :], v_s[pl.ds(koff, nk), :]
        qc = jnp.concatenate([q_s[pl.ds(r0, q_block), j * hd:(j + 1) * hd] for j in range(group)],
                             axis=0)
        s = lax.dot_general(qc, keys, (((1,), (1,)), ((), ())), preferred_element_type=F32) * scale
        e = jnp.exp(s - jnp.max(s, axis=-1, keepdims=True))
        p = e * (1.0 / jnp.sum(e, axis=-1, keepdims=True))
        oc = jnp.dot(p.astype(BF16), vals, preferred_element_type=F32)
        for j in range(group):
            o_s[pl.ds(r0, q_block), j * hd:(j + 1) * hd] = oc[j * q_block:(j + 1) * q_block, :].astype(BF16)
        return carry

    lax.fori_loop(0, nchunks, chunk, 0)
    acc_ref[...] += _mm(o_s[...], wo_ref[...])

    @pl.when(gi == pl.num_programs(1) - 1)
    def _():
        o_ref[...] = x_ref[...] + _gate(m_ref, 1) * acc_ref[...]


def _rope_tables(n_tok, hd):
    rows = n_tok // GRID_W
    r_idx = jnp.broadcast_to(jnp.arange(rows)[:, None], (rows, GRID_W)).reshape(n_tok).astype(F32)
    c_idx = jnp.broadcast_to(jnp.arange(GRID_W)[None, :], (rows, GRID_W)).reshape(n_tok).astype(F32)
    n_freq = hd // 4
    inv = ROPE_THETA ** (-jnp.arange(n_freq, dtype=F32) / n_freq)
    ang = jnp.stack([r_idx[:, None] * inv, c_idx[:, None] * inv], axis=1)
    cos, sin = jnp.cos(ang), jnp.sin(ang)
    cos_full = jnp.concatenate([cos, cos], axis=-1).reshape(n_tok, hd)
    sin_signed = jnp.concatenate([-sin, sin], axis=-1).reshape(n_tok, hd)
    return cos_full, sin_signed


def _attn(x, m, row_of_tile, norm_g, layer, w_qkv, q_g, k_g, w_o, seq_len, q_block,
          cache_k=None, cache_v=None, rope=False, emit_kv=False):
    t, d = x.shape
    hd, kvh = HEAD_DIM, N_KV_HEADS
    n_heads = w_o.shape[0] // hd
    group = n_heads // kvh
    gw = group * hd
    tm = TOKEN_TILE
    assert t % tm == 0 and tm % seq_len == 0 and seq_len % q_block == 0
    past_len = 0 if cache_k is None else cache_k.shape[1]
    assert past_len == 0 or tm == seq_len
    in_specs = [
        pl.BlockSpec((tm, d), lambda i, g: (i, 0)),
        pl.BlockSpec((None, None, 3 * N_SUB, d), lambda i, g: (layer, row_of_tile(i), 0, 0)),
        pl.BlockSpec((None, N_SUB, d), lambda i, g: (layer, 0, 0)),
        pl.BlockSpec((d, gw), lambda i, g: (0, g)),
        pl.BlockSpec((d, hd), lambda i, g: (0, n_heads + g)),
        pl.BlockSpec((d, hd), lambda i, g: (0, n_heads + kvh + g)),
        pl.BlockSpec((1, hd), lambda i, g: (0, 0)),
        pl.BlockSpec((1, hd), lambda i, g: (0, 0)),
        pl.BlockSpec((gw, d), lambda i, g: (g, 0)),
    ]
    args = [x, m, norm_g, w_qkv, w_qkv, w_qkv, q_g.reshape(1, hd), k_g.reshape(1, hd), w_o]
    if rope:
        assert tm == seq_len
        cos, sin = _rope_tables(seq_len, hd)
        in_specs += [pl.BlockSpec((tm, hd), lambda i, g: (0, 0))] * 2
        args += [cos, sin]
    if past_len:
        in_specs += [pl.BlockSpec((None, past_len, hd), lambda i, g: (i, 0, g))] * 2
        args += [cache_k, cache_v]
    out_specs = [pl.BlockSpec((tm, d), lambda i, g: (i, 0))]
    out_shape = [jax.ShapeDtypeStruct((t, d), F32)]
    if emit_kv:
        out_specs += [pl.BlockSpec((tm, hd), lambda i, g: (i, g))] * 2
        out_shape += [jax.ShapeDtypeStruct((t, kvh * hd), F32)] * 2
    nkeys = past_len + tm
    outs = pl.pallas_call(
        functools.partial(_attn_kernel, seq_len=seq_len, q_block=q_block, past_len=past_len,
                          rope=rope, emit_kv=emit_kv, group=group),
        grid=(t // tm, kvh),
        in_specs=in_specs,
        out_specs=out_specs,
        out_shape=out_shape,
        scratch_shapes=[pltpu.VMEM((tm, d), BF16), pltpu.VMEM((tm, d), F32),
                        pltpu.VMEM((tm, gw), BF16), pltpu.VMEM((nkeys, hd), BF16),
                        pltpu.VMEM((nkeys, hd), BF16), pltpu.VMEM((tm, gw), BF16)],
        compiler_params=_params("arbitrary", "arbitrary"),
        name="gqa",
    )(*args)
    return outs


def kernel(x_prompt, x_sample, c, state_lru, cache_k, cache_v, c_ctx, mod_w, mod_b, norm_g,
           ffn_w_gu, ffn_w_down, lru_w_in, lru_conv_w, lru_conv_b, lru_gate_w, lru_gate_b,
           lru_lambda, lru_w_out, att_w_qkv, att_q_g, att_k_g, att_w_o, final_g):
    b, s, d = x_prompt.shape
    db, ds, _ = x_sample.shape
    depth = mod_w.shape[0]
    n_mixers = 2
    assert ds % TOKEN_TILE == 0 and 1 + db <= SUBLANES

    xp = x_prompt.reshape(b * s, d)
    xs = x_sample.reshape(db * ds, d)
    cvecs = jnp.concatenate([c_ctx[None], c, jnp.zeros((SUBLANES - 1 - db, d), F32)], axis=0)
    m = _modulation(cvecs, mod_w, mod_b).reshape(depth, SUBLANES, 3 * N_SUB, d)

    tiles_per_sample = ds // TOKEN_TILE
    prompt_row = lambda i: 0
    sample_row = lambda i: 1 + i // tiles_per_sample

    new_states, new_k, new_v = [], [], []
    for layer in range(depth):
        j = layer // n_mixers
        last = layer == depth - 1
        xp = _ffn(xp, m, prompt_row, norm_g, ffn_w_gu, ffn_w_down, layer, 0, 0)
        xs = _ffn(xs, m, sample_row, norm_g, ffn_w_gu, ffn_w_down, layer, 0, 0)
        if layer % n_mixers == 0:
            lru_p = (lru_w_in[j], lru_conv_w[j], lru_conv_b[j], lru_gate_w[j], lru_gate_b[j],
                     lru_lambda[j], lru_w_out[j])
            xp, st = _lru(xp, m, prompt_row, norm_g, layer, *lru_p, seq_len=s, h0=None,
                          emit_state=True)
            new_states.append(st)
            xs, _ = _lru(xs, m, sample_row, norm_g, layer, *lru_p, seq_len=ds,
                         h0=state_lru[:, j], emit_state=False)
        else:
            att_p = (att_w_qkv[j], att_q_g[j], att_k_g[j], att_w_o[j])
            xp, kp, vp = _attn(xp, m, prompt_row, norm_g, layer, *att_p, seq_len=s, q_block=s,
                               emit_kv=True)
            new_k.append(kp.reshape(b, s, N_KV_HEADS, HEAD_DIM))
            new_v.append(vp.reshape(b, s, N_KV_HEADS, HEAD_DIM))
            past = cache_k.shape[2]
            ck = cache_k[:, j].reshape(db, past, N_KV_HEADS * HEAD_DIM)
            cv = cache_v[:, j].reshape(db, past, N_KV_HEADS * HEAD_DIM)
            (xs,) = _attn(xs, m, sample_row, norm_g, layer, *att_p, seq_len=ds, q_block=128,
                          cache_k=ck, cache_v=cv, rope=True)
        fg = final_g if last else None
        xp = _ffn(xp, m, prompt_row, norm_g, ffn_w_gu, ffn_w_down, layer, 1, 2, final_g=fg)
        xs = _ffn(xs, m, sample_row, norm_g, ffn_w_gu, ffn_w_down, layer, 1, 2, final_g=fg)

    y_prompt = xp.reshape(b, s, d)
    y_sample = xs.reshape(db, ds, d)
    return (y_prompt, y_sample, jnp.stack(new_states, axis=1), jnp.stack(new_k, axis=1),
            jnp.stack(new_v, axis=1))
```

```python
import functools

import jax
import jax.numpy as jnp
import numpy as np
from jax import lax
from jax.experimental import pallas as pl
from jax.experimental.pallas import tpu as pltpu

F32 = jnp.float32
BF16 = jnp.bfloat16

EPS = 1e-6
LRU_C = 8.0
GRID_W = 64
ROPE_THETA = 10000.0
N_SUB = 3
HEAD_DIM = 128
N_KV_HEADS = 2
LRU_BLOCKS = 16

V7X_VMEM_LIMIT_BYTES = 56 * 1024 * 1024
SUBLANES = 8
TOKEN_TILE = 1024
FF_CHUNK = 256
LRU_CHUNK = 256
MOD_CHUNK = 1536


def _params(*semantics):
    return pltpu.CompilerParams(dimension_semantics=semantics,
                                vmem_limit_bytes=V7X_VMEM_LIMIT_BYTES)


def _mm(a_bf16, w_f32):
    return jnp.dot(a_bf16, w_f32.astype(BF16), preferred_element_type=F32)


def _rms(x):
    return x * lax.rsqrt(jnp.mean(x * x, axis=-1, keepdims=True) + EPS)


def _sub_in(x, m_ref, g_ref, sidx):
    shift = m_ref[3 * sidx:3 * sidx + 1, :]
    scale = m_ref[3 * sidx + 1:3 * sidx + 2, :]
    return (_rms(x) * g_ref[sidx:sidx + 1, :]) * (1.0 + scale) + shift


def _gate(m_ref, sidx):
    return m_ref[3 * sidx + 2:3 * sidx + 3, :]


def _mod_kernel(cv_ref, w_ref, b_ref, o_ref):
    cv = cv_ref[...]
    act = (cv * jax.nn.sigmoid(cv)).astype(BF16)
    o_ref[...] = _mm(act, w_ref[...]) + b_ref[...]


def _modulation(cvecs, mod_w, mod_b):
    depth, d, n = mod_w.shape
    rows = cvecs.shape[0]
    tn = MOD_CHUNK
    assert n % tn == 0
    return pl.pallas_call(
        _mod_kernel,
        grid=(depth, n // tn),
        in_specs=[
            pl.BlockSpec((rows, d), lambda l, j: (0, 0)),
            pl.BlockSpec((None, d, tn), lambda l, j: (l, 0, j)),
            pl.BlockSpec((None, 1, tn), lambda l, j: (l, 0, j)),
        ],
        out_specs=pl.BlockSpec((None, rows, tn), lambda l, j: (l, 0, j)),
        out_shape=jax.ShapeDtypeStruct((depth, rows, n), F32),
        compiler_params=_params("arbitrary", "arbitrary"),
        name="modulation",
    )(cvecs, mod_w, mod_b.reshape(depth, 1, n))


def _ffn_kernel(*refs, layer, s, sidx, final):
    if final:
        (x_ref, m_ref, g_ref, wgu_hbm, wd_hbm, fg_ref, o_ref,
         wg_res, wu_res, wd_res, h_ref, act_ref, wg_buf, wu_buf, wd_buf, sem) = refs
    else:
        (x_ref, m_ref, g_ref, wgu_hbm, wd_hbm, o_ref,
         wg_res, wu_res, wd_res, h_ref, act_ref, wg_buf, wu_buf, wd_buf, sem) = refs
    i = pl.program_id(0)
    f = wd_res.shape[0]
    tf = wg_buf.shape[2]
    nk = f // tf

    def chunk_copies(k):
        slot = k % 2
        cols = pl.ds(k * tf, tf)
        return (
            pltpu.make_async_copy(wgu_hbm.at[layer, s, :, cols], wg_buf.at[slot], sem.at[0, slot]),
            pltpu.make_async_copy(wgu_hbm.at[layer, s, :, pl.ds(f + k * tf, tf)], wu_buf.at[slot],
                                  sem.at[1, slot]),
            pltpu.make_async_copy(wd_hbm.at[layer, s, cols, :], wd_buf.at[slot], sem.at[2, slot]),
        )

    def act_chunk(k, wg, wu):
        h = h_ref[...]
        gt = jnp.dot(h, wg, preferred_element_type=F32)
        up = jnp.dot(h, wu, preferred_element_type=F32)
        act_ref[:, k * tf:(k + 1) * tf] = ((gt * jax.nn.sigmoid(gt)) * up).astype(BF16)

    def finish():
        ff = jnp.dot(act_ref[...], wd_res[...], preferred_element_type=F32)
        y = x_ref[...] + (0.5 * _gate(m_ref, sidx)) * ff
        if final:
            y = _rms(y) * fg_ref[...]
        o_ref[...] = y

    @pl.when(i == 0)
    def _():
        for c in chunk_copies(0):
            c.start()
        h_ref[...] = _sub_in(x_ref[...], m_ref, g_ref, sidx).astype(BF16)
        for k in range(nk):
            slot = k % 2
            for c in chunk_copies(k):
                c.wait()
            if k + 1 < nk:
                for c in chunk_copies(k + 1):
                    c.start()
            wg = wg_buf[slot].astype(BF16)
            wu = wu_buf[slot].astype(BF16)
            wg_res[:, k * tf:(k + 1) * tf] = wg
            wu_res[:, k * tf:(k + 1) * tf] = wu
            wd_res[k * tf:(k + 1) * tf, :] = wd_buf[slot].astype(BF16)
            act_chunk(k, wg, wu)
        finish()

    @pl.when(i > 0)
    def _():
        h_ref[...] = _sub_in(x_ref[...], m_ref, g_ref, sidx).astype(BF16)
        for k in range(nk):
            act_chunk(k, wg_res[:, k * tf:(k + 1) * tf], wu_res[:, k * tf:(k + 1) * tf])
        finish()


def _ffn(x, m, row_of_tile, norm_g, w_gu, w_down, layer, s, sidx, final_g=None):
    t, d = x.shape
    f = w_down.shape[2]
    tm, tf = TOKEN_TILE, FF_CHUNK
    assert t % tm == 0 and f % tf == 0
    final = final_g is not None
    in_specs = [
        pl.BlockSpec((tm, d), lambda i: (i, 0)),
        pl.BlockSpec((None, None, 3 * N_SUB, d), lambda i: (layer, row_of_tile(i), 0, 0)),
        pl.BlockSpec((None, N_SUB, d), lambda i: (layer, 0, 0)),
        pl.BlockSpec(memory_space=pl.ANY),
        pl.BlockSpec(memory_space=pl.ANY),
    ]
    args = [x, m, norm_g, w_gu, w_down]
    if final:
        in_specs.append(pl.BlockSpec((1, d), lambda i: (0, 0)))
        args.append(final_g.reshape(1, d))
    return pl.pallas_call(
        functools.partial(_ffn_kernel, layer=layer, s=s, sidx=sidx, final=final),
        grid=(t // tm,),
        in_specs=in_specs,
        out_specs=pl.BlockSpec((tm, d), lambda i: (i, 0)),
        out_shape=jax.ShapeDtypeStruct((t, d), F32),
        scratch_shapes=[
            pltpu.VMEM((d, f), BF16), pltpu.VMEM((d, f), BF16), pltpu.VMEM((f, d), BF16),
            pltpu.VMEM((tm, d), BF16), pltpu.VMEM((tm, f), BF16),
            pltpu.VMEM((2, d, tf), F32), pltpu.VMEM((2, d, tf), F32), pltpu.VMEM((2, tf, d), F32),
            pltpu.SemaphoreType.DMA((3, 2)),
        ],
        compiler_params=_params("arbitrary"),
        name="ffn",
    )(*args)


def _gelu_tanh(x):
    c = np.float32(np.sqrt(2.0 / np.pi))
    return x * (0.5 * (1.0 + jnp.tanh(c * (x + 0.044715 * (x * x * x)))))


def _group_scan(a, u, row8, reverse):
    rows = a.shape[0]
    for s in (1, 2, 4):
        if reverse:
            shift, valid = rows - s, row8 < SUBLANES - s
        else:
            shift, valid = s, row8 >= s
        u = jnp.where(valid, u + a * pltpu.roll(u, shift, 0), u)
        a = jnp.where(valid, a * pltpu.roll(a, shift, 0), a)
    return a, u


def _lru_kernel(*refs, seq_len, has_h0, emit_state, conv_left):
    refs = list(refs)
    (x_ref, m_ref, g_ref, wx_ref, wy_ref, cw_ref, cb_ref, gw_ref, gb_ref, lam_ref,
     wo_ref) = refs[:11]
    pos = 11
    h0_ref = None
    if has_h0:
        h0_ref = refs[pos]
        pos += 1
    o_ref = refs[pos]
    pos += 1
    st_ref = None
    if emit_state:
        st_ref = refs[pos]
        pos += 1
    h_ref, acc_ref, af_ref, uf_ref, ab_ref, ub_ref = refs[pos:]

    c = pl.program_id(1)
    tm, cw_cols = af_ref.shape
    nseq = tm // seq_len
    ngroups = seq_len // SUBLANES

    @pl.when(c == 0)
    def _():
        h_ref[...] = _sub_in(x_ref[...], m_ref, g_ref, 1).astype(BF16)
        acc_ref[...] = jnp.zeros_like(acc_ref)

    h = h_ref[...]
    xb = _mm(h, wx_ref[...])
    yb = _mm(h, wy_ref[...])

    row = lax.broadcasted_iota(jnp.int32, (tm, cw_cols), 0)
    t = row & (seq_len - 1)
    cw = cw_ref[...]
    xc = cb_ref[...]
    for k in range(cw.shape[0]):
        off = k - conv_left
        if off == 0:
            tap = xb
        else:
            valid = (t + off >= 0) & (t + off <= seq_len - 1)
            tap = jnp.where(valid, pltpu.roll(xb, (-off) % tm, 0), 0.0)
        xc = xc + tap * cw[k:k + 1, :]

    gl = jnp.dot(xc.astype(BF16), gw_ref[...], preferred_element_type=F32)
    gb = gb_ref[...]
    lam = lam_ref[...]
    row8 = row & (SUBLANES - 1)
    for d, (a_ref, u_ref) in enumerate(((af_ref, uf_ref), (ab_ref, ub_ref))):
        r = jax.nn.sigmoid(gl[:, (2 * d) * cw_cols:(2 * d + 1) * cw_cols] + gb[2 * d:2 * d + 1, :])
        ig = jax.nn.sigmoid(gl[:, (2 * d + 1) * cw_cols:(2 * d + 2) * cw_cols]
                            + gb[2 * d + 1:2 * d + 2, :])
        lm = lam[d:d + 1, :]
        log_sig = jnp.minimum(lm, 0.0) - jnp.log1p(jnp.exp(-jnp.abs(lm)))
        a = jnp.exp((LRU_C * r) * log_sig)
        u = jnp.sqrt(1.0 - a * a) * (ig * xc)
        a, u = _group_scan(a, u, row8, reverse=(d == 1))
        a_ref[...] = a
        u_ref[...] = u

    if has_h0:
        init = tuple(h0_ref[s, dd:dd + 1, :] for s in range(nseq) for dd in range(2))
    else:
        init = tuple(jnp.zeros((1, cw_cols), F32) for _ in range(2 * nseq))

    def body(j, carry):
        out = []
        for s in range(nseq):
            rf = pl.multiple_of(s * seq_len + j * SUBLANES, SUBLANES)
            rb = pl.multiple_of(s * seq_len + (ngroups - 1 - j) * SUBLANES, SUBLANES)
            hf = uf_ref[pl.ds(rf, SUBLANES), :] + af_ref[pl.ds(rf, SUBLANES), :] * carry[2 * s]
            uf_ref[pl.ds(rf, SUBLANES), :] = hf
            hb = ub_ref[pl.ds(rb, SUBLANES), :] + ab_ref[pl.ds(rb, SUBLANES), :] * carry[2 * s + 1]
            ub_ref[pl.ds(rb, SUBLANES), :] = hb
            out += [hf[SUBLANES - 1:SUBLANES, :], hb[0:1, :]]
        return tuple(out)

    last = lax.fori_loop(0, ngroups, body, init)
    if emit_state:
        for s in range(nseq):
            st_ref[s, 0:1, :] = last[2 * s]
            st_ref[s, 1:2, :] = last[2 * s + 1]

    y = (uf_ref[...] + ub_ref[...]) * _gelu_tanh(yb)
    acc_ref[...] += _mm(y.astype(BF16), wo_ref[...])

    @pl.when(c == pl.num_programs(1) - 1)
    def _():
        o_ref[...] = x_ref[...] + _gate(m_ref, 1) * acc_ref[...]


def _lru_gate_weights(gate_w):
    nd, ng, nb, bw, _ = gate_w.shape
    per = LRU_CHUNK // bw
    nc = nb // per
    w = gate_w.reshape(nd * ng, nc, per, bw, bw)
    eye = jnp.eye(per, dtype=gate_w.dtype)
    w = w[:, :, :, :, None, :] * eye[None, None, :, None, :, None]
    w = jnp.transpose(w, (1, 2, 3, 0, 4, 5))
    return w.reshape(nc, per * bw, nd * ng * per * bw).astype(BF16)


def _lru(x, m, row_of_tile, norm_g, layer, w_in, conv_w, conv_b, gate_w, gate_b, lam, w_out,
         seq_len, h0, emit_state):
    t, d = x.shape
    r = w_out.shape[0]
    tm, cb = TOKEN_TILE, LRU_CHUNK
    assert t % tm == 0 and tm % seq_len == 0 and r % cb == 0 and seq_len % SUBLANES == 0
    assert seq_len & (seq_len - 1) == 0, "in-sequence position is taken with a bit mask"
    nc = r // cb
    nseq = tm // seq_len
    has_h0 = h0 is not None
    in_specs = [
        pl.BlockSpec((tm, d), lambda i, c: (i, 0)),
        pl.BlockSpec((None, None, 3 * N_SUB, d), lambda i, c: (layer, row_of_tile(i), 0, 0)),
        pl.BlockSpec((None, N_SUB, d), lambda i, c: (layer, 0, 0)),
        pl.BlockSpec((d, cb), lambda i, c: (0, c)),
        pl.BlockSpec((d, cb), lambda i, c: (0, nc + c)),
        pl.BlockSpec((conv_w.shape[0], cb), lambda i, c: (0, c)),
        pl.BlockSpec((1, cb), lambda i, c: (0, c)),
        pl.BlockSpec((None, cb, 4 * cb), lambda i, c: (c, 0, 0)),
        pl.BlockSpec((4, cb), lambda i, c: (0, c)),
        pl.BlockSpec((2, cb), lambda i, c: (0, c)),
        pl.BlockSpec((cb, d), lambda i, c: (c, 0)),
    ]
    args = [x, m, norm_g, w_in, w_in, conv_w, conv_b.reshape(1, r), _lru_gate_weights(gate_w),
            gate_b.reshape(4, r), lam, w_out]
    if has_h0:
        in_specs.append(pl.BlockSpec((nseq, 2, cb), lambda i, c: (i, 0, c)))
        args.append(h0)
    out_specs = [pl.BlockSpec((tm, d), lambda i, c: (i, 0))]
    out_shape = [jax.ShapeDtypeStruct((t, d), F32)]
    if emit_state:
        out_specs.append(pl.BlockSpec((nseq, 2, cb), lambda i, c: (i, 0, c)))
        out_shape.append(jax.ShapeDtypeStruct((t // seq_len, 2, r), F32))
    outs = pl.pallas_call(
        functools.partial(_lru_kernel, seq_len=seq_len, has_h0=has_h0, emit_state=emit_state,
                          conv_left=(conv_w.shape[0] - 1) // 2),
        grid=(t // tm, nc),
        in_specs=in_specs,
        out_specs=out_specs,
        out_shape=out_shape,
        scratch_shapes=[pltpu.VMEM((tm, d), BF16), pltpu.VMEM((tm, d), F32)]
        + [pltpu.VMEM((tm, cb), F32)] * 4,
        compiler_params=_params("arbitrary", "arbitrary"),
        name="rglru",
    )(*args)
    return outs if emit_state else (outs[0], None)


def _rope(x, cos, sin_signed, lane):
    hd = x.shape[1]
    partner = jnp.where((lane & 32) == 0, pltpu.roll(x, hd - 32, 1), pltpu.roll(x, 32, 1))
    return x * cos + partner * sin_signed


def _attn_kernel(*refs, seq_len, q_block, past_len, rope, emit_kv, group):
    refs = list(refs)
    x_ref, m_ref, g_ref, wq_ref, wk_ref, wv_ref, qg_ref, kg_ref, wo_ref = refs[:9]
    pos = 9
    if rope:
        cos_ref, sin_ref = refs[pos:pos + 2]
        pos += 2
    if past_len:
        ck_ref, cv_ref = refs[pos:pos + 2]
        pos += 2
    o_ref = refs[pos]
    pos += 1
    if emit_kv:
        kn_ref, vn_ref = refs[pos:pos + 2]
        pos += 2
    h_ref, acc_ref, q_s, k_s, v_s, o_s = refs[pos:]

    gi = pl.program_id(1)
    tm = x_ref.shape[0]
    hd = k_s.shape[1]
    nqb = seq_len // q_block
    nchunks = (tm // seq_len) * nqb
    nk = past_len + seq_len
    scale = hd ** -0.5

    @pl.when(gi == 0)
    def _():
        h_ref[...] = _sub_in(x_ref[...], m_ref, g_ref, 1).astype(BF16)
        acc_ref[...] = jnp.zeros_like(acc_ref)

    h = h_ref[...]
    q = _mm(h, wq_ref[...])
    k = _rms(_mm(h, wk_ref[...])) * kg_ref[...]
    v = _mm(h, wv_ref[...])
    if emit_kv:
        kn_ref[...] = k
        vn_ref[...] = v
    if rope:
        lane = lax.broadcasted_iota(jnp.int32, (tm, hd), 1)
        cos, sin = cos_ref[...], sin_ref[...]
        k = _rope(k, cos, sin, lane)
    k_s[past_len:past_len + tm, :] = k.astype(BF16)
    v_s[past_len:past_len + tm, :] = v.astype(BF16)
    if past_len:
        k_s[0:past_len, :] = ck_ref[...].astype(BF16)
        v_s[0:past_len, :] = cv_ref[...].astype(BF16)
    for j in range(group):
        qh = _rms(q[:, j * hd:(j + 1) * hd]) * qg_ref[...]
        if rope:
            qh = _rope(qh, cos, sin, lane)
        q_s[:, j * hd:(j + 1) * hd] = qh.astype(BF16)

    def chunk(ci, carry):
        r0 = pl.multiple_of(ci * q_block, q_block)
        if past_len:
            keys, vals = k_s[...], v_s[...]
        else:
            koff = pl.multiple_of((ci // nqb) * seq_len, seq_len)
            keys, vals = k_s[pl.ds(koff, nk), :], v_s[pl.ds(koff, nk), :]
        qc = jnp.concatenate([q_s[pl.ds(r0, q_block), j * hd:(j + 1) * hd] for j in range(group)],
                             axis=0)
        s = lax.dot_general(qc, keys, (((1,), (1,)), ((), ())), preferred_element_type=F32) * scale
        e = jnp.exp(s - jnp.max(s, axis=-1, keepdims=True))
        p = e * (1.0 / jnp.sum(e, axis=-1, keepdims=True))
        oc = jnp.dot(p.astype(BF16), vals, preferred_element_type=F32)
        for j in range(group):
            o_s[pl.ds(r0, q_block), j * hd:(j + 1) * hd] = oc[j * q_block:(j + 1) * q_block, :].astype(BF16)
        return carry

    lax.fori_loop(0, nchunks, chunk, 0)
    acc_ref[...] += _mm(o_s[...], wo_ref[...])

    @pl.when(gi == pl.num_programs(1) - 1)
    def _():
        o_ref[...] = x_ref[...] + _gate(m_ref, 1) * acc_ref[...]


def _rope_tables(n_tok, hd):
    rows = n_tok // GRID_W
    r_idx = jnp.broadcast_to(jnp.arange(rows)[:, None], (rows, GRID_W)).reshape(n_tok).astype(F32)
    c_idx = jnp.broadcast_to(jnp.arange(GRID_W)[None, :], (rows, GRID_W)).reshape(n_tok).astype(F32)
    n_freq = hd // 4
    inv = ROPE_THETA ** (-jnp.arange(n_freq, dtype=F32) / n_freq)
    ang = jnp.stack([r_idx[:, None] * inv, c_idx[:, None] * inv], axis=1)
    cos, sin = jnp.cos(ang), jnp.sin(ang)
    cos_full = jnp.concatenate([cos, cos], axis=-1).reshape(n_tok, hd)
    sin_signed = jnp.concatenate([-sin, sin], axis=-1).reshape(n_tok, hd)
    return cos_full, sin_signed


def _attn(x, m, row_of_tile, norm_g, layer, w_qkv, q_g, k_g, w_o, seq_len, q_block,
          cache_k=None, cache_v=None, rope=False, emit_kv=False):
    t, d = x.shape
    hd, kvh = HEAD_DIM, N_KV_HEADS
    n_heads = w_o.shape[0] // hd
    group = n_heads // kvh
    gw = group * hd
    tm = TOKEN_TILE
    assert t % tm == 0 and tm % seq_len == 0 and seq_len % q_block == 0
    past_len = 0 if cache_k is None else cache_k.shape[1]
    assert past_len == 0 or tm == seq_len
    in_specs = [
        pl.BlockSpec((tm, d), lambda i, g: (i, 0)),
        pl.BlockSpec((None, None, 3 * N_SUB, d), lambda i, g: (layer, row_of_tile(i), 0, 0)),
        pl.BlockSpec((None, N_SUB, d), lambda i, g: (layer, 0, 0)),
        pl.BlockSpec((d, gw), lambda i, g: (0, g)),
        pl.BlockSpec((d, hd), lambda i, g: (0, n_heads + g)),
        pl.BlockSpec((d, hd), lambda i, g: (0, n_heads + kvh + g)),
        pl.BlockSpec((1, hd), lambda i, g: (0, 0)),
        pl.BlockSpec((1, hd), lambda i, g: (0, 0)),
        pl.BlockSpec((gw, d), lambda i, g: (g, 0)),
    ]
    args = [x, m, norm_g, w_qkv, w_qkv, w_qkv, q_g.reshape(1, hd), k_g.reshape(1, hd), w_o]
    if rope:
        assert tm == seq_len
        cos, sin = _rope_tables(seq_len, hd)
        in_specs += [pl.BlockSpec((tm, hd), lambda i, g: (0, 0))] * 2
        args += [cos, sin]
    if past_len:
        in_specs += [pl.BlockSpec((None, past_len, hd), lambda i, g: (i, 0, g))] * 2
        args += [cache_k, cache_v]
    out_specs = [pl.BlockSpec((tm, d), lambda i, g: (i, 0))]
    out_shape = [jax.ShapeDtypeStruct((t, d), F32)]
    if emit_kv:
        out_specs += [pl.BlockSpec((tm, hd), lambda i, g: (i, g))] * 2
        out_shape += [jax.ShapeDtypeStruct((t, kvh * hd), F32)] * 2
    nkeys = past_len + tm
    outs = pl.pallas_call(
        functools.partial(_attn_kernel, seq_len=seq_len, q_block=q_block, past_len=past_len,
                          rope=rope, emit_kv=emit_kv, group=group),
        grid=(t // tm, kvh),
        in_specs=in_specs,
        out_specs=out_specs,
        out_shape=out_shape,
        scratch_shapes=[pltpu.VMEM((tm, d), BF16), pltpu.VMEM((tm, d), F32),
                        pltpu.VMEM((tm, gw), BF16), pltpu.VMEM((nkeys, hd), BF16),
                        pltpu.VMEM((nkeys, hd), BF16), pltpu.VMEM((tm, gw), BF16)],
        compiler_params=_params("arbitrary", "arbitrary"),
        name="gqa",
    )(*args)
    return outs


def kernel(x_prompt, x_sample, c, state_lru, cache_k, cache_v, c_ctx, mod_w, mod_b, norm_g,
           ffn_w_gu, ffn_w_down, lru_w_in, lru_conv_w, lru_conv_b, lru_gate_w, lru_gate_b,
           lru_lambda, lru_w_out, att_w_qkv, att_q_g, att_k_g, att_w_o, final_g):
    b, s, d = x_prompt.shape
    db, ds, _ = x_sample.shape
    depth = mod_w.shape[0]
    n_mixers = 2
    assert ds % TOKEN_TILE == 0 and 1 + db <= SUBLANES

    xp = x_prompt.reshape(b * s, d)
    xs = x_sample.reshape(db * ds, d)
    cvecs = jnp.concatenate([c_ctx[None], c, jnp.zeros((SUBLANES - 1 - db, d), F32)], axis=0)
    m = _modulation(cvecs, mod_w, mod_b).reshape(depth, SUBLANES, 3 * N_SUB, d)

    tiles_per_sample = ds // TOKEN_TILE
    prompt_row = lambda i: 0
    sample_row = lambda i: 1 + i // tiles_per_sample

    new_states, new_k, new_v = [], [], []
    for layer in range(depth):
        j = layer // n_mixers
        last = layer == depth - 1
        xp = _ffn(xp, m, prompt_row, norm_g, ffn_w_gu, ffn_w_down, layer, 0, 0)
        xs = _ffn(xs, m, sample_row, norm_g, ffn_w_gu, ffn_w_down, layer, 0, 0)
        if layer % n_mixers == 0:
            lru_p = (lru_w_in[j], lru_conv_w[j], lru_conv_b[j], lru_gate_w[j], lru_gate_b[j],
                     lru_lambda[j], lru_w_out[j])
            xp, st = _lru(xp, m, prompt_row, norm_g, layer, *lru_p, seq_len=s, h0=None,
                          emit_state=True)
            new_states.append(st)
            xs, _ = _lru(xs, m, sample_row, norm_g, layer, *lru_p, seq_len=ds,
                         h0=state_lru[:, j], emit_state=False)
        else:
            att_p = (att_w_qkv[j], att_q_g[j], att_k_g[j], att_w_o[j])
            xp, kp, vp = _attn(xp, m, prompt_row, norm_g, layer, *att_p, seq_len=s, q_block=s,
                               emit_kv=True)
            new_k.append(kp.reshape(b, s, N_KV_HEADS, HEAD_DIM))
            new_v.append(vp.reshape(b, s, N_KV_HEADS, HEAD_DIM))
            past = cache_k.shape[2]
            ck = cache_k[:, j].reshape(db, past, N_KV_HEADS * HEAD_DIM)
            cv = cache_v[:, j].reshape(db, past, N_KV_HEADS * HEAD_DIM)
            (xs,) = _attn(xs, m, sample_row, norm_g, layer, *att_p, seq_len=ds, q_block=128,
                          cache_k=ck, cache_v=cv, rope=True)
        fg = final_g if last else None
        xp = _ffn(xp, m, prompt_row, norm_g, ffn_w_gu, ffn_w_down, layer, 1, 2, final_g=fg)
        xs = _ffn(xs, m, sample_row, norm_g, ffn_w_gu, ffn_w_down, layer, 1, 2, final_g=fg)

    y_prompt = xp.reshape(b, s, d)
    y_sample = xs.reshape(db, ds, d)
    return (y_prompt, y_sample, jnp.stack(new_states, axis=1), jnp.stack(new_k, axis=1),
            jnp.stack(new_v, axis=1))
```

```python
import functools

import jax
import jax.numpy as jnp
import numpy as np
from jax import lax
from jax.experimental import pallas as pl
from jax.experimental.pallas import tpu as pltpu

F32 = jnp.float32
BF16 = jnp.bfloat16

EPS = 1e-6
LRU_C = 8.0
GRID_W = 64
ROPE_THETA = 10000.0
N_SUB = 3
HEAD_DIM = 128
N_KV_HEADS = 2
LRU_BLOCKS = 16

V7X_VMEM_LIMIT_BYTES = 56 * 1024 * 1024
SUBLANES = 8
TOKEN_TILE = 1024
FF_CHUNK = 256
FFN_STAGE_ROWS = 256
FFN_STAGE_COLS = 1536
FFN_STAGE_SLOTS = 4
LRU_CHUNK = 256
MOD_CHUNK = 1536


def _params(*semantics):
    return pltpu.CompilerParams(dimension_semantics=semantics,
                                vmem_limit_bytes=V7X_VMEM_LIMIT_BYTES)


def _mm(a_bf16, w_f32):
    return jnp.dot(a_bf16, w_f32.astype(BF16), preferred_element_type=F32)


def _rms(x):
    return x * lax.rsqrt(jnp.mean(x * x, axis=-1, keepdims=True) + EPS)


def _sub_in(x, m_ref, g_ref, sidx):
    shift = m_ref[3 * sidx:3 * sidx + 1, :]
    scale = m_ref[3 * sidx + 1:3 * sidx + 2, :]
    return (_rms(x) * g_ref[sidx:sidx + 1, :]) * (1.0 + scale) + shift


def _gate(m_ref, sidx):
    return m_ref[3 * sidx + 2:3 * sidx + 3, :]


def _mod_kernel(cv_ref, w_ref, b_ref, o_ref):
    cv = cv_ref[...]
    act = (cv * jax.nn.sigmoid(cv)).astype(BF16)
    o_ref[...] = _mm(act, w_ref[...]) + b_ref[...]


def _modulation(cvecs, mod_w, mod_b):
    depth, d, n = mod_w.shape
    rows = cvecs.shape[0]
    tn = MOD_CHUNK
    assert n % tn == 0
    return pl.pallas_call(
        _mod_kernel,
        grid=(depth, n // tn),
        in_specs=[
            pl.BlockSpec((rows, d), lambda l, j: (0, 0)),
            pl.BlockSpec((None, d, tn), lambda l, j: (l, 0, j)),
            pl.BlockSpec((None, 1, tn), lambda l, j: (l, 0, j)),
        ],
        out_specs=pl.BlockSpec((None, rows, tn), lambda l, j: (l, 0, j)),
        out_shape=jax.ShapeDtypeStruct((depth, rows, n), F32),
        compiler_params=_params("arbitrary", "arbitrary"),
        name="modulation",
    )(cvecs, mod_w, mod_b.reshape(depth, 1, n))


def _interleave(acts, units):
    out = []
    for j, u in enumerate(units):
        out += [a for i, a in enumerate(acts) if (i * len(units)) // len(acts) == j]
        out.append(u)
    return out


def _ffn_stream_plan(d, f, tf):
    assert FFN_STAGE_COLS % tf == 0 and d % FFN_STAGE_ROWS == 0 and f % FFN_STAGE_ROWS == 0
    units, steps, ready_acts = [], [], []
    for c0 in range(0, f, FFN_STAGE_COLS):
        cw = min(FFN_STAGE_COLS, f - c0)
        block = []
        for part in range(2):
            for r0 in range(0, d, FFN_STAGE_ROWS):
                block.append(("unit", len(units)))
                units.append(("gu", part, r0, c0, cw))
        steps += _interleave(ready_acts, block) if ready_acts else block
        ready_acts = [("act", k) for k in range(c0 // tf, (c0 + cw) // tf)]
    block = []
    for r0 in range(0, f, FFN_STAGE_ROWS):
        block.append(("unit", len(units)))
        units.append(("wd", r0))
    steps += _interleave(ready_acts, block)
    return units, steps


def _ffn_kernel(*refs, layer, s, sidx, final):
    if final:
        (x_ref, m_ref, g_ref, wgu_hbm, wd_hbm, fg_ref, o_ref,
         wg_res, wu_res, wd_res, h_ref, act_ref, gu_buf, wd_buf, gu_sem, wd_sem) = refs
    else:
        (x_ref, m_ref, g_ref, wgu_hbm, wd_hbm, o_ref,
         wg_res, wu_res, wd_res, h_ref, act_ref, gu_buf, wd_buf, gu_sem, wd_sem) = refs
    i = pl.program_id(0)
    d, f = wg_res.shape
    tf = FF_CHUNK
    nk = f // tf
    nslot = gu_buf.shape[0]
    units, steps = _ffn_stream_plan(d, f, tf)
    n_gu = sum(1 for u in units if u[0] == "gu")

    def unit_copy(q):
        u = units[q]
        if u[0] == "gu":
            _, part, r0, c0, cw = u
            slot = q % nslot
            src = wgu_hbm.at[layer, s, pl.ds(r0, FFN_STAGE_ROWS), pl.ds(part * f + c0, cw)]
            return pltpu.make_async_copy(src, gu_buf.at[slot, :, pl.ds(0, cw)], gu_sem.at[slot])
        slot = (q - n_gu) % nslot
        src = wd_hbm.at[layer, s, pl.ds(u[1], FFN_STAGE_ROWS), :]
        return pltpu.make_async_copy(src, wd_buf.at[slot], wd_sem.at[slot])

    def unit_cast(q):
        u = units[q]
        if u[0] == "gu":
            _, part, r0, c0, cw = u
            res = wu_res if part else wg_res
            res[r0:r0 + FFN_STAGE_ROWS, c0:c0 + cw] = gu_buf[q % nslot, :, 0:cw].astype(BF16)
        else:
            wd_res[u[1]:u[1] + FFN_STAGE_ROWS, :] = wd_buf[(q - n_gu) % nslot].astype(BF16)

    def act_chunk(k):
        h = h_ref[...]
        gt = jnp.dot(h, wg_res[:, k * tf:(k + 1) * tf], preferred_element_type=F32)
        up = jnp.dot(h, wu_res[:, k * tf:(k + 1) * tf], preferred_element_type=F32)
        act_ref[:, k * tf:(k + 1) * tf] = ((gt * jax.nn.sigmoid(gt)) * up).astype(BF16)

    def finish():
        ff = jnp.dot(act_ref[...], wd_res[...], preferred_element_type=F32)
        y = x_ref[...] + (0.5 * _gate(m_ref, sidx)) * ff
        if final:
            y = _rms(y) * fg_ref[...]
        o_ref[...] = y

    @pl.when(i == 0)
    def _():
        ahead = nslot - 1
        for q in range(min(ahead, len(units))):
            unit_copy(q).start()
        h_ref[...] = _sub_in(x_ref[...], m_ref, g_ref, sidx).astype(BF16)
        for kind, idx in steps:
            if kind == "act":
                act_chunk(idx)
                continue
            unit_copy(idx).wait()
            if idx + ahead < len(units):
                unit_copy(idx + ahead).start()
            unit_cast(idx)
        finish()

    @pl.when(i > 0)
    def _():
        h_ref[...] = _sub_in(x_ref[...], m_ref, g_ref, sidx).astype(BF16)
        for k in range(nk):
            act_chunk(k)
        finish()


def _ffn(x, m, row_of_tile, norm_g, w_gu, w_down, layer, s, sidx, final_g=None):
    t, d = x.shape
    f = w_down.shape[2]
    tm, tf = TOKEN_TILE, FF_CHUNK
    assert t % tm == 0 and f % tf == 0
    final = final_g is not None
    in_specs = [
        pl.BlockSpec((tm, d), lambda i: (i, 0)),
        pl.BlockSpec((None, None, 3 * N_SUB, d), lambda i: (layer, row_of_tile(i), 0, 0)),
        pl.BlockSpec((None, N_SUB, d), lambda i: (layer, 0, 0)),
        pl.BlockSpec(memory_space=pl.ANY),
        pl.BlockSpec(memory_space=pl.ANY),
    ]
    args = [x, m, norm_g, w_gu, w_down]
    if final:
        in_specs.append(pl.BlockSpec((1, d), lambda i: (0, 0)))
        args.append(final_g.reshape(1, d))
    return pl.pallas_call(
        functools.partial(_ffn_kernel, layer=layer, s=s, sidx=sidx, final=final),
        grid=(t // tm,),
        in_specs=in_specs,
        out_specs=pl.BlockSpec((tm, d), lambda i: (i, 0)),
        out_shape=jax.ShapeDtypeStruct((t, d), F32),
        scratch_shapes=[
            pltpu.VMEM((d, f), BF16), pltpu.VMEM((d, f), BF16), pltpu.VMEM((f, d), BF16),
            pltpu.VMEM((tm, d), BF16), pltpu.VMEM((tm, f), BF16),
            pltpu.VMEM((FFN_STAGE_SLOTS, FFN_STAGE_ROWS, FFN_STAGE_COLS), F32),
            pltpu.VMEM((FFN_STAGE_SLOTS, FFN_STAGE_ROWS, d), F32),
            pltpu.SemaphoreType.DMA((FFN_STAGE_SLOTS,)),
            pltpu.SemaphoreType.DMA((FFN_STAGE_SLOTS,)),
        ],
        compiler_params=_params("arbitrary"),
        name="ffn",
    )(*args)


def _gelu_tanh(x):
    c = np.float32(np.sqrt(2.0 / np.pi))
    return x * (0.5 * (1.0 + jnp.tanh(c * (x + 0.044715 * (x * x * x)))))


def _group_scan(a, u, row8, reverse):
    rows = a.shape[0]
    for s in (1, 2, 4):
        if reverse:
            shift, valid = rows - s, row8 < SUBLANES - s
        else:
            shift, valid = s, row8 >= s
        u = jnp.where(valid, u + a * pltpu.roll(u, shift, 0), u)
        a = jnp.where(valid, a * pltpu.roll(a, shift, 0), a)
    return a, u


def _lru_kernel(*refs, seq_len, has_h0, emit_state, conv_left):
    refs = list(refs)
    (x_ref, m_ref, g_ref, wx_ref, wy_ref, cw_ref, cb_ref, gw_ref, gb_ref, lam_ref,
     wo_ref) = refs[:11]
    pos = 11
    h0_ref = None
    if has_h0:
        h0_ref = refs[pos]
        pos += 1
    o_ref = refs[pos]
    pos += 1
    st_ref = None
    if emit_state:
        st_ref = refs[pos]
        pos += 1
    h_ref, acc_ref, af_ref, uf_ref, ab_ref, ub_ref = refs[pos:]

    c = pl.program_id(1)
    tm, cw_cols = af_ref.shape
    nseq = tm // seq_len
    ngroups = seq_len // SUBLANES

    @pl.when(c == 0)
    def _():
        h_ref[...] = _sub_in(x_ref[...], m_ref, g_ref, 1).astype(BF16)
        acc_ref[...] = jnp.zeros_like(acc_ref)

    h = h_ref[...]
    xb = _mm(h, wx_ref[...])
    yb = _mm(h, wy_ref[...])

    row = lax.broadcasted_iota(jnp.int32, (tm, cw_cols), 0)
    t = row & (seq_len - 1)
    cw = cw_ref[...]
    xc = cb_ref[...]
    for k in range(cw.shape[0]):
        off = k - conv_left
        if off == 0:
            tap = xb
        else:
            valid = (t + off >= 0) & (t + off <= seq_len - 1)
            tap = jnp.where(valid, pltpu.roll(xb, (-off) % tm, 0), 0.0)
        xc = xc + tap * cw[k:k + 1, :]

    gl = jnp.dot(xc.astype(BF16), gw_ref[...], preferred_element_type=F32)
    gb = gb_ref[...]
    lam = lam_ref[...]
    row8 = row & (SUBLANES - 1)
    for d, (a_ref, u_ref) in enumerate(((af_ref, uf_ref), (ab_ref, ub_ref))):
        r = jax.nn.sigmoid(gl[:, (2 * d) * cw_cols:(2 * d + 1) * cw_cols] + gb[2 * d:2 * d + 1, :])
        ig = jax.nn.sigmoid(gl[:, (2 * d + 1) * cw_cols:(2 * d + 2) * cw_cols]
                            + gb[2 * d + 1:2 * d + 2, :])
        lm = lam[d:d + 1, :]
        log_sig = jnp.minimum(lm, 0.0) - jnp.log1p(jnp.exp(-jnp.abs(lm)))
        a = jnp.exp((LRU_C * r) * log_sig)
        u = jnp.sqrt(1.0 - a * a) * (ig * xc)
        a, u = _group_scan(a, u, row8, reverse=(d == 1))
        a_ref[...] = a
        u_ref[...] = u

    if has_h0:
        init = tuple(h0_ref[s, dd:dd + 1, :] for s in range(nseq) for dd in range(2))
    else:
        init = tuple(jnp.zeros((1, cw_cols), F32) for _ in range(2 * nseq))

    def body(j, carry):
        out = []
        for s in range(nseq):
            rf = pl.multiple_of(s * seq_len + j * SUBLANES, SUBLANES)
            rb = pl.multiple_of(s * seq_len + (ngroups - 1 - j) * SUBLANES, SUBLANES)
            hf = uf_ref[pl.ds(rf, SUBLANES), :] + af_ref[pl.ds(rf, SUBLANES), :] * carry[2 * s]
            uf_ref[pl.ds(rf, SUBLANES), :] = hf
            hb = ub_ref[pl.ds(rb, SUBLANES), :] + ab_ref[pl.ds(rb, SUBLANES), :] * carry[2 * s + 1]
            ub_ref[pl.ds(rb, SUBLANES), :] = hb
            out += [hf[SUBLANES - 1:SUBLANES, :], hb[0:1, :]]
        return tuple(out)

    last = lax.fori_loop(0, ngroups, body, init)
    if emit_state:
        for s in range(nseq):
            st_ref[s, 0:1, :] = last[2 * s]
            st_ref[s, 1:2, :] = last[2 * s + 1]

    y = (uf_ref[...] + ub_ref[...]) * _gelu_tanh(yb)
    acc_ref[...] += _mm(y.astype(BF16), wo_ref[...])

    @pl.when(c == pl.num_programs(1) - 1)
    def _():
        o_ref[...] = x_ref[...] + _gate(m_ref, 1) * acc_ref[...]


def _lru_gate_weights(gate_w):
    nd, ng, nb, bw, _ = gate_w.shape
    per = LRU_CHUNK // bw
    nc = nb // per
    w = gate_w.reshape(nd * ng, nc, per, bw, bw)
    eye = jnp.eye(per, dtype=gate_w.dtype)
    w = w[:, :, :, :, None, :] * eye[None, None, :, None, :, None]
    w = jnp.transpose(w, (1, 2, 3, 0, 4, 5))
    return w.reshape(nc, per * bw, nd * ng * per * bw).astype(BF16)


def _lru(x, m, row_of_tile, norm_g, layer, w_in, conv_w, conv_b, gate_w, gate_b, lam, w_out,
         seq_len, h0, emit_state):
    t, d = x.shape
    r = w_out.shape[0]
    tm, cb = TOKEN_TILE, LRU_CHUNK
    assert t % tm == 0 and tm % seq_len == 0 and r % cb == 0 and seq_len % SUBLANES == 0
    assert seq_len & (seq_len - 1) == 0, "in-sequence position is taken with a bit mask"
    nc = r // cb
    nseq = tm // seq_len
    has_h0 = h0 is not None
    in_specs = [
        pl.BlockSpec((tm, d), lambda i, c: (i, 0)),
        pl.BlockSpec((None, None, 3 * N_SUB, d), lambda i, c: (layer, row_of_tile(i), 0, 0)),
        pl.BlockSpec((None, N_SUB, d), lambda i, c: (layer, 0, 0)),
        pl.BlockSpec((d, cb), lambda i, c: (0, c)),
        pl.BlockSpec((d, cb), lambda i, c: (0, nc + c)),
        pl.BlockSpec((conv_w.shape[0], cb), lambda i, c: (0, c)),
        pl.BlockSpec((1, cb), lambda i, c: (0, c)),
        pl.BlockSpec((None, cb, 4 * cb), lambda i, c: (c, 0, 0)),
        pl.BlockSpec((4, cb), lambda i, c: (0, c)),
        pl.BlockSpec((2, cb), lambda i, c: (0, c)),
        pl.BlockSpec((cb, d), lambda i, c: (c, 0)),
    ]
    args = [x, m, norm_g, w_in, w_in, conv_w, conv_b.reshape(1, r), _lru_gate_weights(gate_w),
            gate_b.reshape(4, r), lam, w_out]
    if has_h0:
        in_specs.append(pl.BlockSpec((nseq, 2, cb), lambda i, c: (i, 0, c)))
        args.append(h0)
    out_specs = [pl.BlockSpec((tm, d), lambda i, c: (i, 0))]
    out_shape = [jax.ShapeDtypeStruct((t, d), F32)]
    if emit_state:
        out_specs.append(pl.BlockSpec((nseq, 2, cb), lambda i, c: (i, 0, c)))
        out_shape.append(jax.ShapeDtypeStruct((t // seq_len, 2, r), F32))
    outs = pl.pallas_call(
        functools.partial(_lru_kernel, seq_len=seq_len, has_h0=has_h0, emit_state=emit_state,
                          conv_left=(conv_w.shape[0] - 1) // 2),
        grid=(t // tm, nc),
        in_specs=in_specs,
        out_specs=out_specs,
        out_shape=out_shape,
        scratch_shapes=[pltpu.VMEM((tm, d), BF16), pltpu.VMEM((tm, d), F32)]
        + [pltpu.VMEM((tm, cb), F32)] * 4,
        compiler_params=_params("arbitrary", "arbitrary"),
        name="rglru",
    )(*args)
    return outs if emit_state else (outs[0], None)


def _rope(x, cos, sin_signed, lane):
    hd = x.shape[1]
    partner = jnp.where((lane & 32) == 0, pltpu.roll(x, hd - 32, 1), pltpu.roll(x, 32, 1))
    return x * cos + partner * sin_signed


def _attn_kernel(*refs, seq_len, q_block, past_len, rope, emit_kv, group):
    refs = list(refs)
    x_ref, m_ref, g_ref, wq_ref, wk_ref, wv_ref, qg_ref, kg_ref, wo_ref = refs[:9]
    pos = 9
    if rope:
        cos_ref, sin_ref = refs[pos:pos + 2]
        pos += 2
    if past_len:
        ck_ref, cv_ref = refs[pos:pos + 2]
        pos += 2
    o_ref = refs[pos]
    pos += 1
    if emit_kv:
        kn_ref, vn_ref = refs[pos:pos + 2]
        pos += 2
    h_ref, acc_ref, q_s, k_s, v_s, o_s = refs[pos:]

    gi = pl.program_id(1)
    tm = x_ref.shape[0]
    hd = k_s.shape[1]
    nqb = seq_len // q_block
    nchunks = (tm // seq_len) * nqb
    nk = past_len + seq_len
    scale = hd ** -0.5

    @pl.when(gi == 0)
    def _():
        h_ref[...] = _sub_in(x_ref[...], m_ref, g_ref, 1).astype(BF16)
        acc_ref[...] = jnp.zeros_like(acc_ref)

    h = h_ref[...]
    q = _mm(h, wq_ref[...])
    k = _rms(_mm(h, wk_ref[...])) * kg_ref[...]
    v = _mm(h, wv_ref[...])
    if emit_kv:
        kn_ref[...] = k
        vn_ref[...] = v
    if rope:
        lane = lax.broadcasted_iota(jnp.int32, (tm, hd), 1)
        cos, sin = cos_ref[...], sin_ref[...]
        k = _rope(k, cos, sin, lane)
    k_s[past_len:past_len + tm, :] = k.astype(BF16)
    v_s[past_len:past_len + tm, :] = v.astype(BF16)
    if past_len:
        k_s[0:past_len, :] = ck_ref[...].astype(BF16)
        v_s[0:past_len, :] = cv_ref[...].astype(BF16)
    for j in range(group):
        qh = _rms(q[:, j * hd:(j + 1) * hd]) * qg_ref[...]
        if rope:
            qh = _rope(qh, cos, sin, lane)
        q_s[:, j * hd:(j + 1) * hd] = qh.astype(BF16)

    def chunk(ci, carry):
        r0 = pl.multiple_of(ci * q_block, q_block)
        if past_len:
            keys, vals = k_s[...], v_s[...]
        else:
            koff = pl.multiple_of((ci // nqb) * seq_len, seq_len)
            keys, vals = k_s[pl.ds(koff, nk), :], v_s[pl.ds(koff, nk), :]
        qc = jnp.concatenate([q_s[pl.ds(r0, q_block), j * hd:(j + 1) * hd] for j in range(group)],
                             axis=0)
        s = lax.dot_general(qc, keys, (((1,), (1,)), ((), ())), preferred_element_type=F32) * scale
        e = jnp.exp(s - jnp.max(s, axis=-1, keepdims=True))
        p = e * (1.0 / jnp.sum(e, axis=-1, keepdims=True))
        oc = jnp.dot(p.astype(BF16), vals, preferred_element_type=F32)
        for j in range(group):
            o_s[pl.ds(r0, q_block), j * hd:(j + 1) * hd] = oc[j * q_block:(j + 1) * q_block, :].astype(BF16)
        return carry

    lax.fori_loop(0, nchunks, chunk, 0)
    acc_ref[...] += _mm(o_s[...], wo_ref[...])

    @pl.when(gi == pl.num_programs(1) - 1)
    def _():
        o_ref[...] = x_ref[...] + _gate(m_ref, 1) * acc_ref[...]


def _rope_tables(n_tok, hd):
    rows = n_tok // GRID_W
    r_idx = jnp.broadcast_to(jnp.arange(rows)[:, None], (rows, GRID_W)).reshape(n_tok).astype(F32)
    c_idx = jnp.broadcast_to(jnp.arange(GRID_W)[None, :], (rows, GRID_W)).reshape(n_tok).astype(F32)
    n_freq = hd // 4
    inv = ROPE_THETA ** (-jnp.arange(n_freq, dtype=F32) / n_freq)
    ang = jnp.stack([r_idx[:, None] * inv, c_idx[:, None] * inv], axis=1)
    cos, sin = jnp.cos(ang), jnp.sin(ang)
    cos_full = jnp.concatenate([cos, cos], axis=-1).reshape(n_tok, hd)
    sin_signed = jnp.concatenate([-sin, sin], axis=-1).reshape(n_tok, hd)
    return cos_full, sin_signed


def _attn(x, m, row_of_tile, norm_g, layer, w_qkv, q_g, k_g, w_o, seq_len, q_block,
          cache_k=None, cache_v=None, rope=False, emit_kv=False):
    t, d = x.shape
    hd, kvh = HEAD_DIM, N_KV_HEADS
    n_heads = w_o.shape[0] // hd
    group = n_heads // kvh
    gw = group * hd
    tm = TOKEN_TILE
    assert t % tm == 0 and tm % seq_len == 0 and seq_len % q_block == 0
    past_len = 0 if cache_k is None else cache_k.shape[1]
    assert past_len == 0 or tm == seq_len
    in_specs = [
        pl.BlockSpec((tm, d), lambda i, g: (i, 0)),
        pl.BlockSpec((None, None, 3 * N_SUB, d), lambda i, g: (layer, row_of_tile(i), 0, 0)),
        pl.BlockSpec((None, N_SUB, d), lambda i, g: (layer, 0, 0)),
        pl.BlockSpec((d, gw), lambda i, g: (0, g)),
        pl.BlockSpec((d, hd), lambda i, g: (0, n_heads + g)),
        pl.BlockSpec((d, hd), lambda i, g: (0, n_heads + kvh + g)),
        pl.BlockSpec((1, hd), lambda i, g: (0, 0)),
        pl.BlockSpec((1, hd), lambda i, g: (0, 0)),
        pl.BlockSpec((gw, d), lambda i, g: (g, 0)),
    ]
    args = [x, m, norm_g, w_qkv, w_qkv, w_qkv, q_g.reshape(1, hd), k_g.reshape(1, hd), w_o]
    if rope:
        assert tm == seq_len
        cos, sin = _rope_tables(seq_len, hd)
        in_specs += [pl.BlockSpec((tm, hd), lambda i, g: (0, 0))] * 2
        args += [cos, sin]
    if past_len:
        in_specs += [pl.BlockSpec((None, past_len, hd), lambda i, g: (i, 0, g))] * 2
        args += [cache_k, cache_v]
    out_specs = [pl.BlockSpec((tm, d), lambda i, g: (i, 0))]
    out_shape = [jax.ShapeDtypeStruct((t, d), F32)]
    if emit_kv:
        out_specs += [pl.BlockSpec((tm, hd), lambda i, g: (i, g))] * 2
        out_shape += [jax.ShapeDtypeStruct((t, kvh * hd), F32)] * 2
    nkeys = past_len + tm
    outs = pl.pallas_call(
        functools.partial(_attn_kernel, seq_len=seq_len, q_block=q_block, past_len=past_len,
                          rope=rope, emit_kv=emit_kv, group=group),
        grid=(t // tm, kvh),
        in_specs=in_specs,
        out_specs=out_specs,
        out_shape=out_shape,
        scratch_shapes=[pltpu.VMEM((tm, d), BF16), pltpu.VMEM((tm, d), F32),
                        pltpu.VMEM((tm, gw), BF16), pltpu.VMEM((nkeys, hd), BF16),
                        pltpu.VMEM((nkeys, hd), BF16), pltpu.VMEM((tm, gw), BF16)],
        compiler_params=_params("arbitrary", "arbitrary"),
        name="gqa",
    )(*args)
    return outs


def kernel(x_prompt, x_sample, c, state_lru, cache_k, cache_v, c_ctx, mod_w, mod_b, norm_g,
           ffn_w_gu, ffn_w_down, lru_w_in, lru_conv_w, lru_conv_b, lru_gate_w, lru_gate_b,
           lru_lambda, lru_w_out, att_w_qkv, att_q_g, att_k_g, att_w_o, final_g):
    b, s, d = x_prompt.shape
    db, ds, _ = x_sample.shape
    depth = mod_w.shape[0]
    n_mixers = 2
    assert ds % TOKEN_TILE == 0 and 1 + db <= SUBLANES

    xp = x_prompt.reshape(b * s, d)
    xs = x_sample.reshape(db * ds, d)
    cvecs = jnp.concatenate([c_ctx[None], c, jnp.zeros((SUBLANES - 1 - db, d), F32)], axis=0)
    m = _modulation(cvecs, mod_w, mod_b).reshape(depth, SUBLANES, 3 * N_SUB, d)

    tiles_per_sample = ds // TOKEN_TILE
    prompt_row = lambda i: 0
    sample_row = lambda i: 1 + i // tiles_per_sample

    new_states, new_k, new_v = [], [], []
    for layer in range(depth):
        j = layer // n_mixers
        last = layer == depth - 1
        xp = _ffn(xp, m, prompt_row, norm_g, ffn_w_gu, ffn_w_down, layer, 0, 0)
        xs = _ffn(xs, m, sample_row, norm_g, ffn_w_gu, ffn_w_down, layer, 0, 0)
        if layer % n_mixers == 0:
            lru_p = (lru_w_in[j], lru_conv_w[j], lru_conv_b[j], lru_gate_w[j], lru_gate_b[j],
                     lru_lambda[j], lru_w_out[j])
            xp, st = _lru(xp, m, prompt_row, norm_g, layer, *lru_p, seq_len=s, h0=None,
                          emit_state=True)
            new_states.append(st)
            xs, _ = _lru(xs, m, sample_row, norm_g, layer, *lru_p, seq_len=ds,
                         h0=state_lru[:, j], emit_state=False)
        else:
            att_p = (att_w_qkv[j], att_q_g[j], att_k_g[j], att_w_o[j])
            xp, kp, vp = _attn(xp, m, prompt_row, norm_g, layer, *att_p, seq_len=s, q_block=s,
                               emit_kv=True)
            new_k.append(kp.reshape(b, s, N_KV_HEADS, HEAD_DIM))
            new_v.append(vp.reshape(b, s, N_KV_HEADS, HEAD_DIM))
            past = cache_k.shape[2]
            ck = cache_k[:, j].reshape(db, past, N_KV_HEADS * HEAD_DIM)
            cv = cache_v[:, j].reshape(db, past, N_KV_HEADS * HEAD_DIM)
            (xs,) = _attn(xs, m, sample_row, norm_g, layer, *att_p, seq_len=ds, q_block=128,
                          cache_k=ck, cache_v=cv, rope=True)
        fg = final_g if last else None
        xp = _ffn(xp, m, prompt_row, norm_g, ffn_w_gu, ffn_w_down, layer, 1, 2, final_g=fg)
        xs = _ffn(xs, m, sample_row, norm_g, ffn_w_gu, ffn_w_down, layer, 1, 2, final_g=fg)

    y_prompt = xp.reshape(b, s, d)
    y_sample = xs.reshape(db, ds, d)
    return (y_prompt, y_sample, jnp.stack(new_states, axis=1), jnp.stack(new_k, axis=1),
            jnp.stack(new_v, axis=1))
```

```python
import functools

import jax
import jax.numpy as jnp
import numpy as np
from jax import lax
from jax.experimental import pallas as pl
from jax.experimental.pallas import tpu as pltpu

F32 = jnp.float32
BF16 = jnp.bfloat16

EPS = 1e-6
LRU_C = 8.0
GRID_W = 64
ROPE_THETA = 10000.0
N_SUB = 3
HEAD_DIM = 128
N_KV_HEADS = 2
LRU_BLOCKS = 16

V7X_VMEM_LIMIT_BYTES = 56 * 1024 * 1024
SUBLANES = 8
TOKEN_TILE = 1024
FF_CHUNK = 256
FFN_STAGE_ROWS = 256
FFN_STAGE_COLS = 1536
FFN_STAGE_SLOTS = 4
LRU_CHUNK = 256
MOD_CHUNK = 1536


def _params(*semantics):
    return pltpu.CompilerParams(dimension_semantics=semantics,
                                vmem_limit_bytes=V7X_VMEM_LIMIT_BYTES)


def _mm(a_bf16, w_f32):
    return jnp.dot(a_bf16, w_f32.astype(BF16), preferred_element_type=F32)


def _rms(x):
    return x * lax.rsqrt(jnp.mean(x * x, axis=-1, keepdims=True) + EPS)


def _sub_in(x, m_ref, g_ref, sidx):
    shift = m_ref[3 * sidx:3 * sidx + 1, :]
    scale = m_ref[3 * sidx + 1:3 * sidx + 2, :]
    return (_rms(x) * g_ref[sidx:sidx + 1, :]) * (1.0 + scale) + shift


def _gate(m_ref, sidx):
    return m_ref[3 * sidx + 2:3 * sidx + 3, :]


def _mod_kernel(cv_ref, w_ref, b_ref, o_ref):
    cv = cv_ref[...]
    act = (cv * jax.nn.sigmoid(cv)).astype(BF16)
    o_ref[...] = _mm(act, w_ref[...]) + b_ref[...]


def _modulation(cvecs, mod_w, mod_b):
    depth, d, n = mod_w.shape
    rows = cvecs.shape[0]
    tn = MOD_CHUNK
    assert n % tn == 0
    return pl.pallas_call(
        _mod_kernel,
        grid=(depth, n // tn),
        in_specs=[
            pl.BlockSpec((rows, d), lambda l, j: (0, 0)),
            pl.BlockSpec((None, d, tn), lambda l, j: (l, 0, j)),
            pl.BlockSpec((None, 1, tn), lambda l, j: (l, 0, j)),
        ],
        out_specs=pl.BlockSpec((None, rows, tn), lambda l, j: (l, 0, j)),
        out_shape=jax.ShapeDtypeStruct((depth, rows, n), F32),
        compiler_params=_params("arbitrary", "arbitrary"),
        name="modulation",
    )(cvecs, mod_w, mod_b.reshape(depth, 1, n))


def _interleave(acts, units):
    out = []
    for j, u in enumerate(units):
        out += [a for i, a in enumerate(acts) if (i * len(units)) // len(acts) == j]
        out.append(u)
    return out


def _ffn_stream_plan(d, f, tf):
    assert FFN_STAGE_COLS % tf == 0 and d % FFN_STAGE_ROWS == 0 and f % FFN_STAGE_ROWS == 0
    units, steps, ready_acts = [], [], []
    for c0 in range(0, f, FFN_STAGE_COLS):
        cw = min(FFN_STAGE_COLS, f - c0)
        block = []
        for part in range(2):
            for r0 in range(0, d, FFN_STAGE_ROWS):
                block.append(("unit", len(units)))
                units.append(("gu", part, r0, c0, cw))
        steps += _interleave(ready_acts, block) if ready_acts else block
        ready_acts = [("act", k) for k in range(c0 // tf, (c0 + cw) // tf)]
    block = []
    for r0 in range(0, f, FFN_STAGE_ROWS):
        block.append(("unit", len(units)))
        units.append(("wd", r0))
    steps += _interleave(ready_acts, block)
    return units, steps


def _ffn_kernel(*refs, layer, s, sidx, final):
    if final:
        (x_ref, m_ref, g_ref, wgu_hbm, wd_hbm, fg_ref, o_ref,
         wg_res, wu_res, wd_res, h_ref, act_ref, gu_buf, wd_buf, gu_sem, wd_sem) = refs
    else:
        (x_ref, m_ref, g_ref, wgu_hbm, wd_hbm, o_ref,
         wg_res, wu_res, wd_res, h_ref, act_ref, gu_buf, wd_buf, gu_sem, wd_sem) = refs
    i = pl.program_id(0)
    d, f = wg_res.shape
    tf = FF_CHUNK
    nk = f // tf
    nslot = gu_buf.shape[0]
    units, steps = _ffn_stream_plan(d, f, tf)
    n_gu = sum(1 for u in units if u[0] == "gu")

    def unit_copy(q):
        u = units[q]
        if u[0] == "gu":
            _, part, r0, c0, cw = u
            slot = q % nslot
            src = wgu_hbm.at[layer, s, pl.ds(r0, FFN_STAGE_ROWS), pl.ds(part * f + c0, cw)]
            return pltpu.make_async_copy(src, gu_buf.at[slot, :, pl.ds(0, cw)], gu_sem.at[slot])
        slot = (q - n_gu) % nslot
        src = wd_hbm.at[layer, s, pl.ds(u[1], FFN_STAGE_ROWS), :]
        return pltpu.make_async_copy(src, wd_buf.at[slot], wd_sem.at[slot])

    def unit_cast(q):
        u = units[q]
        if u[0] == "gu":
            _, part, r0, c0, cw = u
            res = wu_res if part else wg_res
            res[r0:r0 + FFN_STAGE_ROWS, c0:c0 + cw] = gu_buf[q % nslot, :, 0:cw].astype(BF16)
        else:
            wd_res[u[1]:u[1] + FFN_STAGE_ROWS, :] = wd_buf[(q - n_gu) % nslot].astype(BF16)

    def act_chunk(k):
        h = h_ref[...]
        gt = jnp.dot(h, wg_res[:, k * tf:(k + 1) * tf], preferred_element_type=F32)
        up = jnp.dot(h, wu_res[:, k * tf:(k + 1) * tf], preferred_element_type=F32)
        act_ref[:, k * tf:(k + 1) * tf] = ((gt * jax.nn.sigmoid(gt)) * up).astype(BF16)

    def finish():
        ff = jnp.dot(act_ref[...], wd_res[...], preferred_element_type=F32)
        y = x_ref[...] + (0.5 * _gate(m_ref, sidx)) * ff
        if final:
            y = _rms(y) * fg_ref[...]
        o_ref[...] = y

    @pl.when(i == 0)
    def _():
        ahead = nslot - 1
        for q in range(min(ahead, len(units))):
            unit_copy(q).start()
        h_ref[...] = _sub_in(x_ref[...], m_ref, g_ref, sidx).astype(BF16)
        for kind, idx in steps:
            if kind == "act":
                act_chunk(idx)
                continue
            unit_copy(idx).wait()
            if idx + ahead < len(units):
                unit_copy(idx + ahead).start()
            unit_cast(idx)
        finish()

    @pl.when(i > 0)
    def _():
        h_ref[...] = _sub_in(x_ref[...], m_ref, g_ref, sidx).astype(BF16)
        for k in range(nk):
            act_chunk(k)
        finish()


def _ffn(x, m, row_of_tile, norm_g, w_gu, w_down, layer, s, sidx, final_g=None, w_layer=None):
    w_layer = layer if w_layer is None else w_layer
    t, d = x.shape
    f = w_down.shape[2]
    tm, tf = TOKEN_TILE, FF_CHUNK
    assert t % tm == 0 and f % tf == 0
    final = final_g is not None
    in_specs = [
        pl.BlockSpec((tm, d), lambda i: (i, 0)),
        pl.BlockSpec((None, None, 3 * N_SUB, d), lambda i: (layer, row_of_tile(i), 0, 0)),
        pl.BlockSpec((None, N_SUB, d), lambda i: (layer, 0, 0)),
        pl.BlockSpec(memory_space=pl.ANY),
        pl.BlockSpec(memory_space=pl.ANY),
    ]
    args = [x, m, norm_g, w_gu, w_down]
    if final:
        in_specs.append(pl.BlockSpec((1, d), lambda i: (0, 0)))
        args.append(final_g.reshape(1, d))
    return pl.pallas_call(
        functools.partial(_ffn_kernel, layer=w_layer, s=s, sidx=sidx, final=final),
        grid=(t // tm,),
        in_specs=in_specs,
        out_specs=pl.BlockSpec((tm, d), lambda i: (i, 0)),
        out_shape=jax.ShapeDtypeStruct((t, d), F32),
        scratch_shapes=[
            pltpu.VMEM((d, f), BF16), pltpu.VMEM((d, f), BF16), pltpu.VMEM((f, d), BF16),
            pltpu.VMEM((tm, d), BF16), pltpu.VMEM((tm, f), BF16),
            pltpu.VMEM((FFN_STAGE_SLOTS, FFN_STAGE_ROWS, FFN_STAGE_COLS), F32),
            pltpu.VMEM((FFN_STAGE_SLOTS, FFN_STAGE_ROWS, d), F32),
            pltpu.SemaphoreType.DMA((FFN_STAGE_SLOTS,)),
            pltpu.SemaphoreType.DMA((FFN_STAGE_SLOTS,)),
        ],
        compiler_params=_params("arbitrary"),
        name="ffn",
    )(*args)


def _gelu_tanh(x):
    c = np.float32(np.sqrt(2.0 / np.pi))
    return x * (0.5 * (1.0 + jnp.tanh(c * (x + 0.044715 * (x * x * x)))))


def _group_scan(a, u, row8, reverse):
    rows = a.shape[0]
    for s in (1, 2, 4):
        if reverse:
            shift, valid = rows - s, row8 < SUBLANES - s
        else:
            shift, valid = s, row8 >= s
        u = jnp.where(valid, u + a * pltpu.roll(u, shift, 0), u)
        a = jnp.where(valid, a * pltpu.roll(a, shift, 0), a)
    return a, u


def _lru_kernel(*refs, seq_len, has_h0, emit_state, conv_left):
    refs = list(refs)
    (x_ref, m_ref, g_ref, wx_ref, wy_ref, cw_ref, cb_ref, gw_ref, gb_ref, lam_ref,
     wo_ref) = refs[:11]
    pos = 11
    h0_ref = None
    if has_h0:
        h0_ref = refs[pos]
        pos += 1
    o_ref = refs[pos]
    pos += 1
    st_ref = None
    if emit_state:
        st_ref = refs[pos]
        pos += 1
    h_ref, acc_ref, af_ref, uf_ref, ab_ref, ub_ref = refs[pos:]

    c = pl.program_id(1)
    tm, cw_cols = af_ref.shape
    nseq = tm // seq_len
    ngroups = seq_len // SUBLANES

    @pl.when(c == 0)
    def _():
        h_ref[...] = _sub_in(x_ref[...], m_ref, g_ref, 1).astype(BF16)
        acc_ref[...] = jnp.zeros_like(acc_ref)

    h = h_ref[...]
    xb = _mm(h, wx_ref[...])
    yb = _mm(h, wy_ref[...])

    row = lax.broadcasted_iota(jnp.int32, (tm, cw_cols), 0)
    t = row & (seq_len - 1)
    cw = cw_ref[...]
    xc = cb_ref[...]
    for k in range(cw.shape[0]):
        off = k - conv_left
        if off == 0:
            tap = xb
        else:
            valid = (t + off >= 0) & (t + off <= seq_len - 1)
            tap = jnp.where(valid, pltpu.roll(xb, (-off) % tm, 0), 0.0)
        xc = xc + tap * cw[k:k + 1, :]

    gl = jnp.dot(xc.astype(BF16), gw_ref[...], preferred_element_type=F32)
    gb = gb_ref[...]
    lam = lam_ref[...]
    row8 = row & (SUBLANES - 1)
    for d, (a_ref, u_ref) in enumerate(((af_ref, uf_ref), (ab_ref, ub_ref))):
        r = jax.nn.sigmoid(gl[:, (2 * d) * cw_cols:(2 * d + 1) * cw_cols] + gb[2 * d:2 * d + 1, :])
        ig = jax.nn.sigmoid(gl[:, (2 * d + 1) * cw_cols:(2 * d + 2) * cw_cols]
                            + gb[2 * d + 1:2 * d + 2, :])
        lm = lam[d:d + 1, :]
        log_sig = jnp.minimum(lm, 0.0) - jnp.log1p(jnp.exp(-jnp.abs(lm)))
        a = jnp.exp((LRU_C * r) * log_sig)
        u = jnp.sqrt(1.0 - a * a) * (ig * xc)
        a, u = _group_scan(a, u, row8, reverse=(d == 1))
        a_ref[...] = a
        u_ref[...] = u

    if has_h0:
        init = tuple(h0_ref[s, dd:dd + 1, :] for s in range(nseq) for dd in range(2))
    else:
        init = tuple(jnp.zeros((1, cw_cols), F32) for _ in range(2 * nseq))

    def body(j, carry):
        out = []
        for s in range(nseq):
            rf = pl.multiple_of(s * seq_len + j * SUBLANES, SUBLANES)
            rb = pl.multiple_of(s * seq_len + (ngroups - 1 - j) * SUBLANES, SUBLANES)
            hf = uf_ref[pl.ds(rf, SUBLANES), :] + af_ref[pl.ds(rf, SUBLANES), :] * carry[2 * s]
            uf_ref[pl.ds(rf, SUBLANES), :] = hf
            hb = ub_ref[pl.ds(rb, SUBLANES), :] + ab_ref[pl.ds(rb, SUBLANES), :] * carry[2 * s + 1]
            ub_ref[pl.ds(rb, SUBLANES), :] = hb
            out += [hf[SUBLANES - 1:SUBLANES, :], hb[0:1, :]]
        return tuple(out)

    last = lax.fori_loop(0, ngroups, body, init)
    if emit_state:
        for s in range(nseq):
            st_ref[s, 0:1, :] = last[2 * s]
            st_ref[s, 1:2, :] = last[2 * s + 1]

    y = (uf_ref[...] + ub_ref[...]) * _gelu_tanh(yb)
    acc_ref[...] += _mm(y.astype(BF16), wo_ref[...])

    @pl.when(c == pl.num_programs(1) - 1)
    def _():
        o_ref[...] = x_ref[...] + _gate(m_ref, 1) * acc_ref[...]


def _lru_gate_weights(gate_w):
    nd, ng, nb, bw, _ = gate_w.shape
    per = LRU_CHUNK // bw
    nc = nb // per
    w = gate_w.reshape(nd * ng, nc, per, bw, bw)
    eye = jnp.eye(per, dtype=gate_w.dtype)
    w = w[:, :, :, :, None, :] * eye[None, None, :, None, :, None]
    w = jnp.transpose(w, (1, 2, 3, 0, 4, 5))
    return w.reshape(nc, per * bw, nd * ng * per * bw).astype(BF16)


def _lru(x, m, row_of_tile, norm_g, layer, w_in, conv_w, conv_b, gate_w, gate_b, lam, w_out,
         seq_len, h0, emit_state):
    t, d = x.shape
    r = w_out.shape[0]
    tm, cb = TOKEN_TILE, LRU_CHUNK
    assert t % tm == 0 and tm % seq_len == 0 and r % cb == 0 and seq_len % SUBLANES == 0
    assert seq_len & (seq_len - 1) == 0, "in-sequence position is taken with a bit mask"
    nc = r // cb
    nseq = tm // seq_len
    has_h0 = h0 is not None
    in_specs = [
        pl.BlockSpec((tm, d), lambda i, c: (i, 0)),
        pl.BlockSpec((None, None, 3 * N_SUB, d), lambda i, c: (layer, row_of_tile(i), 0, 0)),
        pl.BlockSpec((None, N_SUB, d), lambda i, c: (layer, 0, 0)),
        pl.BlockSpec((d, cb), lambda i, c: (0, c)),
        pl.BlockSpec((d, cb), lambda i, c: (0, nc + c)),
        pl.BlockSpec((conv_w.shape[0], cb), lambda i, c: (0, c)),
        pl.BlockSpec((1, cb), lambda i, c: (0, c)),
        pl.BlockSpec((None, cb, 4 * cb), lambda i, c: (c, 0, 0)),
        pl.BlockSpec((4, cb), lambda i, c: (0, c)),
        pl.BlockSpec((2, cb), lambda i, c: (0, c)),
        pl.BlockSpec((cb, d), lambda i, c: (c, 0)),
    ]
    args = [x, m, norm_g, w_in, w_in, conv_w, conv_b.reshape(1, r), _lru_gate_weights(gate_w),
            gate_b.reshape(4, r), lam, w_out]
    if has_h0:
        in_specs.append(pl.BlockSpec((nseq, 2, cb), lambda i, c: (i, 0, c)))
        args.append(h0)
    out_specs = [pl.BlockSpec((tm, d), lambda i, c: (i, 0))]
    out_shape = [jax.ShapeDtypeStruct((t, d), F32)]
    if emit_state:
        out_specs.append(pl.BlockSpec((nseq, 2, cb), lambda i, c: (i, 0, c)))
        out_shape.append(jax.ShapeDtypeStruct((t // seq_len, 2, r), F32))
    outs = pl.pallas_call(
        functools.partial(_lru_kernel, seq_len=seq_len, has_h0=has_h0, emit_state=emit_state,
                          conv_left=(conv_w.shape[0] - 1) // 2),
        grid=(t // tm, nc),
        in_specs=in_specs,
        out_specs=out_specs,
        out_shape=out_shape,
        scratch_shapes=[pltpu.VMEM((tm, d), BF16), pltpu.VMEM((tm, d), F32)]
        + [pltpu.VMEM((tm, cb), F32)] * 4,
        compiler_params=_params("arbitrary", "arbitrary"),
        name="rglru",
    )(*args)
    return outs if emit_state else (outs[0], None)


def _rope(x, cos, sin_signed, lane):
    hd = x.shape[1]
    partner = jnp.where((lane & 32) == 0, pltpu.roll(x, hd - 32, 1), pltpu.roll(x, 32, 1))
    return x * cos + partner * sin_signed


def _attn_kernel(*refs, seq_len, q_block, past_len, rope, emit_kv, group):
    refs = list(refs)
    x_ref, m_ref, g_ref, wq_ref, wk_ref, wv_ref, qg_ref, kg_ref, wo_ref = refs[:9]
    pos = 9
    if rope:
        cos_ref, sin_ref = refs[pos:pos + 2]
        pos += 2
    if past_len:
        ck_ref, cv_ref = refs[pos:pos + 2]
        pos += 2
    o_ref = refs[pos]
    pos += 1
    if emit_kv:
        kn_ref, vn_ref = refs[pos:pos + 2]
        pos += 2
    h_ref, acc_ref, q_s, k_s, v_s, o_s = refs[pos:]

    gi = pl.program_id(1)
    tm = x_ref.shape[0]
    hd = k_s.shape[1]
    nqb = seq_len // q_block
    nchunks = (tm // seq_len) * nqb
    nk = past_len + seq_len
    scale = hd ** -0.5

    @pl.when(gi == 0)
    def _():
        h_ref[...] = _sub_in(x_ref[...], m_ref, g_ref, 1).astype(BF16)
        acc_ref[...] = jnp.zeros_like(acc_ref)

    h = h_ref[...]
    q = _mm(h, wq_ref[...])
    k = _rms(_mm(h, wk_ref[...])) * kg_ref[...]
    v = _mm(h, wv_ref[...])
    if emit_kv:
        kn_ref[...] = k
        vn_ref[...] = v
    if rope:
        lane = lax.broadcasted_iota(jnp.int32, (tm, hd), 1)
        cos, sin = cos_ref[...], sin_ref[...]
        k = _rope(k, cos, sin, lane)
    k_s[past_len:past_len + tm, :] = k.astype(BF16)
    v_s[past_len:past_len + tm, :] = v.astype(BF16)
    if past_len:
        k_s[0:past_len, :] = ck_ref[...].astype(BF16)
        v_s[0:past_len, :] = cv_ref[...].astype(BF16)
    for j in range(group):
        qh = _rms(q[:, j * hd:(j + 1) * hd]) * qg_ref[...]
        if rope:
            qh = _rope(qh, cos, sin, lane)
        q_s[:, j * hd:(j + 1) * hd] = qh.astype(BF16)

    def chunk(ci, carry):
        r0 = pl.multiple_of(ci * q_block, q_block)
        if past_len:
            keys, vals = k_s[...], v_s[...]
        else:
            koff = pl.multiple_of((ci // nqb) * seq_len, seq_len)
            keys, vals = k_s[pl.ds(koff, nk), :], v_s[pl.ds(koff, nk), :]
        qc = jnp.concatenate([q_s[pl.ds(r0, q_block), j * hd:(j + 1) * hd] for j in range(group)],
                             axis=0)
        s = lax.dot_general(qc, keys, (((1,), (1,)), ((), ())), preferred_element_type=F32) * scale
        e = jnp.exp(s - jnp.max(s, axis=-1, keepdims=True))
        p = e * (1.0 / jnp.sum(e, axis=-1, keepdims=True))
        oc = jnp.dot(p.astype(BF16), vals, preferred_element_type=F32)
        for j in range(group):
            o_s[pl.ds(r0, q_block), j * hd:(j + 1) * hd] = oc[j * q_block:(j + 1) * q_block, :].astype(BF16)
        return carry

    lax.fori_loop(0, nchunks, chunk, 0)
    acc_ref[...] += _mm(o_s[...], wo_ref[...])

    @pl.when(gi == pl.num_programs(1) - 1)
    def _():
        o_ref[...] = x_ref[...] + _gate(m_ref, 1) * acc_ref[...]


def _rope_tables(n_tok, hd):
    rows = n_tok // GRID_W
    r_idx = jnp.broadcast_to(jnp.arange(rows)[:, None], (rows, GRID_W)).reshape(n_tok).astype(F32)
    c_idx = jnp.broadcast_to(jnp.arange(GRID_W)[None, :], (rows, GRID_W)).reshape(n_tok).astype(F32)
    n_freq = hd // 4
    inv = ROPE_THETA ** (-jnp.arange(n_freq, dtype=F32) / n_freq)
    ang = jnp.stack([r_idx[:, None] * inv, c_idx[:, None] * inv], axis=1)
    cos, sin = jnp.cos(ang), jnp.sin(ang)
    cos_full = jnp.concatenate([cos, cos], axis=-1).reshape(n_tok, hd)
    sin_signed = jnp.concatenate([-sin, sin], axis=-1).reshape(n_tok, hd)
    return cos_full, sin_signed


def _attn(x, m, row_of_tile, norm_g, layer, w_qkv, q_g, k_g, w_o, seq_len, q_block,
          cache_k=None, cache_v=None, rope=False, emit_kv=False):
    t, d = x.shape
    hd, kvh = HEAD_DIM, N_KV_HEADS
    n_heads = w_o.shape[0] // hd
    group = n_heads // kvh
    gw = group * hd
    tm = TOKEN_TILE
    assert t % tm == 0 and tm % seq_len == 0 and seq_len % q_block == 0
    past_len = 0 if cache_k is None else cache_k.shape[1]
    assert past_len == 0 or tm == seq_len
    in_specs = [
        pl.BlockSpec((tm, d), lambda i, g: (i, 0)),
        pl.BlockSpec((None, None, 3 * N_SUB, d), lambda i, g: (layer, row_of_tile(i), 0, 0)),
        pl.BlockSpec((None, N_SUB, d), lambda i, g: (layer, 0, 0)),
        pl.BlockSpec((d, gw), lambda i, g: (0, g)),
        pl.BlockSpec((d, hd), lambda i, g: (0, n_heads + g)),
        pl.BlockSpec((d, hd), lambda i, g: (0, n_heads + kvh + g)),
        pl.BlockSpec((1, hd), lambda i, g: (0, 0)),
        pl.BlockSpec((1, hd), lambda i, g: (0, 0)),
        pl.BlockSpec((gw, d), lambda i, g: (g, 0)),
    ]
    args = [x, m, norm_g, w_qkv, w_qkv, w_qkv, q_g.reshape(1, hd), k_g.reshape(1, hd), w_o]
    if rope:
        assert tm == seq_len
        cos, sin = _rope_tables(seq_len, hd)
        in_specs += [pl.BlockSpec((tm, hd), lambda i, g: (0, 0))] * 2
        args += [cos, sin]
    if past_len:
        in_specs += [pl.BlockSpec((None, past_len, hd), lambda i, g: (i, 0, g))] * 2
        args += [cache_k, cache_v]
    out_specs = [pl.BlockSpec((tm, d), lambda i, g: (i, 0))]
    out_shape = [jax.ShapeDtypeStruct((t, d), F32)]
    if emit_kv:
        out_specs += [pl.BlockSpec((tm, hd), lambda i, g: (i, g))] * 2
        out_shape += [jax.ShapeDtypeStruct((t, kvh * hd), F32)] * 2
    nkeys = past_len + tm
    outs = pl.pallas_call(
        functools.partial(_attn_kernel, seq_len=seq_len, q_block=q_block, past_len=past_len,
                          rope=rope, emit_kv=emit_kv, group=group),
        grid=(t // tm, kvh),
        in_specs=in_specs,
        out_specs=out_specs,
        out_shape=out_shape,
        scratch_shapes=[pltpu.VMEM((tm, d), BF16), pltpu.VMEM((tm, d), F32),
                        pltpu.VMEM((tm, gw), BF16), pltpu.VMEM((nkeys, hd), BF16),
                        pltpu.VMEM((nkeys, hd), BF16), pltpu.VMEM((tm, gw), BF16)],
        compiler_params=_params("arbitrary", "arbitrary"),
        name="gqa",
    )(*args)
    return outs


def kernel(x_prompt, x_sample, c, state_lru, cache_k, cache_v, c_ctx, mod_w, mod_b, norm_g,
           ffn_w_gu, ffn_w_down, lru_w_in, lru_conv_w, lru_conv_b, lru_gate_w, lru_gate_b,
           lru_lambda, lru_w_out, att_w_qkv, att_q_g, att_k_g, att_w_o, final_g):
    b, s, d = x_prompt.shape
    db, ds, _ = x_sample.shape
    depth = mod_w.shape[0]
    n_mixers = 2
    assert ds % TOKEN_TILE == 0 and 1 + db <= SUBLANES

    xp = x_prompt.reshape(b * s, d)
    xs = x_sample.reshape(db * ds, d)
    cvecs = jnp.concatenate([c_ctx[None], c, jnp.zeros((SUBLANES - 1 - db, d), F32)], axis=0)
    m = _modulation(cvecs, mod_w, mod_b).reshape(depth, SUBLANES, 3 * N_SUB, d)

    tiles_per_sample = ds // TOKEN_TILE
    prompt_row = lambda i: 0
    sample_row = lambda i: 1 + i // tiles_per_sample

    new_states, new_k, new_v = [], [], []
    for layer in range(depth):
        j = layer // n_mixers
        last = layer == depth - 1
        wgu_l, wdn_l = ffn_w_gu[layer:layer + 1], ffn_w_down[layer:layer + 1]
        xp = _ffn(xp, m, prompt_row, norm_g, wgu_l, wdn_l, layer, 0, 0, w_layer=0)
        xs = _ffn(xs, m, sample_row, norm_g, wgu_l, wdn_l, layer, 0, 0, w_layer=0)
        if layer % n_mixers == 0:
            lru_p = (lru_w_in[j], lru_conv_w[j], lru_conv_b[j], lru_gate_w[j], lru_gate_b[j],
                     lru_lambda[j], lru_w_out[j])
            xp, st = _lru(xp, m, prompt_row, norm_g, layer, *lru_p, seq_len=s, h0=None,
                          emit_state=True)
            new_states.append(st)
            xs, _ = _lru(xs, m, sample_row, norm_g, layer, *lru_p, seq_len=ds,
                         h0=state_lru[:, j], emit_state=False)
        else:
            att_p = (att_w_qkv[j], att_q_g[j], att_k_g[j], att_w_o[j])
            xp, kp, vp = _attn(xp, m, prompt_row, norm_g, layer, *att_p, seq_len=s, q_block=s,
                               emit_kv=True)
            new_k.append(kp.reshape(b, s, N_KV_HEADS, HEAD_DIM))
            new_v.append(vp.reshape(b, s, N_KV_HEADS, HEAD_DIM))
            past = cache_k.shape[2]
            ck = cache_k[:, j].reshape(db, past, N_KV_HEADS * HEAD_DIM)
            cv = cache_v[:, j].reshape(db, past, N_KV_HEADS * HEAD_DIM)
            (xs,) = _attn(xs, m, sample_row, norm_g, layer, *att_p, seq_len=ds, q_block=128,
                          cache_k=ck, cache_v=cv, rope=True)
        fg = final_g if last else None
        xp = _ffn(xp, m, prompt_row, norm_g, wgu_l, wdn_l, layer, 1, 2, final_g=fg, w_layer=0)
        xs = _ffn(xs, m, sample_row, norm_g, wgu_l, wdn_l, layer, 1, 2, final_g=fg, w_layer=0)

    y_prompt = xp.reshape(b, s, d)
    y_sample = xs.reshape(db, ds, d)
    return (y_prompt, y_sample, jnp.stack(new_states, axis=1), jnp.stack(new_k, axis=1),
            jnp.stack(new_v, axis=1))
```

```python
import functools

import jax
import jax.numpy as jnp
import numpy as np
from jax import lax
from jax.experimental import pallas as pl
from jax.experimental.pallas import tpu as pltpu

F32 = jnp.float32
BF16 = jnp.bfloat16

EPS = 1e-6
LRU_C = 8.0
GRID_W = 64
ROPE_THETA = 10000.0
N_SUB = 3
HEAD_DIM = 128
N_KV_HEADS = 2
LRU_BLOCKS = 16

V7X_VMEM_LIMIT_BYTES = 56 * 1024 * 1024
SUBLANES = 8
TOKEN_TILE = 1024
FF_CHUNK = 256
FFN_STAGE_ROWS = 256
FFN_STAGE_COLS = 1536
FFN_STAGE_SLOTS = 4
LRU_CHUNK = 256
MOD_CHUNK = 1536


def _params(*semantics):
    return pltpu.CompilerParams(dimension_semantics=semantics,
                                vmem_limit_bytes=V7X_VMEM_LIMIT_BYTES)


def _mm(a_bf16, w_f32):
    return jnp.dot(a_bf16, w_f32.astype(BF16), preferred_element_type=F32)


def _rms(x):
    return x * lax.rsqrt(jnp.mean(x * x, axis=-1, keepdims=True) + EPS)


def _sub_in(x, m_ref, g_ref, sidx):
    shift = m_ref[3 * sidx:3 * sidx + 1, :]
    scale = m_ref[3 * sidx + 1:3 * sidx + 2, :]
    return (_rms(x) * g_ref[sidx:sidx + 1, :]) * (1.0 + scale) + shift


def _gate(m_ref, sidx):
    return m_ref[3 * sidx + 2:3 * sidx + 3, :]


def _mod_kernel(cv_ref, w_ref, b_ref, o_ref):
    cv = cv_ref[...]
    act = (cv * jax.nn.sigmoid(cv)).astype(BF16)
    o_ref[...] = _mm(act, w_ref[...]) + b_ref[...]


def _modulation(cvecs, mod_w, mod_b):
    depth, d, n = mod_w.shape
    rows = cvecs.shape[0]
    tn = MOD_CHUNK
    assert n % tn == 0
    return pl.pallas_call(
        _mod_kernel,
        grid=(depth, n // tn),
        in_specs=[
            pl.BlockSpec((rows, d), lambda l, j: (0, 0)),
            pl.BlockSpec((None, d, tn), lambda l, j: (l, 0, j)),
            pl.BlockSpec((None, 1, tn), lambda l, j: (l, 0, j)),
        ],
        out_specs=pl.BlockSpec((None, rows, tn), lambda l, j: (l, 0, j)),
        out_shape=jax.ShapeDtypeStruct((depth, rows, n), F32),
        compiler_params=_params("arbitrary", "arbitrary"),
        name="modulation",
    )(cvecs, mod_w, mod_b.reshape(depth, 1, n))


def _interleave(acts, units):
    out = []
    for j, u in enumerate(units):
        out += [a for i, a in enumerate(acts) if (i * len(units)) // len(acts) == j]
        out.append(u)
    return out


def _ffn_stream_plan(d, f, tf):
    assert FFN_STAGE_COLS % tf == 0 and d % FFN_STAGE_ROWS == 0 and f % FFN_STAGE_ROWS == 0
    units, steps, ready_acts = [], [], []
    for c0 in range(0, f, FFN_STAGE_COLS):
        cw = min(FFN_STAGE_COLS, f - c0)
        block = []
        for part in range(2):
            for r0 in range(0, d, FFN_STAGE_ROWS):
                block.append(("unit", len(units)))
                units.append(("gu", part, r0, c0, cw))
        steps += _interleave(ready_acts, block) if ready_acts else block
        ready_acts = [("act", k) for k in range(c0 // tf, (c0 + cw) // tf)]
    block = []
    for r0 in range(0, f, FFN_STAGE_ROWS):
        block.append(("unit", len(units)))
        units.append(("wd", r0))
    steps += _interleave(ready_acts, block)
    return units, steps


def _ffn_kernel(*refs, layer, s, sidx, final, n_first):
    if final:
        (m_ref, g_ref, fg_ref, xa_hbm, xb_hbm, wgu_hbm, wd_hbm, oa_hbm, ob_hbm,
         wg_res, wu_res, wd_res, h_ref, act_ref, gu_buf, wd_buf, xbuf, obuf,
         gu_sem, wd_sem, xsem, osem) = refs
    else:
        (m_ref, g_ref, xa_hbm, xb_hbm, wgu_hbm, wd_hbm, oa_hbm, ob_hbm,
         wg_res, wu_res, wd_res, h_ref, act_ref, gu_buf, wd_buf, xbuf, obuf,
         gu_sem, wd_sem, xsem, osem) = refs
    i = pl.program_id(0)
    n = pl.num_programs(0)
    tm = xbuf.shape[1]
    d, f = wg_res.shape
    tf = FF_CHUNK
    nk = f // tf
    nslot = gu_buf.shape[0]
    ahead = nslot - 1
    units, steps = _ffn_stream_plan(d, f, tf)
    n_gu = sum(1 for u in units if u[0] == "gu")
    slot = lax.rem(i, 2)
    other = 1 - slot

    def unit_copy(q):
        u = units[q]
        if u[0] == "gu":
            _, part, r0, c0, cw = u
            us = q % nslot
            src = wgu_hbm.at[layer, s, pl.ds(r0, FFN_STAGE_ROWS), pl.ds(part * f + c0, cw)]
            return pltpu.make_async_copy(src, gu_buf.at[us, :, pl.ds(0, cw)], gu_sem.at[us])
        us = (q - n_gu) % nslot
        src = wd_hbm.at[layer, s, pl.ds(u[1], FFN_STAGE_ROWS), :]
        return pltpu.make_async_copy(src, wd_buf.at[us], wd_sem.at[us])

    def unit_cast(q):
        u = units[q]
        if u[0] == "gu":
            _, part, r0, c0, cw = u
            res = wu_res if part else wg_res
            res[r0:r0 + FFN_STAGE_ROWS, c0:c0 + cw] = gu_buf[q % nslot, :, 0:cw].astype(BF16)
        else:
            wd_res[u[1]:u[1] + FFN_STAGE_ROWS, :] = wd_buf[(q - n_gu) % nslot].astype(BF16)

    def tile_rows(hbm, tile):
        return hbm.at[pl.ds(pl.multiple_of(tile * tm, tm), tm), :]

    def x_copy(which, tile, sl):
        return pltpu.make_async_copy(tile_rows((xa_hbm, xb_hbm)[which], tile), xbuf.at[sl],
                                     xsem.at[sl])

    def o_copy(which, tile, sl):
        return pltpu.make_async_copy(obuf.at[sl], tile_rows((oa_hbm, ob_hbm)[which], tile),
                                     osem.at[sl])

    def start_by_stream(make, tile, sl):
        @pl.when(tile < n_first)
        def _():
            make(0, tile, sl).start()

        @pl.when(tile >= n_first)
        def _():
            make(1, tile - n_first, sl).start()

    @pl.when(i == 0)
    def _():
        x_copy(0, 0, 0).start()
        for q in range(min(ahead, len(units))):
            unit_copy(q).start()

    @pl.when(i + 1 < n)
    def _():
        start_by_stream(x_copy, i + 1, other)

    x_copy(0, 0, slot).wait()

    @pl.when(i >= 2)
    def _():
        o_copy(0, 0, slot).wait()

    x_ref = xbuf.at[slot]
    o_ref = obuf.at[slot]

    def act_chunk(k):
        h = h_ref[...]
        gt = jnp.dot(h, wg_res[:, k * tf:(k + 1) * tf], preferred_element_type=F32)
        up = jnp.dot(h, wu_res[:, k * tf:(k + 1) * tf], preferred_element_type=F32)
        act_ref[:, k * tf:(k + 1) * tf] = ((gt * jax.nn.sigmoid(gt)) * up).astype(BF16)

    def finish():
        ff = jnp.dot(act_ref[...], wd_res[...], preferred_element_type=F32)
        y = x_ref[...] + (0.5 * _gate(m_ref, sidx)) * ff
        if final:
            y = _rms(y) * fg_ref[...]
        o_ref[...] = y

    @pl.when(i == 0)
    def _():
        h_ref[...] = _sub_in(x_ref[...], m_ref, g_ref, sidx).astype(BF16)
        for kind, idx in steps:
            if kind == "act":
                act_chunk(idx)
                continue
            unit_copy(idx).wait()
            if idx + ahead < len(units):
                unit_copy(idx + ahead).start()
            unit_cast(idx)
        finish()

    @pl.when(i > 0)
    def _():
        h_ref[...] = _sub_in(x_ref[...], m_ref, g_ref, sidx).astype(BF16)
        for k in range(nk):
            act_chunk(k)
        finish()

    start_by_stream(o_copy, i, slot)

    @pl.when(i == n - 1)
    def _():
        o_copy(0, 0, other).wait()
        o_copy(0, 0, slot).wait()


def _ffn(xa, xb, m, row_of_tile, norm_g, w_gu, w_down, layer, s, sidx, final_g=None):
    (ta, d), tb = xa.shape, xb.shape[0]
    f = w_down.shape[2]
    tm, tf = TOKEN_TILE, FF_CHUNK
    assert ta % tm == 0 and tb % tm == 0 and f % tf == 0 and (ta + tb) // tm >= 2
    final = final_g is not None
    in_specs = [
        pl.BlockSpec((None, None, 3 * N_SUB, d), lambda i: (layer, row_of_tile(i), 0, 0)),
        pl.BlockSpec((None, N_SUB, d), lambda i: (layer, 0, 0)),
    ]
    args = [m, norm_g]
    if final:
        in_specs.append(pl.BlockSpec((1, d), lambda i: (0, 0)))
        args.append(final_g.reshape(1, d))
    in_specs += [pl.BlockSpec(memory_space=pl.ANY)] * 4
    args += [xa, xb, w_gu, w_down]
    return pl.pallas_call(
        functools.partial(_ffn_kernel, layer=layer, s=s, sidx=sidx, final=final,
                          n_first=ta // tm),
        grid=((ta + tb) // tm,),
        in_specs=in_specs,
        out_specs=[pl.BlockSpec(memory_space=pl.ANY)] * 2,
        out_shape=[jax.ShapeDtypeStruct((ta, d), F32), jax.ShapeDtypeStruct((tb, d), F32)],
        scratch_shapes=[
            pltpu.VMEM((d, f), BF16), pltpu.VMEM((d, f), BF16), pltpu.VMEM((f, d), BF16),
            pltpu.VMEM((tm, d), BF16), pltpu.VMEM((tm, f), BF16),
            pltpu.VMEM((FFN_STAGE_SLOTS, FFN_STAGE_ROWS, FFN_STAGE_COLS), F32),
            pltpu.VMEM((FFN_STAGE_SLOTS, FFN_STAGE_ROWS, d), F32),
            pltpu.VMEM((2, tm, d), F32), pltpu.VMEM((2, tm, d), F32),
            pltpu.SemaphoreType.DMA((FFN_STAGE_SLOTS,)),
            pltpu.SemaphoreType.DMA((FFN_STAGE_SLOTS,)),
            pltpu.SemaphoreType.DMA((2,)),
            pltpu.SemaphoreType.DMA((2,)),
        ],
        compiler_params=_params("arbitrary"),
        name="ffn",
    )(*args)


def _gelu_tanh(x):
    c = np.float32(np.sqrt(2.0 / np.pi))
    return x * (0.5 * (1.0 + jnp.tanh(c * (x + 0.044715 * (x * x * x)))))


def _group_scan(a, u, row8, reverse):
    rows = a.shape[0]
    for s in (1, 2, 4):
        if reverse:
            shift, valid = rows - s, row8 < SUBLANES - s
        else:
            shift, valid = s, row8 >= s
        u = jnp.where(valid, u + a * pltpu.roll(u, shift, 0), u)
        a = jnp.where(valid, a * pltpu.roll(a, shift, 0), a)
    return a, u


def _lru_kernel(*refs, seq_len, has_h0, emit_state, conv_left):
    refs = list(refs)
    (x_ref, m_ref, g_ref, wx_ref, wy_ref, cw_ref, cb_ref, gw_ref, gb_ref, lam_ref,
     wo_ref) = refs[:11]
    pos = 11
    h0_ref = None
    if has_h0:
        h0_ref = refs[pos]
        pos += 1
    o_ref = refs[pos]
    pos += 1
    st_ref = None
    if emit_state:
        st_ref = refs[pos]
        pos += 1
    h_ref, acc_ref, af_ref, uf_ref, ab_ref, ub_ref = refs[pos:]

    c = pl.program_id(1)
    tm, cw_cols = af_ref.shape
    nseq = tm // seq_len
    ngroups = seq_len // SUBLANES

    @pl.when(c == 0)
    def _():
        h_ref[...] = _sub_in(x_ref[...], m_ref, g_ref, 1).astype(BF16)
        acc_ref[...] = jnp.zeros_like(acc_ref)

    h = h_ref[...]
    xb = _mm(h, wx_ref[...])
    yb = _mm(h, wy_ref[...])

    row = lax.broadcasted_iota(jnp.int32, (tm, cw_cols), 0)
    t = row & (seq_len - 1)
    cw = cw_ref[...]
    xc = cb_ref[...]
    for k in range(cw.shape[0]):
        off = k - conv_left
        if off == 0:
            tap = xb
        else:
            valid = (t + off >= 0) & (t + off <= seq_len - 1)
            tap = jnp.where(valid, pltpu.roll(xb, (-off) % tm, 0), 0.0)
        xc = xc + tap * cw[k:k + 1, :]

    gl = jnp.dot(xc.astype(BF16), gw_ref[...], preferred_element_type=F32)
    gb = gb_ref[...]
    lam = lam_ref[...]
    row8 = row & (SUBLANES - 1)
    for d, (a_ref, u_ref) in enumerate(((af_ref, uf_ref), (ab_ref, ub_ref))):
        r = jax.nn.sigmoid(gl[:, (2 * d) * cw_cols:(2 * d + 1) * cw_cols] + gb[2 * d:2 * d + 1, :])
        ig = jax.nn.sigmoid(gl[:, (2 * d + 1) * cw_cols:(2 * d + 2) * cw_cols]
                            + gb[2 * d + 1:2 * d + 2, :])
        lm = lam[d:d + 1, :]
        log_sig = jnp.minimum(lm, 0.0) - jnp.log1p(jnp.exp(-jnp.abs(lm)))
        a = jnp.exp((LRU_C * r) * log_sig)
        u = jnp.sqrt(1.0 - a * a) * (ig * xc)
        a, u = _group_scan(a, u, row8, reverse=(d == 1))
        a_ref[...] = a
        u_ref[...] = u

    if has_h0:
        init = tuple(h0_ref[s, dd:dd + 1, :] for s in range(nseq) for dd in range(2))
    else:
        init = tuple(jnp.zeros((1, cw_cols), F32) for _ in range(2 * nseq))

    def body(j, carry):
        out = []
        for s in range(nseq):
            rf = pl.multiple_of(s * seq_len + j * SUBLANES, SUBLANES)
            rb = pl.multiple_of(s * seq_len + (ngroups - 1 - j) * SUBLANES, SUBLANES)
            hf = uf_ref[pl.ds(rf, SUBLANES), :] + af_ref[pl.ds(rf, SUBLANES), :] * carry[2 * s]
            uf_ref[pl.ds(rf, SUBLANES), :] = hf
            hb = ub_ref[pl.ds(rb, SUBLANES), :] + ab_ref[pl.ds(rb, SUBLANES), :] * carry[2 * s + 1]
            ub_ref[pl.ds(rb, SUBLANES), :] = hb
            out += [hf[SUBLANES - 1:SUBLANES, :], hb[0:1, :]]
        return tuple(out)

    last = lax.fori_loop(0, ngroups, body, init)
    if emit_state:
        for s in range(nseq):
            st_ref[s, 0:1, :] = last[2 * s]
            st_ref[s, 1:2, :] = last[2 * s + 1]

    y = (uf_ref[...] + ub_ref[...]) * _gelu_tanh(yb)
    acc_ref[...] += _mm(y.astype(BF16), wo_ref[...])

    @pl.when(c == pl.num_programs(1) - 1)
    def _():
        o_ref[...] = x_ref[...] + _gate(m_ref, 1) * acc_ref[...]


def _lru_gate_weights(gate_w):
    nd, ng, nb, bw, _ = gate_w.shape
    per = LRU_CHUNK // bw
    nc = nb // per
    w = gate_w.reshape(nd * ng, nc, per, bw, bw)
    eye = jnp.eye(per, dtype=gate_w.dtype)
    w = w[:, :, :, :, None, :] * eye[None, None, :, None, :, None]
    w = jnp.transpose(w, (1, 2, 3, 0, 4, 5))
    return w.reshape(nc, per * bw, nd * ng * per * bw).astype(BF16)


def _lru(x, m, row_of_tile, norm_g, layer, w_in, conv_w, conv_b, gate_w, gate_b, lam, w_out,
         seq_len, h0, emit_state):
    t, d = x.shape
    r = w_out.shape[0]
    tm, cb = TOKEN_TILE, LRU_CHUNK
    assert t % tm == 0 and tm % seq_len == 0 and r % cb == 0 and seq_len % SUBLANES == 0
    assert seq_len & (seq_len - 1) == 0, "in-sequence position is taken with a bit mask"
    nc = r // cb
    nseq = tm // seq_len
    has_h0 = h0 is not None
    in_specs = [
        pl.BlockSpec((tm, d), lambda i, c: (i, 0)),
        pl.BlockSpec((None, None, 3 * N_SUB, d), lambda i, c: (layer, row_of_tile(i), 0, 0)),
        pl.BlockSpec((None, N_SUB, d), lambda i, c: (layer, 0, 0)),
        pl.BlockSpec((d, cb), lambda i, c: (0, c)),
        pl.BlockSpec((d, cb), lambda i, c: (0, nc + c)),
        pl.BlockSpec((conv_w.shape[0], cb), lambda i, c: (0, c)),
        pl.BlockSpec((1, cb), lambda i, c: (0, c)),
        pl.BlockSpec((None, cb, 4 * cb), lambda i, c: (c, 0, 0)),
        pl.BlockSpec((4, cb), lambda i, c: (0, c)),
        pl.BlockSpec((2, cb), lambda i, c: (0, c)),
        pl.BlockSpec((cb, d), lambda i, c: (c, 0)),
    ]
    args = [x, m, norm_g, w_in, w_in, conv_w, conv_b.reshape(1, r), _lru_gate_weights(gate_w),
            gate_b.reshape(4, r), lam, w_out]
    if has_h0:
        in_specs.append(pl.BlockSpec((nseq, 2, cb), lambda i, c: (i, 0, c)))
        args.append(h0)
    out_specs = [pl.BlockSpec((tm, d), lambda i, c: (i, 0))]
    out_shape = [jax.ShapeDtypeStruct((t, d), F32)]
    if emit_state:
        out_specs.append(pl.BlockSpec((nseq, 2, cb), lambda i, c: (i, 0, c)))
        out_shape.append(jax.ShapeDtypeStruct((t // seq_len, 2, r), F32))
    outs = pl.pallas_call(
        functools.partial(_lru_kernel, seq_len=seq_len, has_h0=has_h0, emit_state=emit_state,
                          conv_left=(conv_w.shape[0] - 1) // 2),
        grid=(t // tm, nc),
        in_specs=in_specs,
        out_specs=out_specs,
        out_shape=out_shape,
        scratch_shapes=[pltpu.VMEM((tm, d), BF16), pltpu.VMEM((tm, d), F32)]
        + [pltpu.VMEM((tm, cb), F32)] * 4,
        compiler_params=_params("arbitrary", "arbitrary"),
        name="rglru",
    )(*args)
    return outs if emit_state else (outs[0], None)


def _rope(x, cos, sin_signed, lane):
    hd = x.shape[1]
    partner = jnp.where((lane & 32) == 0, pltpu.roll(x, hd - 32, 1), pltpu.roll(x, 32, 1))
    return x * cos + partner * sin_signed


def _attn_kernel(*refs, seq_len, q_block, past_len, rope, emit_kv, group):
    refs = list(refs)
    x_ref, m_ref, g_ref, wq_ref, wk_ref, wv_ref, qg_ref, kg_ref, wo_ref = refs[:9]
    pos = 9
    if rope:
        cos_ref, sin_ref = refs[pos:pos + 2]
        pos += 2
    if past_len:
        ck_ref, cv_ref = refs[pos:pos + 2]
        pos += 2
    o_ref = refs[pos]
    pos += 1
    if emit_kv:
        kn_ref, vn_ref = refs[pos:pos + 2]
        pos += 2
    h_ref, acc_ref, q_s, k_s, v_s, o_s = refs[pos:]

    gi = pl.program_id(1)
    tm = x_ref.shape[0]
    hd = k_s.shape[1]
    nqb = seq_len // q_block
    nchunks = (tm // seq_len) * nqb
    nk = past_len + seq_len
    scale = hd ** -0.5

    @pl.when(gi == 0)
    def _():
        h_ref[...] = _sub_in(x_ref[...], m_ref, g_ref, 1).astype(BF16)
        acc_ref[...] = jnp.zeros_like(acc_ref)

    h = h_ref[...]
    q = _mm(h, wq_ref[...])
    k = _rms(_mm(h, wk_ref[...])) * kg_ref[...]
    v = _mm(h, wv_ref[...])
    if emit_kv:
        kn_ref[...] = k
        vn_ref[...] = v
    if rope:
        lane = lax.broadcasted_iota(jnp.int32, (tm, hd), 1)
        cos, sin = cos_ref[...], sin_ref[...]
        k = _rope(k, cos, sin, lane)
    k_s[past_len:past_len + tm, :] = k.astype(BF16)
    v_s[past_len:past_len + tm, :] = v.astype(BF16)
    if past_len:
        k_s[0:past_len, :] = ck_ref[...].astype(BF16)
        v_s[0:past_len, :] = cv_ref[...].astype(BF16)
    for j in range(group):
        qh = _rms(q[:, j * hd:(j + 1) * hd]) * qg_ref[...]
        if rope:
            qh = _rope(qh, cos, sin, lane)
        q_s[:, j * hd:(j + 1) * hd] = qh.astype(BF16)

    def chunk(ci, carry):
        r0 = pl.multiple_of(ci * q_block, q_block)
        if past_len:
            keys, vals = k_s[...], v_s[...]
        else:
            koff = pl.multiple_of((ci // nqb) * seq_len, seq_len)
            keys, vals = k_s[pl.ds(koff, nk), :], v_s[pl.ds(koff, nk), :]
        qc = jnp.concatenate([q_s[pl.ds(r0, q_block), j * hd:(j + 1) * hd] for j in range(group)],
                             axis=0)
        s = lax.dot_general(qc, keys, (((1,), (1,)), ((), ())), preferred_element_type=F32) * scale
        e = jnp.exp(s - jnp.max(s, axis=-1, keepdims=True))
        p = e * (1.0 / jnp.sum(e, axis=-1, keepdims=True))
        oc = jnp.dot(p.astype(BF16), vals, preferred_element_type=F32)
        for j in range(group):
            o_s[pl.ds(r0, q_block), j * hd:(j + 1) * hd] = oc[j * q_block:(j + 1) * q_block, :].astype(BF16)
        return carry

    lax.fori_loop(0, nchunks, chunk, 0)
    acc_ref[...] += _mm(o_s[...], wo_ref[...])

    @pl.when(gi == pl.num_programs(1) - 1)
    def _():
        o_ref[...] = x_ref[...] + _gate(m_ref, 1) * acc_ref[...]


def _rope_tables(n_tok, hd):
    rows = n_tok // GRID_W
    r_idx = jnp.broadcast_to(jnp.arange(rows)[:, None], (rows, GRID_W)).reshape(n_tok).astype(F32)
    c_idx = jnp.broadcast_to(jnp.arange(GRID_W)[None, :], (rows, GRID_W)).reshape(n_tok).astype(F32)
    n_freq = hd // 4
    inv = ROPE_THETA ** (-jnp.arange(n_freq, dtype=F32) / n_freq)
    ang = jnp.stack([r_idx[:, None] * inv, c_idx[:, None] * inv], axis=1)
    cos, sin = jnp.cos(ang), jnp.sin(ang)
    cos_full = jnp.concatenate([cos, cos], axis=-1).reshape(n_tok, hd)
    sin_signed = jnp.concatenate([-sin, sin], axis=-1).reshape(n_tok, hd)
    return cos_full, sin_signed


def _attn(x, m, row_of_tile, norm_g, layer, w_qkv, q_g, k_g, w_o, seq_len, q_block,
          cache_k=None, cache_v=None, rope=False, emit_kv=False):
    t, d = x.shape
    hd, kvh = HEAD_DIM, N_KV_HEADS
    n_heads = w_o.shape[0] // hd
    group = n_heads // kvh
    gw = group * hd
    tm = TOKEN_TILE
    assert t % tm == 0 and tm % seq_len == 0 and seq_len % q_block == 0
    past_len = 0 if cache_k is None else cache_k.shape[1]
    assert past_len == 0 or tm == seq_len
    in_specs = [
        pl.BlockSpec((tm, d), lambda i, g: (i, 0)),
        pl.BlockSpec((None, None, 3 * N_SUB, d), lambda i, g: (layer, row_of_tile(i), 0, 0)),
        pl.BlockSpec((None, N_SUB, d), lambda i, g: (layer, 0, 0)),
        pl.BlockSpec((d, gw), lambda i, g: (0, g)),
        pl.BlockSpec((d, hd), lambda i, g: (0, n_heads + g)),
        pl.BlockSpec((d, hd), lambda i, g: (0, n_heads + kvh + g)),
        pl.BlockSpec((1, hd), lambda i, g: (0, 0)),
        pl.BlockSpec((1, hd), lambda i, g: (0, 0)),
        pl.BlockSpec((gw, d), lambda i, g: (g, 0)),
    ]
    args = [x, m, norm_g, w_qkv, w_qkv, w_qkv, q_g.reshape(1, hd), k_g.reshape(1, hd), w_o]
    if rope:
        assert tm == seq_len
        cos, sin = _rope_tables(seq_len, hd)
        in_specs += [pl.BlockSpec((tm, hd), lambda i, g: (0, 0))] * 2
        args += [cos, sin]
    if past_len:
        in_specs += [pl.BlockSpec((None, past_len, hd), lambda i, g: (i, 0, g))] * 2
        args += [cache_k, cache_v]
    out_specs = [pl.BlockSpec((tm, d), lambda i, g: (i, 0))]
    out_shape = [jax.ShapeDtypeStruct((t, d), F32)]
    if emit_kv:
        out_specs += [pl.BlockSpec((tm, hd), lambda i, g: (i, g))] * 2
        out_shape += [jax.ShapeDtypeStruct((t, kvh * hd), F32)] * 2
    nkeys = past_len + tm
    outs = pl.pallas_call(
        functools.partial(_attn_kernel, seq_len=seq_len, q_block=q_block, past_len=past_len,
                          rope=rope, emit_kv=emit_kv, group=group),
        grid=(t // tm, kvh),
        in_specs=in_specs,
        out_specs=out_specs,
        out_shape=out_shape,
        scratch_shapes=[pltpu.VMEM((tm, d), BF16), pltpu.VMEM((tm, d), F32),
                        pltpu.VMEM((tm, gw), BF16), pltpu.VMEM((nkeys, hd), BF16),
                        pltpu.VMEM((nkeys, hd), BF16), pltpu.VMEM((tm, gw), BF16)],
        compiler_params=_params("arbitrary", "arbitrary"),
        name="gqa",
    )(*args)
    return outs


def kernel(x_prompt, x_sample, c, state_lru, cache_k, cache_v, c_ctx, mod_w, mod_b, norm_g,
           ffn_w_gu, ffn_w_down, lru_w_in, lru_conv_w, lru_conv_b, lru_gate_w, lru_gate_b,
           lru_lambda, lru_w_out, att_w_qkv, att_q_g, att_k_g, att_w_o, final_g):
    b, s, d = x_prompt.shape
    db, ds, _ = x_sample.shape
    depth = mod_w.shape[0]
    n_mixers = 2
    assert ds % TOKEN_TILE == 0 and 1 + db <= SUBLANES

    xp = x_prompt.reshape(b * s, d)
    xs = x_sample.reshape(db * ds, d)
    cvecs = jnp.concatenate([c_ctx[None], c, jnp.zeros((SUBLANES - 1 - db, d), F32)], axis=0)
    m = _modulation(cvecs, mod_w, mod_b).reshape(depth, SUBLANES, 3 * N_SUB, d)

    tiles_per_sample = ds // TOKEN_TILE
    prompt_row = lambda i: 0
    sample_row = lambda i: 1 + i // tiles_per_sample
    prompt_tiles = (b * s) // TOKEN_TILE
    both_row = lambda i: jnp.where(i < prompt_tiles, 0, 1 + (i - prompt_tiles) // tiles_per_sample)

    new_states, new_k, new_v = [], [], []
    for layer in range(depth):
        j = layer // n_mixers
        last = layer == depth - 1
        xp, xs = _ffn(xp, xs, m, both_row, norm_g, ffn_w_gu, ffn_w_down, layer, 0, 0)
        if layer % n_mixers == 0:
            lru_p = (lru_w_in[j], lru_conv_w[j], lru_conv_b[j], lru_gate_w[j], lru_gate_b[j],
                     lru_lambda[j], lru_w_out[j])
            xp, st = _lru(xp, m, prompt_row, norm_g, layer, *lru_p, seq_len=s, h0=None,
                          emit_state=True)
            new_states.append(st)
            xs, _ = _lru(xs, m, sample_row, norm_g, layer, *lru_p, seq_len=ds,
                         h0=state_lru[:, j], emit_state=False)
        else:
            att_p = (att_w_qkv[j], att_q_g[j], att_k_g[j], att_w_o[j])
            xp, kp, vp = _attn(xp, m, prompt_row, norm_g, layer, *att_p, seq_len=s, q_block=s,
                               emit_kv=True)
            new_k.append(kp.reshape(b, s, N_KV_HEADS, HEAD_DIM))
            new_v.append(vp.reshape(b, s, N_KV_HEADS, HEAD_DIM))
            past = cache_k.shape[2]
            ck = cache_k[:, j].reshape(db, past, N_KV_HEADS * HEAD_DIM)
            cv = cache_v[:, j].reshape(db, past, N_KV_HEADS * HEAD_DIM)
            (xs,) = _attn(xs, m, sample_row, norm_g, layer, *att_p, seq_len=ds, q_block=128,
                          cache_k=ck, cache_v=cv, rope=True)
        fg = final_g if last else None
        xp, xs = _ffn(xp, xs, m, both_row, norm_g, ffn_w_gu, ffn_w_down, layer, 1, 2, final_g=fg)

    y_prompt = xp.reshape(b, s, d)
    y_sample = xs.reshape(db, ds, d)
    return (y_prompt, y_sample, jnp.stack(new_states, axis=1), jnp.stack(new_k, axis=1),
            jnp.stack(new_v, axis=1))
```

```python
import functools

import jax
import jax.numpy as jnp
import numpy as np
from jax import lax
from jax.experimental import pallas as pl
from jax.experimental.pallas import tpu as pltpu

F32 = jnp.float32
BF16 = jnp.bfloat16

EPS = 1e-6
LRU_C = 8.0
GRID_W = 64
ROPE_THETA = 10000.0
N_SUB = 3
HEAD_DIM = 128
N_KV_HEADS = 2
LRU_BLOCKS = 16

V7X_VMEM_LIMIT_BYTES = 56 * 1024 * 1024
SUBLANES = 8
TOKEN_TILE = 1024
FF_CHUNK = 256
FFN_ROW_BLOCK = 256
FFN_STAGE_SLOTS = 4
LRU_CHUNK = 256
MOD_CHUNK = 1536


def _params(*semantics):
    return pltpu.CompilerParams(dimension_semantics=semantics,
                                vmem_limit_bytes=V7X_VMEM_LIMIT_BYTES)


def _mm(a_bf16, w_f32):
    return jnp.dot(a_bf16, w_f32.astype(BF16), preferred_element_type=F32)


def _rms(x):
    return x * lax.rsqrt(jnp.mean(x * x, axis=-1, keepdims=True) + EPS)


def _sub_in(x, m_ref, g_ref, sidx):
    shift = m_ref[3 * sidx:3 * sidx + 1, :]
    scale = m_ref[3 * sidx + 1:3 * sidx + 2, :]
    return (_rms(x) * g_ref[sidx:sidx + 1, :]) * (1.0 + scale) + shift


def _gate(m_ref, sidx):
    return m_ref[3 * sidx + 2:3 * sidx + 3, :]


def _mod_kernel(cv_ref, w_ref, b_ref, o_ref):
    cv = cv_ref[...]
    act = (cv * jax.nn.sigmoid(cv)).astype(BF16)
    o_ref[...] = _mm(act, w_ref[...]) + b_ref[...]


def _modulation(cvecs, mod_w, mod_b):
    depth, d, n = mod_w.shape
    rows = cvecs.shape[0]
    tn = MOD_CHUNK
    assert n % tn == 0
    return pl.pallas_call(
        _mod_kernel,
        grid=(depth, n // tn),
        in_specs=[
            pl.BlockSpec((rows, d), lambda l, j: (0, 0)),
            pl.BlockSpec((None, d, tn), lambda l, j: (l, 0, j)),
            pl.BlockSpec((None, 1, tn), lambda l, j: (l, 0, j)),
        ],
        out_specs=pl.BlockSpec((None, rows, tn), lambda l, j: (l, 0, j)),
        out_shape=jax.ShapeDtypeStruct((depth, rows, n), F32),
        compiler_params=_params("arbitrary", "arbitrary"),
        name="modulation",
    )(cvecs, mod_w, mod_b.reshape(depth, 1, n))


def _ffn_kernel(*refs, layer, s, sidx, final, n_first):
    if final:
        (m_ref, g_ref, fg_ref, xa_hbm, xb_hbm, wgu_hbm, wd_hbm, oa_hbm, ob_hbm,
         wgu_res, wd_res, act_ref, gu_buf, wd_buf, xbuf, obuf,
         gu_sem, wd_sem, xsem, osem) = refs
    else:
        (m_ref, g_ref, xa_hbm, xb_hbm, wgu_hbm, wd_hbm, oa_hbm, ob_hbm,
         wgu_res, wd_res, act_ref, gu_buf, wd_buf, xbuf, obuf,
         gu_sem, wd_sem, xsem, osem) = refs
    i = pl.program_id(0)
    n = pl.num_programs(0)
    tm = xbuf.shape[1]
    n_gu, d, tf = wgu_res.shape
    f = wd_res.shape[0]
    nk = f // tf
    rb = act_ref.shape[0]
    nslot = gu_buf.shape[0]
    ahead = nslot - 1
    slot = lax.rem(i, 2)
    other = 1 - slot

    def gu_copy(q, sl):
        src = wgu_hbm.at[layer, s, :, pl.ds(pl.multiple_of(q * tf, tf), tf)]
        return pltpu.make_async_copy(src, gu_buf.at[sl], gu_sem.at[sl])

    def wd_copy(q, sl):
        src = wd_hbm.at[layer, s, pl.ds(pl.multiple_of(q * tf, tf), tf), :]
        return pltpu.make_async_copy(src, wd_buf.at[sl], wd_sem.at[sl])

    def gu_cast(q, sl):
        wgu_res[q] = gu_buf[sl].astype(BF16)

    def wd_cast(q, sl):
        wd_res[pl.ds(pl.multiple_of(q * tf, tf), tf), :] = wd_buf[sl].astype(BF16)

    def stream(count, copy, cast):
        def body(q, carry):
            sl = lax.rem(q, nslot)
            copy(q, sl).wait()

            @pl.when(q + ahead < count)
            def _():
                copy(q + ahead, lax.rem(q + ahead, nslot)).start()

            cast(q, sl)
            return carry

        lax.fori_loop(0, count, body, 0)

    def tile_rows(hbm, tile):
        return hbm.at[pl.ds(pl.multiple_of(tile * tm, tm), tm), :]

    def x_copy(which, tile, sl):
        return pltpu.make_async_copy(tile_rows((xa_hbm, xb_hbm)[which], tile), xbuf.at[sl],
                                     xsem.at[sl])

    def o_copy(which, tile, sl):
        return pltpu.make_async_copy(obuf.at[sl], tile_rows((oa_hbm, ob_hbm)[which], tile),
                                     osem.at[sl])

    def start_by_stream(make, tile, sl):
        @pl.when(tile < n_first)
        def _():
            make(0, tile, sl).start()

        @pl.when(tile >= n_first)
        def _():
            make(1, tile - n_first, sl).start()

    @pl.when(i == 0)
    def _():
        x_copy(0, 0, 0).start()
        for q in range(ahead):
            gu_copy(q, q).start()
        for q in range(ahead):
            wd_copy(q, q).start()

    @pl.when(i + 1 < n)
    def _():
        start_by_stream(x_copy, i + 1, other)

    @pl.when(i == 0)
    def _():
        stream(n_gu, gu_copy, gu_cast)
        stream(nk, wd_copy, wd_cast)

    x_copy(0, 0, slot).wait()

    @pl.when(i >= 2)
    def _():
        o_copy(0, 0, slot).wait()

    def row_block(r, carry):
        rows = pl.ds(pl.multiple_of(r * rb, rb), rb)
        x = xbuf[slot, rows, :]
        h = _sub_in(x, m_ref, g_ref, sidx).astype(BF16)
        for k in range(nk):
            gt = jnp.dot(h, wgu_res[k], preferred_element_type=F32)
            up = jnp.dot(h, wgu_res[nk + k], preferred_element_type=F32)
            act_ref[:, k * tf:(k + 1) * tf] = ((gt * jax.nn.sigmoid(gt)) * up).astype(BF16)
        ff = jnp.dot(act_ref[...], wd_res[...], preferred_element_type=F32)
        y = x + (0.5 * _gate(m_ref, sidx)) * ff
        if final:
            y = _rms(y) * fg_ref[...]
        obuf[slot, rows, :] = y
        return carry

    lax.fori_loop(0, tm // rb, row_block, 0)

    start_by_stream(o_copy, i, slot)

    @pl.when(i == n - 1)
    def _():
        o_copy(0, 0, other).wait()
        o_copy(0, 0, slot).wait()


def _ffn(xa, xb, m, row_of_tile, norm_g, w_gu, w_down, layer, s, sidx, final_g=None):
    (ta, d), tb = xa.shape, xb.shape[0]
    f = w_down.shape[2]
    tm, tf, rb = TOKEN_TILE, FF_CHUNK, FFN_ROW_BLOCK
    assert ta % tm == 0 and tb % tm == 0 and f % tf == 0 and (ta + tb) // tm >= 2
    assert tm % rb == 0 and f // tf >= FFN_STAGE_SLOTS
    final = final_g is not None
    in_specs = [
        pl.BlockSpec((None, None, 3 * N_SUB, d), lambda i: (layer, row_of_tile(i), 0, 0)),
        pl.BlockSpec((None, N_SUB, d), lambda i: (layer, 0, 0)),
    ]
    args = [m, norm_g]
    if final:
        in_specs.append(pl.BlockSpec((1, d), lambda i: (0, 0)))
        args.append(final_g.reshape(1, d))
    in_specs += [pl.BlockSpec(memory_space=pl.ANY)] * 4
    args += [xa, xb, w_gu, w_down]
    return pl.pallas_call(
        functools.partial(_ffn_kernel, layer=layer, s=s, sidx=sidx, final=final,
                          n_first=ta // tm),
        grid=((ta + tb) // tm,),
        in_specs=in_specs,
        out_specs=[pl.BlockSpec(memory_space=pl.ANY)] * 2,
        out_shape=[jax.ShapeDtypeStruct((ta, d), F32), jax.ShapeDtypeStruct((tb, d), F32)],
        scratch_shapes=[
            pltpu.VMEM((2 * f // tf, d, tf), BF16), pltpu.VMEM((f, d), BF16),
            pltpu.VMEM((rb, f), BF16),
            pltpu.VMEM((FFN_STAGE_SLOTS, d, tf), F32),
            pltpu.VMEM((FFN_STAGE_SLOTS, tf, d), F32),
            pltpu.VMEM((2, tm, d), F32), pltpu.VMEM((2, tm, d), F32),
            pltpu.SemaphoreType.DMA((FFN_STAGE_SLOTS,)),
            pltpu.SemaphoreType.DMA((FFN_STAGE_SLOTS,)),
            pltpu.SemaphoreType.DMA((2,)),
            pltpu.SemaphoreType.DMA((2,)),
        ],
        compiler_params=_params("arbitrary"),
        name="ffn",
    )(*args)


def _gelu_tanh(x):
    c = np.float32(np.sqrt(2.0 / np.pi))
    return x * (0.5 * (1.0 + jnp.tanh(c * (x + 0.044715 * (x * x * x)))))


def _group_scan(a, u, row8, reverse):
    rows = a.shape[0]
    for s in (1, 2, 4):
        if reverse:
            shift, valid = rows - s, row8 < SUBLANES - s
        else:
            shift, valid = s, row8 >= s
        u = jnp.where(valid, u + a * pltpu.roll(u, shift, 0), u)
        a = jnp.where(valid, a * pltpu.roll(a, shift, 0), a)
    return a, u


def _lru_kernel(*refs, seq_len, has_h0, emit_state, conv_left):
    refs = list(refs)
    (x_ref, m_ref, g_ref, wx_ref, wy_ref, cw_ref, cb_ref, gw_ref, gb_ref, lam_ref,
     wo_ref) = refs[:11]
    pos = 11
    h0_ref = None
    if has_h0:
        h0_ref = refs[pos]
        pos += 1
    o_ref = refs[pos]
    pos += 1
    st_ref = None
    if emit_state:
        st_ref = refs[pos]
        pos += 1
    h_ref, acc_ref, af_ref, uf_ref, ab_ref, ub_ref = refs[pos:]

    c = pl.program_id(1)
    tm, cw_cols = af_ref.shape
    nseq = tm // seq_len
    ngroups = seq_len // SUBLANES

    @pl.when(c == 0)
    def _():
        h_ref[...] = _sub_in(x_ref[...], m_ref, g_ref, 1).astype(BF16)
        acc_ref[...] = jnp.zeros_like(acc_ref)

    h = h_ref[...]
    xb = _mm(h, wx_ref[...])
    yb = _mm(h, wy_ref[...])

    row = lax.broadcasted_iota(jnp.int32, (tm, cw_cols), 0)
    t = row & (seq_len - 1)
    cw = cw_ref[...]
    xc = cb_ref[...]
    for k in range(cw.shape[0]):
        off = k - conv_left
        if off == 0:
            tap = xb
        else:
            valid = (t + off >= 0) & (t + off <= seq_len - 1)
            tap = jnp.where(valid, pltpu.roll(xb, (-off) % tm, 0), 0.0)
        xc = xc + tap * cw[k:k + 1, :]

    gl = jnp.dot(xc.astype(BF16), gw_ref[...], preferred_element_type=F32)
    gb = gb_ref[...]
    lam = lam_ref[...]
    row8 = row & (SUBLANES - 1)
    for d, (a_ref, u_ref) in enumerate(((af_ref, uf_ref), (ab_ref, ub_ref))):
        r = jax.nn.sigmoid(gl[:, (2 * d) * cw_cols:(2 * d + 1) * cw_cols] + gb[2 * d:2 * d + 1, :])
        ig = jax.nn.sigmoid(gl[:, (2 * d + 1) * cw_cols:(2 * d + 2) * cw_cols]
                            + gb[2 * d + 1:2 * d + 2, :])
        lm = lam[d:d + 1, :]
        log_sig = jnp.minimum(lm, 0.0) - jnp.log1p(jnp.exp(-jnp.abs(lm)))
        a = jnp.exp((LRU_C * r) * log_sig)
        u = jnp.sqrt(1.0 - a * a) * (ig * xc)
        a, u = _group_scan(a, u, row8, reverse=(d == 1))
        a_ref[...] = a
        u_ref[...] = u

    if has_h0:
        init = tuple(h0_ref[s, dd:dd + 1, :] for s in range(nseq) for dd in range(2))
    else:
        init = tuple(jnp.zeros((1, cw_cols), F32) for _ in range(2 * nseq))

    def body(j, carry):
        out = []
        for s in range(nseq):
            rf = pl.multiple_of(s * seq_len + j * SUBLANES, SUBLANES)
            rb = pl.multiple_of(s * seq_len + (ngroups - 1 - j) * SUBLANES, SUBLANES)
            hf = uf_ref[pl.ds(rf, SUBLANES), :] + af_ref[pl.ds(rf, SUBLANES), :] * carry[2 * s]
            uf_ref[pl.ds(rf, SUBLANES), :] = hf
            hb = ub_ref[pl.ds(rb, SUBLANES), :] + ab_ref[pl.ds(rb, SUBLANES), :] * carry[2 * s + 1]
            ub_ref[pl.ds(rb, SUBLANES), :] = hb
            out += [hf[SUBLANES - 1:SUBLANES, :], hb[0:1, :]]
        return tuple(out)

    last = lax.fori_loop(0, ngroups, body, init)
    if emit_state:
        for s in range(nseq):
            st_ref[s, 0:1, :] = last[2 * s]
            st_ref[s, 1:2, :] = last[2 * s + 1]

    y = (uf_ref[...] + ub_ref[...]) * _gelu_tanh(yb)
    acc_ref[...] += _mm(y.astype(BF16), wo_ref[...])

    @pl.when(c == pl.num_programs(1) - 1)
    def _():
        o_ref[...] = x_ref[...] + _gate(m_ref, 1) * acc_ref[...]


def _lru_gate_weights(gate_w):
    nd, ng, nb, bw, _ = gate_w.shape
    per = LRU_CHUNK // bw
    nc = nb // per
    w = gate_w.reshape(nd * ng, nc, per, bw, bw)
    eye = jnp.eye(per, dtype=gate_w.dtype)
    w = w[:, :, :, :, None, :] * eye[None, None, :, None, :, None]
    w = jnp.transpose(w, (1, 2, 3, 0, 4, 5))
    return w.reshape(nc, per * bw, nd * ng * per * bw).astype(BF16)


def _lru(x, m, row_of_tile, norm_g, layer, w_in, conv_w, conv_b, gate_w, gate_b, lam, w_out,
         seq_len, h0, emit_state):
    t, d = x.shape
    r = w_out.shape[0]
    tm, cb = TOKEN_TILE, LRU_CHUNK
    assert t % tm == 0 and tm % seq_len == 0 and r % cb == 0 and seq_len % SUBLANES == 0
    assert seq_len & (seq_len - 1) == 0, "in-sequence position is taken with a bit mask"
    nc = r // cb
    nseq = tm // seq_len
    has_h0 = h0 is not None
    in_specs = [
        pl.BlockSpec((tm, d), lambda i, c: (i, 0)),
        pl.BlockSpec((None, None, 3 * N_SUB, d), lambda i, c: (layer, row_of_tile(i), 0, 0)),
        pl.BlockSpec((None, N_SUB, d), lambda i, c: (layer, 0, 0)),
        pl.BlockSpec((d, cb), lambda i, c: (0, c)),
        pl.BlockSpec((d, cb), lambda i, c: (0, nc + c)),
        pl.BlockSpec((conv_w.shape[0], cb), lambda i, c: (0, c)),
        pl.BlockSpec((1, cb), lambda i, c: (0, c)),
        pl.BlockSpec((None, cb, 4 * cb), lambda i, c: (c, 0, 0)),
        pl.BlockSpec((4, cb), lambda i, c: (0, c)),
        pl.BlockSpec((2, cb), lambda i, c: (0, c)),
        pl.BlockSpec((cb, d), lambda i, c: (c, 0)),
    ]
    args = [x, m, norm_g, w_in, w_in, conv_w, conv_b.reshape(1, r), _lru_gate_weights(gate_w),
            gate_b.reshape(4, r), lam, w_out]
    if has_h0:
        in_specs.append(pl.BlockSpec((nseq, 2, cb), lambda i, c: (i, 0, c)))
        args.append(h0)
    out_specs = [pl.BlockSpec((tm, d), lambda i, c: (i, 0))]
    out_shape = [jax.ShapeDtypeStruct((t, d), F32)]
    if emit_state:
        out_specs.append(pl.BlockSpec((nseq, 2, cb), lambda i, c: (i, 0, c)))
        out_shape.append(jax.ShapeDtypeStruct((t // seq_len, 2, r), F32))
    outs = pl.pallas_call(
        functools.partial(_lru_kernel, seq_len=seq_len, has_h0=has_h0, emit_state=emit_state,
                          conv_left=(conv_w.shape[0] - 1) // 2),
        grid=(t // tm, nc),
        in_specs=in_specs,
        out_specs=out_specs,
        out_shape=out_shape,
        scratch_shapes=[pltpu.VMEM((tm, d), BF16), pltpu.VMEM((tm, d), F32)]
        + [pltpu.VMEM((tm, cb), F32)] * 4,
        compiler_params=_params("arbitrary", "arbitrary"),
        name="rglru",
    )(*args)
    return outs if emit_state else (outs[0], None)


def _rope(x, cos, sin_signed, lane):
    hd = x.shape[1]
    partner = jnp.where((lane & 32) == 0, pltpu.roll(x, hd - 32, 1), pltpu.roll(x, 32, 1))
    return x * cos + partner * sin_signed


def _attn_kernel(*refs, seq_len, q_block, past_len, rope, emit_kv, group):
    refs = list(refs)
    x_ref, m_ref, g_ref, wq_ref, wk_ref, wv_ref, qg_ref, kg_ref, wo_ref = refs[:9]
    pos = 9
    if rope:
        cos_ref, sin_ref = refs[pos:pos + 2]
        pos += 2
    if past_len:
        ck_ref, cv_ref = refs[pos:pos + 2]
        pos += 2
    o_ref = refs[pos]
    pos += 1
    if emit_kv:
        kn_ref, vn_ref = refs[pos:pos + 2]
        pos += 2
    h_ref, acc_ref, q_s, k_s, v_s, o_s = refs[pos:]

    gi = pl.program_id(1)
    tm = x_ref.shape[0]
    hd = k_s.shape[1]
    nqb = seq_len // q_block
    nchunks = (tm // seq_len) * nqb
    nk = past_len + seq_len
    scale = hd ** -0.5

    @pl.when(gi == 0)
    def _():
        h_ref[...] = _sub_in(x_ref[...], m_ref, g_ref, 1).astype(BF16)
        acc_ref[...] = jnp.zeros_like(acc_ref)

    h = h_ref[...]
    q = _mm(h, wq_ref[...])
    k = _rms(_mm(h, wk_ref[...])) * kg_ref[...]
    v = _mm(h, wv_ref[...])
    if emit_kv:
        kn_ref[...] = k
        vn_ref[...] = v
    if rope:
        lane = lax.broadcasted_iota(jnp.int32, (tm, hd), 1)
        cos, sin = cos_ref[...], sin_ref[...]
        k = _rope(k, cos, sin, lane)
    k_s[past_len:past_len + tm, :] = k.astype(BF16)
    v_s[past_len:past_len + tm, :] = v.astype(BF16)
    if past_len:
        k_s[0:past_len, :] = ck_ref[...].astype(BF16)
        v_s[0:past_len, :] = cv_ref[...].astype(BF16)
    for j in range(group):
        qh = _rms(q[:, j * hd:(j + 1) * hd]) * qg_ref[...]
        if rope:
            qh = _rope(qh, cos, sin, lane)
        q_s[:, j * hd:(j + 1) * hd] = qh.astype(BF16)

    def chunk(ci, carry):
        r0 = pl.multiple_of(ci * q_block, q_block)
        if past_len:
            keys, vals = k_s[...], v_s[...]
        else:
            koff = pl.multiple_of((ci // nqb) * seq_len, seq_len)
            keys, vals = k_s[pl.ds(koff, nk), :], v_s[pl.ds(koff, nk), :]
        qc = jnp.concatenate([q_s[pl.ds(r0, q_block), j * hd:(j + 1) * hd] for j in range(group)],
                             axis=0)
        s = lax.dot_general(qc, keys, (((1,), (1,)), ((), ())), preferred_element_type=F32) * scale
        e = jnp.exp(s - jnp.max(s, axis=-1, keepdims=True))
        p = e * (1.0 / jnp.sum(e, axis=-1, keepdims=True))
        oc = jnp.dot(p.astype(BF16), vals, preferred_element_type=F32)
        for j in range(group):
            o_s[pl.ds(r0, q_block), j * hd:(j + 1) * hd] = oc[j * q_block:(j + 1) * q_block, :].astype(BF16)
        return carry

    lax.fori_loop(0, nchunks, chunk, 0)
    acc_ref[...] += _mm(o_s[...], wo_ref[...])

    @pl.when(gi == pl.num_programs(1) - 1)
    def _():
        o_ref[...] = x_ref[...] + _gate(m_ref, 1) * acc_ref[...]


def _rope_tables(n_tok, hd):
    rows = n_tok // GRID_W
    r_idx = jnp.broadcast_to(jnp.arange(rows)[:, None], (rows, GRID_W)).reshape(n_tok).astype(F32)
    c_idx = jnp.broadcast_to(jnp.arange(GRID_W)[None, :], (rows, GRID_W)).reshape(n_tok).astype(F32)
    n_freq = hd // 4
    inv = ROPE_THETA ** (-jnp.arange(n_freq, dtype=F32) / n_freq)
    ang = jnp.stack([r_idx[:, None] * inv, c_idx[:, None] * inv], axis=1)
    cos, sin = jnp.cos(ang), jnp.sin(ang)
    cos_full = jnp.concatenate([cos, cos], axis=-1).reshape(n_tok, hd)
    sin_signed = jnp.concatenate([-sin, sin], axis=-1).reshape(n_tok, hd)
    return cos_full, sin_signed


def _attn(x, m, row_of_tile, norm_g, layer, w_qkv, q_g, k_g, w_o, seq_len, q_block,
          cache_k=None, cache_v=None, rope=False, emit_kv=False):
    t, d = x.shape
    hd, kvh = HEAD_DIM, N_KV_HEADS
    n_heads = w_o.shape[0] // hd
    group = n_heads // kvh
    gw = group * hd
    tm = TOKEN_TILE
    assert t % tm == 0 and tm % seq_len == 0 and seq_len % q_block == 0
    past_len = 0 if cache_k is None else cache_k.shape[1]
    assert past_len == 0 or tm == seq_len
    in_specs = [
        pl.BlockSpec((tm, d), lambda i, g: (i, 0)),
        pl.BlockSpec((None, None, 3 * N_SUB, d), lambda i, g: (layer, row_of_tile(i), 0, 0)),
        pl.BlockSpec((None, N_SUB, d), lambda i, g: (layer, 0, 0)),
        pl.BlockSpec((d, gw), lambda i, g: (0, g)),
        pl.BlockSpec((d, hd), lambda i, g: (0, n_heads + g)),
        pl.BlockSpec((d, hd), lambda i, g: (0, n_heads + kvh + g)),
        pl.BlockSpec((1, hd), lambda i, g: (0, 0)),
        pl.BlockSpec((1, hd), lambda i, g: (0, 0)),
        pl.BlockSpec((gw, d), lambda i, g: (g, 0)),
    ]
    args = [x, m, norm_g, w_qkv, w_qkv, w_qkv, q_g.reshape(1, hd), k_g.reshape(1, hd), w_o]
    if rope:
        assert tm == seq_len
        cos, sin = _rope_tables(seq_len, hd)
        in_specs += [pl.BlockSpec((tm, hd), lambda i, g: (0, 0))] * 2
        args += [cos, sin]
    if past_len:
        in_specs += [pl.BlockSpec((None, past_len, hd), lambda i, g: (i, 0, g))] * 2
        args += [cache_k, cache_v]
    out_specs = [pl.BlockSpec((tm, d), lambda i, g: (i, 0))]
    out_shape = [jax.ShapeDtypeStruct((t, d), F32)]
    if emit_kv:
        out_specs += [pl.BlockSpec((tm, hd), lambda i, g: (i, g))] * 2
        out_shape += [jax.ShapeDtypeStruct((t, kvh * hd), F32)] * 2
    nkeys = past_len + tm
    outs = pl.pallas_call(
        functools.partial(_attn_kernel, seq_len=seq_len, q_block=q_block, past_len=past_len,
                          rope=rope, emit_kv=emit_kv, group=group),
        grid=(t // tm, kvh),
        in_specs=in_specs,
        out_specs=out_specs,
        out_shape=out_shape,
        scratch_shapes=[pltpu.VMEM((tm, d), BF16), pltpu.VMEM((tm, d), F32),
                        pltpu.VMEM((tm, gw), BF16), pltpu.VMEM((nkeys, hd), BF16),
                        pltpu.VMEM((nkeys, hd), BF16), pltpu.VMEM((tm, gw), BF16)],
        compiler_params=_params("arbitrary", "arbitrary"),
        name="gqa",
    )(*args)
    return outs


def kernel(x_prompt, x_sample, c, state_lru, cache_k, cache_v, c_ctx, mod_w, mod_b, norm_g,
           ffn_w_gu, ffn_w_down, lru_w_in, lru_conv_w, lru_conv_b, lru_gate_w, lru_gate_b,
           lru_lambda, lru_w_out, att_w_qkv, att_q_g, att_k_g, att_w_o, final_g):
    b, s, d = x_prompt.shape
    db, ds, _ = x_sample.shape
    depth = mod_w.shape[0]
    n_mixers = 2
    assert ds % TOKEN_TILE == 0 and 1 + db <= SUBLANES

    xp = x_prompt.reshape(b * s, d)
    xs = x_sample.reshape(db * ds, d)
    cvecs = jnp.concatenate([c_ctx[None], c, jnp.zeros((SUBLANES - 1 - db, d), F32)], axis=0)
    m = _modulation(cvecs, mod_w, mod_b).reshape(depth, SUBLANES, 3 * N_SUB, d)

    tiles_per_sample = ds // TOKEN_TILE
    prompt_row = lambda i: 0
    sample_row = lambda i: 1 + i // tiles_per_sample
    prompt_tiles = (b * s) // TOKEN_TILE
    both_row = lambda i: jnp.where(i < prompt_tiles, 0, 1 + (i - prompt_tiles) // tiles_per_sample)

    new_states, new_k, new_v = [], [], []
    for layer in range(depth):
        j = layer // n_mixers
        last = layer == depth - 1
        xp, xs = _ffn(xp, xs, m, both_row, norm_g, ffn_w_gu, ffn_w_down, layer, 0, 0)
        if layer % n_mixers == 0:
            lru_p = (lru_w_in[j], lru_conv_w[j], lru_conv_b[j], lru_gate_w[j], lru_gate_b[j],
                     lru_lambda[j], lru_w_out[j])
            xp, st = _lru(xp, m, prompt_row, norm_g, layer, *lru_p, seq_len=s, h0=None,
                          emit_state=True)
            new_states.append(st)
            xs, _ = _lru(xs, m, sample_row, norm_g, layer, *lru_p, seq_len=ds,
                         h0=state_lru[:, j], emit_state=False)
        else:
            att_p = (att_w_qkv[j], att_q_g[j], att_k_g[j], att_w_o[j])
            xp, kp, vp = _attn(xp, m, prompt_row, norm_g, layer, *att_p, seq_len=s, q_block=s,
                               emit_kv=True)
            new_k.append(kp.reshape(b, s, N_KV_HEADS, HEAD_DIM))
            new_v.append(vp.reshape(b, s, N_KV_HEADS, HEAD_DIM))
            past = cache_k.shape[2]
            ck = cache_k[:, j].reshape(db, past, N_KV_HEADS * HEAD_DIM)
            cv = cache_v[:, j].reshape(db, past, N_KV_HEADS * HEAD_DIM)
            (xs,) = _attn(xs, m, sample_row, norm_g, layer, *att_p, seq_len=ds, q_block=128,
                          cache_k=ck, cache_v=cv, rope=True)
        fg = final_g if last else None
        xp, xs = _ffn(xp, xs, m, both_row, norm_g, ffn_w_gu, ffn_w_down, layer, 1, 2, final_g=fg)

    y_prompt = xp.reshape(b, s, d)
    y_sample = xs.reshape(db, ds, d)
    return (y_prompt, y_sample, jnp.stack(new_states, axis=1), jnp.stack(new_k, axis=1),
            jnp.stack(new_v, axis=1))
```

```python
import functools

import jax
import jax.numpy as jnp
import numpy as np
from jax import lax
from jax.experimental import pallas as pl
from jax.experimental.pallas import tpu as pltpu

F32 = jnp.float32
BF16 = jnp.bfloat16

EPS = 1e-6
LRU_C = 8.0
GRID_W = 64
ROPE_THETA = 10000.0
N_SUB = 3
HEAD_DIM = 128
N_KV_HEADS = 2
LRU_BLOCKS = 16

V7X_VMEM_LIMIT_BYTES = 56 * 1024 * 1024
SUBLANES = 8
TOKEN_TILE = 1024
FF_CHUNK = 256
FFN_ROW_BLOCK = 256
FFN_STAGE_SLOTS = 6
LRU_CHUNK = 256
MOD_CHUNK = 1536


def _params(*semantics):
    return pltpu.CompilerParams(dimension_semantics=semantics,
                                vmem_limit_bytes=V7X_VMEM_LIMIT_BYTES)


def _mm(a_bf16, w_f32):
    return jnp.dot(a_bf16, w_f32.astype(BF16), preferred_element_type=F32)


def _rms(x):
    return x * lax.rsqrt(jnp.mean(x * x, axis=-1, keepdims=True) + EPS)


def _sub_in(x, m_ref, g_ref, sidx):
    shift = m_ref[3 * sidx:3 * sidx + 1, :]
    scale = m_ref[3 * sidx + 1:3 * sidx + 2, :]
    return (_rms(x) * g_ref[sidx:sidx + 1, :]) * (1.0 + scale) + shift


def _gate(m_ref, sidx):
    return m_ref[3 * sidx + 2:3 * sidx + 3, :]


def _mod_kernel(cv_ref, w_ref, b_ref, o_ref):
    cv = cv_ref[...]
    act = (cv * jax.nn.sigmoid(cv)).astype(BF16)
    o_ref[...] = _mm(act, w_ref[...]) + b_ref[...]


def _modulation(cvecs, mod_w, mod_b):
    depth, d, n = mod_w.shape
    rows = cvecs.shape[0]
    tn = MOD_CHUNK
    assert n % tn == 0
    return pl.pallas_call(
        _mod_kernel,
        grid=(depth, n // tn),
        in_specs=[
            pl.BlockSpec((rows, d), lambda l, j: (0, 0)),
            pl.BlockSpec((None, d, tn), lambda l, j: (l, 0, j)),
            pl.BlockSpec((None, 1, tn), lambda l, j: (l, 0, j)),
        ],
        out_specs=pl.BlockSpec((None, rows, tn), lambda l, j: (l, 0, j)),
        out_shape=jax.ShapeDtypeStruct((depth, rows, n), F32),
        compiler_params=_params("arbitrary", "arbitrary"),
        name="modulation",
    )(cvecs, mod_w, mod_b.reshape(depth, 1, n))


def _ffn_kernel(*refs, layer, s, sidx, final, n_first):
    if final:
        (m_ref, g_ref, fg_ref, xa_hbm, xb_hbm, wgu_hbm, wd_hbm, oa_hbm, ob_hbm,
         wgu_res, wd_res, act_ref, gu_buf, wd_buf, xbuf, obuf,
         gu_sem, wd_sem, xsem, osem) = refs
    else:
        (m_ref, g_ref, xa_hbm, xb_hbm, wgu_hbm, wd_hbm, oa_hbm, ob_hbm,
         wgu_res, wd_res, act_ref, gu_buf, wd_buf, xbuf, obuf,
         gu_sem, wd_sem, xsem, osem) = refs
    i = pl.program_id(0)
    n = pl.num_programs(0)
    tm = xbuf.shape[1]
    n_gu, d, tf = wgu_res.shape
    f = wd_res.shape[0]
    nk = f // tf
    rb = act_ref.shape[0]
    nslot = gu_buf.shape[0]
    ahead = nslot - 1
    slot = lax.rem(i, 2)
    other = 1 - slot

    def gu_copy(q, sl):
        src = wgu_hbm.at[layer, s, :, pl.ds(pl.multiple_of(q * tf, tf), tf)]
        return pltpu.make_async_copy(src, gu_buf.at[sl], gu_sem.at[sl])

    def wd_copy(q, sl):
        src = wd_hbm.at[layer, s, pl.ds(pl.multiple_of(q * tf, tf), tf), :]
        return pltpu.make_async_copy(src, wd_buf.at[sl], wd_sem.at[sl])

    def gu_cast(q, sl):
        wgu_res[q] = gu_buf[sl].astype(BF16)

    def wd_cast(q, sl):
        wd_res[pl.ds(pl.multiple_of(q * tf, tf), tf), :] = wd_buf[sl].astype(BF16)

    def stream(count, copy, cast):
        def body(q, carry):
            sl = lax.rem(q, nslot)
            copy(q, sl).wait()

            @pl.when(q + ahead < count)
            def _():
                copy(q + ahead, lax.rem(q + ahead, nslot)).start()

            cast(q, sl)
            return carry

        lax.fori_loop(0, count, body, 0)

    def tile_rows(hbm, tile):
        return hbm.at[pl.ds(pl.multiple_of(tile * tm, tm), tm), :]

    def x_copy(which, tile, sl):
        return pltpu.make_async_copy(tile_rows((xa_hbm, xb_hbm)[which], tile), xbuf.at[sl],
                                     xsem.at[sl])

    def o_copy(which, tile, sl):
        return pltpu.make_async_copy(obuf.at[sl], tile_rows((oa_hbm, ob_hbm)[which], tile),
                                     osem.at[sl])

    def start_by_stream(make, tile, sl):
        @pl.when(tile < n_first)
        def _():
            make(0, tile, sl).start()

        @pl.when(tile >= n_first)
        def _():
            make(1, tile - n_first, sl).start()

    @pl.when(i == 0)
    def _():
        x_copy(0, 0, 0).start()
        for q in range(ahead):
            gu_copy(q, q).start()
        for q in range(ahead):
            wd_copy(q, q).start()

    @pl.when(i + 1 < n)
    def _():
        start_by_stream(x_copy, i + 1, other)

    @pl.when(i == 0)
    def _():
        stream(n_gu, gu_copy, gu_cast)
        stream(nk, wd_copy, wd_cast)

    x_copy(0, 0, slot).wait()

    @pl.when(i >= 2)
    def _():
        o_copy(0, 0, slot).wait()

    def row_block(r, carry):
        rows = pl.ds(pl.multiple_of(r * rb, rb), rb)
        x = xbuf[slot, rows, :]
        h = _sub_in(x, m_ref, g_ref, sidx).astype(BF16)
        for k in range(nk):
            gt = jnp.dot(h, wgu_res[k], preferred_element_type=F32)
            up = jnp.dot(h, wgu_res[nk + k], preferred_element_type=F32)
            act_ref[:, k * tf:(k + 1) * tf] = ((gt * jax.nn.sigmoid(gt)) * up).astype(BF16)
        ff = jnp.dot(act_ref[...], wd_res[...], preferred_element_type=F32)
        y = x + (0.5 * _gate(m_ref, sidx)) * ff
        if final:
            y = _rms(y) * fg_ref[...]
        obuf[slot, rows, :] = y
        return carry

    lax.fori_loop(0, tm // rb, row_block, 0, unroll=2)

    start_by_stream(o_copy, i, slot)

    @pl.when(i == n - 1)
    def _():
        o_copy(0, 0, other).wait()
        o_copy(0, 0, slot).wait()


def _ffn(xa, xb, m, row_of_tile, norm_g, w_gu, w_down, layer, s, sidx, final_g=None):
    (ta, d), tb = xa.shape, xb.shape[0]
    f = w_down.shape[2]
    tm, tf, rb = TOKEN_TILE, FF_CHUNK, FFN_ROW_BLOCK
    assert ta % tm == 0 and tb % tm == 0 and f % tf == 0 and (ta + tb) // tm >= 2
    assert tm % rb == 0 and f // tf >= FFN_STAGE_SLOTS
    final = final_g is not None
    in_specs = [
        pl.BlockSpec((None, None, 3 * N_SUB, d), lambda i: (layer, row_of_tile(i), 0, 0)),
        pl.BlockSpec((None, N_SUB, d), lambda i: (layer, 0, 0)),
    ]
    args = [m, norm_g]
    if final:
        in_specs.append(pl.BlockSpec((1, d), lambda i: (0, 0)))
        args.append(final_g.reshape(1, d))
    in_specs += [pl.BlockSpec(memory_space=pl.ANY)] * 4
    args += [xa, xb, w_gu, w_down]
    return pl.pallas_call(
        functools.partial(_ffn_kernel, layer=layer, s=s, sidx=sidx, final=final,
                          n_first=ta // tm),
        grid=((ta + tb) // tm,),
        in_specs=in_specs,
        out_specs=[pl.BlockSpec(memory_space=pl.ANY)] * 2,
        out_shape=[jax.ShapeDtypeStruct((ta, d), F32), jax.ShapeDtypeStruct((tb, d), F32)],
        scratch_shapes=[
            pltpu.VMEM((2 * f // tf, d, tf), BF16), pltpu.VMEM((f, d), BF16),
            pltpu.VMEM((rb, f), BF16),
            pltpu.VMEM((FFN_STAGE_SLOTS, d, tf), F32),
            pltpu.VMEM((FFN_STAGE_SLOTS, tf, d), F32),
            pltpu.VMEM((2, tm, d), F32), pltpu.VMEM((2, tm, d), F32),
            pltpu.SemaphoreType.DMA((FFN_STAGE_SLOTS,)),
            pltpu.SemaphoreType.DMA((FFN_STAGE_SLOTS,)),
            pltpu.SemaphoreType.DMA((2,)),
            pltpu.SemaphoreType.DMA((2,)),
        ],
        compiler_params=_params("arbitrary"),
        name="ffn",
    )(*args)


def _gelu_tanh(x):
    c = np.float32(np.sqrt(2.0 / np.pi))
    return x * (0.5 * (1.0 + jnp.tanh(c * (x + 0.044715 * (x * x * x)))))


def _group_scan(a, u, row8, reverse):
    rows = a.shape[0]
    for s in (1, 2, 4):
        if reverse:
            shift, valid = rows - s, row8 < SUBLANES - s
        else:
            shift, valid = s, row8 >= s
        u = jnp.where(valid, u + a * pltpu.roll(u, shift, 0), u)
        a = jnp.where(valid, a * pltpu.roll(a, shift, 0), a)
    return a, u


def _lru_kernel(*refs, seq_len, has_h0, emit_state, conv_left):
    refs = list(refs)
    (x_ref, m_ref, g_ref, wx_ref, wy_ref, cw_ref, cb_ref, gw_ref, gb_ref, lam_ref,
     wo_ref) = refs[:11]
    pos = 11
    h0_ref = None
    if has_h0:
        h0_ref = refs[pos]
        pos += 1
    o_ref = refs[pos]
    pos += 1
    st_ref = None
    if emit_state:
        st_ref = refs[pos]
        pos += 1
    h_ref, acc_ref, af_ref, uf_ref, ab_ref, ub_ref = refs[pos:]

    c = pl.program_id(1)
    tm, cw_cols = af_ref.shape
    nseq = tm // seq_len
    ngroups = seq_len // SUBLANES

    @pl.when(c == 0)
    def _():
        h_ref[...] = _sub_in(x_ref[...], m_ref, g_ref, 1).astype(BF16)
        acc_ref[...] = jnp.zeros_like(acc_ref)

    h = h_ref[...]
    xb = _mm(h, wx_ref[...])
    yb = _mm(h, wy_ref[...])

    row = lax.broadcasted_iota(jnp.int32, (tm, cw_cols), 0)
    t = row & (seq_len - 1)
    cw = cw_ref[...]
    xc = cb_ref[...]
    for k in range(cw.shape[0]):
        off = k - conv_left
        if off == 0:
            tap = xb
        else:
            valid = (t + off >= 0) & (t + off <= seq_len - 1)
            tap = jnp.where(valid, pltpu.roll(xb, (-off) % tm, 0), 0.0)
        xc = xc + tap * cw[k:k + 1, :]

    gl = jnp.dot(xc.astype(BF16), gw_ref[...], preferred_element_type=F32)
    gb = gb_ref[...]
    lam = lam_ref[...]
    row8 = row & (SUBLANES - 1)
    for d, (a_ref, u_ref) in enumerate(((af_ref, uf_ref), (ab_ref, ub_ref))):
        r = jax.nn.sigmoid(gl[:, (2 * d) * cw_cols:(2 * d + 1) * cw_cols] + gb[2 * d:2 * d + 1, :])
        ig = jax.nn.sigmoid(gl[:, (2 * d + 1) * cw_cols:(2 * d + 2) * cw_cols]
                            + gb[2 * d + 1:2 * d + 2, :])
        lm = lam[d:d + 1, :]
        log_sig = jnp.minimum(lm, 0.0) - jnp.log1p(jnp.exp(-jnp.abs(lm)))
        a = jnp.exp((LRU_C * r) * log_sig)
        u = jnp.sqrt(1.0 - a * a) * (ig * xc)
        a, u = _group_scan(a, u, row8, reverse=(d == 1))
        a_ref[...] = a
        u_ref[...] = u

    if has_h0:
        init = tuple(h0_ref[s, dd:dd + 1, :] for s in range(nseq) for dd in range(2))
    else:
        init = tuple(jnp.zeros((1, cw_cols), F32) for _ in range(2 * nseq))

    def body(j, carry):
        out = []
        for s in range(nseq):
            rf = pl.multiple_of(s * seq_len + j * SUBLANES, SUBLANES)
            rb = pl.multiple_of(s * seq_len + (ngroups - 1 - j) * SUBLANES, SUBLANES)
            hf = uf_ref[pl.ds(rf, SUBLANES), :] + af_ref[pl.ds(rf, SUBLANES), :] * carry[2 * s]
            uf_ref[pl.ds(rf, SUBLANES), :] = hf
            hb = ub_ref[pl.ds(rb, SUBLANES), :] + ab_ref[pl.ds(rb, SUBLANES), :] * carry[2 * s + 1]
            ub_ref[pl.ds(rb, SUBLANES), :] = hb
            out += [hf[SUBLANES - 1:SUBLANES, :], hb[0:1, :]]
        return tuple(out)

    last = lax.fori_loop(0, ngroups, body, init)
    if emit_state:
        for s in range(nseq):
            st_ref[s, 0:1, :] = last[2 * s]
            st_ref[s, 1:2, :] = last[2 * s + 1]

    y = (uf_ref[...] + ub_ref[...]) * _gelu_tanh(yb)
    acc_ref[...] += _mm(y.astype(BF16), wo_ref[...])

    @pl.when(c == pl.num_programs(1) - 1)
    def _():
        o_ref[...] = x_ref[...] + _gate(m_ref, 1) * acc_ref[...]


def _lru_gate_weights(gate_w):
    nd, ng, nb, bw, _ = gate_w.shape
    per = LRU_CHUNK // bw
    nc = nb // per
    w = gate_w.reshape(nd * ng, nc, per, bw, bw)
    eye = jnp.eye(per, dtype=gate_w.dtype)
    w = w[:, :, :, :, None, :] * eye[None, None, :, None, :, None]
    w = jnp.transpose(w, (1, 2, 3, 0, 4, 5))
    return w.reshape(nc, per * bw, nd * ng * per * bw).astype(BF16)


def _lru(x, m, row_of_tile, norm_g, layer, w_in, conv_w, conv_b, gate_w, gate_b, lam, w_out,
         seq_len, h0, emit_state):
    t, d = x.shape
    r = w_out.shape[0]
    tm, cb = TOKEN_TILE, LRU_CHUNK
    assert t % tm == 0 and tm % seq_len == 0 and r % cb == 0 and seq_len % SUBLANES == 0
    assert seq_len & (seq_len - 1) == 0, "in-sequence position is taken with a bit mask"
    nc = r // cb
    nseq = tm // seq_len
    has_h0 = h0 is not None
    in_specs = [
        pl.BlockSpec((tm, d), lambda i, c: (i, 0)),
        pl.BlockSpec((None, None, 3 * N_SUB, d), lambda i, c: (layer, row_of_tile(i), 0, 0)),
        pl.BlockSpec((None, N_SUB, d), lambda i, c: (layer, 0, 0)),
        pl.BlockSpec((d, cb), lambda i, c: (0, c)),
        pl.BlockSpec((d, cb), lambda i, c: (0, nc + c)),
        pl.BlockSpec((conv_w.shape[0], cb), lambda i, c: (0, c)),
        pl.BlockSpec((1, cb), lambda i, c: (0, c)),
        pl.BlockSpec((None, cb, 4 * cb), lambda i, c: (c, 0, 0)),
        pl.BlockSpec((4, cb), lambda i, c: (0, c)),
        pl.BlockSpec((2, cb), lambda i, c: (0, c)),
        pl.BlockSpec((cb, d), lambda i, c: (c, 0)),
    ]
    args = [x, m, norm_g, w_in, w_in, conv_w, conv_b.reshape(1, r), _lru_gate_weights(gate_w),
            gate_b.reshape(4, r), lam, w_out]
    if has_h0:
        in_specs.append(pl.BlockSpec((nseq, 2, cb), lambda i, c: (i, 0, c)))
        args.append(h0)
    out_specs = [pl.BlockSpec((tm, d), lambda i, c: (i, 0))]
    out_shape = [jax.ShapeDtypeStruct((t, d), F32)]
    if emit_state:
        out_specs.append(pl.BlockSpec((nseq, 2, cb), lambda i, c: (i, 0, c)))
        out_shape.append(jax.ShapeDtypeStruct((t // seq_len, 2, r), F32))
    outs = pl.pallas_call(
        functools.partial(_lru_kernel, seq_len=seq_len, has_h0=has_h0, emit_state=emit_state,
                          conv_left=(conv_w.shape[0] - 1) // 2),
        grid=(t // tm, nc),
        in_specs=in_specs,
        out_specs=out_specs,
        out_shape=out_shape,
        scratch_shapes=[pltpu.VMEM((tm, d), BF16), pltpu.VMEM((tm, d), F32)]
        + [pltpu.VMEM((tm, cb), F32)] * 4,
        compiler_params=_params("arbitrary", "arbitrary"),
        name="rglru",
    )(*args)
    return outs if emit_state else (outs[0], None)


def _rope(x, cos, sin_signed, lane):
    hd = x.shape[1]
    partner = jnp.where((lane & 32) == 0, pltpu.roll(x, hd - 32, 1), pltpu.roll(x, 32, 1))
    return x * cos + partner * sin_signed


def _attn_kernel(*refs, seq_len, q_block, past_len, rope, emit_kv, group):
    refs = list(refs)
    x_ref, m_ref, g_ref, wq_ref, wk_ref, wv_ref, qg_ref, kg_ref, wo_ref = refs[:9]
    pos = 9
    if rope:
        cos_ref, sin_ref = refs[pos:pos + 2]
        pos += 2
    if past_len:
        ck_ref, cv_ref = refs[pos:pos + 2]
        pos += 2
    o_ref = refs[pos]
    pos += 1
    if emit_kv:
        kn_ref, vn_ref = refs[pos:pos + 2]
        pos += 2
    h_ref, acc_ref, q_s, k_s, v_s, o_s = refs[pos:]

    gi = pl.program_id(1)
    tm = x_ref.shape[0]
    hd = k_s.shape[1]
    nqb = seq_len // q_block
    nchunks = (tm // seq_len) * nqb
    nk = past_len + seq_len
    scale = hd ** -0.5

    @pl.when(gi == 0)
    def _():
        h_ref[...] = _sub_in(x_ref[...], m_ref, g_ref, 1).astype(BF16)
        acc_ref[...] = jnp.zeros_like(acc_ref)

    h = h_ref[...]
    q = _mm(h, wq_ref[...])
    k = _rms(_mm(h, wk_ref[...])) * kg_ref[...]
    v = _mm(h, wv_ref[...])
    if emit_kv:
        kn_ref[...] = k
        vn_ref[...] = v
    if rope:
        lane = lax.broadcasted_iota(jnp.int32, (tm, hd), 1)
        cos, sin = cos_ref[...], sin_ref[...]
        k = _rope(k, cos, sin, lane)
    k_s[past_len:past_len + tm, :] = k.astype(BF16)
    v_s[past_len:past_len + tm, :] = v.astype(BF16)
    if past_len:
        k_s[0:past_len, :] = ck_ref[...].astype(BF16)
        v_s[0:past_len, :] = cv_ref[...].astype(BF16)
    for j in range(group):
        qh = _rms(q[:, j * hd:(j + 1) * hd]) * qg_ref[...]
        if rope:
            qh = _rope(qh, cos, sin, lane)
        q_s[:, j * hd:(j + 1) * hd] = qh.astype(BF16)

    def chunk(ci, carry):
        r0 = pl.multiple_of(ci * q_block, q_block)
        if past_len:
            keys, vals = k_s[...], v_s[...]
        else:
            koff = pl.multiple_of((ci // nqb) * seq_len, seq_len)
            keys, vals = k_s[pl.ds(koff, nk), :], v_s[pl.ds(koff, nk), :]
        qc = jnp.concatenate([q_s[pl.ds(r0, q_block), j * hd:(j + 1) * hd] for j in range(group)],
                             axis=0)
        s = lax.dot_general(qc, keys, (((1,), (1,)), ((), ())), preferred_element_type=F32) * scale
        e = jnp.exp(s - jnp.max(s, axis=-1, keepdims=True))
        p = e * (1.0 / jnp.sum(e, axis=-1, keepdims=True))
        oc = jnp.dot(p.astype(BF16), vals, preferred_element_type=F32)
        for j in range(group):
            o_s[pl.ds(r0, q_block), j * hd:(j + 1) * hd] = oc[j * q_block:(j + 1) * q_block, :].astype(BF16)
        return carry

    lax.fori_loop(0, nchunks, chunk, 0)
    acc_ref[...] += _mm(o_s[...], wo_ref[...])

    @pl.when(gi == pl.num_programs(1) - 1)
    def _():
        o_ref[...] = x_ref[...] + _gate(m_ref, 1) * acc_ref[...]


def _rope_tables(n_tok, hd):
    rows = n_tok // GRID_W
    r_idx = jnp.broadcast_to(jnp.arange(rows)[:, None], (rows, GRID_W)).reshape(n_tok).astype(F32)
    c_idx = jnp.broadcast_to(jnp.arange(GRID_W)[None, :], (rows, GRID_W)).reshape(n_tok).astype(F32)
    n_freq = hd // 4
    inv = ROPE_THETA ** (-jnp.arange(n_freq, dtype=F32) / n_freq)
    ang = jnp.stack([r_idx[:, None] * inv, c_idx[:, None] * inv], axis=1)
    cos, sin = jnp.cos(ang), jnp.sin(ang)
    cos_full = jnp.concatenate([cos, cos], axis=-1).reshape(n_tok, hd)
    sin_signed = jnp.concatenate([-sin, sin], axis=-1).reshape(n_tok, hd)
    return cos_full, sin_signed


def _attn(x, m, row_of_tile, norm_g, layer, w_qkv, q_g, k_g, w_o, seq_len, q_block,
          cache_k=None, cache_v=None, rope=False, emit_kv=False):
    t, d = x.shape
    hd, kvh = HEAD_DIM, N_KV_HEADS
    n_heads = w_o.shape[0] // hd
    group = n_heads // kvh
    gw = group * hd
    tm = TOKEN_TILE
    assert t % tm == 0 and tm % seq_len == 0 and seq_len % q_block == 0
    past_len = 0 if cache_k is None else cache_k.shape[1]
    assert past_len == 0 or tm == seq_len
    in_specs = [
        pl.BlockSpec((tm, d), lambda i, g: (i, 0)),
        pl.BlockSpec((None, None, 3 * N_SUB, d), lambda i, g: (layer, row_of_tile(i), 0, 0)),
        pl.BlockSpec((None, N_SUB, d), lambda i, g: (layer, 0, 0)),
        pl.BlockSpec((d, gw), lambda i, g: (0, g)),
        pl.BlockSpec((d, hd), lambda i, g: (0, n_heads + g)),
        pl.BlockSpec((d, hd), lambda i, g: (0, n_heads + kvh + g)),
        pl.BlockSpec((1, hd), lambda i, g: (0, 0)),
        pl.BlockSpec((1, hd), lambda i, g: (0, 0)),
        pl.BlockSpec((gw, d), lambda i, g: (g, 0)),
    ]
    args = [x, m, norm_g, w_qkv, w_qkv, w_qkv, q_g.reshape(1, hd), k_g.reshape(1, hd), w_o]
    if rope:
        assert tm == seq_len
        cos, sin = _rope_tables(seq_len, hd)
        in_specs += [pl.BlockSpec((tm, hd), lambda i, g: (0, 0))] * 2
        args += [cos, sin]
    if past_len:
        in_specs += [pl.BlockSpec((None, past_len, hd), lambda i, g: (i, 0, g))] * 2
        args += [cache_k, cache_v]
    out_specs = [pl.BlockSpec((tm, d), lambda i, g: (i, 0))]
    out_shape = [jax.ShapeDtypeStruct((t, d), F32)]
    if emit_kv:
        out_specs += [pl.BlockSpec((tm, hd), lambda i, g: (i, g))] * 2
        out_shape += [jax.ShapeDtypeStruct((t, kvh * hd), F32)] * 2
    nkeys = past_len + tm
    outs = pl.pallas_call(
        functools.partial(_attn_kernel, seq_len=seq_len, q_block=q_block, past_len=past_len,
                          rope=rope, emit_kv=emit_kv, group=group),
        grid=(t // tm, kvh),
        in_specs=in_specs,
        out_specs=out_specs,
        out_shape=out_shape,
        scratch_shapes=[pltpu.VMEM((tm, d), BF16), pltpu.VMEM((tm, d), F32),
                        pltpu.VMEM((tm, gw), BF16), pltpu.VMEM((nkeys, hd), BF16),
                        pltpu.VMEM((nkeys, hd), BF16), pltpu.VMEM((tm, gw), BF16)],
        compiler_params=_params("arbitrary", "arbitrary"),
        name="gqa",
    )(*args)
    return outs


def kernel(x_prompt, x_sample, c, state_lru, cache_k, cache_v, c_ctx, mod_w, mod_b, norm_g,
           ffn_w_gu, ffn_w_down, lru_w_in, lru_conv_w, lru_conv_b, lru_gate_w, lru_gate_b,
           lru_lambda, lru_w_out, att_w_qkv, att_q_g, att_k_g, att_w_o, final_g):
    b, s, d = x_prompt.shape
    db, ds, _ = x_sample.shape
    depth = mod_w.shape[0]
    n_mixers = 2
    assert ds % TOKEN_TILE == 0 and 1 + db <= SUBLANES

    xp = x_prompt.reshape(b * s, d)
    xs = x_sample.reshape(db * ds, d)
    cvecs = jnp.concatenate([c_ctx[None], c, jnp.zeros((SUBLANES - 1 - db, d), F32)], axis=0)
    m = _modulation(cvecs, mod_w, mod_b).reshape(depth, SUBLANES, 3 * N_SUB, d)

    tiles_per_sample = ds // TOKEN_TILE
    prompt_row = lambda i: 0
    sample_row = lambda i: 1 + i // tiles_per_sample
    prompt_tiles = (b * s) // TOKEN_TILE
    both_row = lambda i: jnp.where(i < prompt_tiles, 0, 1 + (i - prompt_tiles) // tiles_per_sample)

    new_states, new_k, new_v = [], [], []
    for layer in range(depth):
        j = layer // n_mixers
        last = layer == depth - 1
        xp, xs = _ffn(xp, xs, m, both_row, norm_g, ffn_w_gu, ffn_w_down, layer, 0, 0)
        if layer % n_mixers == 0:
            lru_p = (lru_w_in[j], lru_conv_w[j], lru_conv_b[j], lru_gate_w[j], lru_gate_b[j],
                     lru_lambda[j], lru_w_out[j])
            xp, st = _lru(xp, m, prompt_row, norm_g, layer, *lru_p, seq_len=s, h0=None,
                          emit_state=True)
            new_states.append(st)
            xs, _ = _lru(xs, m, sample_row, norm_g, layer, *lru_p, seq_len=ds,
                         h0=state_lru[:, j], emit_state=False)
        else:
            att_p = (att_w_qkv[j], att_q_g[j], att_k_g[j], att_w_o[j])
            xp, kp, vp = _attn(xp, m, prompt_row, norm_g, layer, *att_p, seq_len=s, q_block=s,
                               emit_kv=True)
            new_k.append(kp.reshape(b, s, N_KV_HEADS, HEAD_DIM))
            new_v.append(vp.reshape(b, s, N_KV_HEADS, HEAD_DIM))
            past = cache_k.shape[2]
            ck = cache_k[:, j].reshape(db, past, N_KV_HEADS * HEAD_DIM)
            cv = cache_v[:, j].reshape(db, past, N_KV_HEADS * HEAD_DIM)
            (xs,) = _attn(xs, m, sample_row, norm_g, layer, *att_p, seq_len=ds, q_block=128,
                          cache_k=ck, cache_v=cv, rope=True)
        fg = final_g if last else None
        xp, xs = _ffn(xp, xs, m, both_row, norm_g, ffn_w_gu, ffn_w_down, layer, 1, 2, final_g=fg)

    y_prompt = xp.reshape(b, s, d)
    y_sample = xs.reshape(db, ds, d)
    return (y_prompt, y_sample, jnp.stack(new_states, axis=1), jnp.stack(new_k, axis=1),
            jnp.stack(new_v, axis=1))
```

```python
import functools

import jax
import jax.numpy as jnp
import numpy as np
from jax import lax
from jax.experimental import pallas as pl
from jax.experimental.pallas import tpu as pltpu

F32 = jnp.float32
BF16 = jnp.bfloat16

EPS = 1e-6
LRU_C = 8.0
LOG2_E = 1.4426950408889634
GRID_W = 64
ROPE_THETA = 10000.0
N_SUB = 3
HEAD_DIM = 128
N_KV_HEADS = 2
LRU_BLOCKS = 16

V7X_VMEM_LIMIT_BYTES = 56 * 1024 * 1024
SUBLANES = 8
TOKEN_TILE = 1024
FF_CHUNK = 256
FFN_ROW_BLOCK = 256
FFN_STAGE_SLOTS = 6
LRU_CHUNK = 256
MOD_CHUNK = 1536


def _params(*semantics):
    return pltpu.CompilerParams(dimension_semantics=semantics,
                                vmem_limit_bytes=V7X_VMEM_LIMIT_BYTES)


def _mm(a_bf16, w_f32):
    return jnp.dot(a_bf16, w_f32.astype(BF16), preferred_element_type=F32)


def _rms(x):
    return x * lax.rsqrt(jnp.mean(x * x, axis=-1, keepdims=True) + EPS)


def _sub_in(x, m_ref, g_ref, sidx):
    shift = m_ref[3 * sidx:3 * sidx + 1, :]
    scale = m_ref[3 * sidx + 1:3 * sidx + 2, :]
    return (_rms(x) * g_ref[sidx:sidx + 1, :]) * (1.0 + scale) + shift


def _gate(m_ref, sidx):
    return m_ref[3 * sidx + 2:3 * sidx + 3, :]


def _mod_kernel(cv_ref, w_ref, b_ref, o_ref):
    cv = cv_ref[...]
    act = (cv * jax.nn.sigmoid(cv)).astype(BF16)
    o_ref[...] = _mm(act, w_ref[...]) + b_ref[...]


def _modulation(cvecs, mod_w, mod_b):
    depth, d, n = mod_w.shape
    rows = cvecs.shape[0]
    tn = MOD_CHUNK
    assert n % tn == 0
    return pl.pallas_call(
        _mod_kernel,
        grid=(depth, n // tn),
        in_specs=[
            pl.BlockSpec((rows, d), lambda l, j: (0, 0)),
            pl.BlockSpec((None, d, tn), lambda l, j: (l, 0, j)),
            pl.BlockSpec((None, 1, tn), lambda l, j: (l, 0, j)),
        ],
        out_specs=pl.BlockSpec((None, rows, tn), lambda l, j: (l, 0, j)),
        out_shape=jax.ShapeDtypeStruct((depth, rows, n), F32),
        compiler_params=_params("arbitrary", "arbitrary"),
        name="modulation",
    )(cvecs, mod_w, mod_b.reshape(depth, 1, n))


def _ffn_kernel(*refs, layer, s, sidx, final, n_first):
    if final:
        (m_ref, g_ref, fg_ref, xa_hbm, xb_hbm, wgu_hbm, wd_hbm, oa_hbm, ob_hbm,
         wgu_res, wd_res, act_ref, gu_buf, wd_buf, xbuf, obuf,
         gu_sem, wd_sem, xsem, osem) = refs
    else:
        (m_ref, g_ref, xa_hbm, xb_hbm, wgu_hbm, wd_hbm, oa_hbm, ob_hbm,
         wgu_res, wd_res, act_ref, gu_buf, wd_buf, xbuf, obuf,
         gu_sem, wd_sem, xsem, osem) = refs
    i = pl.program_id(0)
    n = pl.num_programs(0)
    tm = xbuf.shape[1]
    n_gu, d, tf = wgu_res.shape
    f = wd_res.shape[0]
    nk = f // tf
    rb = act_ref.shape[0]
    nslot = gu_buf.shape[0]
    ahead = nslot - 1
    slot = lax.rem(i, 2)
    other = 1 - slot

    def gu_copy(q, sl):
        src = wgu_hbm.at[layer, s, :, pl.ds(pl.multiple_of(q * tf, tf), tf)]
        return pltpu.make_async_copy(src, gu_buf.at[sl], gu_sem.at[sl])

    def wd_copy(q, sl):
        src = wd_hbm.at[layer, s, pl.ds(pl.multiple_of(q * tf, tf), tf), :]
        return pltpu.make_async_copy(src, wd_buf.at[sl], wd_sem.at[sl])

    def gu_cast(q, sl):
        wgu_res[q] = gu_buf[sl].astype(BF16)

    def wd_cast(q, sl):
        wd_res[pl.ds(pl.multiple_of(q * tf, tf), tf), :] = wd_buf[sl].astype(BF16)

    def stream(count, copy, cast):
        def body(q, carry):
            sl = lax.rem(q, nslot)
            copy(q, sl).wait()

            @pl.when(q + ahead < count)
            def _():
                copy(q + ahead, lax.rem(q + ahead, nslot)).start()

            cast(q, sl)
            return carry

        lax.fori_loop(0, count, body, 0)

    def tile_rows(hbm, tile):
        return hbm.at[pl.ds(pl.multiple_of(tile * tm, tm), tm), :]

    def x_copy(which, tile, sl):
        return pltpu.make_async_copy(tile_rows((xa_hbm, xb_hbm)[which], tile), xbuf.at[sl],
                                     xsem.at[sl])

    def o_copy(which, tile, sl):
        return pltpu.make_async_copy(obuf.at[sl], tile_rows((oa_hbm, ob_hbm)[which], tile),
                                     osem.at[sl])

    def start_by_stream(make, tile, sl):
        @pl.when(tile < n_first)
        def _():
            make(0, tile, sl).start()

        @pl.when(tile >= n_first)
        def _():
            make(1, tile - n_first, sl).start()

    @pl.when(i == 0)
    def _():
        x_copy(0, 0, 0).start()
        for q in range(ahead):
            gu_copy(q, q).start()
        for q in range(ahead):
            wd_copy(q, q).start()

    @pl.when(i + 1 < n)
    def _():
        start_by_stream(x_copy, i + 1, other)

    @pl.when(i == 0)
    def _():
        stream(n_gu, gu_copy, gu_cast)
        stream(nk, wd_copy, wd_cast)

    x_copy(0, 0, slot).wait()

    @pl.when(i >= 2)
    def _():
        o_copy(0, 0, slot).wait()

    def row_block(r, carry):
        rows = pl.ds(pl.multiple_of(r * rb, rb), rb)
        x = xbuf[slot, rows, :]
        h = _sub_in(x, m_ref, g_ref, sidx).astype(BF16)
        for k in range(nk):
            gt = jnp.dot(h, wgu_res[k], preferred_element_type=F32)
            up = jnp.dot(h, wgu_res[nk + k], preferred_element_type=F32)
            act_ref[:, k * tf:(k + 1) * tf] = ((gt * jax.nn.sigmoid(gt)) * up).astype(BF16)
        ff = jnp.dot(act_ref[...], wd_res[...], preferred_element_type=F32)
        y = x + (0.5 * _gate(m_ref, sidx)) * ff
        if final:
            y = _rms(y) * fg_ref[...]
        obuf[slot, rows, :] = y
        return carry

    lax.fori_loop(0, tm // rb, row_block, 0, unroll=2)

    start_by_stream(o_copy, i, slot)

    @pl.when(i == n - 1)
    def _():
        o_copy(0, 0, other).wait()
        o_copy(0, 0, slot).wait()


def _ffn(xa, xb, m, row_of_tile, norm_g, w_gu, w_down, layer, s, sidx, final_g=None):
    (ta, d), tb = xa.shape, xb.shape[0]
    f = w_down.shape[2]
    tm, tf, rb = TOKEN_TILE, FF_CHUNK, FFN_ROW_BLOCK
    assert ta % tm == 0 and tb % tm == 0 and f % tf == 0 and (ta + tb) // tm >= 2
    assert tm % rb == 0 and f // tf >= FFN_STAGE_SLOTS
    final = final_g is not None
    in_specs = [
        pl.BlockSpec((None, None, 3 * N_SUB, d), lambda i: (layer, row_of_tile(i), 0, 0)),
        pl.BlockSpec((None, N_SUB, d), lambda i: (layer, 0, 0)),
    ]
    args = [m, norm_g]
    if final:
        in_specs.append(pl.BlockSpec((1, d), lambda i: (0, 0)))
        args.append(final_g.reshape(1, d))
    in_specs += [pl.BlockSpec(memory_space=pl.ANY)] * 4
    args += [xa, xb, w_gu, w_down]
    return pl.pallas_call(
        functools.partial(_ffn_kernel, layer=layer, s=s, sidx=sidx, final=final,
                          n_first=ta // tm),
        grid=((ta + tb) // tm,),
        in_specs=in_specs,
        out_specs=[pl.BlockSpec(memory_space=pl.ANY)] * 2,
        out_shape=[jax.ShapeDtypeStruct((ta, d), F32), jax.ShapeDtypeStruct((tb, d), F32)],
        scratch_shapes=[
            pltpu.VMEM((2 * f // tf, d, tf), BF16), pltpu.VMEM((f, d), BF16),
            pltpu.VMEM((rb, f), BF16),
            pltpu.VMEM((FFN_STAGE_SLOTS, d, tf), F32),
            pltpu.VMEM((FFN_STAGE_SLOTS, tf, d), F32),
            pltpu.VMEM((2, tm, d), F32), pltpu.VMEM((2, tm, d), F32),
            pltpu.SemaphoreType.DMA((FFN_STAGE_SLOTS,)),
            pltpu.SemaphoreType.DMA((FFN_STAGE_SLOTS,)),
            pltpu.SemaphoreType.DMA((2,)),
            pltpu.SemaphoreType.DMA((2,)),
        ],
        compiler_params=_params("arbitrary"),
        name="ffn",
    )(*args)


def _gelu_tanh(x):
    c = np.float32(np.sqrt(2.0 / np.pi))
    return x * (0.5 * (1.0 + jnp.tanh(c * (x + 0.044715 * (x * x * x)))))


def _group_scan(a, u, row8, reverse):
    rows, cols = a.shape
    grouped = (rows // SUBLANES, SUBLANES, cols)
    a, u, row8 = a.reshape(grouped), u.reshape(grouped), row8.reshape(grouped)
    for s in (1, 2, 4):
        if reverse:
            shift, valid = SUBLANES - s, row8 < SUBLANES - s
        else:
            shift, valid = s, row8 >= s
        u = jnp.where(valid, u + a * pltpu.roll(u, shift, 1), u)
        a = jnp.where(valid, a * pltpu.roll(a, shift, 1), a)
    return a.reshape(rows, cols), u.reshape(rows, cols)


def _lru_kernel(*refs, seq_len, has_h0, emit_state, conv_left):
    refs = list(refs)
    (x_ref, m_ref, g_ref, wx_ref, wy_ref, cw_ref, cb_ref, gw_ref, gb_ref, lam_ref,
     wo_ref) = refs[:11]
    pos = 11
    h0_ref = None
    if has_h0:
        h0_ref = refs[pos]
        pos += 1
    o_ref = refs[pos]
    pos += 1
    st_ref = None
    if emit_state:
        st_ref = refs[pos]
        pos += 1
    h_ref, acc_ref, af_ref, uf_ref, ab_ref, ub_ref, xpad_ref = refs[pos:]

    c = pl.program_id(1)
    tm, cw_cols = af_ref.shape
    nseq = tm // seq_len
    ngroups = seq_len // SUBLANES
    pitch = seq_len + 2 * SUBLANES

    @pl.when(c == 0)
    def _():
        h_ref[...] = _sub_in(x_ref[...], m_ref, g_ref, 1).astype(BF16)
        acc_ref[...] = jnp.zeros_like(acc_ref)

    h = h_ref[...]
    xb = _mm(h, wx_ref[...])
    yb = _mm(h, wy_ref[...])

    zeros = jnp.zeros((SUBLANES, cw_cols), F32)
    for sq in range(nseq):
        base = sq * pitch
        xpad_ref[base:base + SUBLANES, :] = zeros
        xpad_ref[base + SUBLANES:base + SUBLANES + seq_len, :] = xb[sq * seq_len:(sq + 1) * seq_len, :]
        xpad_ref[base + SUBLANES + seq_len:base + pitch, :] = zeros
    cw = cw_ref[...]
    xc = cb_ref[...]
    for k in range(cw.shape[0]):
        off = k - conv_left
        assert abs(off) <= SUBLANES
        if off == 0:
            tap = xb
        else:
            tap = jnp.concatenate(
                [xpad_ref[sq * pitch + SUBLANES + off:sq * pitch + SUBLANES + off + seq_len, :]
                 for sq in range(nseq)], axis=0)
        xc = xc + tap * cw[k:k + 1, :]

    gl = jnp.dot(xc.astype(BF16), gw_ref[...], preferred_element_type=F32)
    gb = gb_ref[...]
    lam = lam_ref[...]
    row8 = lax.broadcasted_iota(jnp.int32, (tm, cw_cols), 0) & (SUBLANES - 1)
    for d, (a_ref, u_ref) in enumerate(((af_ref, uf_ref), (ab_ref, ub_ref))):
        tr = jnp.tanh(gl[:, (2 * d) * cw_cols:(2 * d + 1) * cw_cols] + 0.5 * gb[2 * d:2 * d + 1, :])
        ig = 0.5 * jnp.tanh(gl[:, (2 * d + 1) * cw_cols:(2 * d + 2) * cw_cols]
                            + 0.5 * gb[2 * d + 1:2 * d + 2, :]) + 0.5
        lm = lam[d:d + 1, :]
        log_sig = jnp.minimum(lm, 0.0) - jnp.log1p(jnp.exp(-jnp.abs(lm)))
        c2 = (0.5 * LRU_C * LOG2_E) * log_sig
        a = jnp.exp2(c2 * tr + c2)
        v = 1.0 - a * a
        u = jnp.where(v > 0.0, v * lax.rsqrt(v), 0.0) * (ig * xc)
        a, u = _group_scan(a, u, row8, reverse=(d == 1))
        a_ref[...] = a
        u_ref[...] = u

    if has_h0:
        init = tuple(h0_ref[s, dd:dd + 1, :] for s in range(nseq) for dd in range(2))
    else:
        init = tuple(jnp.zeros((1, cw_cols), F32) for _ in range(2 * nseq))

    def body(j, carry):
        out = []
        for s in range(nseq):
            rf = pl.multiple_of(s * seq_len + j * SUBLANES, SUBLANES)
            rb = pl.multiple_of(s * seq_len + (ngroups - 1 - j) * SUBLANES, SUBLANES)
            hf = uf_ref[pl.ds(rf, SUBLANES), :] + af_ref[pl.ds(rf, SUBLANES), :] * carry[2 * s]
            uf_ref[pl.ds(rf, SUBLANES), :] = hf
            hb = ub_ref[pl.ds(rb, SUBLANES), :] + ab_ref[pl.ds(rb, SUBLANES), :] * carry[2 * s + 1]
            ub_ref[pl.ds(rb, SUBLANES), :] = hb
            out += [hf[SUBLANES - 1:SUBLANES, :], hb[0:1, :]]
        return tuple(out)

    last = lax.fori_loop(0, ngroups, body, init)
    if emit_state:
        for s in range(nseq):
            st_ref[s, 0:1, :] = last[2 * s]
            st_ref[s, 1:2, :] = last[2 * s + 1]

    y = (uf_ref[...] + ub_ref[...]) * _gelu_tanh(yb)
    acc_ref[...] += _mm(y.astype(BF16), wo_ref[...])

    @pl.when(c == pl.num_programs(1) - 1)
    def _():
        o_ref[...] = x_ref[...] + _gate(m_ref, 1) * acc_ref[...]


def _lru_gate_weights(gate_w):
    nd, ng, nb, bw, _ = gate_w.shape
    per = LRU_CHUNK // bw
    nc = nb // per
    w = gate_w.reshape(nd * ng, nc, per, bw, bw)
    eye = jnp.eye(per, dtype=gate_w.dtype)
    w = w[:, :, :, :, None, :] * eye[None, None, :, None, :, None]
    w = jnp.transpose(w, (1, 2, 3, 0, 4, 5))
    return (0.5 * w).reshape(nc, per * bw, nd * ng * per * bw).astype(BF16)


def _lru(x, m, row_of_tile, norm_g, layer, w_in, conv_w, conv_b, gate_w, gate_b, lam, w_out,
         seq_len, h0, emit_state):
    t, d = x.shape
    r = w_out.shape[0]
    tm, cb = TOKEN_TILE, LRU_CHUNK
    assert t % tm == 0 and tm % seq_len == 0 and r % cb == 0 and seq_len % SUBLANES == 0
    assert seq_len & (seq_len - 1) == 0, "in-sequence position is taken with a bit mask"
    nc = r // cb
    nseq = tm // seq_len
    has_h0 = h0 is not None
    in_specs = [
        pl.BlockSpec((tm, d), lambda i, c: (i, 0)),
        pl.BlockSpec((None, None, 3 * N_SUB, d), lambda i, c: (layer, row_of_tile(i), 0, 0)),
        pl.BlockSpec((None, N_SUB, d), lambda i, c: (layer, 0, 0)),
        pl.BlockSpec((d, cb), lambda i, c: (0, c)),
        pl.BlockSpec((d, cb), lambda i, c: (0, nc + c)),
        pl.BlockSpec((conv_w.shape[0], cb), lambda i, c: (0, c)),
        pl.BlockSpec((1, cb), lambda i, c: (0, c)),
        pl.BlockSpec((None, cb, 4 * cb), lambda i, c: (c, 0, 0)),
        pl.BlockSpec((4, cb), lambda i, c: (0, c)),
        pl.BlockSpec((2, cb), lambda i, c: (0, c)),
        pl.BlockSpec((cb, d), lambda i, c: (c, 0)),
    ]
    args = [x, m, norm_g, w_in, w_in, conv_w, conv_b.reshape(1, r), _lru_gate_weights(gate_w),
            gate_b.reshape(4, r), lam, w_out]
    if has_h0:
        in_specs.append(pl.BlockSpec((nseq, 2, cb), lambda i, c: (i, 0, c)))
        args.append(h0)
    out_specs = [pl.BlockSpec((tm, d), lambda i, c: (i, 0))]
    out_shape = [jax.ShapeDtypeStruct((t, d), F32)]
    if emit_state:
        out_specs.append(pl.BlockSpec((nseq, 2, cb), lambda i, c: (i, 0, c)))
        out_shape.append(jax.ShapeDtypeStruct((t // seq_len, 2, r), F32))
    outs = pl.pallas_call(
        functools.partial(_lru_kernel, seq_len=seq_len, has_h0=has_h0, emit_state=emit_state,
                          conv_left=(conv_w.shape[0] - 1) // 2),
        grid=(t // tm, nc),
        in_specs=in_specs,
        out_specs=out_specs,
        out_shape=out_shape,
        scratch_shapes=[pltpu.VMEM((tm, d), BF16), pltpu.VMEM((tm, d), F32)]
        + [pltpu.VMEM((tm, cb), F32)] * 4
        + [pltpu.VMEM((nseq * (seq_len + 2 * SUBLANES), cb), F32)],
        compiler_params=_params("arbitrary", "arbitrary"),
        name="rglru",
    )(*args)
    return outs if emit_state else (outs[0], None)


def _rope(x, cos, sin_signed, lane):
    hd = x.shape[1]
    partner = jnp.where((lane & 32) == 0, pltpu.roll(x, hd - 32, 1), pltpu.roll(x, 32, 1))
    return x * cos + partner * sin_signed


def _attn_kernel(*refs, seq_len, q_block, past_len, rope, emit_kv, group):
    refs = list(refs)
    x_ref, m_ref, g_ref, wq_ref, wk_ref, wv_ref, qg_ref, kg_ref, wo_ref = refs[:9]
    pos = 9
    if rope:
        cos_ref, sin_ref = refs[pos:pos + 2]
        pos += 2
    if past_len:
        ck_ref, cv_ref = refs[pos:pos + 2]
        pos += 2
    o_ref = refs[pos]
    pos += 1
    if emit_kv:
        kn_ref, vn_ref = refs[pos:pos + 2]
        pos += 2
    h_ref, acc_ref, q_s, k_s, v_s, o_s = refs[pos:]

    gi = pl.program_id(1)
    tm = x_ref.shape[0]
    hd = k_s.shape[1]
    nqb = seq_len // q_block
    nchunks = (tm // seq_len) * nqb
    nk = past_len + seq_len
    scale = hd ** -0.5

    @pl.when(gi == 0)
    def _():
        h_ref[...] = _sub_in(x_ref[...], m_ref, g_ref, 1).astype(BF16)
        acc_ref[...] = jnp.zeros_like(acc_ref)

    h = h_ref[...]
    q = _mm(h, wq_ref[...])
    k = _rms(_mm(h, wk_ref[...])) * kg_ref[...]
    v = _mm(h, wv_ref[...])
    if emit_kv:
        kn_ref[...] = k
        vn_ref[...] = v
    if rope:
        lane = lax.broadcasted_iota(jnp.int32, (tm, hd), 1)
        cos, sin = cos_ref[...], sin_ref[...]
        k = _rope(k, cos, sin, lane)
    k_s[past_len:past_len + tm, :] = k.astype(BF16)
    v_s[past_len:past_len + tm, :] = v.astype(BF16)
    if past_len:
        k_s[0:past_len, :] = ck_ref[...].astype(BF16)
        v_s[0:past_len, :] = cv_ref[...].astype(BF16)
    for j in range(group):
        qh = _rms(q[:, j * hd:(j + 1) * hd]) * qg_ref[...]
        if rope:
            qh = _rope(qh, cos, sin, lane)
        q_s[:, j * hd:(j + 1) * hd] = qh.astype(BF16)

    def chunk(ci, carry):
        r0 = pl.multiple_of(ci * q_block, q_block)
        if past_len:
            keys, vals = k_s[...], v_s[...]
        else:
            koff = pl.multiple_of((ci // nqb) * seq_len, seq_len)
            keys, vals = k_s[pl.ds(koff, nk), :], v_s[pl.ds(koff, nk), :]
        qc = jnp.concatenate([q_s[pl.ds(r0, q_block), j * hd:(j + 1) * hd] for j in range(group)],
                             axis=0)
        s = lax.dot_general(qc, keys, (((1,), (1,)), ((), ())), preferred_element_type=F32) * scale
        e = jnp.exp(s - jnp.max(s, axis=-1, keepdims=True))
        p = e * (1.0 / jnp.sum(e, axis=-1, keepdims=True))
        oc = jnp.dot(p.astype(BF16), vals, preferred_element_type=F32)
        for j in range(group):
            o_s[pl.ds(r0, q_block), j * hd:(j + 1) * hd] = oc[j * q_block:(j + 1) * q_block, :].astype(BF16)
        return carry

    lax.fori_loop(0, nchunks, chunk, 0, unroll=True)
    acc_ref[...] += _mm(o_s[...], wo_ref[...])

    @pl.when(gi == pl.num_programs(1) - 1)
    def _():
        o_ref[...] = x_ref[...] + _gate(m_ref, 1) * acc_ref[...]


def _rope_tables(n_tok, hd):
    rows = n_tok // GRID_W
    r_idx = jnp.broadcast_to(jnp.arange(rows)[:, None], (rows, GRID_W)).reshape(n_tok).astype(F32)
    c_idx = jnp.broadcast_to(jnp.arange(GRID_W)[None, :], (rows, GRID_W)).reshape(n_tok).astype(F32)
    n_freq = hd // 4
    inv = ROPE_THETA ** (-jnp.arange(n_freq, dtype=F32) / n_freq)
    ang = jnp.stack([r_idx[:, None] * inv, c_idx[:, None] * inv], axis=1)
    cos, sin = jnp.cos(ang), jnp.sin(ang)
    cos_full = jnp.concatenate([cos, cos], axis=-1).reshape(n_tok, hd)
    sin_signed = jnp.concatenate([-sin, sin], axis=-1).reshape(n_tok, hd)
    return cos_full, sin_signed


def _attn(x, m, row_of_tile, norm_g, layer, w_qkv, q_g, k_g, w_o, seq_len, q_block,
          cache_k=None, cache_v=None, rope=False, emit_kv=False):
    t, d = x.shape
    hd, kvh = HEAD_DIM, N_KV_HEADS
    n_heads = w_o.shape[0] // hd
    group = n_heads // kvh
    gw = group * hd
    tm = TOKEN_TILE
    assert t % tm == 0 and tm % seq_len == 0 and seq_len % q_block == 0
    past_len = 0 if cache_k is None else cache_k.shape[1]
    assert past_len == 0 or tm == seq_len
    in_specs = [
        pl.BlockSpec((tm, d), lambda i, g: (i, 0)),
        pl.BlockSpec((None, None, 3 * N_SUB, d), lambda i, g: (layer, row_of_tile(i), 0, 0)),
        pl.BlockSpec((None, N_SUB, d), lambda i, g: (layer, 0, 0)),
        pl.BlockSpec((d, gw), lambda i, g: (0, g)),
        pl.BlockSpec((d, hd), lambda i, g: (0, n_heads + g)),
        pl.BlockSpec((d, hd), lambda i, g: (0, n_heads + kvh + g)),
        pl.BlockSpec((1, hd), lambda i, g: (0, 0)),
        pl.BlockSpec((1, hd), lambda i, g: (0, 0)),
        pl.BlockSpec((gw, d), lambda i, g: (g, 0)),
    ]
    args = [x, m, norm_g, w_qkv, w_qkv, w_qkv, q_g.reshape(1, hd), k_g.reshape(1, hd), w_o]
    if rope:
        assert tm == seq_len
        cos, sin = _rope_tables(seq_len, hd)
        in_specs += [pl.BlockSpec((tm, hd), lambda i, g: (0, 0))] * 2
        args += [cos, sin]
    if past_len:
        in_specs += [pl.BlockSpec((None, past_len, hd), lambda i, g: (i, 0, g))] * 2
        args += [cache_k, cache_v]
    out_specs = [pl.BlockSpec((tm, d), lambda i, g: (i, 0))]
    out_shape = [jax.ShapeDtypeStruct((t, d), F32)]
    if emit_kv:
        out_specs += [pl.BlockSpec((tm, hd), lambda i, g: (i, g))] * 2
        out_shape += [jax.ShapeDtypeStruct((t, kvh * hd), F32)] * 2
    nkeys = past_len + tm
    outs = pl.pallas_call(
        functools.partial(_attn_kernel, seq_len=seq_len, q_block=q_block, past_len=past_len,
                          rope=rope, emit_kv=emit_kv, group=group),
        grid=(t // tm, kvh),
        in_specs=in_specs,
        out_specs=out_specs,
        out_shape=out_shape,
        scratch_shapes=[pltpu.VMEM((tm, d), BF16), pltpu.VMEM((tm, d), F32),
                        pltpu.VMEM((tm, gw), BF16), pltpu.VMEM((nkeys, hd), BF16),
                        pltpu.VMEM((nkeys, hd), BF16), pltpu.VMEM((tm, gw), BF16)],
        compiler_params=_params("arbitrary", "arbitrary"),
        name="gqa",
    )(*args)
    return outs


def kernel(x_prompt, x_sample, c, state_lru, cache_k, cache_v, c_ctx, mod_w, mod_b, norm_g,
           ffn_w_gu, ffn_w_down, lru_w_in, lru_conv_w, lru_conv_b, lru_gate_w, lru_gate_b,
           lru_lambda, lru_w_out, att_w_qkv, att_q_g, att_k_g, att_w_o, final_g):
    b, s, d = x_prompt.shape
    db, ds, _ = x_sample.shape
    depth = mod_w.shape[0]
    n_mixers = 2
    assert ds % TOKEN_TILE == 0 and 1 + db <= SUBLANES

    xp = x_prompt.reshape(b * s, d)
    xs = x_sample.reshape(db * ds, d)
    cvecs = jnp.concatenate([c_ctx[None], c, jnp.zeros((SUBLANES - 1 - db, d), F32)], axis=0)
    m = _modulation(cvecs, mod_w, mod_b).reshape(depth, SUBLANES, 3 * N_SUB, d)

    tiles_per_sample = ds // TOKEN_TILE
    prompt_row = lambda i: 0
    sample_row = lambda i: 1 + i // tiles_per_sample
    prompt_tiles = (b * s) // TOKEN_TILE
    both_row = lambda i: jnp.where(i < prompt_tiles, 0, 1 + (i - prompt_tiles) // tiles_per_sample)

    new_states, new_k, new_v = [], [], []
    for layer in range(depth):
        j = layer // n_mixers
        last = layer == depth - 1
        xp, xs = _ffn(xp, xs, m, both_row, norm_g, ffn_w_gu, ffn_w_down, layer, 0, 0)
        if layer % n_mixers == 0:
            lru_p = (lru_w_in[j], lru_conv_w[j], lru_conv_b[j], lru_gate_w[j], lru_gate_b[j],
                     lru_lambda[j], lru_w_out[j])
            xp, st = _lru(xp, m, prompt_row, norm_g, layer, *lru_p, seq_len=s, h0=None,
                          emit_state=True)
            new_states.append(st)
            xs, _ = _lru(xs, m, sample_row, norm_g, layer, *lru_p, seq_len=ds,
                         h0=state_lru[:, j], emit_state=False)
        else:
            att_p = (att_w_qkv[j], att_q_g[j], att_k_g[j], att_w_o[j])
            xp, kp, vp = _attn(xp, m, prompt_row, norm_g, layer, *att_p, seq_len=s, q_block=s,
                               emit_kv=True)
            new_k.append(kp.reshape(b, s, N_KV_HEADS, HEAD_DIM))
            new_v.append(vp.reshape(b, s, N_KV_HEADS, HEAD_DIM))
            past = cache_k.shape[2]
            ck = cache_k[:, j].reshape(db, past, N_KV_HEADS * HEAD_DIM)
            cv = cache_v[:, j].reshape(db, past, N_KV_HEADS * HEAD_DIM)
            (xs,) = _attn(xs, m, sample_row, norm_g, layer, *att_p, seq_len=ds, q_block=128,
                          cache_k=ck, cache_v=cv, rope=True)
        fg = final_g if last else None
        xp, xs = _ffn(xp, xs, m, both_row, norm_g, ffn_w_gu, ffn_w_down, layer, 1, 2, final_g=fg)

    y_prompt = xp.reshape(b, s, d)
    y_sample = xs.reshape(db, ds, d)
    return (y_prompt, y_sample, jnp.stack(new_states, axis=1), jnp.stack(new_k, axis=1),
            jnp.stack(new_v, axis=1))
```

```python
import functools

import jax
import jax.numpy as jnp
import numpy as np
from jax import lax
from jax.experimental import pallas as pl
from jax.experimental.pallas import tpu as pltpu

F32 = jnp.float32
BF16 = jnp.bfloat16

EPS = 1e-6
LRU_C = 8.0
LOG2_E = 1.4426950408889634
GRID_W = 64
ROPE_THETA = 10000.0
N_SUB = 3
HEAD_DIM = 128
N_KV_HEADS = 2
LRU_BLOCKS = 16

V7X_VMEM_LIMIT_BYTES = 56 * 1024 * 1024
SUBLANES = 8
TOKEN_TILE = 1024
FF_CHUNK = 256
FFN_ROW_BLOCK = 256
FFN_STAGE_SLOTS = 3
LRU_CHUNK = 256
MOD_CHUNK = 1536


def _params(*semantics):
    return pltpu.CompilerParams(dimension_semantics=semantics,
                                vmem_limit_bytes=V7X_VMEM_LIMIT_BYTES)


def _mm(a_bf16, w_f32):
    return jnp.dot(a_bf16, w_f32.astype(BF16), preferred_element_type=F32)


def _rms(x):
    return x * lax.rsqrt(jnp.mean(x * x, axis=-1, keepdims=True) + EPS)


def _sub_in(x, m_ref, g_ref, sidx):
    shift = m_ref[3 * sidx:3 * sidx + 1, :]
    scale = m_ref[3 * sidx + 1:3 * sidx + 2, :]
    return (_rms(x) * g_ref[sidx:sidx + 1, :]) * (1.0 + scale) + shift


def _gate(m_ref, sidx):
    return m_ref[3 * sidx + 2:3 * sidx + 3, :]


def _mod_kernel(cv_ref, w_ref, b_ref, o_ref):
    cv = cv_ref[...]
    act = (cv * jax.nn.sigmoid(cv)).astype(BF16)
    o_ref[...] = _mm(act, w_ref[...]) + b_ref[...]


def _modulation(cvecs, mod_w, mod_b):
    depth, d, n = mod_w.shape
    rows = cvecs.shape[0]
    tn = MOD_CHUNK
    assert n % tn == 0
    return pl.pallas_call(
        _mod_kernel,
        grid=(depth, n // tn),
        in_specs=[
            pl.BlockSpec((rows, d), lambda l, j: (0, 0)),
            pl.BlockSpec((None, d, tn), lambda l, j: (l, 0, j)),
            pl.BlockSpec((None, 1, tn), lambda l, j: (l, 0, j)),
        ],
        out_specs=pl.BlockSpec((None, rows, tn), lambda l, j: (l, 0, j)),
        out_shape=jax.ShapeDtypeStruct((depth, rows, n), F32),
        compiler_params=_params("arbitrary", "arbitrary"),
        name="modulation",
    )(cvecs, mod_w, mod_b.reshape(depth, 1, n))


def _ffn_kernel(*refs, layer, s, sidx, final, n_first):
    if final:
        (m_ref, g_ref, fg_ref, xa_hbm, xb_hbm, wgu_hbm, wd_hbm, oa_hbm, ob_hbm,
         wgu_res, wd_res, act_ref, h0_ref, act0_ref, gu_buf, wd_buf, xbuf, obuf,
         w_sem, xsem, osem) = refs
    else:
        (m_ref, g_ref, xa_hbm, xb_hbm, wgu_hbm, wd_hbm, oa_hbm, ob_hbm,
         wgu_res, wd_res, act_ref, h0_ref, act0_ref, gu_buf, wd_buf, xbuf, obuf,
         w_sem, xsem, osem) = refs
    i = pl.program_id(0)
    n = pl.num_programs(0)
    tm = xbuf.shape[1]
    _, d, tf = wgu_res.shape
    f = wd_res.shape[0]
    nk = f // tf
    rb = act_ref.shape[0]
    nslot = gu_buf.shape[0]
    ahead = nslot - 1
    slot = lax.rem(i, 2)
    other = 1 - slot

    def stage_copies(k, sl):
        gcols = pl.ds(pl.multiple_of(k * tf, tf), tf)
        ucols = pl.ds(pl.multiple_of(f + k * tf, tf), tf)
        return (
            pltpu.make_async_copy(wgu_hbm.at[layer, s, :, gcols], gu_buf.at[sl, 0], w_sem.at[sl, 0]),
            pltpu.make_async_copy(wgu_hbm.at[layer, s, :, ucols], gu_buf.at[sl, 1], w_sem.at[sl, 1]),
            pltpu.make_async_copy(wd_hbm.at[layer, s, gcols, :], wd_buf.at[sl], w_sem.at[sl, 2]),
        )

    def tile_rows(hbm, tile):
        return hbm.at[pl.ds(pl.multiple_of(tile * tm, tm), tm), :]

    def x_copy(which, tile, sl):
        return pltpu.make_async_copy(tile_rows((xa_hbm, xb_hbm)[which], tile), xbuf.at[sl],
                                     xsem.at[sl])

    def o_copy(which, tile, sl):
        return pltpu.make_async_copy(obuf.at[sl], tile_rows((oa_hbm, ob_hbm)[which], tile),
                                     osem.at[sl])

    def start_by_stream(make, tile, sl):
        @pl.when(tile < n_first)
        def _():
            make(0, tile, sl).start()

        @pl.when(tile >= n_first)
        def _():
            make(1, tile - n_first, sl).start()

    @pl.when(i == 0)
    def _():
        x_copy(0, 0, 0).start()
        for k in range(ahead):
            for cp in stage_copies(k, k):
                cp.start()

    @pl.when(i + 1 < n)
    def _():
        start_by_stream(x_copy, i + 1, other)

    x_copy(0, 0, slot).wait()

    @pl.when(i >= 2)
    def _():
        o_copy(0, 0, slot).wait()

    def swiglu(h, wg, wu):
        gt = jnp.dot(h, wg, preferred_element_type=F32)
        up = jnp.dot(h, wu, preferred_element_type=F32)
        return ((gt * jax.nn.sigmoid(gt)) * up).astype(BF16)

    def down_and_store(x, rows):
        ff = jnp.dot(act_ref[...], wd_res[...], preferred_element_type=F32)
        y = x + (0.5 * _gate(m_ref, sidx)) * ff
        if final:
            y = _rms(y) * fg_ref[...]
        obuf[slot, rows, :] = y

    def row_block(r, carry):
        rows = pl.ds(pl.multiple_of(r * rb, rb), rb)
        x = xbuf[slot, rows, :]
        h = _sub_in(x, m_ref, g_ref, sidx).astype(BF16)
        for k in range(nk):
            act_ref[:, k * tf:(k + 1) * tf] = swiglu(h, wgu_res[k], wgu_res[nk + k])
        down_and_store(x, rows)
        return carry

    def first_tile_chunk(k, carry):
        sl = lax.rem(k, nslot)
        for cp in stage_copies(k, sl):
            cp.wait()

        @pl.when(k + ahead < nk)
        def _():
            for cp in stage_copies(k + ahead, lax.rem(k + ahead, nslot)):
                cp.start()

        wg = gu_buf[sl, 0].astype(BF16)
        wu = gu_buf[sl, 1].astype(BF16)
        wgu_res[k] = wg
        wgu_res[nk + k] = wu
        wd_res[pl.ds(pl.multiple_of(k * tf, tf), tf), :] = wd_buf[sl].astype(BF16)
        act0_ref[k] = swiglu(h0_ref[...], wg, wu)
        return carry

    def first_tile_down(r, carry):
        rows = pl.ds(pl.multiple_of(r * rb, rb), rb)
        for k in range(nk):
            act_ref[:, k * tf:(k + 1) * tf] = act0_ref[k, rows, :]
        down_and_store(xbuf[slot, rows, :], rows)
        return carry

    @pl.when(i == 0)
    def _():
        h0_ref[...] = _sub_in(xbuf[slot], m_ref, g_ref, sidx).astype(BF16)
        lax.fori_loop(0, nk, first_tile_chunk, 0)
        lax.fori_loop(0, tm // rb, first_tile_down, 0)

    @pl.when(i > 0)
    def _():
        lax.fori_loop(0, tm // rb, row_block, 0, unroll=2)

    start_by_stream(o_copy, i, slot)

    @pl.when(i == n - 1)
    def _():
        o_copy(0, 0, other).wait()
        o_copy(0, 0, slot).wait()


def _ffn(xa, xb, m, row_of_tile, norm_g, w_gu, w_down, layer, s, sidx, final_g=None):
    (ta, d), tb = xa.shape, xb.shape[0]
    f = w_down.shape[2]
    tm, tf, rb = TOKEN_TILE, FF_CHUNK, FFN_ROW_BLOCK
    assert ta % tm == 0 and tb % tm == 0 and f % tf == 0 and (ta + tb) // tm >= 2
    assert tm % rb == 0 and f // tf >= FFN_STAGE_SLOTS
    final = final_g is not None
    in_specs = [
        pl.BlockSpec((None, None, 3 * N_SUB, d), lambda i: (layer, row_of_tile(i), 0, 0)),
        pl.BlockSpec((None, N_SUB, d), lambda i: (layer, 0, 0)),
    ]
    args = [m, norm_g]
    if final:
        in_specs.append(pl.BlockSpec((1, d), lambda i: (0, 0)))
        args.append(final_g.reshape(1, d))
    in_specs += [pl.BlockSpec(memory_space=pl.ANY)] * 4
    args += [xa, xb, w_gu, w_down]
    return pl.pallas_call(
        functools.partial(_ffn_kernel, layer=layer, s=s, sidx=sidx, final=final,
                          n_first=ta // tm),
        grid=((ta + tb) // tm,),
        in_specs=in_specs,
        out_specs=[pl.BlockSpec(memory_space=pl.ANY)] * 2,
        out_shape=[jax.ShapeDtypeStruct((ta, d), F32), jax.ShapeDtypeStruct((tb, d), F32)],
        scratch_shapes=[
            pltpu.VMEM((2 * f // tf, d, tf), BF16), pltpu.VMEM((f, d), BF16),
            pltpu.VMEM((rb, f), BF16),
            pltpu.VMEM((tm, d), BF16), pltpu.VMEM((f // tf, tm, tf), BF16),
            pltpu.VMEM((FFN_STAGE_SLOTS, 2, d, tf), F32),
            pltpu.VMEM((FFN_STAGE_SLOTS, tf, d), F32),
            pltpu.VMEM((2, tm, d), F32), pltpu.VMEM((2, tm, d), F32),
            pltpu.SemaphoreType.DMA((FFN_STAGE_SLOTS, 3)),
            pltpu.SemaphoreType.DMA((2,)),
            pltpu.SemaphoreType.DMA((2,)),
        ],
        compiler_params=_params("arbitrary"),
        name="ffn",
    )(*args)


def _gelu_tanh(x):
    c = np.float32(np.sqrt(2.0 / np.pi))
    return x * (0.5 * (1.0 + jnp.tanh(c * (x + 0.044715 * (x * x * x)))))


def _group_scan(a, u, row8, reverse):
    rows, cols = a.shape
    grouped = (rows // SUBLANES, SUBLANES, cols)
    a, u, row8 = a.reshape(grouped), u.reshape(grouped), row8.reshape(grouped)
    for s in (1, 2, 4):
        if reverse:
            shift, valid = SUBLANES - s, row8 < SUBLANES - s
        else:
            shift, valid = s, row8 >= s
        u = jnp.where(valid, u + a * pltpu.roll(u, shift, 1), u)
        a = jnp.where(valid, a * pltpu.roll(a, shift, 1), a)
    return a.reshape(rows, cols), u.reshape(rows, cols)


def _lru_kernel(*refs, seq_len, has_h0, emit_state, conv_left):
    refs = list(refs)
    (x_ref, m_ref, g_ref, wx_ref, wy_ref, cw_ref, cb_ref, gw_ref, gb_ref, lam_ref,
     wo_ref) = refs[:11]
    pos = 11
    h0_ref = None
    if has_h0:
        h0_ref = refs[pos]
        pos += 1
    o_ref = refs[pos]
    pos += 1
    st_ref = None
    if emit_state:
        st_ref = refs[pos]
        pos += 1
    h_ref, acc_ref, af_ref, uf_ref, ab_ref, ub_ref, xpad_ref = refs[pos:]

    c = pl.program_id(1)
    tm, cw_cols = af_ref.shape
    nseq = tm // seq_len
    ngroups = seq_len // SUBLANES
    pitch = seq_len + 2 * SUBLANES

    @pl.when(c == 0)
    def _():
        h_ref[...] = _sub_in(x_ref[...], m_ref, g_ref, 1).astype(BF16)
        acc_ref[...] = jnp.zeros_like(acc_ref)

    h = h_ref[...]
    xb = _mm(h, wx_ref[...])
    yb = _mm(h, wy_ref[...])

    zeros = jnp.zeros((SUBLANES, cw_cols), F32)
    for sq in range(nseq):
        base = sq * pitch
        xpad_ref[base:base + SUBLANES, :] = zeros
        xpad_ref[base + SUBLANES:base + SUBLANES + seq_len, :] = xb[sq * seq_len:(sq + 1) * seq_len, :]
        xpad_ref[base + SUBLANES + seq_len:base + pitch, :] = zeros
    cw = cw_ref[...]
    xc = cb_ref[...]
    for k in range(cw.shape[0]):
        off = k - conv_left
        assert abs(off) <= SUBLANES
        if off == 0:
            tap = xb
        else:
            tap = jnp.concatenate(
                [xpad_ref[sq * pitch + SUBLANES + off:sq * pitch + SUBLANES + off + seq_len, :]
                 for sq in range(nseq)], axis=0)
        xc = xc + tap * cw[k:k + 1, :]

    gl = jnp.dot(xc.astype(BF16), gw_ref[...], preferred_element_type=F32)
    gb = gb_ref[...]
    lam = lam_ref[...]
    row8 = lax.broadcasted_iota(jnp.int32, (tm, cw_cols), 0) & (SUBLANES - 1)
    for d, (a_ref, u_ref) in enumerate(((af_ref, uf_ref), (ab_ref, ub_ref))):
        tr = jnp.tanh(gl[:, (2 * d) * cw_cols:(2 * d + 1) * cw_cols] + 0.5 * gb[2 * d:2 * d + 1, :])
        ig = 0.5 * jnp.tanh(gl[:, (2 * d + 1) * cw_cols:(2 * d + 2) * cw_cols]
                            + 0.5 * gb[2 * d + 1:2 * d + 2, :]) + 0.5
        lm = lam[d:d + 1, :]
        log_sig = jnp.minimum(lm, 0.0) - jnp.log1p(jnp.exp(-jnp.abs(lm)))
        c2 = (0.5 * LRU_C * LOG2_E) * log_sig
        a = jnp.exp2(c2 * tr + c2)
        v = 1.0 - a * a
        u = jnp.where(v > 0.0, v * lax.rsqrt(v), 0.0) * (ig * xc)
        a, u = _group_scan(a, u, row8, reverse=(d == 1))
        a_ref[...] = a
        u_ref[...] = u

    if has_h0:
        init = tuple(h0_ref[s, dd:dd + 1, :] for s in range(nseq) for dd in range(2))
    else:
        init = tuple(jnp.zeros((1, cw_cols), F32) for _ in range(2 * nseq))

    def body(j, carry):
        out = []
        for s in range(nseq):
            rf = pl.multiple_of(s * seq_len + j * SUBLANES, SUBLANES)
            rb = pl.multiple_of(s * seq_len + (ngroups - 1 - j) * SUBLANES, SUBLANES)
            hf = uf_ref[pl.ds(rf, SUBLANES), :] + af_ref[pl.ds(rf, SUBLANES), :] * carry[2 * s]
            uf_ref[pl.ds(rf, SUBLANES), :] = hf
            hb = ub_ref[pl.ds(rb, SUBLANES), :] + ab_ref[pl.ds(rb, SUBLANES), :] * carry[2 * s + 1]
            ub_ref[pl.ds(rb, SUBLANES), :] = hb
            out += [hf[SUBLANES - 1:SUBLANES, :], hb[0:1, :]]
        return tuple(out)

    last = lax.fori_loop(0, ngroups, body, init)
    if emit_state:
        for s in range(nseq):
            st_ref[s, 0:1, :] = last[2 * s]
            st_ref[s, 1:2, :] = last[2 * s + 1]

    y = (uf_ref[...] + ub_ref[...]) * _gelu_tanh(yb)
    acc_ref[...] += _mm(y.astype(BF16), wo_ref[...])

    @pl.when(c == pl.num_programs(1) - 1)
    def _():
        o_ref[...] = x_ref[...] + _gate(m_ref, 1) * acc_ref[...]


def _lru_gate_weights(gate_w):
    nd, ng, nb, bw, _ = gate_w.shape
    per = LRU_CHUNK // bw
    nc = nb // per
    w = (0.5 * gate_w).astype(BF16).reshape(nd * ng, nc, per, bw, bw)
    rows = []
    for n in range(per):
        blk = jnp.transpose(w[:, :, n], (1, 2, 0, 3))
        blk = jnp.pad(blk, ((0, 0), (0, 0), (0, 0), (n * bw, (per - 1 - n) * bw)))
        rows.append(blk.reshape(nc, bw, nd * ng * per * bw))
    return jnp.concatenate(rows, axis=1)


def _lru(x, m, row_of_tile, norm_g, layer, w_in, conv_w, conv_b, gate_w, gate_b, lam, w_out,
         seq_len, h0, emit_state):
    t, d = x.shape
    r = w_out.shape[0]
    tm, cb = TOKEN_TILE, LRU_CHUNK
    assert t % tm == 0 and tm % seq_len == 0 and r % cb == 0 and seq_len % SUBLANES == 0
    assert seq_len & (seq_len - 1) == 0, "in-sequence position is taken with a bit mask"
    nc = r // cb
    nseq = tm // seq_len
    has_h0 = h0 is not None
    in_specs = [
        pl.BlockSpec((tm, d), lambda i, c: (i, 0)),
        pl.BlockSpec((None, None, 3 * N_SUB, d), lambda i, c: (layer, row_of_tile(i), 0, 0)),
        pl.BlockSpec((None, N_SUB, d), lambda i, c: (layer, 0, 0)),
        pl.BlockSpec((d, cb), lambda i, c: (0, c)),
        pl.BlockSpec((d, cb), lambda i, c: (0, nc + c)),
        pl.BlockSpec((conv_w.shape[0], cb), lambda i, c: (0, c)),
        pl.BlockSpec((1, cb), lambda i, c: (0, c)),
        pl.BlockSpec((None, cb, 4 * cb), lambda i, c: (c, 0, 0)),
        pl.BlockSpec((4, cb), lambda i, c: (0, c)),
        pl.BlockSpec((2, cb), lambda i, c: (0, c)),
        pl.BlockSpec((cb, d), lambda i, c: (c, 0)),
    ]
    args = [x, m, norm_g, w_in, w_in, conv_w, conv_b.reshape(1, r), _lru_gate_weights(gate_w),
            gate_b.reshape(4, r), lam, w_out]
    if has_h0:
        in_specs.append(pl.BlockSpec((nseq, 2, cb), lambda i, c: (i, 0, c)))
        args.append(h0)
    out_specs = [pl.BlockSpec((tm, d), lambda i, c: (i, 0))]
    out_shape = [jax.ShapeDtypeStruct((t, d), F32)]
    if emit_state:
        out_specs.append(pl.BlockSpec((nseq, 2, cb), lambda i, c: (i, 0, c)))
        out_shape.append(jax.ShapeDtypeStruct((t // seq_len, 2, r), F32))
    outs = pl.pallas_call(
        functools.partial(_lru_kernel, seq_len=seq_len, has_h0=has_h0, emit_state=emit_state,
                          conv_left=(conv_w.shape[0] - 1) // 2),
        grid=(t // tm, nc),
        in_specs=in_specs,
        out_specs=out_specs,
        out_shape=out_shape,
        scratch_shapes=[pltpu.VMEM((tm, d), BF16), pltpu.VMEM((tm, d), F32)]
        + [pltpu.VMEM((tm, cb), F32)] * 4
        + [pltpu.VMEM((nseq * (seq_len + 2 * SUBLANES), cb), F32)],
        compiler_params=_params("arbitrary", "arbitrary"),
        name="rglru",
    )(*args)
    return outs if emit_state else (outs[0], None)


def _rope(x, cos, sin_signed, lane):
    hd = x.shape[1]
    partner = jnp.where((lane & 32) == 0, pltpu.roll(x, hd - 32, 1), pltpu.roll(x, 32, 1))
    return x * cos + partner * sin_signed


def _attn_kernel(*refs, seq_len, q_block, past_len, rope, emit_kv, group):
    refs = list(refs)
    x_ref, m_ref, g_ref, wq_ref, wk_ref, wv_ref, qg_ref, kg_ref, wo_ref = refs[:9]
    pos = 9
    if rope:
        cos_ref, sin_ref = refs[pos:pos + 2]
        pos += 2
    if past_len:
        ck_ref, cv_ref = refs[pos:pos + 2]
        pos += 2
    o_ref = refs[pos]
    pos += 1
    if emit_kv:
        kn_ref, vn_ref = refs[pos:pos + 2]
        pos += 2
    h_ref, acc_ref, q_s, k_s, v_s, o_s = refs[pos:]

    gi = pl.program_id(1)
    tm = x_ref.shape[0]
    hd = k_s.shape[1]
    nqb = seq_len // q_block
    nchunks = (tm // seq_len) * nqb
    nk = past_len + seq_len
    scale = hd ** -0.5

    @pl.when(gi == 0)
    def _():
        h_ref[...] = _sub_in(x_ref[...], m_ref, g_ref, 1).astype(BF16)
        acc_ref[...] = jnp.zeros_like(acc_ref)

    h = h_ref[...]
    q = _mm(h, wq_ref[...])
    k = _rms(_mm(h, wk_ref[...])) * kg_ref[...]
    v = _mm(h, wv_ref[...])
    if emit_kv:
        kn_ref[...] = k
        vn_ref[...] = v
    if rope:
        lane = lax.broadcasted_iota(jnp.int32, (tm, hd), 1)
        cos, sin = cos_ref[...], sin_ref[...]
        k = _rope(k, cos, sin, lane)
    k_s[past_len:past_len + tm, :] = k.astype(BF16)
    v_s[past_len:past_len + tm, :] = v.astype(BF16)
    if past_len:
        k_s[0:past_len, :] = ck_ref[...].astype(BF16)
        v_s[0:past_len, :] = cv_ref[...].astype(BF16)
    for j in range(group):
        qh = _rms(q[:, j * hd:(j + 1) * hd]) * qg_ref[...]
        if rope:
            qh = _rope(qh, cos, sin, lane)
        q_s[:, j * hd:(j + 1) * hd] = qh.astype(BF16)

    def chunk(ci, carry):
        r0 = pl.multiple_of(ci * q_block, q_block)
        if past_len:
            keys, vals = k_s[...], v_s[...]
        else:
            koff = pl.multiple_of((ci // nqb) * seq_len, seq_len)
            keys, vals = k_s[pl.ds(koff, nk), :], v_s[pl.ds(koff, nk), :]
        qc = jnp.concatenate([q_s[pl.ds(r0, q_block), j * hd:(j + 1) * hd] for j in range(group)],
                             axis=0)
        s = lax.dot_general(qc, keys, (((1,), (1,)), ((), ())), preferred_element_type=F32) * scale
        e = jnp.exp(s - jnp.max(s, axis=-1, keepdims=True))
        p = e * (1.0 / jnp.sum(e, axis=-1, keepdims=True))
        oc = jnp.dot(p.astype(BF16), vals, preferred_element_type=F32)
        for j in range(group):
            o_s[pl.ds(r0, q_block), j * hd:(j + 1) * hd] = oc[j * q_block:(j + 1) * q_block, :].astype(BF16)
        return carry

    lax.fori_loop(0, nchunks, chunk, 0, unroll=True)
    acc_ref[...] += _mm(o_s[...], wo_ref[...])

    @pl.when(gi == pl.num_programs(1) - 1)
    def _():
        o_ref[...] = x_ref[...] + _gate(m_ref, 1) * acc_ref[...]


def _rope_tables(n_tok, hd):
    rows = n_tok // GRID_W
    r_idx = jnp.broadcast_to(jnp.arange(rows)[:, None], (rows, GRID_W)).reshape(n_tok).astype(F32)
    c_idx = jnp.broadcast_to(jnp.arange(GRID_W)[None, :], (rows, GRID_W)).reshape(n_tok).astype(F32)
    n_freq = hd // 4
    inv = ROPE_THETA ** (-jnp.arange(n_freq, dtype=F32) / n_freq)
    ang = jnp.stack([r_idx[:, None] * inv, c_idx[:, None] * inv], axis=1)
    cos, sin = jnp.cos(ang), jnp.sin(ang)
    cos_full = jnp.concatenate([cos, cos], axis=-1).reshape(n_tok, hd)
    sin_signed = jnp.concatenate([-sin, sin], axis=-1).reshape(n_tok, hd)
    return cos_full, sin_signed


def _attn(x, m, row_of_tile, norm_g, layer, w_qkv, q_g, k_g, w_o, seq_len, q_block,
          cache_k=None, cache_v=None, rope=False, emit_kv=False):
    t, d = x.shape
    hd, kvh = HEAD_DIM, N_KV_HEADS
    n_heads = w_o.shape[0] // hd
    group = n_heads // kvh
    gw = group * hd
    tm = TOKEN_TILE
    assert t % tm == 0 and tm % seq_len == 0 and seq_len % q_block == 0
    past_len = 0 if cache_k is None else cache_k.shape[1]
    assert past_len == 0 or tm == seq_len
    in_specs = [
        pl.BlockSpec((tm, d), lambda i, g: (i, 0)),
        pl.BlockSpec((None, None, 3 * N_SUB, d), lambda i, g: (layer, row_of_tile(i), 0, 0)),
        pl.BlockSpec((None, N_SUB, d), lambda i, g: (layer, 0, 0)),
        pl.BlockSpec((d, gw), lambda i, g: (0, g)),
        pl.BlockSpec((d, hd), lambda i, g: (0, n_heads + g)),
        pl.BlockSpec((d, hd), lambda i, g: (0, n_heads + kvh + g)),
        pl.BlockSpec((1, hd), lambda i, g: (0, 0)),
        pl.BlockSpec((1, hd), lambda i, g: (0, 0)),
        pl.BlockSpec((gw, d), lambda i, g: (g, 0)),
    ]
    args = [x, m, norm_g, w_qkv, w_qkv, w_qkv, q_g.reshape(1, hd), k_g.reshape(1, hd), w_o]
    if rope:
        assert tm == seq_len
        cos, sin = _rope_tables(seq_len, hd)
        in_specs += [pl.BlockSpec((tm, hd), lambda i, g: (0, 0))] * 2
        args += [cos, sin]
    if past_len:
        in_specs += [pl.BlockSpec((None, past_len, hd), lambda i, g: (i, 0, g))] * 2
        args += [cache_k, cache_v]
    out_specs = [pl.BlockSpec((tm, d), lambda i, g: (i, 0))]
    out_shape = [jax.ShapeDtypeStruct((t, d), F32)]
    if emit_kv:
        out_specs += [pl.BlockSpec((tm, hd), lambda i, g: (i, g))] * 2
        out_shape += [jax.ShapeDtypeStruct((t, kvh * hd), F32)] * 2
    nkeys = past_len + tm
    outs = pl.pallas_call(
        functools.partial(_attn_kernel, seq_len=seq_len, q_block=q_block, past_len=past_len,
                          rope=rope, emit_kv=emit_kv, group=group),
        grid=(t // tm, kvh),
        in_specs=in_specs,
        out_specs=out_specs,
        out_shape=out_shape,
        scratch_shapes=[pltpu.VMEM((tm, d), BF16), pltpu.VMEM((tm, d), F32),
                        pltpu.VMEM((tm, gw), BF16), pltpu.VMEM((nkeys, hd), BF16),
                        pltpu.VMEM((nkeys, hd), BF16), pltpu.VMEM((tm, gw), BF16)],
        compiler_params=_params("arbitrary", "arbitrary"),
        name="gqa",
    )(*args)
    return outs


def kernel(x_prompt, x_sample, c, state_lru, cache_k, cache_v, c_ctx, mod_w, mod_b, norm_g,
           ffn_w_gu, ffn_w_down, lru_w_in, lru_conv_w, lru_conv_b, lru_gate_w, lru_gate_b,
           lru_lambda, lru_w_out, att_w_qkv, att_q_g, att_k_g, att_w_o, final_g):
    b, s, d = x_prompt.shape
    db, ds, _ = x_sample.shape
    depth = mod_w.shape[0]
    n_mixers = 2
    assert ds % TOKEN_TILE == 0 and 1 + db <= SUBLANES

    xp = x_prompt.reshape(b * s, d)
    xs = x_sample.reshape(db * ds, d)
    cvecs = jnp.concatenate([c_ctx[None], c, jnp.zeros((SUBLANES - 1 - db, d), F32)], axis=0)
    m = _modulation(cvecs, mod_w, mod_b).reshape(depth, SUBLANES, 3 * N_SUB, d)

    tiles_per_sample = ds // TOKEN_TILE
    prompt_row = lambda i: 0
    sample_row = lambda i: 1 + i // tiles_per_sample
    prompt_tiles = (b * s) // TOKEN_TILE
    both_row = lambda i: jnp.where(i < prompt_tiles, 0, 1 + (i - prompt_tiles) // tiles_per_sample)

    new_states, new_k, new_v = [], [], []
    for layer in range(depth):
        j = layer // n_mixers
        last = layer == depth - 1
        xp, xs = _ffn(xp, xs, m, both_row, norm_g, ffn_w_gu, ffn_w_down, layer, 0, 0)
        if layer % n_mixers == 0:
            lru_p = (lru_w_in[j], lru_conv_w[j], lru_conv_b[j], lru_gate_w[j], lru_gate_b[j],
                     lru_lambda[j], lru_w_out[j])
            xp, st = _lru(xp, m, prompt_row, norm_g, layer, *lru_p, seq_len=s, h0=None,
                          emit_state=True)
            new_states.append(st)
            xs, _ = _lru(xs, m, sample_row, norm_g, layer, *lru_p, seq_len=ds,
                         h0=state_lru[:, j], emit_state=False)
        else:
            att_p = (att_w_qkv[j], att_q_g[j], att_k_g[j], att_w_o[j])
            xp, kp, vp = _attn(xp, m, prompt_row, norm_g, layer, *att_p, seq_len=s, q_block=s,
                               emit_kv=True)
            new_k.append(kp.reshape(b, s, N_KV_HEADS, HEAD_DIM))
            new_v.append(vp.reshape(b, s, N_KV_HEADS, HEAD_DIM))
            past = cache_k.shape[2]
            ck = cache_k[:, j].reshape(db, past, N_KV_HEADS * HEAD_DIM)
            cv = cache_v[:, j].reshape(db, past, N_KV_HEADS * HEAD_DIM)
            (xs,) = _attn(xs, m, sample_row, norm_g, layer, *att_p, seq_len=ds, q_block=128,
                          cache_k=ck, cache_v=cv, rope=True)
        fg = final_g if last else None
        xp, xs = _ffn(xp, xs, m, both_row, norm_g, ffn_w_gu, ffn_w_down, layer, 1, 2, final_g=fg)

    y_prompt = xp.reshape(b, s, d)
    y_sample = xs.reshape(db, ds, d)
    return (y_prompt, y_sample, jnp.stack(new_states, axis=1), jnp.stack(new_k, axis=1),
            jnp.stack(new_v, axis=1))
```

```python
import functools

import jax
import jax.numpy as jnp
import numpy as np
from jax import lax
from jax.experimental import pallas as pl
from jax.experimental.pallas import tpu as pltpu

F32 = jnp.float32
BF16 = jnp.bfloat16

EPS = 1e-6
LRU_C = 8.0
LOG2_E = 1.4426950408889634
GRID_W = 64
ROPE_THETA = 10000.0
N_SUB = 3
HEAD_DIM = 128
N_KV_HEADS = 2
LRU_BLOCKS = 16

V7X_VMEM_LIMIT_BYTES = 56 * 1024 * 1024
SUBLANES = 8
TOKEN_TILE = 1024
FF_CHUNK = 256
FFN_ROW_BLOCK = 256
FFN_STAGE_SLOTS = 3
LRU_CHUNK = 512
LRU_GATE_CHUNK = 256
MOD_CHUNK = 4608


def _params(*semantics):
    return pltpu.CompilerParams(dimension_semantics=semantics,
                                vmem_limit_bytes=V7X_VMEM_LIMIT_BYTES)


def _mm(a_bf16, w_f32):
    return jnp.dot(a_bf16, w_f32.astype(BF16), preferred_element_type=F32)


def _rms(x):
    return x * lax.rsqrt(jnp.mean(x * x, axis=-1, keepdims=True) + EPS)


def _sub_in(x, m_ref, g_ref, sidx):
    shift = m_ref[3 * sidx:3 * sidx + 1, :]
    scale = m_ref[3 * sidx + 1:3 * sidx + 2, :]
    return (_rms(x) * g_ref[sidx:sidx + 1, :]) * (1.0 + scale) + shift


def _gate(m_ref, sidx):
    return m_ref[3 * sidx + 2:3 * sidx + 3, :]


def _mod_kernel(cv_ref, w_ref, b_ref, o_ref):
    cv = cv_ref[...]
    act = (cv * jax.nn.sigmoid(cv)).astype(BF16)
    o_ref[...] = _mm(act, w_ref[...]) + b_ref[...]


def _modulation(cvecs, mod_w, mod_b):
    depth, d, n = mod_w.shape
    rows = cvecs.shape[0]
    tn = MOD_CHUNK
    assert n % tn == 0
    return pl.pallas_call(
        _mod_kernel,
        grid=(depth, n // tn),
        in_specs=[
            pl.BlockSpec((rows, d), lambda l, j: (0, 0)),
            pl.BlockSpec((None, d, tn), lambda l, j: (l, 0, j)),
            pl.BlockSpec((None, 1, tn), lambda l, j: (l, 0, j)),
        ],
        out_specs=pl.BlockSpec((None, rows, tn), lambda l, j: (l, 0, j)),
        out_shape=jax.ShapeDtypeStruct((depth, rows, n), F32),
        compiler_params=_params("arbitrary", "arbitrary"),
        name="modulation",
    )(cvecs, mod_w, mod_b.reshape(depth, 1, n))


def _ffn_kernel(*refs, layer, s, sidx, final, n_first):
    if final:
        (m_ref, g_ref, fg_ref, xa_hbm, xb_hbm, wgu_hbm, wd_hbm, oa_hbm, ob_hbm,
         wgu_res, wd_res, act_ref, h0_ref, act0_ref, gu_buf, wd_buf, xbuf, obuf,
         w_sem, xsem, osem) = refs
    else:
        (m_ref, g_ref, xa_hbm, xb_hbm, wgu_hbm, wd_hbm, oa_hbm, ob_hbm,
         wgu_res, wd_res, act_ref, h0_ref, act0_ref, gu_buf, wd_buf, xbuf, obuf,
         w_sem, xsem, osem) = refs
    i = pl.program_id(0)
    n = pl.num_programs(0)
    tm = xbuf.shape[1]
    _, d, tf = wgu_res.shape
    f = wd_res.shape[0]
    nk = f // tf
    rb = act_ref.shape[0]
    nslot = gu_buf.shape[0]
    ahead = nslot - 1
    slot = lax.rem(i, 2)
    other = 1 - slot

    def stage_copies(k, sl):
        gcols = pl.ds(pl.multiple_of(k * tf, tf), tf)
        ucols = pl.ds(pl.multiple_of(f + k * tf, tf), tf)
        return (
            pltpu.make_async_copy(wgu_hbm.at[layer, s, :, gcols], gu_buf.at[sl, 0], w_sem.at[sl, 0]),
            pltpu.make_async_copy(wgu_hbm.at[layer, s, :, ucols], gu_buf.at[sl, 1], w_sem.at[sl, 1]),
            pltpu.make_async_copy(wd_hbm.at[layer, s, gcols, :], wd_buf.at[sl], w_sem.at[sl, 2]),
        )

    def tile_rows(hbm, tile):
        return hbm.at[pl.ds(pl.multiple_of(tile * tm, tm), tm), :]

    def x_copy(which, tile, sl):
        return pltpu.make_async_copy(tile_rows((xa_hbm, xb_hbm)[which], tile), xbuf.at[sl],
                                     xsem.at[sl])

    def o_copy(which, tile, sl):
        return pltpu.make_async_copy(obuf.at[sl], tile_rows((oa_hbm, ob_hbm)[which], tile),
                                     osem.at[sl])

    def start_by_stream(make, tile, sl):
        @pl.when(tile < n_first)
        def _():
            make(0, tile, sl).start()

        @pl.when(tile >= n_first)
        def _():
            make(1, tile - n_first, sl).start()

    @pl.when(i == 0)
    def _():
        x_copy(0, 0, 0).start()
        for k in range(ahead):
            for cp in stage_copies(k, k):
                cp.start()

    @pl.when(i + 1 < n)
    def _():
        start_by_stream(x_copy, i + 1, other)

    x_copy(0, 0, slot).wait()

    @pl.when(i >= 2)
    def _():
        o_copy(0, 0, slot).wait()

    def swiglu(h, wg, wu):
        gt = jnp.dot(h, wg, preferred_element_type=F32)
        up = jnp.dot(h, wu, preferred_element_type=F32)
        return ((gt * jax.nn.sigmoid(gt)) * up).astype(BF16)

    def down_and_store(x, rows):
        ff = jnp.dot(act_ref[...], wd_res[...], preferred_element_type=F32)
        y = x + (0.5 * _gate(m_ref, sidx)) * ff
        if final:
            y = _rms(y) * fg_ref[...]
        obuf[slot, rows, :] = y

    def row_block(r, carry):
        rows = pl.ds(pl.multiple_of(r * rb, rb), rb)
        x = xbuf[slot, rows, :]
        h = _sub_in(x, m_ref, g_ref, sidx).astype(BF16)
        for k in range(nk):
            act_ref[:, k * tf:(k + 1) * tf] = swiglu(h, wgu_res[k], wgu_res[nk + k])
        down_and_store(x, rows)
        return carry

    def first_tile_chunk(k, carry):
        sl = lax.rem(k, nslot)
        for cp in stage_copies(k, sl):
            cp.wait()

        @pl.when(k + ahead < nk)
        def _():
            for cp in stage_copies(k + ahead, lax.rem(k + ahead, nslot)):
                cp.start()

        wg = gu_buf[sl, 0].astype(BF16)
        wu = gu_buf[sl, 1].astype(BF16)
        wgu_res[k] = wg
        wgu_res[nk + k] = wu
        wd_res[pl.ds(pl.multiple_of(k * tf, tf), tf), :] = wd_buf[sl].astype(BF16)
        act0_ref[k] = swiglu(h0_ref[...], wg, wu)
        return carry

    def first_tile_down(r, carry):
        rows = pl.ds(pl.multiple_of(r * rb, rb), rb)
        for k in range(nk):
            act_ref[:, k * tf:(k + 1) * tf] = act0_ref[k, rows, :]
        down_and_store(xbuf[slot, rows, :], rows)
        return carry

    @pl.when(i == 0)
    def _():
        h0_ref[...] = _sub_in(xbuf[slot], m_ref, g_ref, sidx).astype(BF16)
        lax.fori_loop(0, nk, first_tile_chunk, 0)
        lax.fori_loop(0, tm // rb, first_tile_down, 0)

    @pl.when(i > 0)
    def _():
        lax.fori_loop(0, tm // rb, row_block, 0, unroll=2)

    start_by_stream(o_copy, i, slot)

    @pl.when(i == n - 1)
    def _():
        o_copy(0, 0, other).wait()
        o_copy(0, 0, slot).wait()


def _ffn(xa, xb, m, row_of_tile, norm_g, w_gu, w_down, layer, s, sidx, final_g=None):
    (ta, d), tb = xa.shape, xb.shape[0]
    f = w_down.shape[2]
    tm, tf, rb = TOKEN_TILE, FF_CHUNK, FFN_ROW_BLOCK
    assert ta % tm == 0 and tb % tm == 0 and f % tf == 0 and (ta + tb) // tm >= 2
    assert tm % rb == 0 and f // tf >= FFN_STAGE_SLOTS
    final = final_g is not None
    in_specs = [
        pl.BlockSpec((None, None, 3 * N_SUB, d), lambda i: (layer, row_of_tile(i), 0, 0)),
        pl.BlockSpec((None, N_SUB, d), lambda i: (layer, 0, 0)),
    ]
    args = [m, norm_g]
    if final:
        in_specs.append(pl.BlockSpec((1, d), lambda i: (0, 0)))
        args.append(final_g.reshape(1, d))
    in_specs += [pl.BlockSpec(memory_space=pl.ANY)] * 4
    args += [xa, xb, w_gu, w_down]
    return pl.pallas_call(
        functools.partial(_ffn_kernel, layer=layer, s=s, sidx=sidx, final=final,
                          n_first=ta // tm),
        grid=((ta + tb) // tm,),
        in_specs=in_specs,
        out_specs=[pl.BlockSpec(memory_space=pl.ANY)] * 2,
        out_shape=[jax.ShapeDtypeStruct((ta, d), F32), jax.ShapeDtypeStruct((tb, d), F32)],
        scratch_shapes=[
            pltpu.VMEM((2 * f // tf, d, tf), BF16), pltpu.VMEM((f, d), BF16),
            pltpu.VMEM((rb, f), BF16),
            pltpu.VMEM((tm, d), BF16), pltpu.VMEM((f // tf, tm, tf), BF16),
            pltpu.VMEM((FFN_STAGE_SLOTS, 2, d, tf), F32),
            pltpu.VMEM((FFN_STAGE_SLOTS, tf, d), F32),
            pltpu.VMEM((2, tm, d), F32), pltpu.VMEM((2, tm, d), F32),
            pltpu.SemaphoreType.DMA((FFN_STAGE_SLOTS, 3)),
            pltpu.SemaphoreType.DMA((2,)),
            pltpu.SemaphoreType.DMA((2,)),
        ],
        compiler_params=_params("arbitrary"),
        name="ffn",
    )(*args)


def _gelu_tanh(x):
    c = np.float32(np.sqrt(2.0 / np.pi))
    return x * (0.5 * (1.0 + jnp.tanh(c * (x + 0.044715 * (x * x * x)))))


def _time_permutation(tm, seq_len):
    segs = seq_len // SUBLANES
    p = np.arange(tm)
    q, lp = p // seq_len, p % seq_len
    j, s = lp // SUBLANES, lp % SUBLANES
    mat = np.zeros((tm, tm), np.float32)
    mat[p, q * seq_len + s * segs + j] = 1.0
    return mat


def _lru_kernel(*refs, seq_len, has_h0, emit_state, conv_left):
    refs = list(refs)
    (x_ref, m_ref, g_ref, wx_ref, wy_ref, cw_ref, cb_ref, gw_ref, gb_ref, lam_ref,
     wo_ref, pin_ref, pout_ref) = refs[:13]
    pos = 13
    h0_ref = None
    if has_h0:
        h0_ref = refs[pos]
        pos += 1
    o_ref = refs[pos]
    pos += 1
    st_ref = None
    if emit_state:
        st_ref = refs[pos]
        pos += 1
    h_ref, acc_ref, af_ref, uf_ref, ab_ref, ub_ref = refs[pos:]

    c = pl.program_id(1)
    tm, cw_cols = af_ref.shape
    nseq = tm // seq_len
    segs = seq_len // SUBLANES

    @pl.when(c == 0)
    def _():
        h = _sub_in(x_ref[...], m_ref, g_ref, 1).astype(BF16)
        h_ref[...] = jnp.dot(pin_ref[...], h, preferred_element_type=F32).astype(BF16)
        acc_ref[...] = jnp.zeros_like(acc_ref)

    h = h_ref[...]
    xb = _mm(h, wx_ref[...])
    yb = _mm(h, wy_ref[...])

    sub = lax.broadcasted_iota(jnp.int32, (SUBLANES, cw_cols), 0)

    def next_segment(grp):
        return jnp.where(sub < SUBLANES - 1, pltpu.roll(grp, SUBLANES - 1, 0), 0.0)

    def prev_segment(grp):
        return jnp.where(sub > 0, pltpu.roll(grp, 1, 0), 0.0)

    def tap(seq, off):
        n = abs(off) * SUBLANES
        if off == 0:
            return seq
        if off > 0:
            edge = [next_segment(seq[g * SUBLANES:(g + 1) * SUBLANES, :]) for g in range(off)]
            return jnp.concatenate([seq[n:, :]] + edge, axis=0)
        start = seq_len - n
        edge = [prev_segment(seq[start + g * SUBLANES:start + (g + 1) * SUBLANES, :])
                for g in range(-off)]
        return jnp.concatenate(edge + [seq[:start, :]], axis=0)

    cw = cw_ref[...]
    assert cw.shape[0] - 1 < segs
    xcs = []
    for sq in range(nseq):
        seq = xb[sq * seq_len:(sq + 1) * seq_len, :]
        acc = cb_ref[...]
        for k in range(cw.shape[0]):
            acc = acc + tap(seq, k - conv_left) * cw[k:k + 1, :]
        xcs.append(acc)
    xc = jnp.concatenate(xcs, axis=0) if nseq > 1 else xcs[0]

    gb = gb_ref[...]
    lam = lam_ref[...]
    gc = gw_ref.shape[1]
    for ch in range(cw_cols // gc):
        cols = slice(ch * gc, (ch + 1) * gc)
        xcc = xc[:, cols]
        gl = jnp.dot(xcc.astype(BF16), gw_ref[ch], preferred_element_type=F32)
        for d, (a_ref, u_ref) in enumerate(((af_ref, uf_ref), (ab_ref, ub_ref))):
            tr = jnp.tanh(gl[:, (2 * d) * gc:(2 * d + 1) * gc] + 0.5 * gb[2 * d:2 * d + 1, cols])
            ig = 0.5 * jnp.tanh(gl[:, (2 * d + 1) * gc:(2 * d + 2) * gc]
                                + 0.5 * gb[2 * d + 1:2 * d + 2, cols]) + 0.5
            lm = lam[d:d + 1, cols]
            log_sig = jnp.minimum(lm, 0.0) - jnp.log1p(jnp.exp(-jnp.abs(lm)))
            c2 = (0.5 * LRU_C * LOG2_E) * log_sig
            a = jnp.exp2(c2 * tr + c2)
            v = 1.0 - a * a
            u = jnp.where(v > 0.0, v * lax.rsqrt(v), 0.0) * (ig * xcc)
            a_ref[:, cols] = a
            u_ref[:, cols] = u

    zero = jnp.zeros((SUBLANES, cw_cols), F32)
    one = jnp.ones((SUBLANES, cw_cols), F32)

    def local_scan(j, carry):
        out = []
        for sq in range(nseq):
            for dd, (a_ref, u_ref) in enumerate(((af_ref, uf_ref), (ab_ref, ub_ref))):
                grp = j if dd == 0 else segs - 1 - j
                rows = pl.ds(pl.multiple_of(sq * seq_len + grp * SUBLANES, SUBLANES), SUBLANES)
                hl, pc = carry[2 * (2 * sq + dd)], carry[2 * (2 * sq + dd) + 1]
                a = a_ref[rows, :]
                hl = a * hl + u_ref[rows, :]
                pc = a * pc
                u_ref[rows, :] = hl
                a_ref[rows, :] = pc
                out += [hl, pc]
        return tuple(out)

    totals = lax.fori_loop(0, segs, local_scan, (zero, one) * (2 * nseq))

    gelu_y = _gelu_tanh(yb)
    ys = []
    for sq in range(nseq):
        entry = []
        for dd in range(2):
            hl, pc = totals[2 * (2 * sq + dd)], totals[2 * (2 * sq + dd) + 1]
            state = h0_ref[sq, dd:dd + 1, :] if has_h0 else jnp.zeros((1, cw_cols), F32)
            rows = [None] * SUBLANES
            order = range(SUBLANES) if dd == 0 else range(SUBLANES - 1, -1, -1)
            for s in order:
                rows[s] = state
                state = pc[s:s + 1, :] * state + hl[s:s + 1, :]
            entry.append(jnp.concatenate(rows, axis=0))
            if emit_state:
                st_ref[sq, dd:dd + 1, :] = state
        lo, hi = sq * seq_len, (sq + 1) * seq_len
        grouped = (segs, SUBLANES, cw_cols)
        hsum = ((uf_ref[lo:hi, :].reshape(grouped) + af_ref[lo:hi, :].reshape(grouped) * entry[0][None])
                + (ub_ref[lo:hi, :].reshape(grouped) + ab_ref[lo:hi, :].reshape(grouped) * entry[1][None]))
        ys.append(hsum.reshape(seq_len, cw_cols) * gelu_y[lo:hi, :])
    y = jnp.concatenate(ys, axis=0) if nseq > 1 else ys[0]

    y = jnp.dot(pout_ref[...], y.astype(BF16), preferred_element_type=F32).astype(BF16)
    acc_ref[...] += _mm(y, wo_ref[...])

    @pl.when(c == pl.num_programs(1) - 1)
    def _():
        o_ref[...] = x_ref[...] + _gate(m_ref, 1) * acc_ref[...]


def _lru_gate_weights(gate_w):
    nd, ng, nb, bw, _ = gate_w.shape
    per = LRU_GATE_CHUNK // bw
    nc = nb // per
    w = (0.5 * gate_w).astype(BF16).reshape(nd * ng, nc, per, bw, bw)
    rows = []
    for n in range(per):
        blk = jnp.transpose(w[:, :, n], (1, 2, 0, 3))
        blk = jnp.pad(blk, ((0, 0), (0, 0), (0, 0), (n * bw, (per - 1 - n) * bw)))
        rows.append(blk.reshape(nc, bw, nd * ng * per * bw))
    return jnp.concatenate(rows, axis=1)


def _lru(x, m, row_of_tile, norm_g, layer, w_in, conv_w, conv_b, gate_w, gate_b, lam, w_out,
         seq_len, h0, emit_state):
    t, d = x.shape
    r = w_out.shape[0]
    tm, cb, gc = TOKEN_TILE, LRU_CHUNK, LRU_GATE_CHUNK
    assert t % tm == 0 and tm % seq_len == 0 and r % cb == 0 and seq_len % SUBLANES == 0
    assert cb % gc == 0
    nc = r // cb
    nseq = tm // seq_len
    has_h0 = h0 is not None
    in_specs = [
        pl.BlockSpec((tm, d), lambda i, c: (i, 0)),
        pl.BlockSpec((None, None, 3 * N_SUB, d), lambda i, c: (layer, row_of_tile(i), 0, 0)),
        pl.BlockSpec((None, N_SUB, d), lambda i, c: (layer, 0, 0)),
        pl.BlockSpec((d, cb), lambda i, c: (0, c)),
        pl.BlockSpec((d, cb), lambda i, c: (0, nc + c)),
        pl.BlockSpec((conv_w.shape[0], cb), lambda i, c: (0, c)),
        pl.BlockSpec((1, cb), lambda i, c: (0, c)),
        pl.BlockSpec((cb // gc, gc, 4 * gc), lambda i, c: (c, 0, 0)),
        pl.BlockSpec((4, cb), lambda i, c: (0, c)),
        pl.BlockSpec((2, cb), lambda i, c: (0, c)),
        pl.BlockSpec((cb, d), lambda i, c: (c, 0)),
        pl.BlockSpec((tm, tm), lambda i, c: (0, 0), pipeline_mode=pl.Buffered(1)),
        pl.BlockSpec((tm, tm), lambda i, c: (0, 0), pipeline_mode=pl.Buffered(1)),
    ]
    perm = _time_permutation(tm, seq_len)
    args = [x, m, norm_g, w_in, w_in, conv_w, conv_b.reshape(1, r), _lru_gate_weights(gate_w),
            gate_b.reshape(4, r), lam, w_out, jnp.asarray(perm, BF16), jnp.asarray(perm.T, BF16)]
    if has_h0:
        in_specs.append(pl.BlockSpec((nseq, 2, cb), lambda i, c: (i, 0, c)))
        args.append(h0)
    out_specs = [pl.BlockSpec((tm, d), lambda i, c: (i, 0))]
    out_shape = [jax.ShapeDtypeStruct((t, d), F32)]
    if emit_state:
        out_specs.append(pl.BlockSpec((nseq, 2, cb), lambda i, c: (i, 0, c)))
        out_shape.append(jax.ShapeDtypeStruct((t // seq_len, 2, r), F32))
    outs = pl.pallas_call(
        functools.partial(_lru_kernel, seq_len=seq_len, has_h0=has_h0, emit_state=emit_state,
                          conv_left=(conv_w.shape[0] - 1) // 2),
        grid=(t // tm, nc),
        in_specs=in_specs,
        out_specs=out_specs,
        out_shape=out_shape,
        scratch_shapes=[pltpu.VMEM((tm, d), BF16), pltpu.VMEM((tm, d), F32)]
        + [pltpu.VMEM((tm, cb), F32)] * 4,
        compiler_params=_params("arbitrary", "arbitrary"),
        name="rglru",
    )(*args)
    return outs if emit_state else (outs[0], None)


def _rope(x, cos, sin_signed, lane):
    hd = x.shape[1]
    partner = jnp.where((lane & 32) == 0, pltpu.roll(x, hd - 32, 1), pltpu.roll(x, 32, 1))
    return x * cos + partner * sin_signed


def _attn_kernel(*refs, seq_len, q_block, past_len, rope, emit_kv, group):
    refs = list(refs)
    x_ref, m_ref, g_ref, wq_ref, wk_ref, wv_ref, qg_ref, kg_ref, wo_ref = refs[:9]
    pos = 9
    if rope:
        cos_ref, sin_ref = refs[pos:pos + 2]
        pos += 2
    if past_len:
        ck_ref, cv_ref = refs[pos:pos + 2]
        pos += 2
    o_ref = refs[pos]
    pos += 1
    if emit_kv:
        kn_ref, vn_ref = refs[pos:pos + 2]
        pos += 2
    h_ref, acc_ref, q_s, k_s, v_s, o_s = refs[pos:]

    gi = pl.program_id(1)
    tm = x_ref.shape[0]
    hd = k_s.shape[1]
    nqb = seq_len // q_block
    nchunks = (tm // seq_len) * nqb
    nk = past_len + seq_len
    scale = hd ** -0.5

    @pl.when(gi == 0)
    def _():
        h_ref[...] = _sub_in(x_ref[...], m_ref, g_ref, 1).astype(BF16)
        acc_ref[...] = jnp.zeros_like(acc_ref)

    h = h_ref[...]
    q = _mm(h, wq_ref[...])
    k = _rms(_mm(h, wk_ref[...])) * kg_ref[...]
    v = _mm(h, wv_ref[...])
    if emit_kv:
        kn_ref[...] = k
        vn_ref[...] = v
    if rope:
        lane = lax.broadcasted_iota(jnp.int32, (tm, hd), 1)
        cos, sin = cos_ref[...], sin_ref[...]
        k = _rope(k, cos, sin, lane)
    k_s[past_len:past_len + tm, :] = k.astype(BF16)
    v_s[past_len:past_len + tm, :] = v.astype(BF16)
    if past_len:
        k_s[0:past_len, :] = ck_ref[...].astype(BF16)
        v_s[0:past_len, :] = cv_ref[...].astype(BF16)
    for j in range(group):
        qh = _rms(q[:, j * hd:(j + 1) * hd]) * qg_ref[...]
        if rope:
            qh = _rope(qh, cos, sin, lane)
        q_s[:, j * hd:(j + 1) * hd] = qh.astype(BF16)

    def chunk(ci, carry):
        r0 = pl.multiple_of(ci * q_block, q_block)
        if past_len:
            keys, vals = k_s[...], v_s[...]
        else:
            koff = pl.multiple_of((ci // nqb) * seq_len, seq_len)
            keys, vals = k_s[pl.ds(koff, nk), :], v_s[pl.ds(koff, nk), :]
        qc = jnp.concatenate([q_s[pl.ds(r0, q_block), j * hd:(j + 1) * hd] for j in range(group)],
                             axis=0)
        raw = lax.dot_general(qc, keys, (((1,), (1,)), ((), ())), preferred_element_type=F32)
        e = jnp.exp2((raw - jnp.max(raw, axis=-1, keepdims=True)) * (scale * LOG2_E))
        inv = 1.0 / jnp.sum(e, axis=-1, keepdims=True)
        oc = jnp.dot(e.astype(BF16), vals, preferred_element_type=F32) * inv
        for j in range(group):
            o_s[pl.ds(r0, q_block), j * hd:(j + 1) * hd] = oc[j * q_block:(j + 1) * q_block, :].astype(BF16)
        return carry

    lax.fori_loop(0, nchunks, chunk, 0, unroll=min(nchunks, 4))
    acc_ref[...] += _mm(o_s[...], wo_ref[...])

    @pl.when(gi == pl.num_programs(1) - 1)
    def _():
        o_ref[...] = x_ref[...] + _gate(m_ref, 1) * acc_ref[...]


def _rope_tables(n_tok, hd):
    rows = n_tok // GRID_W
    r_idx = jnp.broadcast_to(jnp.arange(rows)[:, None], (rows, GRID_W)).reshape(n_tok).astype(F32)
    c_idx = jnp.broadcast_to(jnp.arange(GRID_W)[None, :], (rows, GRID_W)).reshape(n_tok).astype(F32)
    n_freq = hd // 4
    inv = ROPE_THETA ** (-jnp.arange(n_freq, dtype=F32) / n_freq)
    ang = jnp.stack([r_idx[:, None] * inv, c_idx[:, None] * inv], axis=1)
    cos, sin = jnp.cos(ang), jnp.sin(ang)
    cos_full = jnp.concatenate([cos, cos], axis=-1).reshape(n_tok, hd)
    sin_signed = jnp.concatenate([-sin, sin], axis=-1).reshape(n_tok, hd)
    return cos_full, sin_signed


def _attn(x, m, row_of_tile, norm_g, layer, w_qkv, q_g, k_g, w_o, seq_len, q_block,
          cache_k=None, cache_v=None, rope=False, emit_kv=False):
    t, d = x.shape
    hd, kvh = HEAD_DIM, N_KV_HEADS
    n_heads = w_o.shape[0] // hd
    group = n_heads // kvh
    gw = group * hd
    tm = TOKEN_TILE
    assert t % tm == 0 and tm % seq_len == 0 and seq_len % q_block == 0
    past_len = 0 if cache_k is None else cache_k.shape[1]
    assert past_len == 0 or tm == seq_len
    in_specs = [
        pl.BlockSpec((tm, d), lambda i, g: (i, 0)),
        pl.BlockSpec((None, None, 3 * N_SUB, d), lambda i, g: (layer, row_of_tile(i), 0, 0)),
        pl.BlockSpec((None, N_SUB, d), lambda i, g: (layer, 0, 0)),
        pl.BlockSpec((d, gw), lambda i, g: (0, g)),
        pl.BlockSpec((d, hd), lambda i, g: (0, n_heads + g)),
        pl.BlockSpec((d, hd), lambda i, g: (0, n_heads + kvh + g)),
        pl.BlockSpec((1, hd), lambda i, g: (0, 0)),
        pl.BlockSpec((1, hd), lambda i, g: (0, 0)),
        pl.BlockSpec((gw, d), lambda i, g: (g, 0)),
    ]
    args = [x, m, norm_g, w_qkv, w_qkv, w_qkv, q_g.reshape(1, hd), k_g.reshape(1, hd), w_o]
    if rope:
        assert tm == seq_len
        cos, sin = _rope_tables(seq_len, hd)
        in_specs += [pl.BlockSpec((tm, hd), lambda i, g: (0, 0))] * 2
        args += [cos, sin]
    if past_len:
        in_specs += [pl.BlockSpec((None, past_len, hd), lambda i, g: (i, 0, g))] * 2
        args += [cache_k, cache_v]
    out_specs = [pl.BlockSpec((tm, d), lambda i, g: (i, 0))]
    out_shape = [jax.ShapeDtypeStruct((t, d), F32)]
    if emit_kv:
        out_specs += [pl.BlockSpec((tm, hd), lambda i, g: (i, g))] * 2
        out_shape += [jax.ShapeDtypeStruct((t, kvh * hd), F32)] * 2
    nkeys = past_len + tm
    outs = pl.pallas_call(
        functools.partial(_attn_kernel, seq_len=seq_len, q_block=q_block, past_len=past_len,
                          rope=rope, emit_kv=emit_kv, group=group),
        grid=(t // tm, kvh),
        in_specs=in_specs,
        out_specs=out_specs,
        out_shape=out_shape,
        scratch_shapes=[pltpu.VMEM((tm, d), BF16), pltpu.VMEM((tm, d), F32),
                        pltpu.VMEM((tm, gw), BF16), pltpu.VMEM((nkeys, hd), BF16),
                        pltpu.VMEM((nkeys, hd), BF16), pltpu.VMEM((tm, gw), BF16)],
        compiler_params=_params("arbitrary", "arbitrary"),
        name="gqa",
    )(*args)
    return outs


def kernel(x_prompt, x_sample, c, state_lru, cache_k, cache_v, c_ctx, mod_w, mod_b, norm_g,
           ffn_w_gu, ffn_w_down, lru_w_in, lru_conv_w, lru_conv_b, lru_gate_w, lru_gate_b,
           lru_lambda, lru_w_out, att_w_qkv, att_q_g, att_k_g, att_w_o, final_g):
    b, s, d = x_prompt.shape
    db, ds, _ = x_sample.shape
    depth = mod_w.shape[0]
    n_mixers = 2
    assert ds % TOKEN_TILE == 0 and 1 + db <= SUBLANES

    xp = x_prompt.reshape(b * s, d)
    xs = x_sample.reshape(db * ds, d)
    cvecs = jnp.concatenate([c_ctx[None], c, jnp.zeros((SUBLANES - 1 - db, d), F32)], axis=0)
    m = _modulation(cvecs, mod_w, mod_b).reshape(depth, SUBLANES, 3 * N_SUB, d)

    tiles_per_sample = ds // TOKEN_TILE
    prompt_row = lambda i: 0
    sample_row = lambda i: 1 + i // tiles_per_sample
    prompt_tiles = (b * s) // TOKEN_TILE
    both_row = lambda i: jnp.where(i < prompt_tiles, 0, 1 + (i - prompt_tiles) // tiles_per_sample)

    new_states, new_k, new_v = [], [], []
    for layer in range(depth):
        j = layer // n_mixers
        last = layer == depth - 1
        xp, xs = _ffn(xp, xs, m, both_row, norm_g, ffn_w_gu, ffn_w_down, layer, 0, 0)
        if layer % n_mixers == 0:
            lru_p = (lru_w_in[j], lru_conv_w[j], lru_conv_b[j], lru_gate_w[j], lru_gate_b[j],
                     lru_lambda[j], lru_w_out[j])
            xp, st = _lru(xp, m, prompt_row, norm_g, layer, *lru_p, seq_len=s, h0=None,
                          emit_state=True)
            new_states.append(st)
            xs, _ = _lru(xs, m, sample_row, norm_g, layer, *lru_p, seq_len=ds,
                         h0=state_lru[:, j], emit_state=False)
        else:
            att_p = (att_w_qkv[j], att_q_g[j], att_k_g[j], att_w_o[j])
            xp, kp, vp = _attn(xp, m, prompt_row, norm_g, layer, *att_p, seq_len=s, q_block=s,
                               emit_kv=True)
            new_k.append(kp.reshape(b, s, N_KV_HEADS, HEAD_DIM))
            new_v.append(vp.reshape(b, s, N_KV_HEADS, HEAD_DIM))
            past = cache_k.shape[2]
            ck = cache_k[:, j].reshape(db, past, N_KV_HEADS * HEAD_DIM)
            cv = cache_v[:, j].reshape(db, past, N_KV_HEADS * HEAD_DIM)
            (xs,) = _attn(xs, m, sample_row, norm_g, layer, *att_p, seq_len=ds, q_block=128,
                          cache_k=ck, cache_v=cv, rope=True)
        fg = final_g if last else None
        xp, xs = _ffn(xp, xs, m, both_row, norm_g, ffn_w_gu, ffn_w_down, layer, 1, 2, final_g=fg)

    y_prompt = xp.reshape(b, s, d)
    y_sample = xs.reshape(db, ds, d)
    return (y_prompt, y_sample, jnp.stack(new_states, axis=1), jnp.stack(new_k, axis=1),
            jnp.stack(new_v, axis=1))
```

```python
import functools

import jax
import jax.numpy as jnp
import numpy as np
from jax import lax
from jax.experimental import pallas as pl
from jax.experimental.pallas import tpu as pltpu

F32 = jnp.float32
BF16 = jnp.bfloat16

EPS = 1e-6
LRU_C = 8.0
LOG2_E = 1.4426950408889634
GRID_W = 64
ROPE_THETA = 10000.0
N_SUB = 3
HEAD_DIM = 128
N_KV_HEADS = 2
LRU_BLOCKS = 16

V7X_VMEM_LIMIT_BYTES = 56 * 1024 * 1024
SUBLANES = 8
TOKEN_TILE = 1024
FF_CHUNK = 256
FFN_ROW_BLOCK = 256
FFN_STAGE_SLOTS = 3
LRU_CHUNK = 512
LRU_GATE_CHUNK = 256
MOD_CHUNK = 1536


def _params(*semantics):
    return pltpu.CompilerParams(dimension_semantics=semantics,
                                vmem_limit_bytes=V7X_VMEM_LIMIT_BYTES)


def _mm(a_bf16, w_f32):
    return jnp.dot(a_bf16, w_f32.astype(BF16), preferred_element_type=F32)


def _rms(x):
    return x * lax.rsqrt(jnp.mean(x * x, axis=-1, keepdims=True) + EPS)


def _sub_in(x, m_ref, g_ref, sidx):
    shift = m_ref[3 * sidx:3 * sidx + 1, :]
    scale = m_ref[3 * sidx + 1:3 * sidx + 2, :]
    return (_rms(x) * g_ref[sidx:sidx + 1, :]) * (1.0 + scale) + shift


def _gate(m_ref, sidx):
    return m_ref[3 * sidx + 2:3 * sidx + 3, :]


def _mod_kernel(cv_ref, w_ref, b_ref, o_ref):
    cv = cv_ref[...]
    act = (cv * jax.nn.sigmoid(cv)).astype(BF16)
    o_ref[...] = _mm(act, w_ref[...]) + b_ref[...]


def _modulation(cvecs, mod_w, mod_b):
    depth, d, n = mod_w.shape
    rows = cvecs.shape[0]
    tn = MOD_CHUNK
    assert n % tn == 0
    return pl.pallas_call(
        _mod_kernel,
        grid=(depth, n // tn),
        in_specs=[
            pl.BlockSpec((rows, d), lambda l, j: (0, 0)),
            pl.BlockSpec((None, d, tn), lambda l, j: (l, 0, j)),
            pl.BlockSpec((None, 1, tn), lambda l, j: (l, 0, j)),
        ],
        out_specs=pl.BlockSpec((None, rows, tn), lambda l, j: (l, 0, j)),
        out_shape=jax.ShapeDtypeStruct((depth, rows, n), F32),
        compiler_params=_params("arbitrary", "arbitrary"),
        name="modulation",
    )(cvecs, mod_w, mod_b.reshape(depth, 1, n))


def _ffn_kernel(*refs, layer, s, sidx, final, n_first):
    if final:
        (m_ref, g_ref, fg_ref, xa_hbm, xb_hbm, wgu_hbm, wd_hbm, oa_hbm, ob_hbm,
         wgu_res, wd_res, act_ref, h0_ref, act0_ref, gu_buf, wd_buf, xbuf, obuf,
         w_sem, xsem, osem) = refs
    else:
        (m_ref, g_ref, xa_hbm, xb_hbm, wgu_hbm, wd_hbm, oa_hbm, ob_hbm,
         wgu_res, wd_res, act_ref, h0_ref, act0_ref, gu_buf, wd_buf, xbuf, obuf,
         w_sem, xsem, osem) = refs
    i = pl.program_id(0)
    n = pl.num_programs(0)
    tm = xbuf.shape[1]
    _, d, tf = wgu_res.shape
    f = wd_res.shape[0]
    nk = f // tf
    rb = act_ref.shape[0]
    nslot = gu_buf.shape[0]
    ahead = nslot - 1
    slot = lax.rem(i, 2)
    other = 1 - slot

    def stage_copies(k, sl):
        gcols = pl.ds(pl.multiple_of(k * tf, tf), tf)
        ucols = pl.ds(pl.multiple_of(f + k * tf, tf), tf)
        return (
            pltpu.make_async_copy(wgu_hbm.at[layer, s, :, gcols], gu_buf.at[sl, 0], w_sem.at[sl, 0]),
            pltpu.make_async_copy(wgu_hbm.at[layer, s, :, ucols], gu_buf.at[sl, 1], w_sem.at[sl, 1]),
            pltpu.make_async_copy(wd_hbm.at[layer, s, gcols, :], wd_buf.at[sl], w_sem.at[sl, 2]),
        )

    def tile_rows(hbm, tile):
        return hbm.at[pl.ds(pl.multiple_of(tile * tm, tm), tm), :]

    def x_copy(which, tile, sl):
        return pltpu.make_async_copy(tile_rows((xa_hbm, xb_hbm)[which], tile), xbuf.at[sl],
                                     xsem.at[sl])

    def o_copy(which, tile, sl):
        return pltpu.make_async_copy(obuf.at[sl], tile_rows((oa_hbm, ob_hbm)[which], tile),
                                     osem.at[sl])

    def start_by_stream(make, tile, sl):
        @pl.when(tile < n_first)
        def _():
            make(0, tile, sl).start()

        @pl.when(tile >= n_first)
        def _():
            make(1, tile - n_first, sl).start()

    @pl.when(i == 0)
    def _():
        x_copy(0, 0, 0).start()
        for k in range(ahead):
            for cp in stage_copies(k, k):
                cp.start()

    @pl.when(i + 1 < n)
    def _():
        start_by_stream(x_copy, i + 1, other)

    x_copy(0, 0, slot).wait()

    @pl.when(i >= 2)
    def _():
        o_copy(0, 0, slot).wait()

    def swiglu(h, wg, wu):
        gt = jnp.dot(h, wg, preferred_element_type=F32)
        up = jnp.dot(h, wu, preferred_element_type=F32)
        return ((gt * jax.nn.sigmoid(gt)) * up).astype(BF16)

    def down_and_store(x, rows):
        ff = jnp.dot(act_ref[...], wd_res[...], preferred_element_type=F32)
        y = x + (0.5 * _gate(m_ref, sidx)) * ff
        if final:
            y = _rms(y) * fg_ref[...]
        obuf[slot, rows, :] = y

    def row_block(r, carry):
        rows = pl.ds(pl.multiple_of(r * rb, rb), rb)
        x = xbuf[slot, rows, :]
        h = _sub_in(x, m_ref, g_ref, sidx).astype(BF16)
        for k in range(nk):
            act_ref[:, k * tf:(k + 1) * tf] = swiglu(h, wgu_res[k], wgu_res[nk + k])
        down_and_store(x, rows)
        return carry

    def first_tile_chunk(k, carry):
        sl = lax.rem(k, nslot)
        for cp in stage_copies(k, sl):
            cp.wait()

        @pl.when(k + ahead < nk)
        def _():
            for cp in stage_copies(k + ahead, lax.rem(k + ahead, nslot)):
                cp.start()

        wg = gu_buf[sl, 0].astype(BF16)
        wu = gu_buf[sl, 1].astype(BF16)
        wgu_res[k] = wg
        wgu_res[nk + k] = wu
        wd_res[pl.ds(pl.multiple_of(k * tf, tf), tf), :] = wd_buf[sl].astype(BF16)
        act0_ref[k] = swiglu(h0_ref[...], wg, wu)
        return carry

    def first_tile_down(r, carry):
        rows = pl.ds(pl.multiple_of(r * rb, rb), rb)
        for k in range(nk):
            act_ref[:, k * tf:(k + 1) * tf] = act0_ref[k, rows, :]
        down_and_store(xbuf[slot, rows, :], rows)
        return carry

    @pl.when(i == 0)
    def _():
        h0_ref[...] = _sub_in(xbuf[slot], m_ref, g_ref, sidx).astype(BF16)
        lax.fori_loop(0, nk, first_tile_chunk, 0)
        lax.fori_loop(0, tm // rb, first_tile_down, 0)

    @pl.when(i > 0)
    def _():
        lax.fori_loop(0, tm // rb, row_block, 0, unroll=2)

    start_by_stream(o_copy, i, slot)

    @pl.when(i == n - 1)
    def _():
        o_copy(0, 0, other).wait()
        o_copy(0, 0, slot).wait()


def _ffn(xa, xb, m, row_of_tile, norm_g, w_gu, w_down, layer, s, sidx, final_g=None):
    (ta, d), tb = xa.shape, xb.shape[0]
    f = w_down.shape[2]
    tm, tf, rb = TOKEN_TILE, FF_CHUNK, FFN_ROW_BLOCK
    assert ta % tm == 0 and tb % tm == 0 and f % tf == 0 and (ta + tb) // tm >= 2
    assert tm % rb == 0 and f // tf >= FFN_STAGE_SLOTS
    final = final_g is not None
    in_specs = [
        pl.BlockSpec((None, None, 3 * N_SUB, d), lambda i: (layer, row_of_tile(i), 0, 0)),
        pl.BlockSpec((None, N_SUB, d), lambda i: (layer, 0, 0)),
    ]
    args = [m, norm_g]
    if final:
        in_specs.append(pl.BlockSpec((1, d), lambda i: (0, 0)))
        args.append(final_g.reshape(1, d))
    in_specs += [pl.BlockSpec(memory_space=pl.ANY)] * 4
    args += [xa, xb, w_gu, w_down]
    return pl.pallas_call(
        functools.partial(_ffn_kernel, layer=layer, s=s, sidx=sidx, final=final,
                          n_first=ta // tm),
        grid=((ta + tb) // tm,),
        in_specs=in_specs,
        out_specs=[pl.BlockSpec(memory_space=pl.ANY)] * 2,
        out_shape=[jax.ShapeDtypeStruct((ta, d), F32), jax.ShapeDtypeStruct((tb, d), F32)],
        scratch_shapes=[
            pltpu.VMEM((2 * f // tf, d, tf), BF16), pltpu.VMEM((f, d), BF16),
            pltpu.VMEM((rb, f), BF16),
            pltpu.VMEM((tm, d), BF16), pltpu.VMEM((f // tf, tm, tf), BF16),
            pltpu.VMEM((FFN_STAGE_SLOTS, 2, d, tf), F32),
            pltpu.VMEM((FFN_STAGE_SLOTS, tf, d), F32),
            pltpu.VMEM((2, tm, d), F32), pltpu.VMEM((2, tm, d), F32),
            pltpu.SemaphoreType.DMA((FFN_STAGE_SLOTS, 3)),
            pltpu.SemaphoreType.DMA((2,)),
            pltpu.SemaphoreType.DMA((2,)),
        ],
        compiler_params=_params("arbitrary"),
        name="ffn",
    )(*args)


def _gelu_tanh(x):
    c = np.float32(np.sqrt(2.0 / np.pi))
    return x * (0.5 * (1.0 + jnp.tanh(c * (x + 0.044715 * (x * x * x)))))


def _time_permutation(seq_len):
    segs = seq_len // SUBLANES
    p = np.arange(seq_len)
    j, s = p // SUBLANES, p % SUBLANES
    mat = np.zeros((seq_len, seq_len), np.float32)
    mat[p, s * segs + j] = 1.0
    return mat


def _per_sequence(mat, rows, seq_len):
    parts = [jnp.dot(mat, rows[lo:lo + seq_len, :], preferred_element_type=F32).astype(BF16)
             for lo in range(0, rows.shape[0], seq_len)]
    return jnp.concatenate(parts, axis=0) if len(parts) > 1 else parts[0]


def _lru_kernel(*refs, seq_len, has_h0, emit_state, conv_left):
    refs = list(refs)
    (x_ref, m_ref, g_ref, wx_ref, wy_ref, cw_ref, cb_ref, gw_ref, gb_ref, lam_ref,
     wo_ref, pin_ref, pout_ref) = refs[:13]
    pos = 13
    h0_ref = None
    if has_h0:
        h0_ref = refs[pos]
        pos += 1
    o_ref = refs[pos]
    pos += 1
    st_ref = None
    if emit_state:
        st_ref = refs[pos]
        pos += 1
    h_ref, acc_ref, af_ref, uf_ref, ab_ref, ub_ref = refs[pos:]

    c = pl.program_id(1)
    tm, cw_cols = af_ref.shape
    nseq = tm // seq_len
    segs = seq_len // SUBLANES

    @pl.when(c == 0)
    def _():
        h = _sub_in(x_ref[...], m_ref, g_ref, 1).astype(BF16)
        h_ref[...] = _per_sequence(pin_ref[...], h, seq_len)
        acc_ref[...] = jnp.zeros_like(acc_ref)

    h = h_ref[...]
    xb = _mm(h, wx_ref[...])
    yb = _mm(h, wy_ref[...])

    sub = lax.broadcasted_iota(jnp.int32, (SUBLANES, cw_cols), 0)

    def next_segment(grp):
        return jnp.where(sub < SUBLANES - 1, pltpu.roll(grp, SUBLANES - 1, 0), 0.0)

    def prev_segment(grp):
        return jnp.where(sub > 0, pltpu.roll(grp, 1, 0), 0.0)

    def tap(seq, off):
        n = abs(off) * SUBLANES
        if off == 0:
            return seq
        if off > 0:
            edge = [next_segment(seq[g * SUBLANES:(g + 1) * SUBLANES, :]) for g in range(off)]
            return jnp.concatenate([seq[n:, :]] + edge, axis=0)
        start = seq_len - n
        edge = [prev_segment(seq[start + g * SUBLANES:start + (g + 1) * SUBLANES, :])
                for g in range(-off)]
        return jnp.concatenate(edge + [seq[:start, :]], axis=0)

    cw = cw_ref[...]
    assert cw.shape[0] - 1 < segs
    xcs = []
    for sq in range(nseq):
        seq = xb[sq * seq_len:(sq + 1) * seq_len, :]
        acc = cb_ref[...]
        for k in range(cw.shape[0]):
            acc = acc + tap(seq, k - conv_left) * cw[k:k + 1, :]
        xcs.append(acc)
    xc = jnp.concatenate(xcs, axis=0) if nseq > 1 else xcs[0]

    gb = gb_ref[...]
    lam = lam_ref[...]
    gc = gw_ref.shape[1]
    for ch in range(cw_cols // gc):
        cols = slice(ch * gc, (ch + 1) * gc)
        xcc = xc[:, cols]
        gl = jnp.dot(xcc.astype(BF16), gw_ref[ch], preferred_element_type=F32)
        for d, (a_ref, u_ref) in enumerate(((af_ref, uf_ref), (ab_ref, ub_ref))):
            tr = jnp.tanh(gl[:, (2 * d) * gc:(2 * d + 1) * gc] + 0.5 * gb[2 * d:2 * d + 1, cols])
            ig = 0.5 * jnp.tanh(gl[:, (2 * d + 1) * gc:(2 * d + 2) * gc]
                                + 0.5 * gb[2 * d + 1:2 * d + 2, cols]) + 0.5
            lm = lam[d:d + 1, cols]
            log_sig = jnp.minimum(lm, 0.0) - jnp.log1p(jnp.exp(-jnp.abs(lm)))
            c2 = (0.5 * LRU_C * LOG2_E) * log_sig
            a = jnp.exp2(c2 * tr + c2)
            v = 1.0 - a * a
            u = jnp.where(v > 0.0, v * lax.rsqrt(v), 0.0) * (ig * xcc)
            a_ref[:, cols] = a
            u_ref[:, cols] = u

    zero = jnp.zeros((SUBLANES, cw_cols), F32)
    one = jnp.ones((SUBLANES, cw_cols), F32)

    def local_scan(j, carry):
        out = []
        for sq in range(nseq):
            for dd, (a_ref, u_ref) in enumerate(((af_ref, uf_ref), (ab_ref, ub_ref))):
                grp = j if dd == 0 else segs - 1 - j
                rows = pl.ds(pl.multiple_of(sq * seq_len + grp * SUBLANES, SUBLANES), SUBLANES)
                hl, pc = carry[2 * (2 * sq + dd)], carry[2 * (2 * sq + dd) + 1]
                a = a_ref[rows, :]
                hl = a * hl + u_ref[rows, :]
                pc = a * pc
                u_ref[rows, :] = hl
                a_ref[rows, :] = pc
                out += [hl, pc]
        return tuple(out)

    totals = lax.fori_loop(0, segs, local_scan, (zero, one) * (2 * nseq))

    gelu_y = _gelu_tanh(yb)
    ys = []
    for sq in range(nseq):
        entry = []
        for dd in range(2):
            hl, pc = totals[2 * (2 * sq + dd)], totals[2 * (2 * sq + dd) + 1]
            state = h0_ref[sq, dd:dd + 1, :] if has_h0 else jnp.zeros((1, cw_cols), F32)
            rows = [None] * SUBLANES
            order = range(SUBLANES) if dd == 0 else range(SUBLANES - 1, -1, -1)
            for s in order:
                rows[s] = state
                state = pc[s:s + 1, :] * state + hl[s:s + 1, :]
            entry.append(jnp.concatenate(rows, axis=0))
            if emit_state:
                st_ref[sq, dd:dd + 1, :] = state
        lo, hi = sq * seq_len, (sq + 1) * seq_len
        grouped = (segs, SUBLANES, cw_cols)
        hsum = ((uf_ref[lo:hi, :].reshape(grouped) + af_ref[lo:hi, :].reshape(grouped) * entry[0][None])
                + (ub_ref[lo:hi, :].reshape(grouped) + ab_ref[lo:hi, :].reshape(grouped) * entry[1][None]))
        ys.append(hsum.reshape(seq_len, cw_cols) * gelu_y[lo:hi, :])
    y = jnp.concatenate(ys, axis=0) if nseq > 1 else ys[0]

    y = _per_sequence(pout_ref[...], y.astype(BF16), seq_len)
    acc_ref[...] += _mm(y, wo_ref[...])

    @pl.when(c == pl.num_programs(1) - 1)
    def _():
        o_ref[...] = x_ref[...] + _gate(m_ref, 1) * acc_ref[...]


def _lru_gate_weights(gate_w):
    nd, ng, nb, bw, _ = gate_w.shape
    per = LRU_GATE_CHUNK // bw
    nc = nb // per
    w = (0.5 * gate_w).astype(BF16).reshape(nd * ng, nc, per, bw, bw)
    rows = []
    for n in range(per):
        blk = jnp.transpose(w[:, :, n], (1, 2, 0, 3))
        blk = jnp.pad(blk, ((0, 0), (0, 0), (0, 0), (n * bw, (per - 1 - n) * bw)))
        rows.append(blk.reshape(nc, bw, nd * ng * per * bw))
    return jnp.concatenate(rows, axis=1)


def _lru(x, m, row_of_tile, norm_g, layer, w_in, conv_w, conv_b, gate_w, gate_b, lam, w_out,
         seq_len, h0, emit_state):
    t, d = x.shape
    r = w_out.shape[0]
    tm, cb, gc = TOKEN_TILE, LRU_CHUNK, LRU_GATE_CHUNK
    assert t % tm == 0 and tm % seq_len == 0 and r % cb == 0 and seq_len % SUBLANES == 0
    assert cb % gc == 0
    nc = r // cb
    nseq = tm // seq_len
    has_h0 = h0 is not None
    in_specs = [
        pl.BlockSpec((tm, d), lambda i, c: (i, 0)),
        pl.BlockSpec((None, None, 3 * N_SUB, d), lambda i, c: (layer, row_of_tile(i), 0, 0)),
        pl.BlockSpec((None, N_SUB, d), lambda i, c: (layer, 0, 0)),
        pl.BlockSpec((d, cb), lambda i, c: (0, c)),
        pl.BlockSpec((d, cb), lambda i, c: (0, nc + c)),
        pl.BlockSpec((conv_w.shape[0], cb), lambda i, c: (0, c)),
        pl.BlockSpec((1, cb), lambda i, c: (0, c)),
        pl.BlockSpec((cb // gc, gc, 4 * gc), lambda i, c: (c, 0, 0)),
        pl.BlockSpec((4, cb), lambda i, c: (0, c)),
        pl.BlockSpec((2, cb), lambda i, c: (0, c)),
        pl.BlockSpec((cb, d), lambda i, c: (c, 0)),
        pl.BlockSpec((seq_len, seq_len), lambda i, c: (0, 0), pipeline_mode=pl.Buffered(1)),
        pl.BlockSpec((seq_len, seq_len), lambda i, c: (0, 0), pipeline_mode=pl.Buffered(1)),
    ]
    perm = _time_permutation(seq_len)
    args = [x, m, norm_g, w_in, w_in, conv_w, conv_b.reshape(1, r), _lru_gate_weights(gate_w),
            gate_b.reshape(4, r), lam, w_out, jnp.asarray(perm, BF16), jnp.asarray(perm.T, BF16)]
    if has_h0:
        in_specs.append(pl.BlockSpec((nseq, 2, cb), lambda i, c: (i, 0, c)))
        args.append(h0)
    out_specs = [pl.BlockSpec((tm, d), lambda i, c: (i, 0))]
    out_shape = [jax.ShapeDtypeStruct((t, d), F32)]
    if emit_state:
        out_specs.append(pl.BlockSpec((nseq, 2, cb), lambda i, c: (i, 0, c)))
        out_shape.append(jax.ShapeDtypeStruct((t // seq_len, 2, r), F32))
    outs = pl.pallas_call(
        functools.partial(_lru_kernel, seq_len=seq_len, has_h0=has_h0, emit_state=emit_state,
                          conv_left=(conv_w.shape[0] - 1) // 2),
        grid=(t // tm, nc),
        in_specs=in_specs,
        out_specs=out_specs,
        out_shape=out_shape,
        scratch_shapes=[pltpu.VMEM((tm, d), BF16), pltpu.VMEM((tm, d), F32)]
        + [pltpu.VMEM((tm, cb), F32)] * 4,
        compiler_params=_params("arbitrary", "arbitrary"),
        name="rglru",
    )(*args)
    return outs if emit_state else (outs[0], None)


def _rope(x, cos, sin_signed, lane):
    hd = x.shape[1]
    partner = jnp.where((lane & 32) == 0, pltpu.roll(x, hd - 32, 1), pltpu.roll(x, 32, 1))
    return x * cos + partner * sin_signed


def _attn_kernel(*refs, seq_len, q_block, past_len, rope, emit_kv, group, kvh):
    refs = list(refs)
    x_ref, m_ref, g_ref, wq_ref, wk_ref, wv_ref, qg_ref, kg_ref, wo_ref = refs[:9]
    pos = 9
    if rope:
        cos_ref, sin_ref = refs[pos:pos + 2]
        pos += 2
    if past_len:
        ck_ref, cv_ref = refs[pos:pos + 2]
        pos += 2
    o_ref = refs[pos]
    pos += 1
    if emit_kv:
        kn_ref, vn_ref = refs[pos:pos + 2]
        pos += 2
    h_ref, acc_ref, q_s, k_s, v_s, o_s = refs[pos:]

    gi = pl.program_id(1)
    tm = x_ref.shape[0]
    hd = k_s.shape[1]
    nqb = seq_len // q_block
    nchunks = (tm // seq_len) * nqb
    nk = past_len + seq_len
    scale = hd ** -0.5

    @pl.when(gi == 0)
    def _():
        h_ref[...] = _sub_in(x_ref[...], m_ref, g_ref, 1).astype(BF16)
        acc_ref[...] = jnp.zeros_like(acc_ref)

    h = h_ref[...]
    q = _mm(h, wq_ref[...])
    k = _rms(_mm(h, wk_ref[...])) * kg_ref[...]
    v = _mm(h, wv_ref[...])
    if emit_kv:
        kn_ref[pl.ds(gi, tm, stride=kvh), :] = k
        vn_ref[pl.ds(gi, tm, stride=kvh), :] = v
    if rope:
        lane = lax.broadcasted_iota(jnp.int32, (tm, hd), 1)
        cos, sin = cos_ref[...], sin_ref[...]
        k = _rope(k, cos, sin, lane)
    k_s[past_len:past_len + tm, :] = k.astype(BF16)
    v_s[past_len:past_len + tm, :] = v.astype(BF16)
    if past_len:
        k_s[0:past_len, :] = ck_ref[pl.ds(gi, past_len, stride=kvh), :].astype(BF16)
        v_s[0:past_len, :] = cv_ref[pl.ds(gi, past_len, stride=kvh), :].astype(BF16)
    for j in range(group):
        qh = _rms(q[:, j * hd:(j + 1) * hd]) * qg_ref[...]
        if rope:
            qh = _rope(qh, cos, sin, lane)
        q_s[:, j * hd:(j + 1) * hd] = qh.astype(BF16)

    def chunk(ci, carry):
        r0 = pl.multiple_of(ci * q_block, q_block)
        if past_len:
            keys, vals = k_s[...], v_s[...]
        else:
            koff = pl.multiple_of((ci // nqb) * seq_len, seq_len)
            keys, vals = k_s[pl.ds(koff, nk), :], v_s[pl.ds(koff, nk), :]
        qc = jnp.concatenate([q_s[pl.ds(r0, q_block), j * hd:(j + 1) * hd] for j in range(group)],
                             axis=0)
        raw = lax.dot_general(qc, keys, (((1,), (1,)), ((), ())), preferred_element_type=F32)
        e = jnp.exp2((raw - jnp.max(raw, axis=-1, keepdims=True)) * (scale * LOG2_E))
        inv = 1.0 / jnp.sum(e, axis=-1, keepdims=True)
        oc = jnp.dot(e.astype(BF16), vals, preferred_element_type=F32) * inv
        for j in range(group):
            o_s[pl.ds(r0, q_block), j * hd:(j + 1) * hd] = oc[j * q_block:(j + 1) * q_block, :].astype(BF16)
        return carry

    lax.fori_loop(0, nchunks, chunk, 0, unroll=min(nchunks, 4))
    acc_ref[...] += _mm(o_s[...], wo_ref[...])

    @pl.when(gi == pl.num_programs(1) - 1)
    def _():
        o_ref[...] = x_ref[...] + _gate(m_ref, 1) * acc_ref[...]


def _rope_tables(n_tok, hd):
    rows = n_tok // GRID_W
    r_idx = jnp.broadcast_to(jnp.arange(rows)[:, None], (rows, GRID_W)).reshape(n_tok).astype(F32)
    c_idx = jnp.broadcast_to(jnp.arange(GRID_W)[None, :], (rows, GRID_W)).reshape(n_tok).astype(F32)
    n_freq = hd // 4
    inv = ROPE_THETA ** (-jnp.arange(n_freq, dtype=F32) / n_freq)
    ang = jnp.stack([r_idx[:, None] * inv, c_idx[:, None] * inv], axis=1)
    cos, sin = jnp.cos(ang), jnp.sin(ang)
    cos_full = jnp.concatenate([cos, cos], axis=-1).reshape(n_tok, hd)
    sin_signed = jnp.concatenate([-sin, sin], axis=-1).reshape(n_tok, hd)
    return cos_full, sin_signed


def _attn(x, m, row_of_tile, norm_g, layer, w_qkv, q_g, k_g, w_o, seq_len, q_block,
          cache_k=None, cache_v=None, rope=False, emit_kv=False):
    t, d = x.shape
    hd, kvh = HEAD_DIM, N_KV_HEADS
    n_heads = w_o.shape[0] // hd
    group = n_heads // kvh
    gw = group * hd
    tm = TOKEN_TILE
    assert t % tm == 0 and tm % seq_len == 0 and seq_len % q_block == 0
    past_len = 0 if cache_k is None else cache_k.shape[1] // kvh
    assert past_len == 0 or tm == seq_len
    in_specs = [
        pl.BlockSpec((tm, d), lambda i, g: (i, 0)),
        pl.BlockSpec((None, None, 3 * N_SUB, d), lambda i, g: (layer, row_of_tile(i), 0, 0)),
        pl.BlockSpec((None, N_SUB, d), lambda i, g: (layer, 0, 0)),
        pl.BlockSpec((d, gw), lambda i, g: (0, g)),
        pl.BlockSpec((d, hd), lambda i, g: (0, n_heads + g)),
        pl.BlockSpec((d, hd), lambda i, g: (0, n_heads + kvh + g)),
        pl.BlockSpec((1, hd), lambda i, g: (0, 0)),
        pl.BlockSpec((1, hd), lambda i, g: (0, 0)),
        pl.BlockSpec((gw, d), lambda i, g: (g, 0)),
    ]
    args = [x, m, norm_g, w_qkv, w_qkv, w_qkv, q_g.reshape(1, hd), k_g.reshape(1, hd), w_o]
    if rope:
        assert tm == seq_len
        cos, sin = _rope_tables(seq_len, hd)
        in_specs += [pl.BlockSpec((tm, hd), lambda i, g: (0, 0))] * 2
        args += [cos, sin]
    if past_len:
        in_specs += [pl.BlockSpec((None, past_len * kvh, hd), lambda i, g: (i, 0, 0))] * 2
        args += [cache_k, cache_v]
    out_specs = [pl.BlockSpec((tm, d), lambda i, g: (i, 0))]
    out_shape = [jax.ShapeDtypeStruct((t, d), F32)]
    if emit_kv:
        out_specs += [pl.BlockSpec((tm * kvh, hd), lambda i, g: (i, 0))] * 2
        out_shape += [jax.ShapeDtypeStruct((t * kvh, hd), F32)] * 2
    nkeys = past_len + tm
    outs = pl.pallas_call(
        functools.partial(_attn_kernel, seq_len=seq_len, q_block=q_block, past_len=past_len,
                          rope=rope, emit_kv=emit_kv, group=group, kvh=kvh),
        grid=(t // tm, kvh),
        in_specs=in_specs,
        out_specs=out_specs,
        out_shape=out_shape,
        scratch_shapes=[pltpu.VMEM((tm, d), BF16), pltpu.VMEM((tm, d), F32),
                        pltpu.VMEM((tm, gw), BF16), pltpu.VMEM((nkeys, hd), BF16),
                        pltpu.VMEM((nkeys, hd), BF16), pltpu.VMEM((tm, gw), BF16)],
        compiler_params=_params("arbitrary", "arbitrary"),
        name="gqa",
    )(*args)
    return outs


def kernel(x_prompt, x_sample, c, state_lru, cache_k, cache_v, c_ctx, mod_w, mod_b, norm_g,
           ffn_w_gu, ffn_w_down, lru_w_in, lru_conv_w, lru_conv_b, lru_gate_w, lru_gate_b,
           lru_lambda, lru_w_out, att_w_qkv, att_q_g, att_k_g, att_w_o, final_g):
    b, s, d = x_prompt.shape
    db, ds, _ = x_sample.shape
    depth = mod_w.shape[0]
    n_mixers = 2
    assert ds % TOKEN_TILE == 0 and 1 + db <= SUBLANES

    xp = x_prompt.reshape(b * s, d)
    xs = x_sample.reshape(db * ds, d)
    cvecs = jnp.concatenate([c_ctx[None], c, jnp.zeros((SUBLANES - 1 - db, d), F32)], axis=0)
    m = _modulation(cvecs, mod_w, mod_b).reshape(depth, SUBLANES, 3 * N_SUB, d)

    tiles_per_sample = ds // TOKEN_TILE
    prompt_row = lambda i: 0
    sample_row = lambda i: 1 + i // tiles_per_sample
    prompt_tiles = (b * s) // TOKEN_TILE
    both_row = lambda i: jnp.where(i < prompt_tiles, 0, 1 + (i - prompt_tiles) // tiles_per_sample)

    new_states, new_k, new_v = [], [], []
    for layer in range(depth):
        j = layer // n_mixers
        last = layer == depth - 1
        xp, xs = _ffn(xp, xs, m, both_row, norm_g, ffn_w_gu, ffn_w_down, layer, 0, 0)
        if layer % n_mixers == 0:
            lru_p = (lru_w_in[j], lru_conv_w[j], lru_conv_b[j], lru_gate_w[j], lru_gate_b[j],
                     lru_lambda[j], lru_w_out[j])
            xp, st = _lru(xp, m, prompt_row, norm_g, layer, *lru_p, seq_len=s, h0=None,
                          emit_state=True)
            new_states.append(st)
            xs, _ = _lru(xs, m, sample_row, norm_g, layer, *lru_p, seq_len=ds,
                         h0=state_lru[:, j], emit_state=False)
        else:
            att_p = (att_w_qkv[j], att_q_g[j], att_k_g[j], att_w_o[j])
            xp, kp, vp = _attn(xp, m, prompt_row, norm_g, layer, *att_p, seq_len=s, q_block=s,
                               emit_kv=True)
            new_k.append(kp.reshape(b, s, N_KV_HEADS, HEAD_DIM))
            new_v.append(vp.reshape(b, s, N_KV_HEADS, HEAD_DIM))
            past = cache_k.shape[2]
            ck = cache_k[:, j].reshape(db, past * N_KV_HEADS, HEAD_DIM)
            cv = cache_v[:, j].reshape(db, past * N_KV_HEADS, HEAD_DIM)
            (xs,) = _attn(xs, m, sample_row, norm_g, layer, *att_p, seq_len=ds, q_block=128,
                          cache_k=ck, cache_v=cv, rope=True)
        fg = final_g if last else None
        xp, xs = _ffn(xp, xs, m, both_row, norm_g, ffn_w_gu, ffn_w_down, layer, 1, 2, final_g=fg)

    y_prompt = xp.reshape(b, s, d)
    y_sample = xs.reshape(db, ds, d)
    return (y_prompt, y_sample, jnp.stack(new_states, axis=1), jnp.stack(new_k, axis=1),
            jnp.stack(new_v, axis=1))
```

```python
import functools

import jax
import jax.numpy as jnp
import numpy as np
from jax import lax
from jax.experimental import pallas as pl
from jax.experimental.pallas import tpu as pltpu

F32 = jnp.float32
BF16 = jnp.bfloat16

EPS = 1e-6
LRU_C = 8.0
LOG2_E = 1.4426950408889634
GRID_W = 64
ROPE_THETA = 10000.0
N_SUB = 3
HEAD_DIM = 128
N_KV_HEADS = 2
LRU_BLOCKS = 16

V7X_VMEM_LIMIT_BYTES = 56 * 1024 * 1024
SUBLANES = 8
TOKEN_TILE = 1024
FF_CHUNK = 256
FFN_ROW_BLOCK = 256
FFN_STAGE_SLOTS = 4
LRU_CHUNK = 512
LRU_GATE_CHUNK = 256
MOD_CHUNK = 1536


def _params(*semantics):
    return pltpu.CompilerParams(dimension_semantics=semantics,
                                vmem_limit_bytes=V7X_VMEM_LIMIT_BYTES)


def _mm(a_bf16, w_f32):
    return jnp.dot(a_bf16, w_f32.astype(BF16), preferred_element_type=F32)


def _rms(x):
    return x * lax.rsqrt(jnp.mean(x * x, axis=-1, keepdims=True) + EPS)


def _sub_in(x, m_ref, g_ref, sidx):
    shift = m_ref[3 * sidx:3 * sidx + 1, :]
    scale = m_ref[3 * sidx + 1:3 * sidx + 2, :]
    return (_rms(x) * g_ref[sidx:sidx + 1, :]) * (1.0 + scale) + shift


def _gate(m_ref, sidx):
    return m_ref[3 * sidx + 2:3 * sidx + 3, :]


def _mod_kernel(cv_ref, w_ref, b_ref, o_ref):
    cv = cv_ref[...]
    act = (cv * jax.nn.sigmoid(cv)).astype(BF16)
    o_ref[...] = _mm(act, w_ref[...]) + b_ref[...]


def _modulation(cvecs, mod_w, mod_b):
    depth, d, n = mod_w.shape
    rows = cvecs.shape[0]
    tn = MOD_CHUNK
    assert n % tn == 0
    return pl.pallas_call(
        _mod_kernel,
        grid=(depth, n // tn),
        in_specs=[
            pl.BlockSpec((rows, d), lambda l, j: (0, 0)),
            pl.BlockSpec((None, d, tn), lambda l, j: (l, 0, j)),
            pl.BlockSpec((None, 1, tn), lambda l, j: (l, 0, j)),
        ],
        out_specs=pl.BlockSpec((None, rows, tn), lambda l, j: (l, 0, j)),
        out_shape=jax.ShapeDtypeStruct((depth, rows, n), F32),
        compiler_params=_params("arbitrary", "arbitrary"),
        name="modulation",
    )(cvecs, mod_w, mod_b.reshape(depth, 1, n))


def _ffn_kernel(*refs, layer, s, sidx, final, n_first):
    if final:
        (m_ref, g_ref, fg_ref, xa_hbm, xb_hbm, wgu_hbm, wd_hbm, oa_hbm, ob_hbm,
         wgu_res, wd_res, act_ref, h0_ref, act0_ref, gu_buf, wd_buf, xbuf, obuf,
         w_sem, xsem, osem) = refs
    else:
        (m_ref, g_ref, xa_hbm, xb_hbm, wgu_hbm, wd_hbm, oa_hbm, ob_hbm,
         wgu_res, wd_res, act_ref, h0_ref, act0_ref, gu_buf, wd_buf, xbuf, obuf,
         w_sem, xsem, osem) = refs
    i = pl.program_id(0)
    n = pl.num_programs(0)
    tm = xbuf.shape[1]
    _, d, tf = wgu_res.shape
    f = wd_res.shape[0]
    nk = f // tf
    rb = act_ref.shape[0]
    nslot = gu_buf.shape[0]
    ahead = nslot - 1
    slot = lax.rem(i, 2)
    other = 1 - slot

    def stage_copies(k, sl):
        gcols = pl.ds(pl.multiple_of(k * tf, tf), tf)
        ucols = pl.ds(pl.multiple_of(f + k * tf, tf), tf)
        return (
            pltpu.make_async_copy(wgu_hbm.at[layer, s, :, gcols], gu_buf.at[sl, 0], w_sem.at[sl, 0]),
            pltpu.make_async_copy(wgu_hbm.at[layer, s, :, ucols], gu_buf.at[sl, 1], w_sem.at[sl, 1]),
            pltpu.make_async_copy(wd_hbm.at[layer, s, gcols, :], wd_buf.at[sl], w_sem.at[sl, 2]),
        )

    def tile_rows(hbm, tile):
        return hbm.at[pl.ds(pl.multiple_of(tile * tm, tm), tm), :]

    def x_copy(which, tile, sl):
        return pltpu.make_async_copy(tile_rows((xa_hbm, xb_hbm)[which], tile), xbuf.at[sl],
                                     xsem.at[sl])

    def o_copy(which, tile, sl):
        return pltpu.make_async_copy(obuf.at[sl], tile_rows((oa_hbm, ob_hbm)[which], tile),
                                     osem.at[sl])

    def start_by_stream(make, tile, sl):
        @pl.when(tile < n_first)
        def _():
            make(0, tile, sl).start()

        @pl.when(tile >= n_first)
        def _():
            make(1, tile - n_first, sl).start()

    @pl.when(i == 0)
    def _():
        x_copy(0, 0, 0).start()
        for k in range(ahead):
            for cp in stage_copies(k, k):
                cp.start()

    @pl.when(i + 1 < n)
    def _():
        start_by_stream(x_copy, i + 1, other)

    x_copy(0, 0, slot).wait()

    @pl.when(i >= 2)
    def _():
        o_copy(0, 0, slot).wait()

    def swiglu(h, wg, wu):
        gt = jnp.dot(h, wg, preferred_element_type=F32)
        up = jnp.dot(h, wu, preferred_element_type=F32)
        return ((gt * jax.nn.sigmoid(gt)) * up).astype(BF16)

    def down_and_store(x, rows):
        ff = jnp.dot(act_ref[...], wd_res[...], preferred_element_type=F32)
        y = x + (0.5 * _gate(m_ref, sidx)) * ff
        if final:
            y = _rms(y) * fg_ref[...]
        obuf[slot, rows, :] = y

    def row_block(r, carry):
        rows = pl.ds(pl.multiple_of(r * rb, rb), rb)
        x = xbuf[slot, rows, :]
        h = _sub_in(x, m_ref, g_ref, sidx).astype(BF16)
        for k in range(nk):
            act_ref[:, k * tf:(k + 1) * tf] = swiglu(h, wgu_res[k], wgu_res[nk + k])
        down_and_store(x, rows)
        return carry

    def first_tile_chunk(k, carry):
        sl = lax.rem(k, nslot)
        for cp in stage_copies(k, sl):
            cp.wait()

        @pl.when(k + ahead < nk)
        def _():
            for cp in stage_copies(k + ahead, lax.rem(k + ahead, nslot)):
                cp.start()

        wg = gu_buf[sl, 0].astype(BF16)
        wu = gu_buf[sl, 1].astype(BF16)
        wgu_res[k] = wg
        wgu_res[nk + k] = wu
        wd_res[pl.ds(pl.multiple_of(k * tf, tf), tf), :] = wd_buf[sl].astype(BF16)
        act0_ref[k] = swiglu(h0_ref[...], wg, wu)
        return carry

    def first_tile_down(r, carry):
        rows = pl.ds(pl.multiple_of(r * rb, rb), rb)
        for k in range(nk):
            act_ref[:, k * tf:(k + 1) * tf] = act0_ref[k, rows, :]
        down_and_store(xbuf[slot, rows, :], rows)
        return carry

    @pl.when(i == 0)
    def _():
        h0_ref[...] = _sub_in(xbuf[slot], m_ref, g_ref, sidx).astype(BF16)
        lax.fori_loop(0, nk, first_tile_chunk, 0)
        lax.fori_loop(0, tm // rb, first_tile_down, 0)

    @pl.when(i > 0)
    def _():
        lax.fori_loop(0, tm // rb, row_block, 0, unroll=2)

    start_by_stream(o_copy, i, slot)

    @pl.when(i == n - 1)
    def _():
        o_copy(0, 0, other).wait()
        o_copy(0, 0, slot).wait()


def _ffn(xa, xb, m, row_of_tile, norm_g, w_gu, w_down, layer, s, sidx, final_g=None):
    (ta, d), tb = xa.shape, xb.shape[0]
    f = w_down.shape[2]
    tm, tf, rb = TOKEN_TILE, FF_CHUNK, FFN_ROW_BLOCK
    assert ta % tm == 0 and tb % tm == 0 and f % tf == 0 and (ta + tb) // tm >= 2
    assert tm % rb == 0 and f // tf >= FFN_STAGE_SLOTS
    final = final_g is not None
    in_specs = [
        pl.BlockSpec((None, None, 3 * N_SUB, d), lambda i: (layer, row_of_tile(i), 0, 0)),
        pl.BlockSpec((None, N_SUB, d), lambda i: (layer, 0, 0)),
    ]
    args = [m, norm_g]
    if final:
        in_specs.append(pl.BlockSpec((1, d), lambda i: (0, 0)))
        args.append(final_g.reshape(1, d))
    in_specs += [pl.BlockSpec(memory_space=pl.ANY)] * 4
    args += [xa, xb, w_gu, w_down]
    return pl.pallas_call(
        functools.partial(_ffn_kernel, layer=layer, s=s, sidx=sidx, final=final,
                          n_first=ta // tm),
        grid=((ta + tb) // tm,),
        in_specs=in_specs,
        out_specs=[pl.BlockSpec(memory_space=pl.ANY)] * 2,
        out_shape=[jax.ShapeDtypeStruct((ta, d), F32), jax.ShapeDtypeStruct((tb, d), F32)],
        scratch_shapes=[
            pltpu.VMEM((2 * f // tf, d, tf), BF16), pltpu.VMEM((f, d), BF16),
            pltpu.VMEM((rb, f), BF16),
            pltpu.VMEM((tm, d), BF16), pltpu.VMEM((f // tf, tm, tf), BF16),
            pltpu.VMEM((FFN_STAGE_SLOTS, 2, d, tf), F32),
            pltpu.VMEM((FFN_STAGE_SLOTS, tf, d), F32),
            pltpu.VMEM((2, tm, d), F32), pltpu.VMEM((2, tm, d), F32),
            pltpu.SemaphoreType.DMA((FFN_STAGE_SLOTS, 3)),
            pltpu.SemaphoreType.DMA((2,)),
            pltpu.SemaphoreType.DMA((2,)),
        ],
        compiler_params=_params("arbitrary"),
        name="ffn",
    )(*args)


def _gelu_tanh(x):
    c = np.float32(np.sqrt(2.0 / np.pi))
    return x * (0.5 * (1.0 + jnp.tanh(c * (x + 0.044715 * (x * x * x)))))


def _time_permutation(seq_len):
    segs = seq_len // SUBLANES
    p = np.arange(seq_len)
    j, s = p // SUBLANES, p % SUBLANES
    mat = np.zeros((seq_len, seq_len), np.float32)
    mat[p, s * segs + j] = 1.0
    return mat


def _per_sequence(mat, rows, seq_len):
    parts = [jnp.dot(mat, rows[lo:lo + seq_len, :], preferred_element_type=F32).astype(BF16)
             for lo in range(0, rows.shape[0], seq_len)]
    return jnp.concatenate(parts, axis=0) if len(parts) > 1 else parts[0]


def _lru_kernel(*refs, seq_len, has_h0, emit_state, conv_left):
    refs = list(refs)
    (x_ref, m_ref, g_ref, wx_ref, wy_ref, cw_ref, cb_ref, gw_ref, gb_ref, lam_ref,
     wo_ref, pin_ref, pout_ref) = refs[:13]
    pos = 13
    h0_ref = None
    if has_h0:
        h0_ref = refs[pos]
        pos += 1
    o_ref = refs[pos]
    pos += 1
    st_ref = None
    if emit_state:
        st_ref = refs[pos]
        pos += 1
    h_ref, acc_ref, af_ref, uf_ref, ab_ref, ub_ref = refs[pos:]

    c = pl.program_id(1)
    tm, cw_cols = af_ref.shape
    nseq = tm // seq_len
    segs = seq_len // SUBLANES

    @pl.when(c == 0)
    def _():
        h = _sub_in(x_ref[...], m_ref, g_ref, 1).astype(BF16)
        h_ref[...] = _per_sequence(pin_ref[...], h, seq_len)
        acc_ref[...] = jnp.zeros_like(acc_ref)

    h = h_ref[...]
    xb = _mm(h, wx_ref[...])
    yb = _mm(h, wy_ref[...])

    sub = lax.broadcasted_iota(jnp.int32, (SUBLANES, cw_cols), 0)

    def next_segment(grp):
        return jnp.where(sub < SUBLANES - 1, pltpu.roll(grp, SUBLANES - 1, 0), 0.0)

    def prev_segment(grp):
        return jnp.where(sub > 0, pltpu.roll(grp, 1, 0), 0.0)

    def tap(seq, off):
        n = abs(off) * SUBLANES
        if off == 0:
            return seq
        if off > 0:
            edge = [next_segment(seq[g * SUBLANES:(g + 1) * SUBLANES, :]) for g in range(off)]
            return jnp.concatenate([seq[n:, :]] + edge, axis=0)
        start = seq_len - n
        edge = [prev_segment(seq[start + g * SUBLANES:start + (g + 1) * SUBLANES, :])
                for g in range(-off)]
        return jnp.concatenate(edge + [seq[:start, :]], axis=0)

    cw = cw_ref[...]
    assert cw.shape[0] - 1 < segs
    xcs = []
    for sq in range(nseq):
        seq = xb[sq * seq_len:(sq + 1) * seq_len, :]
        acc = cb_ref[...]
        for k in range(cw.shape[0]):
            acc = acc + tap(seq, k - conv_left) * cw[k:k + 1, :]
        xcs.append(acc)
    xc = jnp.concatenate(xcs, axis=0) if nseq > 1 else xcs[0]

    gb = gb_ref[...]
    lam = lam_ref[...]
    gc = gw_ref.shape[1]
    for ch in range(cw_cols // gc):
        cols = slice(ch * gc, (ch + 1) * gc)
        xcc = xc[:, cols]
        gl = jnp.dot(xcc.astype(BF16), gw_ref[ch], preferred_element_type=F32)
        for d, (a_ref, u_ref) in enumerate(((af_ref, uf_ref), (ab_ref, ub_ref))):
            tr = jnp.tanh(gl[:, (2 * d) * gc:(2 * d + 1) * gc] + 0.5 * gb[2 * d:2 * d + 1, cols])
            ig = 0.5 * jnp.tanh(gl[:, (2 * d + 1) * gc:(2 * d + 2) * gc]
                                + 0.5 * gb[2 * d + 1:2 * d + 2, cols]) + 0.5
            lm = lam[d:d + 1, cols]
            log_sig = jnp.minimum(lm, 0.0) - jnp.log1p(jnp.exp(-jnp.abs(lm)))
            c2 = (0.5 * LRU_C * LOG2_E) * log_sig
            a = jnp.exp2(c2 * tr + c2)
            v = 1.0 - a * a
            u = jnp.where(v > 0.0, v * lax.rsqrt(v), 0.0) * (ig * xcc)
            a_ref[:, cols] = a
            u_ref[:, cols] = u

    zero = jnp.zeros((SUBLANES, cw_cols), F32)
    one = jnp.ones((SUBLANES, cw_cols), F32)

    def local_scan(j, carry):
        out = []
        for sq in range(nseq):
            for dd, (a_ref, u_ref) in enumerate(((af_ref, uf_ref), (ab_ref, ub_ref))):
                grp = j if dd == 0 else segs - 1 - j
                rows = pl.ds(pl.multiple_of(sq * seq_len + grp * SUBLANES, SUBLANES), SUBLANES)
                hl, pc = carry[2 * (2 * sq + dd)], carry[2 * (2 * sq + dd) + 1]
                a = a_ref[rows, :]
                hl = a * hl + u_ref[rows, :]
                pc = a * pc
                u_ref[rows, :] = hl
                a_ref[rows, :] = pc
                out += [hl, pc]
        return tuple(out)

    totals = lax.fori_loop(0, segs, local_scan, (zero, one) * (2 * nseq),
                           unroll=2 if nseq == 1 else 1)

    gelu_y = _gelu_tanh(yb)
    ys = []
    for sq in range(nseq):
        entry = []
        for dd in range(2):
            hl, pc = totals[2 * (2 * sq + dd)], totals[2 * (2 * sq + dd) + 1]
            state = h0_ref[sq, dd:dd + 1, :] if has_h0 else jnp.zeros((1, cw_cols), F32)
            rows = [None] * SUBLANES
            order = range(SUBLANES) if dd == 0 else range(SUBLANES - 1, -1, -1)
            for s in order:
                rows[s] = state
                state = pc[s:s + 1, :] * state + hl[s:s + 1, :]
            entry.append(jnp.concatenate(rows, axis=0))
            if emit_state:
                st_ref[sq, dd:dd + 1, :] = state
        lo, hi = sq * seq_len, (sq + 1) * seq_len
        grouped = (segs, SUBLANES, cw_cols)
        hsum = ((uf_ref[lo:hi, :].reshape(grouped) + af_ref[lo:hi, :].reshape(grouped) * entry[0][None])
                + (ub_ref[lo:hi, :].reshape(grouped) + ab_ref[lo:hi, :].reshape(grouped) * entry[1][None]))
        ys.append(hsum.reshape(seq_len, cw_cols) * gelu_y[lo:hi, :])
    y = jnp.concatenate(ys, axis=0) if nseq > 1 else ys[0]

    y = _per_sequence(pout_ref[...], y.astype(BF16), seq_len)
    acc_ref[...] += _mm(y, wo_ref[...])

    @pl.when(c == pl.num_programs(1) - 1)
    def _():
        o_ref[...] = x_ref[...] + _gate(m_ref, 1) * acc_ref[...]


def _lru_gate_weights(gate_w):
    nd, ng, nb, bw, _ = gate_w.shape
    per = LRU_GATE_CHUNK // bw
    nc = nb // per
    w = (0.5 * gate_w).astype(BF16).reshape(nd * ng, nc, per, bw, bw)
    rows = []
    for n in range(per):
        blk = jnp.transpose(w[:, :, n], (1, 2, 0, 3))
        blk = jnp.pad(blk, ((0, 0), (0, 0), (0, 0), (n * bw, (per - 1 - n) * bw)))
        rows.append(blk.reshape(nc, bw, nd * ng * per * bw))
    return jnp.concatenate(rows, axis=1)


def _lru(x, m, row_of_tile, norm_g, layer, w_in, conv_w, conv_b, gate_w, gate_b, lam, w_out,
         seq_len, h0, emit_state):
    t, d = x.shape
    r = w_out.shape[0]
    tm, cb, gc = TOKEN_TILE, LRU_CHUNK, LRU_GATE_CHUNK
    assert t % tm == 0 and tm % seq_len == 0 and r % cb == 0 and seq_len % SUBLANES == 0
    assert cb % gc == 0
    nc = r // cb
    nseq = tm // seq_len
    has_h0 = h0 is not None
    in_specs = [
        pl.BlockSpec((tm, d), lambda i, c: (i, 0)),
        pl.BlockSpec((None, None, 3 * N_SUB, d), lambda i, c: (layer, row_of_tile(i), 0, 0)),
        pl.BlockSpec((None, N_SUB, d), lambda i, c: (layer, 0, 0)),
        pl.BlockSpec((d, cb), lambda i, c: (0, c)),
        pl.BlockSpec((d, cb), lambda i, c: (0, nc + c)),
        pl.BlockSpec((conv_w.shape[0], cb), lambda i, c: (0, c)),
        pl.BlockSpec((1, cb), lambda i, c: (0, c)),
        pl.BlockSpec((cb // gc, gc, 4 * gc), lambda i, c: (c, 0, 0)),
        pl.BlockSpec((4, cb), lambda i, c: (0, c)),
        pl.BlockSpec((2, cb), lambda i, c: (0, c)),
        pl.BlockSpec((cb, d), lambda i, c: (c, 0)),
        pl.BlockSpec((seq_len, seq_len), lambda i, c: (0, 0), pipeline_mode=pl.Buffered(1)),
        pl.BlockSpec((seq_len, seq_len), lambda i, c: (0, 0), pipeline_mode=pl.Buffered(1)),
    ]
    perm = _time_permutation(seq_len)
    args = [x, m, norm_g, w_in, w_in, conv_w, conv_b.reshape(1, r), _lru_gate_weights(gate_w),
            gate_b.reshape(4, r), lam, w_out, jnp.asarray(perm, BF16), jnp.asarray(perm.T, BF16)]
    if has_h0:
        in_specs.append(pl.BlockSpec((nseq, 2, cb), lambda i, c: (i, 0, c)))
        args.append(h0)
    out_specs = [pl.BlockSpec((tm, d), lambda i, c: (i, 0))]
    out_shape = [jax.ShapeDtypeStruct((t, d), F32)]
    if emit_state:
        out_specs.append(pl.BlockSpec((nseq, 2, cb), lambda i, c: (i, 0, c)))
        out_shape.append(jax.ShapeDtypeStruct((t // seq_len, 2, r), F32))
    outs = pl.pallas_call(
        functools.partial(_lru_kernel, seq_len=seq_len, has_h0=has_h0, emit_state=emit_state,
                          conv_left=(conv_w.shape[0] - 1) // 2),
        grid=(t // tm, nc),
        in_specs=in_specs,
        out_specs=out_specs,
        out_shape=out_shape,
        scratch_shapes=[pltpu.VMEM((tm, d), BF16), pltpu.VMEM((tm, d), F32)]
        + [pltpu.VMEM((tm, cb), F32)] * 4,
        compiler_params=_params("arbitrary", "arbitrary"),
        name="rglru",
    )(*args)
    return outs if emit_state else (outs[0], None)


def _rope(x, cos, sin_signed, lane):
    hd = x.shape[1]
    partner = jnp.where((lane & 32) == 0, pltpu.roll(x, hd - 32, 1), pltpu.roll(x, 32, 1))
    return x * cos + partner * sin_signed


def _attn_kernel(*refs, seq_len, q_block, past_len, rope, emit_kv, group, kvh):
    refs = list(refs)
    x_ref, m_ref, g_ref, wq_ref, wk_ref, wv_ref, qg_ref, kg_ref, wo_ref = refs[:9]
    pos = 9
    if rope:
        cos_ref, sin_ref = refs[pos:pos + 2]
        pos += 2
    if past_len:
        ck_ref, cv_ref = refs[pos:pos + 2]
        pos += 2
    o_ref = refs[pos]
    pos += 1
    if emit_kv:
        kn_ref, vn_ref = refs[pos:pos + 2]
        pos += 2
    h_ref, acc_ref, q_s, k_s, v_s, o_s = refs[pos:]

    gi = pl.program_id(1)
    tm = x_ref.shape[0]
    hd = k_s.shape[1]
    nqb = seq_len // q_block
    nchunks = (tm // seq_len) * nqb
    nk = past_len + seq_len
    scale = hd ** -0.5

    @pl.when(gi == 0)
    def _():
        h_ref[...] = _sub_in(x_ref[...], m_ref, g_ref, 1).astype(BF16)
        acc_ref[...] = jnp.zeros_like(acc_ref)

    h = h_ref[...]
    q = _mm(h, wq_ref[...])
    k = _rms(_mm(h, wk_ref[...])) * kg_ref[...]
    v = _mm(h, wv_ref[...])
    if emit_kv:
        kn_ref[pl.ds(gi, tm, stride=kvh), :] = k
        vn_ref[pl.ds(gi, tm, stride=kvh), :] = v
    if rope:
        lane = lax.broadcasted_iota(jnp.int32, (tm, hd), 1)
        cos, sin = cos_ref[...], sin_ref[...]
        k = _rope(k, cos, sin, lane)
    k_s[past_len:past_len + tm, :] = k.astype(BF16)
    v_s[past_len:past_len + tm, :] = v.astype(BF16)
    if past_len:
        k_s[0:past_len, :] = ck_ref[pl.ds(gi, past_len, stride=kvh), :].astype(BF16)
        v_s[0:past_len, :] = cv_ref[pl.ds(gi, past_len, stride=kvh), :].astype(BF16)
    for j in range(group):
        qh = _rms(q[:, j * hd:(j + 1) * hd]) * qg_ref[...]
        if rope:
            qh = _rope(qh, cos, sin, lane)
        q_s[:, j * hd:(j + 1) * hd] = qh.astype(BF16)

    def chunk(ci, carry):
        r0 = pl.multiple_of(ci * q_block, q_block)
        if past_len:
            keys, vals = k_s[...], v_s[...]
        else:
            koff = pl.multiple_of((ci // nqb) * seq_len, seq_len)
            keys, vals = k_s[pl.ds(koff, nk), :], v_s[pl.ds(koff, nk), :]
        qc = jnp.concatenate([q_s[pl.ds(r0, q_block), j * hd:(j + 1) * hd] for j in range(group)],
                             axis=0)
        raw = lax.dot_general(qc, keys, (((1,), (1,)), ((), ())), preferred_element_type=F32)
        e = jnp.exp2((raw - jnp.max(raw, axis=-1, keepdims=True)) * (scale * LOG2_E))
        inv = 1.0 / jnp.sum(e, axis=-1, keepdims=True)
        oc = jnp.dot(e.astype(BF16), vals, preferred_element_type=F32) * inv
        for j in range(group):
            o_s[pl.ds(r0, q_block), j * hd:(j + 1) * hd] = oc[j * q_block:(j + 1) * q_block, :].astype(BF16)
        return carry

    lax.fori_loop(0, nchunks, chunk, 0, unroll=min(nchunks, 4))
    acc_ref[...] += _mm(o_s[...], wo_ref[...])

    @pl.when(gi == pl.num_programs(1) - 1)
    def _():
        o_ref[...] = x_ref[...] + _gate(m_ref, 1) * acc_ref[...]


def _rope_tables(n_tok, hd):
    rows = n_tok // GRID_W
    r_idx = jnp.broadcast_to(jnp.arange(rows)[:, None], (rows, GRID_W)).reshape(n_tok).astype(F32)
    c_idx = jnp.broadcast_to(jnp.arange(GRID_W)[None, :], (rows, GRID_W)).reshape(n_tok).astype(F32)
    n_freq = hd // 4
    inv = ROPE_THETA ** (-jnp.arange(n_freq, dtype=F32) / n_freq)
    ang = jnp.stack([r_idx[:, None] * inv, c_idx[:, None] * inv], axis=1)
    cos, sin = jnp.cos(ang), jnp.sin(ang)
    cos_full = jnp.concatenate([cos, cos], axis=-1).reshape(n_tok, hd)
    sin_signed = jnp.concatenate([-sin, sin], axis=-1).reshape(n_tok, hd)
    return cos_full, sin_signed


def _attn(x, m, row_of_tile, norm_g, layer, w_qkv, q_g, k_g, w_o, seq_len, q_block,
          cache_k=None, cache_v=None, rope=False, emit_kv=False):
    t, d = x.shape
    hd, kvh = HEAD_DIM, N_KV_HEADS
    n_heads = w_o.shape[0] // hd
    group = n_heads // kvh
    gw = group * hd
    tm = TOKEN_TILE
    assert t % tm == 0 and tm % seq_len == 0 and seq_len % q_block == 0
    past_len = 0 if cache_k is None else cache_k.shape[1] // kvh
    assert past_len == 0 or tm == seq_len
    in_specs = [
        pl.BlockSpec((tm, d), lambda i, g: (i, 0)),
        pl.BlockSpec((None, None, 3 * N_SUB, d), lambda i, g: (layer, row_of_tile(i), 0, 0)),
        pl.BlockSpec((None, N_SUB, d), lambda i, g: (layer, 0, 0)),
        pl.BlockSpec((d, gw), lambda i, g: (0, g)),
        pl.BlockSpec((d, hd), lambda i, g: (0, n_heads + g)),
        pl.BlockSpec((d, hd), lambda i, g: (0, n_heads + kvh + g)),
        pl.BlockSpec((1, hd), lambda i, g: (0, 0)),
        pl.BlockSpec((1, hd), lambda i, g: (0, 0)),
        pl.BlockSpec((gw, d), lambda i, g: (g, 0)),
    ]
    args = [x, m, norm_g, w_qkv, w_qkv, w_qkv, q_g.reshape(1, hd), k_g.reshape(1, hd), w_o]
    if rope:
        assert tm == seq_len
        cos, sin = _rope_tables(seq_len, hd)
        in_specs += [pl.BlockSpec((tm, hd), lambda i, g: (0, 0))] * 2
        args += [cos, sin]
    if past_len:
        in_specs += [pl.BlockSpec((None, past_len * kvh, hd), lambda i, g: (i, 0, 0))] * 2
        args += [cache_k, cache_v]
    out_specs = [pl.BlockSpec((tm, d), lambda i, g: (i, 0))]
    out_shape = [jax.ShapeDtypeStruct((t, d), F32)]
    if emit_kv:
        out_specs += [pl.BlockSpec((tm * kvh, hd), lambda i, g: (i, 0))] * 2
        out_shape += [jax.ShapeDtypeStruct((t * kvh, hd), F32)] * 2
    nkeys = past_len + tm
    outs = pl.pallas_call(
        functools.partial(_attn_kernel, seq_len=seq_len, q_block=q_block, past_len=past_len,
                          rope=rope, emit_kv=emit_kv, group=group, kvh=kvh),
        grid=(t // tm, kvh),
        in_specs=in_specs,
        out_specs=out_specs,
        out_shape=out_shape,
        scratch_shapes=[pltpu.VMEM((tm, d), BF16), pltpu.VMEM((tm, d), F32),
                        pltpu.VMEM((tm, gw), BF16), pltpu.VMEM((nkeys, hd), BF16),
                        pltpu.VMEM((nkeys, hd), BF16), pltpu.VMEM((tm, gw), BF16)],
        compiler_params=_params("arbitrary", "arbitrary"),
        name="gqa",
    )(*args)
    return outs


def kernel(x_prompt, x_sample, c, state_lru, cache_k, cache_v, c_ctx, mod_w, mod_b, norm_g,
           ffn_w_gu, ffn_w_down, lru_w_in, lru_conv_w, lru_conv_b, lru_gate_w, lru_gate_b,
           lru_lambda, lru_w_out, att_w_qkv, att_q_g, att_k_g, att_w_o, final_g):
    b, s, d = x_prompt.shape
    db, ds, _ = x_sample.shape
    depth = mod_w.shape[0]
    n_mixers = 2
    assert ds % TOKEN_TILE == 0 and 1 + db <= SUBLANES

    xp = x_prompt.reshape(b * s, d)
    xs = x_sample.reshape(db * ds, d)
    cvecs = jnp.concatenate([c_ctx[None], c, jnp.zeros((SUBLANES - 1 - db, d), F32)], axis=0)
    m = _modulation(cvecs, mod_w, mod_b).reshape(depth, SUBLANES, 3 * N_SUB, d)

    tiles_per_sample = ds // TOKEN_TILE
    prompt_row = lambda i: 0
    sample_row = lambda i: 1 + i // tiles_per_sample
    prompt_tiles = (b * s) // TOKEN_TILE
    both_row = lambda i: jnp.where(i < prompt_tiles, 0, 1 + (i - prompt_tiles) // tiles_per_sample)

    new_states, new_k, new_v = [], [], []
    for layer in range(depth):
        j = layer // n_mixers
        last = layer == depth - 1
        xp, xs = _ffn(xp, xs, m, both_row, norm_g, ffn_w_gu, ffn_w_down, layer, 0, 0)
        if layer % n_mixers == 0:
            lru_p = (lru_w_in[j], lru_conv_w[j], lru_conv_b[j], lru_gate_w[j], lru_gate_b[j],
                     lru_lambda[j], lru_w_out[j])
            xp, st = _lru(xp, m, prompt_row, norm_g, layer, *lru_p, seq_len=s, h0=None,
                          emit_state=True)
            new_states.append(st)
            xs, _ = _lru(xs, m, sample_row, norm_g, layer, *lru_p, seq_len=ds,
                         h0=state_lru[:, j], emit_state=False)
        else:
            att_p = (att_w_qkv[j], att_q_g[j], att_k_g[j], att_w_o[j])
            xp, kp, vp = _attn(xp, m, prompt_row, norm_g, layer, *att_p, seq_len=s, q_block=s,
                               emit_kv=True)
            new_k.append(kp.reshape(b, s, N_KV_HEADS, HEAD_DIM))
            new_v.append(vp.reshape(b, s, N_KV_HEADS, HEAD_DIM))
            past = cache_k.shape[2]
            ck = cache_k[:, j].reshape(db, past * N_KV_HEADS, HEAD_DIM)
            cv = cache_v[:, j].reshape(db, past * N_KV_HEADS, HEAD_DIM)
            (xs,) = _attn(xs, m, sample_row, norm_g, layer, *att_p, seq_len=ds, q_block=128,
                          cache_k=ck, cache_v=cv, rope=True)
        fg = final_g if last else None
        xp, xs = _ffn(xp, xs, m, both_row, norm_g, ffn_w_gu, ffn_w_down, layer, 1, 2, final_g=fg)

    y_prompt = xp.reshape(b, s, d)
    y_sample = xs.reshape(db, ds, d)
    return (y_prompt, y_sample, jnp.stack(new_states, axis=1), jnp.stack(new_k, axis=1),
            jnp.stack(new_v, axis=1))
```

```python
import functools

import jax
import jax.numpy as jnp
import numpy as np
from jax import lax
from jax.experimental import pallas as pl
from jax.experimental.pallas import tpu as pltpu

F32 = jnp.float32
BF16 = jnp.bfloat16

EPS = 1e-6
LRU_C = 8.0
LOG2_E = 1.4426950408889634
GRID_W = 64
ROPE_THETA = 10000.0
N_SUB = 3
HEAD_DIM = 128
N_KV_HEADS = 2
LRU_BLOCKS = 16

V7X_VMEM_LIMIT_BYTES = 56 * 1024 * 1024
SUBLANES = 8
TOKEN_TILE = 1024
FF_CHUNK = 256
FFN_ROW_BLOCK = 256
FFN_STAGE_SLOTS = 3
LRU_CHUNK = 512
LRU_GATE_CHUNK = 256
MOD_CHUNK = 1536


def _params(*semantics):
    return pltpu.CompilerParams(dimension_semantics=semantics,
                                vmem_limit_bytes=V7X_VMEM_LIMIT_BYTES)


def _mm(a_bf16, w_f32):
    return jnp.dot(a_bf16, w_f32.astype(BF16), preferred_element_type=F32)


def _rms(x):
    return x * lax.rsqrt(jnp.mean(x * x, axis=-1, keepdims=True) + EPS)


def _sub_in(x, m_ref, g_ref, sidx):
    shift = m_ref[3 * sidx:3 * sidx + 1, :]
    scale = m_ref[3 * sidx + 1:3 * sidx + 2, :]
    return (_rms(x) * g_ref[sidx:sidx + 1, :]) * (1.0 + scale) + shift


def _gate(m_ref, sidx):
    return m_ref[3 * sidx + 2:3 * sidx + 3, :]


def _mod_kernel(cv_ref, w_ref, b_ref, o_ref):
    cv = cv_ref[...]
    act = (cv * jax.nn.sigmoid(cv)).astype(BF16)
    o_ref[...] = _mm(act, w_ref[...]) + b_ref[...]


def _modulation(cvecs, mod_w, mod_b):
    depth, d, n = mod_w.shape
    rows = cvecs.shape[0]
    tn = MOD_CHUNK
    assert n % tn == 0
    return pl.pallas_call(
        _mod_kernel,
        grid=(depth, n // tn),
        in_specs=[
            pl.BlockSpec((rows, d), lambda l, j: (0, 0)),
            pl.BlockSpec((None, d, tn), lambda l, j: (l, 0, j)),
            pl.BlockSpec((None, 1, tn), lambda l, j: (l, 0, j)),
        ],
        out_specs=pl.BlockSpec((None, rows, tn), lambda l, j: (l, 0, j)),
        out_shape=jax.ShapeDtypeStruct((depth, rows, n), F32),
        compiler_params=_params("arbitrary", "arbitrary"),
        name="modulation",
    )(cvecs, mod_w, mod_b.reshape(depth, 1, n))


def _ffn_kernel(*refs, layer, s, sidx, final, n_first):
    if final:
        (m_ref, g_ref, fg_ref, xa_hbm, xb_hbm, wgu_hbm, wd_hbm, oa_hbm, ob_hbm,
         wgu_res, wd_res, act_ref, h0_ref, act0_ref, gu_buf, wd_buf, xbuf, obuf,
         w_sem, xsem, osem) = refs
    else:
        (m_ref, g_ref, xa_hbm, xb_hbm, wgu_hbm, wd_hbm, oa_hbm, ob_hbm,
         wgu_res, wd_res, act_ref, h0_ref, act0_ref, gu_buf, wd_buf, xbuf, obuf,
         w_sem, xsem, osem) = refs
    i = pl.program_id(0)
    n = pl.num_programs(0)
    tm = xbuf.shape[1]
    _, d, tf = wgu_res.shape
    f = wd_res.shape[0]
    nk = f // tf
    rb = act_ref.shape[0]
    nslot = gu_buf.shape[0]
    ahead = nslot - 1
    slot = lax.rem(i, 2)
    other = 1 - slot

    def stage_copies(k, sl):
        gcols = pl.ds(pl.multiple_of(k * tf, tf), tf)
        ucols = pl.ds(pl.multiple_of(f + k * tf, tf), tf)
        return (
            pltpu.make_async_copy(wgu_hbm.at[layer, s, :, gcols], gu_buf.at[sl, 0], w_sem.at[sl, 0]),
            pltpu.make_async_copy(wgu_hbm.at[layer, s, :, ucols], gu_buf.at[sl, 1], w_sem.at[sl, 1]),
            pltpu.make_async_copy(wd_hbm.at[layer, s, gcols, :], wd_buf.at[sl], w_sem.at[sl, 2]),
        )

    def tile_rows(hbm, tile):
        return hbm.at[pl.ds(pl.multiple_of(tile * tm, tm), tm), :]

    def x_copy(which, tile, sl):
        return pltpu.make_async_copy(tile_rows((xa_hbm, xb_hbm)[which], tile), xbuf.at[sl],
                                     xsem.at[sl])

    def o_copy(which, tile, sl):
        return pltpu.make_async_copy(obuf.at[sl], tile_rows((oa_hbm, ob_hbm)[which], tile),
                                     osem.at[sl])

    def start_by_stream(make, tile, sl):
        @pl.when(tile < n_first)
        def _():
            make(0, tile, sl).start()

        @pl.when(tile >= n_first)
        def _():
            make(1, tile - n_first, sl).start()

    @pl.when(i == 0)
    def _():
        x_copy(0, 0, 0).start()
        for k in range(ahead):
            for cp in stage_copies(k, k):
                cp.start()

    @pl.when(i + 1 < n)
    def _():
        start_by_stream(x_copy, i + 1, other)

    x_copy(0, 0, slot).wait()

    @pl.when(i >= 2)
    def _():
        o_copy(0, 0, slot).wait()

    def swiglu(h, wg, wu):
        gt = jnp.dot(h, wg, preferred_element_type=F32)
        up = jnp.dot(h, wu, preferred_element_type=F32)
        return ((gt * jax.nn.sigmoid(gt)) * up).astype(BF16)

    def down_and_store(x, rows):
        ff = jnp.dot(act_ref[...], wd_res[...], preferred_element_type=F32)
        y = x + (0.5 * _gate(m_ref, sidx)) * ff
        if final:
            y = _rms(y) * fg_ref[...]
        obuf[slot, rows, :] = y

    def row_block(r, carry):
        rows = pl.ds(pl.multiple_of(r * rb, rb), rb)
        x = xbuf[slot, rows, :]
        h = _sub_in(x, m_ref, g_ref, sidx).astype(BF16)
        for k in range(nk):
            act_ref[:, k * tf:(k + 1) * tf] = swiglu(h, wgu_res[k], wgu_res[nk + k])
        down_and_store(x, rows)
        return carry

    def first_tile_chunk(k, carry):
        sl = lax.rem(k, nslot)
        for cp in stage_copies(k, sl):
            cp.wait()

        @pl.when(k + ahead < nk)
        def _():
            for cp in stage_copies(k + ahead, lax.rem(k + ahead, nslot)):
                cp.start()

        wg = gu_buf[sl, 0].astype(BF16)
        wu = gu_buf[sl, 1].astype(BF16)
        wgu_res[k] = wg
        wgu_res[nk + k] = wu
        wd_res[pl.ds(pl.multiple_of(k * tf, tf), tf), :] = wd_buf[sl].astype(BF16)
        act0_ref[k] = swiglu(h0_ref[...], wg, wu)
        return carry

    def first_tile_down(r, carry):
        rows = pl.ds(pl.multiple_of(r * rb, rb), rb)
        for k in range(nk):
            act_ref[:, k * tf:(k + 1) * tf] = act0_ref[k, rows, :]
        down_and_store(xbuf[slot, rows, :], rows)
        return carry

    @pl.when(i == 0)
    def _():
        h0_ref[...] = _sub_in(xbuf[slot], m_ref, g_ref, sidx).astype(BF16)
        lax.fori_loop(0, nk, first_tile_chunk, 0)
        lax.fori_loop(0, tm // rb, first_tile_down, 0)

    @pl.when(i > 0)
    def _():
        lax.fori_loop(0, tm // rb, row_block, 0, unroll=2)

    start_by_stream(o_copy, i, slot)

    @pl.when(i == n - 1)
    def _():
        o_copy(0, 0, other).wait()
        o_copy(0, 0, slot).wait()


def _ffn(xa, xb, m, row_of_tile, norm_g, w_gu, w_down, layer, s, sidx, final_g=None):
    (ta, d), tb = xa.shape, xb.shape[0]
    f = w_down.shape[2]
    tm, tf, rb = TOKEN_TILE, FF_CHUNK, FFN_ROW_BLOCK
    assert ta % tm == 0 and tb % tm == 0 and f % tf == 0 and (ta + tb) // tm >= 2
    assert tm % rb == 0 and f // tf >= FFN_STAGE_SLOTS
    final = final_g is not None
    in_specs = [
        pl.BlockSpec((None, None, 3 * N_SUB, d), lambda i: (layer, row_of_tile(i), 0, 0)),
        pl.BlockSpec((None, N_SUB, d), lambda i: (layer, 0, 0)),
    ]
    args = [m, norm_g]
    if final:
        in_specs.append(pl.BlockSpec((1, d), lambda i: (0, 0)))
        args.append(final_g.reshape(1, d))
    in_specs += [pl.BlockSpec(memory_space=pl.ANY)] * 4
    args += [xa, xb, w_gu, w_down]
    return pl.pallas_call(
        functools.partial(_ffn_kernel, layer=layer, s=s, sidx=sidx, final=final,
                          n_first=ta // tm),
        grid=((ta + tb) // tm,),
        in_specs=in_specs,
        out_specs=[pl.BlockSpec(memory_space=pl.ANY)] * 2,
        out_shape=[jax.ShapeDtypeStruct((ta, d), F32), jax.ShapeDtypeStruct((tb, d), F32)],
        scratch_shapes=[
            pltpu.VMEM((2 * f // tf, d, tf), BF16), pltpu.VMEM((f, d), BF16),
            pltpu.VMEM((rb, f), BF16),
            pltpu.VMEM((tm, d), BF16), pltpu.VMEM((f // tf, tm, tf), BF16),
            pltpu.VMEM((FFN_STAGE_SLOTS, 2, d, tf), F32),
            pltpu.VMEM((FFN_STAGE_SLOTS, tf, d), F32),
            pltpu.VMEM((2, tm, d), F32), pltpu.VMEM((2, tm, d), F32),
            pltpu.SemaphoreType.DMA((FFN_STAGE_SLOTS, 3)),
            pltpu.SemaphoreType.DMA((2,)),
            pltpu.SemaphoreType.DMA((2,)),
        ],
        compiler_params=_params("arbitrary"),
        name="ffn",
    )(*args)


def _gelu_tanh(x):
    c = np.float32(np.sqrt(2.0 / np.pi))
    return x * (0.5 * (1.0 + jnp.tanh(c * (x + 0.044715 * (x * x * x)))))


def _time_permutation(seq_len):
    segs = seq_len // SUBLANES
    p = np.arange(seq_len)
    j, s = p // SUBLANES, p % SUBLANES
    mat = np.zeros((seq_len, seq_len), np.float32)
    mat[p, s * segs + j] = 1.0
    return mat


def _per_sequence(mat, rows, seq_len):
    parts = [jnp.dot(mat, rows[lo:lo + seq_len, :], preferred_element_type=F32).astype(BF16)
             for lo in range(0, rows.shape[0], seq_len)]
    return jnp.concatenate(parts, axis=0) if len(parts) > 1 else parts[0]


def _lru_kernel(*refs, seq_len, has_h0, emit_state, conv_left):
    refs = list(refs)
    (x_ref, m_ref, g_ref, wx_ref, wy_ref, cw_ref, cb_ref, gw_ref, gb_ref, lam_ref,
     wo_ref, pin_ref, pout_ref) = refs[:13]
    pos = 13
    h0_ref = None
    if has_h0:
        h0_ref = refs[pos]
        pos += 1
    o_ref = refs[pos]
    pos += 1
    st_ref = None
    if emit_state:
        st_ref = refs[pos]
        pos += 1
    h_ref, acc_ref, af_ref, uf_ref, ab_ref, ub_ref = refs[pos:]

    c = pl.program_id(1)
    tm, cw_cols = af_ref.shape
    nseq = tm // seq_len
    segs = seq_len // SUBLANES

    @pl.when(c == 0)
    def _():
        h = _sub_in(x_ref[...], m_ref, g_ref, 1).astype(BF16)
        h_ref[...] = _per_sequence(pin_ref[...], h, seq_len)
        acc_ref[...] = jnp.zeros_like(acc_ref)

    h = h_ref[...]
    xb = _mm(h, wx_ref[...])
    yb = _mm(h, wy_ref[...])

    sub = lax.broadcasted_iota(jnp.int32, (SUBLANES, cw_cols), 0)

    def next_segment(grp):
        return jnp.where(sub < SUBLANES - 1, pltpu.roll(grp, SUBLANES - 1, 0), 0.0)

    def prev_segment(grp):
        return jnp.where(sub > 0, pltpu.roll(grp, 1, 0), 0.0)

    def tap(seq, off):
        n = abs(off) * SUBLANES
        if off == 0:
            return seq
        if off > 0:
            edge = [next_segment(seq[g * SUBLANES:(g + 1) * SUBLANES, :]) for g in range(off)]
            return jnp.concatenate([seq[n:, :]] + edge, axis=0)
        start = seq_len - n
        edge = [prev_segment(seq[start + g * SUBLANES:start + (g + 1) * SUBLANES, :])
                for g in range(-off)]
        return jnp.concatenate(edge + [seq[:start, :]], axis=0)

    cw = cw_ref[...]
    assert cw.shape[0] - 1 < segs
    xcs = []
    for sq in range(nseq):
        seq = xb[sq * seq_len:(sq + 1) * seq_len, :]
        acc = cb_ref[...]
        for k in range(cw.shape[0]):
            acc = acc + tap(seq, k - conv_left) * cw[k:k + 1, :]
        xcs.append(acc)
    xc = jnp.concatenate(xcs, axis=0) if nseq > 1 else xcs[0]

    gb = gb_ref[...]
    lam = lam_ref[...]
    gc = gw_ref.shape[1]
    for ch in range(cw_cols // gc):
        cols = slice(ch * gc, (ch + 1) * gc)
        xcc = xc[:, cols]
        gl = jnp.dot(xcc.astype(BF16), gw_ref[ch], preferred_element_type=F32)
        for d, (a_ref, u_ref) in enumerate(((af_ref, uf_ref), (ab_ref, ub_ref))):
            tr = jnp.tanh(gl[:, (2 * d) * gc:(2 * d + 1) * gc] + 0.5 * gb[2 * d:2 * d + 1, cols])
            ig = 0.5 * jnp.tanh(gl[:, (2 * d + 1) * gc:(2 * d + 2) * gc]
                                + 0.5 * gb[2 * d + 1:2 * d + 2, cols]) + 0.5
            lm = lam[d:d + 1, cols]
            log_sig = jnp.minimum(lm, 0.0) - jnp.log1p(jnp.exp(-jnp.abs(lm)))
            c2 = (0.5 * LRU_C * LOG2_E) * log_sig
            a = jnp.exp2(c2 * tr + c2)
            v = 1.0 - a * a
            u = jnp.where(v > 0.0, v * lax.rsqrt(v), 0.0) * (ig * xcc)
            a_ref[:, cols] = a
            u_ref[:, cols] = u

    zero = jnp.zeros((SUBLANES, cw_cols), F32)
    one = jnp.ones((SUBLANES, cw_cols), F32)

    def local_scan(j, carry):
        out = []
        for sq in range(nseq):
            for dd, (a_ref, u_ref) in enumerate(((af_ref, uf_ref), (ab_ref, ub_ref))):
                grp = j if dd == 0 else segs - 1 - j
                rows = pl.ds(pl.multiple_of(sq * seq_len + grp * SUBLANES, SUBLANES), SUBLANES)
                hl, pc = carry[2 * (2 * sq + dd)], carry[2 * (2 * sq + dd) + 1]
                a = a_ref[rows, :]
                hl = a * hl + u_ref[rows, :]
                pc = a * pc
                u_ref[rows, :] = hl
                a_ref[rows, :] = pc
                out += [hl, pc]
        return tuple(out)

    totals = lax.fori_loop(0, segs, local_scan, (zero, one) * (2 * nseq),
                           unroll=True if segs <= 32 else 4)

    gelu_y = _gelu_tanh(yb)
    ys = []
    for sq in range(nseq):
        entry = []
        for dd in range(2):
            hl, pc = totals[2 * (2 * sq + dd)], totals[2 * (2 * sq + dd) + 1]
            state = h0_ref[sq, dd:dd + 1, :] if has_h0 else jnp.zeros((1, cw_cols), F32)
            rows = [None] * SUBLANES
            order = range(SUBLANES) if dd == 0 else range(SUBLANES - 1, -1, -1)
            for s in order:
                rows[s] = state
                state = pc[s:s + 1, :] * state + hl[s:s + 1, :]
            entry.append(jnp.concatenate(rows, axis=0))
            if emit_state:
                st_ref[sq, dd:dd + 1, :] = state
        lo, hi = sq * seq_len, (sq + 1) * seq_len
        grouped = (segs, SUBLANES, cw_cols)
        hsum = ((uf_ref[lo:hi, :].reshape(grouped) + af_ref[lo:hi, :].reshape(grouped) * entry[0][None])
                + (ub_ref[lo:hi, :].reshape(grouped) + ab_ref[lo:hi, :].reshape(grouped) * entry[1][None]))
        ys.append(hsum.reshape(seq_len, cw_cols) * gelu_y[lo:hi, :])
    y = jnp.concatenate(ys, axis=0) if nseq > 1 else ys[0]

    y = _per_sequence(pout_ref[...], y.astype(BF16), seq_len)
    acc_ref[...] += _mm(y, wo_ref[...])

    @pl.when(c == pl.num_programs(1) - 1)
    def _():
        o_ref[...] = x_ref[...] + _gate(m_ref, 1) * acc_ref[...]


def _lru_gate_weights(gate_w):
    nd, ng, nb, bw, _ = gate_w.shape
    per = LRU_GATE_CHUNK // bw
    nc = nb // per
    w = (0.5 * gate_w).astype(BF16).reshape(nd * ng, nc, per, bw, bw)
    rows = []
    for n in range(per):
        blk = jnp.transpose(w[:, :, n], (1, 2, 0, 3))
        blk = jnp.pad(blk, ((0, 0), (0, 0), (0, 0), (n * bw, (per - 1 - n) * bw)))
        rows.append(blk.reshape(nc, bw, nd * ng * per * bw))
    return jnp.concatenate(rows, axis=1)


def _lru(x, m, row_of_tile, norm_g, layer, w_in, conv_w, conv_b, gate_w, gate_b, lam, w_out,
         seq_len, h0, emit_state):
    t, d = x.shape
    r = w_out.shape[0]
    tm, cb, gc = TOKEN_TILE, LRU_CHUNK, LRU_GATE_CHUNK
    assert t % tm == 0 and tm % seq_len == 0 and r % cb == 0 and seq_len % SUBLANES == 0
    assert cb % gc == 0
    nc = r // cb
    nseq = tm // seq_len
    has_h0 = h0 is not None
    in_specs = [
        pl.BlockSpec((tm, d), lambda i, c: (i, 0)),
        pl.BlockSpec((None, None, 3 * N_SUB, d), lambda i, c: (layer, row_of_tile(i), 0, 0)),
        pl.BlockSpec((None, N_SUB, d), lambda i, c: (layer, 0, 0)),
        pl.BlockSpec((d, cb), lambda i, c: (0, c)),
        pl.BlockSpec((d, cb), lambda i, c: (0, nc + c)),
        pl.BlockSpec((conv_w.shape[0], cb), lambda i, c: (0, c)),
        pl.BlockSpec((1, cb), lambda i, c: (0, c)),
        pl.BlockSpec((cb // gc, gc, 4 * gc), lambda i, c: (c, 0, 0)),
        pl.BlockSpec((4, cb), lambda i, c: (0, c)),
        pl.BlockSpec((2, cb), lambda i, c: (0, c)),
        pl.BlockSpec((cb, d), lambda i, c: (c, 0)),
        pl.BlockSpec((seq_len, seq_len), lambda i, c: (0, 0), pipeline_mode=pl.Buffered(1)),
        pl.BlockSpec((seq_len, seq_len), lambda i, c: (0, 0), pipeline_mode=pl.Buffered(1)),
    ]
    perm = _time_permutation(seq_len)
    args = [x, m, norm_g, w_in, w_in, conv_w, conv_b.reshape(1, r), _lru_gate_weights(gate_w),
            gate_b.reshape(4, r), lam, w_out, jnp.asarray(perm, BF16), jnp.asarray(perm.T, BF16)]
    if has_h0:
        in_specs.append(pl.BlockSpec((nseq, 2, cb), lambda i, c: (i, 0, c)))
        args.append(h0)
    out_specs = [pl.BlockSpec((tm, d), lambda i, c: (i, 0))]
    out_shape = [jax.ShapeDtypeStruct((t, d), F32)]
    if emit_state:
        out_specs.append(pl.BlockSpec((nseq, 2, cb), lambda i, c: (i, 0, c)))
        out_shape.append(jax.ShapeDtypeStruct((t // seq_len, 2, r), F32))
    outs = pl.pallas_call(
        functools.partial(_lru_kernel, seq_len=seq_len, has_h0=has_h0, emit_state=emit_state,
                          conv_left=(conv_w.shape[0] - 1) // 2),
        grid=(t // tm, nc),
        in_specs=in_specs,
        out_specs=out_specs,
        out_shape=out_shape,
        scratch_shapes=[pltpu.VMEM((tm, d), BF16), pltpu.VMEM((tm, d), F32)]
        + [pltpu.VMEM((tm, cb), F32)] * 4,
        compiler_params=_params("arbitrary", "arbitrary"),
        name="rglru",
    )(*args)
    return outs if emit_state else (outs[0], None)


def _rope(x, cos, sin_signed, lane):
    hd = x.shape[1]
    partner = jnp.where((lane & 32) == 0, pltpu.roll(x, hd - 32, 1), pltpu.roll(x, 32, 1))
    return x * cos + partner * sin_signed


def _attn_kernel(*refs, seq_len, q_block, past_len, rope, emit_kv, group, kvh):
    refs = list(refs)
    x_ref, m_ref, g_ref, wq_ref, wk_ref, wv_ref, qg_ref, kg_ref, wo_ref = refs[:9]
    pos = 9
    if rope:
        cos_ref, sin_ref = refs[pos:pos + 2]
        pos += 2
    if past_len:
        ck_ref, cv_ref = refs[pos:pos + 2]
        pos += 2
    o_ref = refs[pos]
    pos += 1
    if emit_kv:
        kn_ref, vn_ref = refs[pos:pos + 2]
        pos += 2
    h_ref, acc_ref, q_s, k_s, v_s, o_s = refs[pos:]

    gi = pl.program_id(1)
    tm = x_ref.shape[0]
    hd = k_s.shape[1]
    nqb = seq_len // q_block
    nchunks = (tm // seq_len) * nqb
    nk = past_len + seq_len
    scale = hd ** -0.5

    @pl.when(gi == 0)
    def _():
        h_ref[...] = _sub_in(x_ref[...], m_ref, g_ref, 1).astype(BF16)
        acc_ref[...] = jnp.zeros_like(acc_ref)

    h = h_ref[...]
    q = _mm(h, wq_ref[...])
    k = _rms(_mm(h, wk_ref[...])) * kg_ref[...]
    v = _mm(h, wv_ref[...])
    if emit_kv:
        kn_ref[pl.ds(gi, tm, stride=kvh), :] = k
        vn_ref[pl.ds(gi, tm, stride=kvh), :] = v
    if rope:
        lane = lax.broadcasted_iota(jnp.int32, (tm, hd), 1)
        cos, sin = cos_ref[...], sin_ref[...]
        k = _rope(k, cos, sin, lane)
    k_s[past_len:past_len + tm, :] = k.astype(BF16)
    v_s[past_len:past_len + tm, :] = v.astype(BF16)
    if past_len:
        k_s[0:past_len, :] = ck_ref[pl.ds(gi, past_len, stride=kvh), :].astype(BF16)
        v_s[0:past_len, :] = cv_ref[pl.ds(gi, past_len, stride=kvh), :].astype(BF16)
    for j in range(group):
        qh = _rms(q[:, j * hd:(j + 1) * hd]) * qg_ref[...]
        if rope:
            qh = _rope(qh, cos, sin, lane)
        q_s[:, j * hd:(j + 1) * hd] = qh.astype(BF16)

    def chunk(ci, carry):
        r0 = pl.multiple_of(ci * q_block, q_block)
        if past_len:
            keys, vals = k_s[...], v_s[...]
        else:
            koff = pl.multiple_of((ci // nqb) * seq_len, seq_len)
            keys, vals = k_s[pl.ds(koff, nk), :], v_s[pl.ds(koff, nk), :]
        qc = jnp.concatenate([q_s[pl.ds(r0, q_block), j * hd:(j + 1) * hd] for j in range(group)],
                             axis=0)
        raw = lax.dot_general(qc, keys, (((1,), (1,)), ((), ())), preferred_element_type=F32)
        e = jnp.exp2((raw - jnp.max(raw, axis=-1, keepdims=True)) * (scale * LOG2_E))
        inv = 1.0 / jnp.sum(e, axis=-1, keepdims=True)
        oc = jnp.dot(e.astype(BF16), vals, preferred_element_type=F32) * inv
        for j in range(group):
            o_s[pl.ds(r0, q_block), j * hd:(j + 1) * hd] = oc[j * q_block:(j + 1) * q_block, :].astype(BF16)
        return carry

    lax.fori_loop(0, nchunks, chunk, 0, unroll=min(nchunks, 4))
    acc_ref[...] += _mm(o_s[...], wo_ref[...])

    @pl.when(gi == pl.num_programs(1) - 1)
    def _():
        o_ref[...] = x_ref[...] + _gate(m_ref, 1) * acc_ref[...]


def _rope_tables(n_tok, hd):
    rows = n_tok // GRID_W
    r_idx = jnp.broadcast_to(jnp.arange(rows)[:, None], (rows, GRID_W)).reshape(n_tok).astype(F32)
    c_idx = jnp.broadcast_to(jnp.arange(GRID_W)[None, :], (rows, GRID_W)).reshape(n_tok).astype(F32)
    n_freq = hd // 4
    inv = ROPE_THETA ** (-jnp.arange(n_freq, dtype=F32) / n_freq)
    ang = jnp.stack([r_idx[:, None] * inv, c_idx[:, None] * inv], axis=1)
    cos, sin = jnp.cos(ang), jnp.sin(ang)
    cos_full = jnp.concatenate([cos, cos], axis=-1).reshape(n_tok, hd)
    sin_signed = jnp.concatenate([-sin, sin], axis=-1).reshape(n_tok, hd)
    return cos_full, sin_signed


def _attn(x, m, row_of_tile, norm_g, layer, w_qkv, q_g, k_g, w_o, seq_len, q_block,
          cache_k=None, cache_v=None, rope=False, emit_kv=False):
    t, d = x.shape
    hd, kvh = HEAD_DIM, N_KV_HEADS
    n_heads = w_o.shape[0] // hd
    group = n_heads // kvh
    gw = group * hd
    tm = TOKEN_TILE
    assert t % tm == 0 and tm % seq_len == 0 and seq_len % q_block == 0
    past_len = 0 if cache_k is None else cache_k.shape[1] // kvh
    assert past_len == 0 or tm == seq_len
    in_specs = [
        pl.BlockSpec((tm, d), lambda i, g: (i, 0)),
        pl.BlockSpec((None, None, 3 * N_SUB, d), lambda i, g: (layer, row_of_tile(i), 0, 0)),
        pl.BlockSpec((None, N_SUB, d), lambda i, g: (layer, 0, 0)),
        pl.BlockSpec((d, gw), lambda i, g: (0, g)),
        pl.BlockSpec((d, hd), lambda i, g: (0, n_heads + g)),
        pl.BlockSpec((d, hd), lambda i, g: (0, n_heads + kvh + g)),
        pl.BlockSpec((1, hd), lambda i, g: (0, 0)),
        pl.BlockSpec((1, hd), lambda i, g: (0, 0)),
        pl.BlockSpec((gw, d), lambda i, g: (g, 0)),
    ]
    args = [x, m, norm_g, w_qkv, w_qkv, w_qkv, q_g.reshape(1, hd), k_g.reshape(1, hd), w_o]
    if rope:
        assert tm == seq_len
        cos, sin = _rope_tables(seq_len, hd)
        in_specs += [pl.BlockSpec((tm, hd), lambda i, g: (0, 0))] * 2
        args += [cos, sin]
    if past_len:
        in_specs += [pl.BlockSpec((None, past_len * kvh, hd), lambda i, g: (i, 0, 0))] * 2
        args += [cache_k, cache_v]
    out_specs = [pl.BlockSpec((tm, d), lambda i, g: (i, 0))]
    out_shape = [jax.ShapeDtypeStruct((t, d), F32)]
    if emit_kv:
        out_specs += [pl.BlockSpec((tm * kvh, hd), lambda i, g: (i, 0))] * 2
        out_shape += [jax.ShapeDtypeStruct((t * kvh, hd), F32)] * 2
    nkeys = past_len + tm
    outs = pl.pallas_call(
        functools.partial(_attn_kernel, seq_len=seq_len, q_block=q_block, past_len=past_len,
                          rope=rope, emit_kv=emit_kv, group=group, kvh=kvh),
        grid=(t // tm, kvh),
        in_specs=in_specs,
        out_specs=out_specs,
        out_shape=out_shape,
        scratch_shapes=[pltpu.VMEM((tm, d), BF16), pltpu.VMEM((tm, d), F32),
                        pltpu.VMEM((tm, gw), BF16), pltpu.VMEM((nkeys, hd), BF16),
                        pltpu.VMEM((nkeys, hd), BF16), pltpu.VMEM((tm, gw), BF16)],
        compiler_params=_params("arbitrary", "arbitrary"),
        name="gqa",
    )(*args)
    return outs


def kernel(x_prompt, x_sample, c, state_lru, cache_k, cache_v, c_ctx, mod_w, mod_b, norm_g,
           ffn_w_gu, ffn_w_down, lru_w_in, lru_conv_w, lru_conv_b, lru_gate_w, lru_gate_b,
           lru_lambda, lru_w_out, att_w_qkv, att_q_g, att_k_g, att_w_o, final_g):
    b, s, d = x_prompt.shape
    db, ds, _ = x_sample.shape
    depth = mod_w.shape[0]
    n_mixers = 2
    assert ds % TOKEN_TILE == 0 and 1 + db <= SUBLANES

    xp = x_prompt.reshape(b * s, d)
    xs = x_sample.reshape(db * ds, d)
    cvecs = jnp.concatenate([c_ctx[None], c, jnp.zeros((SUBLANES - 1 - db, d), F32)], axis=0)
    m = _modulation(cvecs, mod_w, mod_b).reshape(depth, SUBLANES, 3 * N_SUB, d)

    tiles_per_sample = ds // TOKEN_TILE
    prompt_row = lambda i: 0
    sample_row = lambda i: 1 + i // tiles_per_sample
    prompt_tiles = (b * s) // TOKEN_TILE
    both_row = lambda i: jnp.where(i < prompt_tiles, 0, 1 + (i - prompt_tiles) // tiles_per_sample)

    new_states, new_k, new_v = [], [], []
    for layer in range(depth):
        j = layer // n_mixers
        last = layer == depth - 1
        xp, xs = _ffn(xp, xs, m, both_row, norm_g, ffn_w_gu, ffn_w_down, layer, 0, 0)
        if layer % n_mixers == 0:
            lru_p = (lru_w_in[j], lru_conv_w[j], lru_conv_b[j], lru_gate_w[j], lru_gate_b[j],
                     lru_lambda[j], lru_w_out[j])
            xp, st = _lru(xp, m, prompt_row, norm_g, layer, *lru_p, seq_len=s, h0=None,
                          emit_state=True)
            new_states.append(st)
            xs, _ = _lru(xs, m, sample_row, norm_g, layer, *lru_p, seq_len=ds,
                         h0=state_lru[:, j], emit_state=False)
        else:
            att_p = (att_w_qkv[j], att_q_g[j], att_k_g[j], att_w_o[j])
            xp, kp, vp = _attn(xp, m, prompt_row, norm_g, layer, *att_p, seq_len=s, q_block=s,
                               emit_kv=True)
            new_k.append(kp.reshape(b, s, N_KV_HEADS, HEAD_DIM))
            new_v.append(vp.reshape(b, s, N_KV_HEADS, HEAD_DIM))
            past = cache_k.shape[2]
            ck = cache_k[:, j].reshape(db, past * N_KV_HEADS, HEAD_DIM)
            cv = cache_v[:, j].reshape(db, past * N_KV_HEADS, HEAD_DIM)
            (xs,) = _attn(xs, m, sample_row, norm_g, layer, *att_p, seq_len=ds, q_block=128,
                          cache_k=ck, cache_v=cv, rope=True)
        fg = final_g if last else None
        xp, xs = _ffn(xp, xs, m, both_row, norm_g, ffn_w_gu, ffn_w_down, layer, 1, 2, final_g=fg)

    y_prompt = xp.reshape(b, s, d)
    y_sample = xs.reshape(db, ds, d)
    return (y_prompt, y_sample, jnp.stack(new_states, axis=1), jnp.stack(new_k, axis=1),
            jnp.stack(new_v, axis=1))
```

```python
import functools

import jax
import jax.numpy as jnp
import numpy as np
from jax import lax
from jax.experimental import pallas as pl
from jax.experimental.pallas import tpu as pltpu

F32 = jnp.float32
BF16 = jnp.bfloat16

EPS = 1e-6
LRU_C = 8.0
LOG2_E = 1.4426950408889634
GRID_W = 64
ROPE_THETA = 10000.0
N_SUB = 3
HEAD_DIM = 128
N_KV_HEADS = 2
LRU_BLOCKS = 16

V7X_VMEM_LIMIT_BYTES = 56 * 1024 * 1024
SUBLANES = 8
TOKEN_TILE = 1024
FF_CHUNK = 256
FFN_ROW_BLOCK = 256
FFN_STAGE_SLOTS = 3
LRU_CHUNK = 512
LRU_GATE_CHUNK = 256
MOD_CHUNK = 1536


def _params(*semantics):
    return pltpu.CompilerParams(dimension_semantics=semantics,
                                vmem_limit_bytes=V7X_VMEM_LIMIT_BYTES)


def _mm(a_bf16, w_f32):
    return jnp.dot(a_bf16, w_f32.astype(BF16), preferred_element_type=F32)


def _rms(x):
    return x * lax.rsqrt(jnp.mean(x * x, axis=-1, keepdims=True) + EPS)


def _sub_in(x, m_ref, g_ref, sidx):
    shift = m_ref[3 * sidx:3 * sidx + 1, :]
    scale = m_ref[3 * sidx + 1:3 * sidx + 2, :]
    return (_rms(x) * g_ref[sidx:sidx + 1, :]) * (1.0 + scale) + shift


def _gate(m_ref, sidx):
    return m_ref[3 * sidx + 2:3 * sidx + 3, :]


def _mod_kernel(cv_ref, w_ref, b_ref, o_ref):
    cv = cv_ref[...]
    act = (cv * jax.nn.sigmoid(cv)).astype(BF16)
    o_ref[...] = _mm(act, w_ref[...]) + b_ref[...]


def _modulation(cvecs, mod_w, mod_b):
    depth, d, n = mod_w.shape
    rows = cvecs.shape[0]
    tn = MOD_CHUNK
    assert n % tn == 0
    return pl.pallas_call(
        _mod_kernel,
        grid=(depth, n // tn),
        in_specs=[
            pl.BlockSpec((rows, d), lambda l, j: (0, 0)),
            pl.BlockSpec((None, d, tn), lambda l, j: (l, 0, j)),
            pl.BlockSpec((None, 1, tn), lambda l, j: (l, 0, j)),
        ],
        out_specs=pl.BlockSpec((None, rows, tn), lambda l, j: (l, 0, j)),
        out_shape=jax.ShapeDtypeStruct((depth, rows, n), F32),
        compiler_params=_params("arbitrary", "arbitrary"),
        name="modulation",
    )(cvecs, mod_w, mod_b.reshape(depth, 1, n))


def _ffn_kernel(*refs, layer, s, sidx, final, n_first):
    if final:
        (m_ref, g_ref, fg_ref, xa_hbm, xb_hbm, wgu_hbm, wd_hbm, oa_hbm, ob_hbm,
         wgu_res, wd_res, act_ref, h0_ref, act0_ref, gu_buf, wd_buf, xbuf, obuf,
         w_sem, xsem, osem) = refs
    else:
        (m_ref, g_ref, xa_hbm, xb_hbm, wgu_hbm, wd_hbm, oa_hbm, ob_hbm,
         wgu_res, wd_res, act_ref, h0_ref, act0_ref, gu_buf, wd_buf, xbuf, obuf,
         w_sem, xsem, osem) = refs
    i = pl.program_id(0)
    n = pl.num_programs(0)
    tm = xbuf.shape[1]
    _, d, tf = wgu_res.shape
    f = wd_res.shape[0]
    nk = f // tf
    rb = act_ref.shape[0]
    nslot = gu_buf.shape[0]
    ahead = nslot - 1
    slot = lax.rem(i, 2)
    other = 1 - slot

    def stage_copies(k, sl):
        gcols = pl.ds(pl.multiple_of(k * tf, tf), tf)
        ucols = pl.ds(pl.multiple_of(f + k * tf, tf), tf)
        return (
            pltpu.make_async_copy(wgu_hbm.at[layer, s, :, gcols], gu_buf.at[sl, 0], w_sem.at[sl, 0]),
            pltpu.make_async_copy(wgu_hbm.at[layer, s, :, ucols], gu_buf.at[sl, 1], w_sem.at[sl, 1]),
            pltpu.make_async_copy(wd_hbm.at[layer, s, gcols, :], wd_buf.at[sl], w_sem.at[sl, 2]),
        )

    def tile_rows(hbm, tile):
        return hbm.at[pl.ds(pl.multiple_of(tile * tm, tm), tm), :]

    def x_copy(which, tile, sl):
        return pltpu.make_async_copy(tile_rows((xa_hbm, xb_hbm)[which], tile), xbuf.at[sl],
                                     xsem.at[sl])

    def o_copy(which, tile, sl):
        return pltpu.make_async_copy(obuf.at[sl], tile_rows((oa_hbm, ob_hbm)[which], tile),
                                     osem.at[sl])

    def start_by_stream(make, tile, sl):
        @pl.when(tile < n_first)
        def _():
            make(0, tile, sl).start()

        @pl.when(tile >= n_first)
        def _():
            make(1, tile - n_first, sl).start()

    @pl.when(i == 0)
    def _():
        x_copy(0, 0, 0).start()
        for k in range(ahead):
            for cp in stage_copies(k, k):
                cp.start()

    @pl.when(i + 1 < n)
    def _():
        start_by_stream(x_copy, i + 1, other)

    x_copy(0, 0, slot).wait()

    @pl.when(i >= 2)
    def _():
        o_copy(0, 0, slot).wait()

    def swiglu(h, wg, wu):
        gt = jnp.dot(h, wg, preferred_element_type=F32)
        up = jnp.dot(h, wu, preferred_element_type=F32)
        return ((gt * jax.nn.sigmoid(gt)) * up).astype(BF16)

    def down_and_store(x, rows):
        ff = jnp.dot(act_ref[...], wd_res[...], preferred_element_type=F32)
        y = x + (0.5 * _gate(m_ref, sidx)) * ff
        if final:
            y = _rms(y) * fg_ref[...]
        obuf[slot, rows, :] = y

    def row_block(r, carry):
        rows = pl.ds(pl.multiple_of(r * rb, rb), rb)
        x = xbuf[slot, rows, :]
        h = _sub_in(x, m_ref, g_ref, sidx).astype(BF16)
        for k in range(nk):
            act_ref[:, k * tf:(k + 1) * tf] = swiglu(h, wgu_res[k], wgu_res[nk + k])
        down_and_store(x, rows)
        return carry

    def first_tile_chunk(k, carry):
        sl = lax.rem(k, nslot)
        for cp in stage_copies(k, sl):
            cp.wait()

        @pl.when(k + ahead < nk)
        def _():
            for cp in stage_copies(k + ahead, lax.rem(k + ahead, nslot)):
                cp.start()

        wg = gu_buf[sl, 0].astype(BF16)
        wu = gu_buf[sl, 1].astype(BF16)
        wgu_res[k] = wg
        wgu_res[nk + k] = wu
        wd_res[pl.ds(pl.multiple_of(k * tf, tf), tf), :] = wd_buf[sl].astype(BF16)
        act0_ref[k] = swiglu(h0_ref[...], wg, wu)
        return carry

    def first_tile_down(r, carry):
        rows = pl.ds(pl.multiple_of(r * rb, rb), rb)
        for k in range(nk):
            act_ref[:, k * tf:(k + 1) * tf] = act0_ref[k, rows, :]
        down_and_store(xbuf[slot, rows, :], rows)
        return carry

    @pl.when(i == 0)
    def _():
        h0_ref[...] = _sub_in(xbuf[slot], m_ref, g_ref, sidx).astype(BF16)
        lax.fori_loop(0, nk, first_tile_chunk, 0)
        lax.fori_loop(0, tm // rb, first_tile_down, 0, unroll=2)

    @pl.when(i > 0)
    def _():
        lax.fori_loop(0, tm // rb, row_block, 0, unroll=2)

    start_by_stream(o_copy, i, slot)

    @pl.when(i == n - 1)
    def _():
        o_copy(0, 0, other).wait()
        o_copy(0, 0, slot).wait()


def _ffn(xa, xb, m, row_of_tile, norm_g, w_gu, w_down, layer, s, sidx, final_g=None):
    (ta, d), tb = xa.shape, xb.shape[0]
    f = w_down.shape[2]
    tm, tf, rb = TOKEN_TILE, FF_CHUNK, FFN_ROW_BLOCK
    assert ta % tm == 0 and tb % tm == 0 and f % tf == 0 and (ta + tb) // tm >= 2
    assert tm % rb == 0 and f // tf >= FFN_STAGE_SLOTS
    final = final_g is not None
    in_specs = [
        pl.BlockSpec((None, None, 3 * N_SUB, d), lambda i: (layer, row_of_tile(i), 0, 0)),
        pl.BlockSpec((None, N_SUB, d), lambda i: (layer, 0, 0)),
    ]
    args = [m, norm_g]
    if final:
        in_specs.append(pl.BlockSpec((1, d), lambda i: (0, 0)))
        args.append(final_g.reshape(1, d))
    in_specs += [pl.BlockSpec(memory_space=pl.ANY)] * 4
    args += [xa, xb, w_gu, w_down]
    return pl.pallas_call(
        functools.partial(_ffn_kernel, layer=layer, s=s, sidx=sidx, final=final,
                          n_first=ta // tm),
        grid=((ta + tb) // tm,),
        in_specs=in_specs,
        out_specs=[pl.BlockSpec(memory_space=pl.ANY)] * 2,
        out_shape=[jax.ShapeDtypeStruct((ta, d), F32), jax.ShapeDtypeStruct((tb, d), F32)],
        scratch_shapes=[
            pltpu.VMEM((2 * f // tf, d, tf), BF16), pltpu.VMEM((f, d), BF16),
            pltpu.VMEM((rb, f), BF16),
            pltpu.VMEM((tm, d), BF16), pltpu.VMEM((f // tf, tm, tf), BF16),
            pltpu.VMEM((FFN_STAGE_SLOTS, 2, d, tf), F32),
            pltpu.VMEM((FFN_STAGE_SLOTS, tf, d), F32),
            pltpu.VMEM((2, tm, d), F32), pltpu.VMEM((2, tm, d), F32),
            pltpu.SemaphoreType.DMA((FFN_STAGE_SLOTS, 3)),
            pltpu.SemaphoreType.DMA((2,)),
            pltpu.SemaphoreType.DMA((2,)),
        ],
        compiler_params=_params("arbitrary"),
        name="ffn",
    )(*args)


def _gelu_tanh(x):
    c = np.float32(np.sqrt(2.0 / np.pi))
    return x * (0.5 * (1.0 + jnp.tanh(c * (x + 0.044715 * (x * x * x)))))


def _time_permutation(seq_len):
    segs = seq_len // SUBLANES
    p = np.arange(seq_len)
    j, s = p // SUBLANES, p % SUBLANES
    mat = np.zeros((seq_len, seq_len), np.float32)
    mat[p, s * segs + j] = 1.0
    return mat


def _per_sequence(mat, rows, seq_len):
    parts = [jnp.dot(mat, rows[lo:lo + seq_len, :], preferred_element_type=F32).astype(BF16)
             for lo in range(0, rows.shape[0], seq_len)]
    return jnp.concatenate(parts, axis=0) if len(parts) > 1 else parts[0]


def _lru_kernel(*refs, seq_len, has_h0, emit_state, conv_left):
    refs = list(refs)
    (x_ref, m_ref, g_ref, wx_ref, wy_ref, cw_ref, cb_ref, gw_ref, gb_ref, lam_ref,
     wo_ref, pin_ref, pout_ref) = refs[:13]
    pos = 13
    h0_ref = None
    if has_h0:
        h0_ref = refs[pos]
        pos += 1
    o_ref = refs[pos]
    pos += 1
    st_ref = None
    if emit_state:
        st_ref = refs[pos]
        pos += 1
    h_ref, acc_ref, af_ref, uf_ref, ab_ref, ub_ref = refs[pos:]

    c = pl.program_id(1)
    tm, cw_cols = af_ref.shape
    nseq = tm // seq_len
    segs = seq_len // SUBLANES

    @pl.when(c == 0)
    def _():
        h = _sub_in(x_ref[...], m_ref, g_ref, 1).astype(BF16)
        h_ref[...] = _per_sequence(pin_ref[...], h, seq_len)
        acc_ref[...] = jnp.zeros_like(acc_ref)

    h = h_ref[...]
    xb = _mm(h, wx_ref[...])
    yb = _mm(h, wy_ref[...])

    sub = lax.broadcasted_iota(jnp.int32, (SUBLANES, cw_cols), 0)

    def next_segment(grp):
        return jnp.where(sub < SUBLANES - 1, pltpu.roll(grp, SUBLANES - 1, 0), 0.0)

    def prev_segment(grp):
        return jnp.where(sub > 0, pltpu.roll(grp, 1, 0), 0.0)

    def tap(seq, off):
        n = abs(off) * SUBLANES
        if off == 0:
            return seq
        if off > 0:
            edge = [next_segment(seq[g * SUBLANES:(g + 1) * SUBLANES, :]) for g in range(off)]
            return jnp.concatenate([seq[n:, :]] + edge, axis=0)
        start = seq_len - n
        edge = [prev_segment(seq[start + g * SUBLANES:start + (g + 1) * SUBLANES, :])
                for g in range(-off)]
        return jnp.concatenate(edge + [seq[:start, :]], axis=0)

    cw = cw_ref[...]
    assert cw.shape[0] - 1 < segs
    xcs = []
    for sq in range(nseq):
        seq = xb[sq * seq_len:(sq + 1) * seq_len, :]
        acc = cb_ref[...]
        for k in range(cw.shape[0]):
            acc = acc + tap(seq, k - conv_left) * cw[k:k + 1, :]
        xcs.append(acc)
    xc = jnp.concatenate(xcs, axis=0) if nseq > 1 else xcs[0]

    gb = gb_ref[...]
    lam = lam_ref[...]
    gc = gw_ref.shape[1]
    for ch in range(cw_cols // gc):
        cols = slice(ch * gc, (ch + 1) * gc)
        xcc = xc[:, cols]
        gl = jnp.dot(xcc.astype(BF16), gw_ref[ch], preferred_element_type=F32)
        for d, (a_ref, u_ref) in enumerate(((af_ref, uf_ref), (ab_ref, ub_ref))):
            tr = jnp.tanh(gl[:, (2 * d) * gc:(2 * d + 1) * gc] + 0.5 * gb[2 * d:2 * d + 1, cols])
            ig = 0.5 * jnp.tanh(gl[:, (2 * d + 1) * gc:(2 * d + 2) * gc]
                                + 0.5 * gb[2 * d + 1:2 * d + 2, cols]) + 0.5
            lm = lam[d:d + 1, cols]
            log_sig = jnp.minimum(lm, 0.0) - jnp.log1p(jnp.exp(-jnp.abs(lm)))
            c2 = (0.5 * LRU_C * LOG2_E) * log_sig
            a = jnp.exp2(c2 * tr + c2)
            v = 1.0 - a * a
            u = jnp.where(v > 0.0, v * lax.rsqrt(v), 0.0) * (ig * xcc)
            a_ref[:, cols] = a
            u_ref[:, cols] = u

    zero = jnp.zeros((SUBLANES, cw_cols), F32)
    one = jnp.ones((SUBLANES, cw_cols), F32)

    def local_scan(j, carry):
        out = []
        for sq in range(nseq):
            for dd, (a_ref, u_ref) in enumerate(((af_ref, uf_ref), (ab_ref, ub_ref))):
                grp = j if dd == 0 else segs - 1 - j
                rows = pl.ds(pl.multiple_of(sq * seq_len + grp * SUBLANES, SUBLANES), SUBLANES)
                hl, pc = carry[2 * (2 * sq + dd)], carry[2 * (2 * sq + dd) + 1]
                a = a_ref[rows, :]
                hl = a * hl + u_ref[rows, :]
                pc = a * pc
                u_ref[rows, :] = hl
                a_ref[rows, :] = pc
                out += [hl, pc]
        return tuple(out)

    totals = lax.fori_loop(0, segs, local_scan, (zero, one) * (2 * nseq), unroll=True)

    gelu_y = _gelu_tanh(yb)
    ys = []
    for sq in range(nseq):
        entry = []
        for dd in range(2):
            hl, pc = totals[2 * (2 * sq + dd)], totals[2 * (2 * sq + dd) + 1]
            state = h0_ref[sq, dd:dd + 1, :] if has_h0 else jnp.zeros((1, cw_cols), F32)
            rows = [None] * SUBLANES
            order = range(SUBLANES) if dd == 0 else range(SUBLANES - 1, -1, -1)
            for s in order:
                rows[s] = state
                state = pc[s:s + 1, :] * state + hl[s:s + 1, :]
            entry.append(jnp.concatenate(rows, axis=0))
            if emit_state:
                st_ref[sq, dd:dd + 1, :] = state
        lo, hi = sq * seq_len, (sq + 1) * seq_len
        grouped = (segs, SUBLANES, cw_cols)
        hsum = ((uf_ref[lo:hi, :].reshape(grouped) + af_ref[lo:hi, :].reshape(grouped) * entry[0][None])
                + (ub_ref[lo:hi, :].reshape(grouped) + ab_ref[lo:hi, :].reshape(grouped) * entry[1][None]))
        ys.append(hsum.reshape(seq_len, cw_cols) * gelu_y[lo:hi, :])
    y = jnp.concatenate(ys, axis=0) if nseq > 1 else ys[0]

    y = _per_sequence(pout_ref[...], y.astype(BF16), seq_len)
    acc_ref[...] += _mm(y, wo_ref[...])

    @pl.when(c == pl.num_programs(1) - 1)
    def _():
        o_ref[...] = x_ref[...] + _gate(m_ref, 1) * acc_ref[...]


def _lru_gate_weights(gate_w):
    nd, ng, nb, bw, _ = gate_w.shape
    per = LRU_GATE_CHUNK // bw
    nc = nb // per
    w = (0.5 * gate_w).astype(BF16).reshape(nd * ng, nc, per, bw, bw)
    rows = []
    for n in range(per):
        blk = jnp.transpose(w[:, :, n], (1, 2, 0, 3))
        blk = jnp.pad(blk, ((0, 0), (0, 0), (0, 0), (n * bw, (per - 1 - n) * bw)))
        rows.append(blk.reshape(nc, bw, nd * ng * per * bw))
    return jnp.concatenate(rows, axis=1)


def _lru(x, m, row_of_tile, norm_g, layer, w_in, conv_w, conv_b, gate_w, gate_b, lam, w_out,
         seq_len, h0, emit_state):
    t, d = x.shape
    r = w_out.shape[0]
    tm, cb, gc = TOKEN_TILE, LRU_CHUNK, LRU_GATE_CHUNK
    assert t % tm == 0 and tm % seq_len == 0 and r % cb == 0 and seq_len % SUBLANES == 0
    assert cb % gc == 0
    nc = r // cb
    nseq = tm // seq_len
    has_h0 = h0 is not None
    in_specs = [
        pl.BlockSpec((tm, d), lambda i, c: (i, 0)),
        pl.BlockSpec((None, None, 3 * N_SUB, d), lambda i, c: (layer, row_of_tile(i), 0, 0)),
        pl.BlockSpec((None, N_SUB, d), lambda i, c: (layer, 0, 0)),
        pl.BlockSpec((d, cb), lambda i, c: (0, c)),
        pl.BlockSpec((d, cb), lambda i, c: (0, nc + c)),
        pl.BlockSpec((conv_w.shape[0], cb), lambda i, c: (0, c)),
        pl.BlockSpec((1, cb), lambda i, c: (0, c)),
        pl.BlockSpec((cb // gc, gc, 4 * gc), lambda i, c: (c, 0, 0)),
        pl.BlockSpec((4, cb), lambda i, c: (0, c)),
        pl.BlockSpec((2, cb), lambda i, c: (0, c)),
        pl.BlockSpec((cb, d), lambda i, c: (c, 0)),
        pl.BlockSpec((seq_len, seq_len), lambda i, c: (0, 0), pipeline_mode=pl.Buffered(1)),
        pl.BlockSpec((seq_len, seq_len), lambda i, c: (0, 0), pipeline_mode=pl.Buffered(1)),
    ]
    perm = _time_permutation(seq_len)
    args = [x, m, norm_g, w_in, w_in, conv_w, conv_b.reshape(1, r), _lru_gate_weights(gate_w),
            gate_b.reshape(4, r), lam, w_out, jnp.asarray(perm, BF16), jnp.asarray(perm.T, BF16)]
    if has_h0:
        in_specs.append(pl.BlockSpec((nseq, 2, cb), lambda i, c: (i, 0, c)))
        args.append(h0)
    out_specs = [pl.BlockSpec((tm, d), lambda i, c: (i, 0))]
    out_shape = [jax.ShapeDtypeStruct((t, d), F32)]
    if emit_state:
        out_specs.append(pl.BlockSpec((nseq, 2, cb), lambda i, c: (i, 0, c)))
        out_shape.append(jax.ShapeDtypeStruct((t // seq_len, 2, r), F32))
    outs = pl.pallas_call(
        functools.partial(_lru_kernel, seq_len=seq_len, has_h0=has_h0, emit_state=emit_state,
                          conv_left=(conv_w.shape[0] - 1) // 2),
        grid=(t // tm, nc),
        in_specs=in_specs,
        out_specs=out_specs,
        out_shape=out_shape,
        scratch_shapes=[pltpu.VMEM((tm, d), BF16), pltpu.VMEM((tm, d), F32)]
        + [pltpu.VMEM((tm, cb), F32)] * 4,
        compiler_params=_params("arbitrary", "arbitrary"),
        name="rglru",
    )(*args)
    return outs if emit_state else (outs[0], None)


def _rope(x, cos, sin_signed, lane):
    hd = x.shape[1]
    partner = jnp.where((lane & 32) == 0, pltpu.roll(x, hd - 32, 1), pltpu.roll(x, 32, 1))
    return x * cos + partner * sin_signed


def _attn_kernel(*refs, seq_len, q_block, past_len, rope, emit_kv, group, kvh):
    refs = list(refs)
    x_ref, m_ref, g_ref, wq_ref, wk_ref, wv_ref, qg_ref, kg_ref, wo_ref = refs[:9]
    pos = 9
    if rope:
        cos_ref, sin_ref = refs[pos:pos + 2]
        pos += 2
    if past_len:
        ck_ref, cv_ref = refs[pos:pos + 2]
        pos += 2
    o_ref = refs[pos]
    pos += 1
    if emit_kv:
        kn_ref, vn_ref = refs[pos:pos + 2]
        pos += 2
    h_ref, acc_ref, q_s, k_s, v_s, o_s = refs[pos:]

    gi = pl.program_id(1)
    tm = x_ref.shape[0]
    hd = k_s.shape[1]
    nqb = seq_len // q_block
    nchunks = (tm // seq_len) * nqb
    nk = past_len + seq_len
    scale = hd ** -0.5

    @pl.when(gi == 0)
    def _():
        h_ref[...] = _sub_in(x_ref[...], m_ref, g_ref, 1).astype(BF16)
        acc_ref[...] = jnp.zeros_like(acc_ref)

    h = h_ref[...]
    q = _mm(h, wq_ref[...])
    k = _rms(_mm(h, wk_ref[...])) * kg_ref[...]
    v = _mm(h, wv_ref[...])
    if emit_kv:
        kn_ref[pl.ds(gi, tm, stride=kvh), :] = k
        vn_ref[pl.ds(gi, tm, stride=kvh), :] = v
    if rope:
        lane = lax.broadcasted_iota(jnp.int32, (tm, hd), 1)
        cos, sin = cos_ref[...], sin_ref[...]
        k = _rope(k, cos, sin, lane)
    k_s[past_len:past_len + tm, :] = k.astype(BF16)
    v_s[past_len:past_len + tm, :] = v.astype(BF16)
    if past_len:
        k_s[0:past_len, :] = ck_ref[pl.ds(gi, past_len, stride=kvh), :].astype(BF16)
        v_s[0:past_len, :] = cv_ref[pl.ds(gi, past_len, stride=kvh), :].astype(BF16)
    for j in range(group):
        qh = _rms(q[:, j * hd:(j + 1) * hd]) * qg_ref[...]
        if rope:
            qh = _rope(qh, cos, sin, lane)
        q_s[:, j * hd:(j + 1) * hd] = qh.astype(BF16)

    def chunk(ci, carry):
        r0 = pl.multiple_of(ci * q_block, q_block)
        if past_len:
            keys, vals = k_s[...], v_s[...]
        else:
            koff = pl.multiple_of((ci // nqb) * seq_len, seq_len)
            keys, vals = k_s[pl.ds(koff, nk), :], v_s[pl.ds(koff, nk), :]
        qc = jnp.concatenate([q_s[pl.ds(r0, q_block), j * hd:(j + 1) * hd] for j in range(group)],
                             axis=0)
        raw = lax.dot_general(qc, keys, (((1,), (1,)), ((), ())), preferred_element_type=F32)
        e = jnp.exp2((raw - jnp.max(raw, axis=-1, keepdims=True)) * (scale * LOG2_E))
        inv = 1.0 / jnp.sum(e, axis=-1, keepdims=True)
        oc = jnp.dot(e.astype(BF16), vals, preferred_element_type=F32) * inv
        for j in range(group):
            o_s[pl.ds(r0, q_block), j * hd:(j + 1) * hd] = oc[j * q_block:(j + 1) * q_block, :].astype(BF16)
        return carry

    lax.fori_loop(0, nchunks, chunk, 0, unroll=min(nchunks, 4))
    acc_ref[...] += _mm(o_s[...], wo_ref[...])

    @pl.when(gi == pl.num_programs(1) - 1)
    def _():
        o_ref[...] = x_ref[...] + _gate(m_ref, 1) * acc_ref[...]


def _rope_tables(n_tok, hd):
    rows = n_tok // GRID_W
    r_idx = jnp.broadcast_to(jnp.arange(rows)[:, None], (rows, GRID_W)).reshape(n_tok).astype(F32)
    c_idx = jnp.broadcast_to(jnp.arange(GRID_W)[None, :], (rows, GRID_W)).reshape(n_tok).astype(F32)
    n_freq = hd // 4
    inv = ROPE_THETA ** (-jnp.arange(n_freq, dtype=F32) / n_freq)
    ang = jnp.stack([r_idx[:, None] * inv, c_idx[:, None] * inv], axis=1)
    cos, sin = jnp.cos(ang), jnp.sin(ang)
    cos_full = jnp.concatenate([cos, cos], axis=-1).reshape(n_tok, hd)
    sin_signed = jnp.concatenate([-sin, sin], axis=-1).reshape(n_tok, hd)
    return cos_full, sin_signed


def _attn(x, m, row_of_tile, norm_g, layer, w_qkv, q_g, k_g, w_o, seq_len, q_block,
          cache_k=None, cache_v=None, rope=False, emit_kv=False):
    t, d = x.shape
    hd, kvh = HEAD_DIM, N_KV_HEADS
    n_heads = w_o.shape[0] // hd
    group = n_heads // kvh
    gw = group * hd
    tm = TOKEN_TILE
    assert t % tm == 0 and tm % seq_len == 0 and seq_len % q_block == 0
    past_len = 0 if cache_k is None else cache_k.shape[1] // kvh
    assert past_len == 0 or tm == seq_len
    in_specs = [
        pl.BlockSpec((tm, d), lambda i, g: (i, 0)),
        pl.BlockSpec((None, None, 3 * N_SUB, d), lambda i, g: (layer, row_of_tile(i), 0, 0)),
        pl.BlockSpec((None, N_SUB, d), lambda i, g: (layer, 0, 0)),
        pl.BlockSpec((d, gw), lambda i, g: (0, g)),
        pl.BlockSpec((d, hd), lambda i, g: (0, n_heads + g)),
        pl.BlockSpec((d, hd), lambda i, g: (0, n_heads + kvh + g)),
        pl.BlockSpec((1, hd), lambda i, g: (0, 0)),
        pl.BlockSpec((1, hd), lambda i, g: (0, 0)),
        pl.BlockSpec((gw, d), lambda i, g: (g, 0)),
    ]
    args = [x, m, norm_g, w_qkv, w_qkv, w_qkv, q_g.reshape(1, hd), k_g.reshape(1, hd), w_o]
    if rope:
        assert tm == seq_len
        cos, sin = _rope_tables(seq_len, hd)
        in_specs += [pl.BlockSpec((tm, hd), lambda i, g: (0, 0))] * 2
        args += [cos, sin]
    if past_len:
        in_specs += [pl.BlockSpec((None, past_len * kvh, hd), lambda i, g: (i, 0, 0))] * 2
        args += [cache_k, cache_v]
    out_specs = [pl.BlockSpec((tm, d), lambda i, g: (i, 0))]
    out_shape = [jax.ShapeDtypeStruct((t, d), F32)]
    if emit_kv:
        out_specs += [pl.BlockSpec((tm * kvh, hd), lambda i, g: (i, 0))] * 2
        out_shape += [jax.ShapeDtypeStruct((t * kvh, hd), F32)] * 2
    nkeys = past_len + tm
    outs = pl.pallas_call(
        functools.partial(_attn_kernel, seq_len=seq_len, q_block=q_block, past_len=past_len,
                          rope=rope, emit_kv=emit_kv, group=group, kvh=kvh),
        grid=(t // tm, kvh),
        in_specs=in_specs,
        out_specs=out_specs,
        out_shape=out_shape,
        scratch_shapes=[pltpu.VMEM((tm, d), BF16), pltpu.VMEM((tm, d), F32),
                        pltpu.VMEM((tm, gw), BF16), pltpu.VMEM((nkeys, hd), BF16),
                        pltpu.VMEM((nkeys, hd), BF16), pltpu.VMEM((tm, gw), BF16)],
        compiler_params=_params("arbitrary", "arbitrary"),
        name="gqa",
    )(*args)
    return outs


def kernel(x_prompt, x_sample, c, state_lru, cache_k, cache_v, c_ctx, mod_w, mod_b, norm_g,
           ffn_w_gu, ffn_w_down, lru_w_in, lru_conv_w, lru_conv_b, lru_gate_w, lru_gate_b,
           lru_lambda, lru_w_out, att_w_qkv, att_q_g, att_k_g, att_w_o, final_g):
    b, s, d = x_prompt.shape
    db, ds, _ = x_sample.shape
    depth = mod_w.shape[0]
    n_mixers = 2
    assert ds % TOKEN_TILE == 0 and 1 + db <= SUBLANES

    xp = x_prompt.reshape(b * s, d)
    xs = x_sample.reshape(db * ds, d)
    cvecs = jnp.concatenate([c_ctx[None], c, jnp.zeros((SUBLANES - 1 - db, d), F32)], axis=0)
    m = _modulation(cvecs, mod_w, mod_b).reshape(depth, SUBLANES, 3 * N_SUB, d)

    tiles_per_sample = ds // TOKEN_TILE
    prompt_row = lambda i: 0
    sample_row = lambda i: 1 + i // tiles_per_sample
    prompt_tiles = (b * s) // TOKEN_TILE
    both_row = lambda i: jnp.where(i < prompt_tiles, 0, 1 + (i - prompt_tiles) // tiles_per_sample)

    new_states, new_k, new_v = [], [], []
    for layer in range(depth):
        j = layer // n_mixers
        last = layer == depth - 1
        xp, xs = _ffn(xp, xs, m, both_row, norm_g, ffn_w_gu, ffn_w_down, layer, 0, 0)
        if layer % n_mixers == 0:
            lru_p = (lru_w_in[j], lru_conv_w[j], lru_conv_b[j], lru_gate_w[j], lru_gate_b[j],
                     lru_lambda[j], lru_w_out[j])
            xp, st = _lru(xp, m, prompt_row, norm_g, layer, *lru_p, seq_len=s, h0=None,
                          emit_state=True)
            new_states.append(st)
            xs, _ = _lru(xs, m, sample_row, norm_g, layer, *lru_p, seq_len=ds,
                         h0=state_lru[:, j], emit_state=False)
        else:
            att_p = (att_w_qkv[j], att_q_g[j], att_k_g[j], att_w_o[j])
            xp, kp, vp = _attn(xp, m, prompt_row, norm_g, layer, *att_p, seq_len=s, q_block=s,
                               emit_kv=True)
            new_k.append(kp.reshape(b, s, N_KV_HEADS, HEAD_DIM))
            new_v.append(vp.reshape(b, s, N_KV_HEADS, HEAD_DIM))
            past = cache_k.shape[2]
            ck = cache_k[:, j].reshape(db, past * N_KV_HEADS, HEAD_DIM)
            cv = cache_v[:, j].reshape(db, past * N_KV_HEADS, HEAD_DIM)
            (xs,) = _attn(xs, m, sample_row, norm_g, layer, *att_p, seq_len=ds, q_block=128,
                          cache_k=ck, cache_v=cv, rope=True)
        fg = final_g if last else None
        xp, xs = _ffn(xp, xs, m, both_row, norm_g, ffn_w_gu, ffn_w_down, layer, 1, 2, final_g=fg)

    y_prompt = xp.reshape(b, s, d)
    y_sample = xs.reshape(db, ds, d)
    return (y_prompt, y_sample, jnp.stack(new_states, axis=1), jnp.stack(new_k, axis=1),
            jnp.stack(new_v, axis=1))
```

```python
import functools

import jax
import jax.numpy as jnp
import numpy as np
from jax import lax
from jax.experimental import pallas as pl
from jax.experimental.pallas import tpu as pltpu

F32 = jnp.float32
BF16 = jnp.bfloat16

EPS = 1e-6
LRU_C = 8.0
LOG2_E = 1.4426950408889634
GRID_W = 64
ROPE_THETA = 10000.0
N_SUB = 3
HEAD_DIM = 128
N_KV_HEADS = 2
LRU_BLOCKS = 16

V7X_VMEM_LIMIT_BYTES = 56 * 1024 * 1024
SUBLANES = 8
TOKEN_TILE = 1024
FF_CHUNK = 256
FFN_ROW_BLOCK = 256
FFN_STAGE_SLOTS = 3
LRU_CHUNK = 512
LRU_GATE_CHUNK = 256
MOD_CHUNK = 1536


def _params(*semantics):
    return pltpu.CompilerParams(dimension_semantics=semantics,
                                vmem_limit_bytes=V7X_VMEM_LIMIT_BYTES)


def _mm(a_bf16, w_f32):
    return jnp.dot(a_bf16, w_f32.astype(BF16), preferred_element_type=F32)


def _rms(x):
    return x * lax.rsqrt(jnp.mean(x * x, axis=-1, keepdims=True) + EPS)


def _sub_in(x, m_ref, g_ref, sidx):
    shift = m_ref[3 * sidx:3 * sidx + 1, :]
    scale = m_ref[3 * sidx + 1:3 * sidx + 2, :]
    return (_rms(x) * g_ref[sidx:sidx + 1, :]) * (1.0 + scale) + shift


def _gate(m_ref, sidx):
    return m_ref[3 * sidx + 2:3 * sidx + 3, :]


def _mod_kernel(cv_ref, w_ref, b_ref, o_ref):
    cv = cv_ref[...]
    act = (cv * jax.nn.sigmoid(cv)).astype(BF16)
    o_ref[...] = _mm(act, w_ref[...]) + b_ref[...]


def _modulation(cvecs, mod_w, mod_b):
    depth, d, n = mod_w.shape
    rows = cvecs.shape[0]
    tn = MOD_CHUNK
    assert n % tn == 0
    return pl.pallas_call(
        _mod_kernel,
        grid=(depth, n // tn),
        in_specs=[
            pl.BlockSpec((rows, d), lambda l, j: (0, 0)),
            pl.BlockSpec((None, d, tn), lambda l, j: (l, 0, j)),
            pl.BlockSpec((None, 1, tn), lambda l, j: (l, 0, j)),
        ],
        out_specs=pl.BlockSpec((None, rows, tn), lambda l, j: (l, 0, j)),
        out_shape=jax.ShapeDtypeStruct((depth, rows, n), F32),
        compiler_params=_params("arbitrary", "arbitrary"),
        name="modulation",
    )(cvecs, mod_w, mod_b.reshape(depth, 1, n))


def _ffn_kernel(*refs, layer, s, sidx, final, n_first):
    if final:
        (m_ref, g_ref, fg_ref, xa_hbm, xb_hbm, wgu_hbm, wd_hbm, oa_hbm, ob_hbm,
         wgu_res, wd_res, act_ref, h0_ref, act0_ref, gu_buf, wd_buf, xbuf, obuf,
         w_sem, xsem, osem) = refs
    else:
        (m_ref, g_ref, xa_hbm, xb_hbm, wgu_hbm, wd_hbm, oa_hbm, ob_hbm,
         wgu_res, wd_res, act_ref, h0_ref, act0_ref, gu_buf, wd_buf, xbuf, obuf,
         w_sem, xsem, osem) = refs
    i = pl.program_id(0)
    n = pl.num_programs(0)
    tm = xbuf.shape[1]
    _, d, tf = wgu_res.shape
    f = wd_res.shape[0]
    nk = f // tf
    rb = act_ref.shape[0]
    nslot = gu_buf.shape[0]
    ahead = nslot - 1
    slot = lax.rem(i, 2)
    other = 1 - slot

    def stage_copies(k, sl):
        gcols = pl.ds(pl.multiple_of(k * tf, tf), tf)
        ucols = pl.ds(pl.multiple_of(f + k * tf, tf), tf)
        return (
            pltpu.make_async_copy(wgu_hbm.at[layer, s, :, gcols], gu_buf.at[sl, 0], w_sem.at[sl, 0]),
            pltpu.make_async_copy(wgu_hbm.at[layer, s, :, ucols], gu_buf.at[sl, 1], w_sem.at[sl, 1]),
            pltpu.make_async_copy(wd_hbm.at[layer, s, gcols, :], wd_buf.at[sl], w_sem.at[sl, 2]),
        )

    def tile_rows(hbm, tile):
        return hbm.at[pl.ds(pl.multiple_of(tile * tm, tm), tm), :]

    def x_copy(which, tile, sl):
        return pltpu.make_async_copy(tile_rows((xa_hbm, xb_hbm)[which], tile), xbuf.at[sl],
                                     xsem.at[sl])

    def o_copy(which, tile, sl):
        return pltpu.make_async_copy(obuf.at[sl], tile_rows((oa_hbm, ob_hbm)[which], tile),
                                     osem.at[sl])

    def start_by_stream(make, tile, sl):
        @pl.when(tile < n_first)
        def _():
            make(0, tile, sl).start()

        @pl.when(tile >= n_first)
        def _():
            make(1, tile - n_first, sl).start()

    @pl.when(i == 0)
    def _():
        x_copy(0, 0, 0).start()
        for k in range(ahead):
            for cp in stage_copies(k, k):
                cp.start()

    @pl.when(i + 1 < n)
    def _():
        start_by_stream(x_copy, i + 1, other)

    x_copy(0, 0, slot).wait()

    @pl.when(i >= 2)
    def _():
        o_copy(0, 0, slot).wait()

    def swiglu(h, wg, wu):
        gt = jnp.dot(h, wg, preferred_element_type=F32)
        up = jnp.dot(h, wu, preferred_element_type=F32)
        return ((gt * jax.nn.sigmoid(gt)) * up).astype(BF16)

    def down_and_store(x, rows):
        ff = jnp.dot(act_ref[...], wd_res[...], preferred_element_type=F32)
        y = x + (0.5 * _gate(m_ref, sidx)) * ff
        if final:
            y = _rms(y) * fg_ref[...]
        obuf[slot, rows, :] = y

    def row_block(r, carry):
        rows = pl.ds(pl.multiple_of(r * rb, rb), rb)
        x = xbuf[slot, rows, :]
        h = _sub_in(x, m_ref, g_ref, sidx).astype(BF16)
        for k in range(nk):
            act_ref[:, k * tf:(k + 1) * tf] = swiglu(h, wgu_res[k], wgu_res[nk + k])
        down_and_store(x, rows)
        return carry

    def first_tile_chunk(k, carry):
        sl = lax.rem(k, nslot)
        for cp in stage_copies(k, sl):
            cp.wait()

        @pl.when(k + ahead < nk)
        def _():
            for cp in stage_copies(k + ahead, lax.rem(k + ahead, nslot)):
                cp.start()

        wg = gu_buf[sl, 0].astype(BF16)
        wu = gu_buf[sl, 1].astype(BF16)
        wgu_res[k] = wg
        wgu_res[nk + k] = wu
        wd_res[pl.ds(pl.multiple_of(k * tf, tf), tf), :] = wd_buf[sl].astype(BF16)
        act0_ref[k] = swiglu(h0_ref[...], wg, wu)
        return carry

    def first_tile_down(r, carry):
        rows = pl.ds(pl.multiple_of(r * rb, rb), rb)
        for k in range(nk):
            act_ref[:, k * tf:(k + 1) * tf] = act0_ref[k, rows, :]
        down_and_store(xbuf[slot, rows, :], rows)
        return carry

    @pl.when(i == 0)
    def _():
        h0_ref[...] = _sub_in(xbuf[slot], m_ref, g_ref, sidx).astype(BF16)
        lax.fori_loop(0, nk, first_tile_chunk, 0)
        lax.fori_loop(0, tm // rb, first_tile_down, 0, unroll=2)

    @pl.when(i > 0)
    def _():
        lax.fori_loop(0, tm // rb, row_block, 0, unroll=True)

    start_by_stream(o_copy, i, slot)

    @pl.when(i == n - 1)
    def _():
        o_copy(0, 0, other).wait()
        o_copy(0, 0, slot).wait()


def _ffn(xa, xb, m, row_of_tile, norm_g, w_gu, w_down, layer, s, sidx, final_g=None):
    (ta, d), tb = xa.shape, xb.shape[0]
    f = w_down.shape[2]
    tm, tf, rb = TOKEN_TILE, FF_CHUNK, FFN_ROW_BLOCK
    assert ta % tm == 0 and tb % tm == 0 and f % tf == 0 and (ta + tb) // tm >= 2
    assert tm % rb == 0 and f // tf >= FFN_STAGE_SLOTS
    final = final_g is not None
    in_specs = [
        pl.BlockSpec((None, None, 3 * N_SUB, d), lambda i: (layer, row_of_tile(i), 0, 0)),
        pl.BlockSpec((None, N_SUB, d), lambda i: (layer, 0, 0)),
    ]
    args = [m, norm_g]
    if final:
        in_specs.append(pl.BlockSpec((1, d), lambda i: (0, 0)))
        args.append(final_g.reshape(1, d))
    in_specs += [pl.BlockSpec(memory_space=pl.ANY)] * 4
    args += [xa, xb, w_gu, w_down]
    return pl.pallas_call(
        functools.partial(_ffn_kernel, layer=layer, s=s, sidx=sidx, final=final,
                          n_first=ta // tm),
        grid=((ta + tb) // tm,),
        in_specs=in_specs,
        out_specs=[pl.BlockSpec(memory_space=pl.ANY)] * 2,
        out_shape=[jax.ShapeDtypeStruct((ta, d), F32), jax.ShapeDtypeStruct((tb, d), F32)],
        scratch_shapes=[
            pltpu.VMEM((2 * f // tf, d, tf), BF16), pltpu.VMEM((f, d), BF16),
            pltpu.VMEM((rb, f), BF16),
            pltpu.VMEM((tm, d), BF16), pltpu.VMEM((f // tf, tm, tf), BF16),
            pltpu.VMEM((FFN_STAGE_SLOTS, 2, d, tf), F32),
            pltpu.VMEM((FFN_STAGE_SLOTS, tf, d), F32),
            pltpu.VMEM((2, tm, d), F32), pltpu.VMEM((2, tm, d), F32),
            pltpu.SemaphoreType.DMA((FFN_STAGE_SLOTS, 3)),
            pltpu.SemaphoreType.DMA((2,)),
            pltpu.SemaphoreType.DMA((2,)),
        ],
        compiler_params=_params("arbitrary"),
        name="ffn",
    )(*args)


def _gelu_tanh(x):
    c = np.float32(np.sqrt(2.0 / np.pi))
    return x * (0.5 * (1.0 + jnp.tanh(c * (x + 0.044715 * (x * x * x)))))


def _time_permutation(seq_len):
    segs = seq_len // SUBLANES
    p = np.arange(seq_len)
    j, s = p // SUBLANES, p % SUBLANES
    mat = np.zeros((seq_len, seq_len), np.float32)
    mat[p, s * segs + j] = 1.0
    return mat


def _per_sequence(mat, rows, seq_len):
    parts = [jnp.dot(mat, rows[lo:lo + seq_len, :], preferred_element_type=F32).astype(BF16)
             for lo in range(0, rows.shape[0], seq_len)]
    return jnp.concatenate(parts, axis=0) if len(parts) > 1 else parts[0]


def _lru_kernel(*refs, seq_len, has_h0, emit_state, conv_left):
    refs = list(refs)
    (x_ref, m_ref, g_ref, wx_ref, wy_ref, cw_ref, cb_ref, gw_ref, gb_ref, lam_ref,
     wo_ref, pin_ref, pout_ref) = refs[:13]
    pos = 13
    h0_ref = None
    if has_h0:
        h0_ref = refs[pos]
        pos += 1
    o_ref = refs[pos]
    pos += 1
    st_ref = None
    if emit_state:
        st_ref = refs[pos]
        pos += 1
    h_ref, acc_ref, af_ref, uf_ref, ab_ref, ub_ref = refs[pos:]

    c = pl.program_id(1)
    tm, cw_cols = af_ref.shape
    nseq = tm // seq_len
    segs = seq_len // SUBLANES

    @pl.when(c == 0)
    def _():
        h = _sub_in(x_ref[...], m_ref, g_ref, 1).astype(BF16)
        h_ref[...] = _per_sequence(pin_ref[...], h, seq_len)
        acc_ref[...] = jnp.zeros_like(acc_ref)

    h = h_ref[...]
    xb = _mm(h, wx_ref[...])
    yb = _mm(h, wy_ref[...])

    sub = lax.broadcasted_iota(jnp.int32, (SUBLANES, cw_cols), 0)

    def next_segment(grp):
        return jnp.where(sub < SUBLANES - 1, pltpu.roll(grp, SUBLANES - 1, 0), 0.0)

    def prev_segment(grp):
        return jnp.where(sub > 0, pltpu.roll(grp, 1, 0), 0.0)

    def tap(seq, off):
        n = abs(off) * SUBLANES
        if off == 0:
            return seq
        if off > 0:
            edge = [next_segment(seq[g * SUBLANES:(g + 1) * SUBLANES, :]) for g in range(off)]
            return jnp.concatenate([seq[n:, :]] + edge, axis=0)
        start = seq_len - n
        edge = [prev_segment(seq[start + g * SUBLANES:start + (g + 1) * SUBLANES, :])
                for g in range(-off)]
        return jnp.concatenate(edge + [seq[:start, :]], axis=0)

    cw = cw_ref[...]
    assert cw.shape[0] - 1 < segs
    xcs = []
    for sq in range(nseq):
        seq = xb[sq * seq_len:(sq + 1) * seq_len, :]
        acc = cb_ref[...]
        for k in range(cw.shape[0]):
            acc = acc + tap(seq, k - conv_left) * cw[k:k + 1, :]
        xcs.append(acc)
    xc = jnp.concatenate(xcs, axis=0) if nseq > 1 else xcs[0]

    gb = gb_ref[...]
    lam = lam_ref[...]
    gc = gw_ref.shape[1]
    for ch in range(cw_cols // gc):
        cols = slice(ch * gc, (ch + 1) * gc)
        xcc = xc[:, cols]
        gl = jnp.dot(xcc.astype(BF16), gw_ref[ch], preferred_element_type=F32)
        for d, (a_ref, u_ref) in enumerate(((af_ref, uf_ref), (ab_ref, ub_ref))):
            tr = jnp.tanh(gl[:, (2 * d) * gc:(2 * d + 1) * gc] + 0.5 * gb[2 * d:2 * d + 1, cols])
            ig = 0.5 * jnp.tanh(gl[:, (2 * d + 1) * gc:(2 * d + 2) * gc]
                                + 0.5 * gb[2 * d + 1:2 * d + 2, cols]) + 0.5
            lm = lam[d:d + 1, cols]
            log_sig = jnp.minimum(lm, 0.0) - jnp.log1p(jnp.exp(-jnp.abs(lm)))
            c2 = (0.5 * LRU_C * LOG2_E) * log_sig
            a = jnp.exp2(c2 * tr + c2)
            v = 1.0 - a * a
            u = jnp.where(v > 0.0, v * lax.rsqrt(v), 0.0) * (ig * xcc)
            a_ref[:, cols] = a
            u_ref[:, cols] = u

    zero = jnp.zeros((SUBLANES, cw_cols), F32)
    one = jnp.ones((SUBLANES, cw_cols), F32)

    def local_scan(j, carry):
        out = []
        for sq in range(nseq):
            for dd, (a_ref, u_ref) in enumerate(((af_ref, uf_ref), (ab_ref, ub_ref))):
                grp = j if dd == 0 else segs - 1 - j
                rows = pl.ds(pl.multiple_of(sq * seq_len + grp * SUBLANES, SUBLANES), SUBLANES)
                hl, pc = carry[2 * (2 * sq + dd)], carry[2 * (2 * sq + dd) + 1]
                a = a_ref[rows, :]
                hl = a * hl + u_ref[rows, :]
                pc = a * pc
                u_ref[rows, :] = hl
                a_ref[rows, :] = pc
                out += [hl, pc]
        return tuple(out)

    totals = lax.fori_loop(0, segs, local_scan, (zero, one) * (2 * nseq), unroll=True)

    gelu_y = _gelu_tanh(yb)
    ys = []
    for sq in range(nseq):
        entry = []
        for dd in range(2):
            hl, pc = totals[2 * (2 * sq + dd)], totals[2 * (2 * sq + dd) + 1]
            state = h0_ref[sq, dd:dd + 1, :] if has_h0 else jnp.zeros((1, cw_cols), F32)
            rows = [None] * SUBLANES
            order = range(SUBLANES) if dd == 0 else range(SUBLANES - 1, -1, -1)
            for s in order:
                rows[s] = state
                state = pc[s:s + 1, :] * state + hl[s:s + 1, :]
            entry.append(jnp.concatenate(rows, axis=0))
            if emit_state:
                st_ref[sq, dd:dd + 1, :] = state
        lo, hi = sq * seq_len, (sq + 1) * seq_len
        grouped = (segs, SUBLANES, cw_cols)
        hsum = ((uf_ref[lo:hi, :].reshape(grouped) + af_ref[lo:hi, :].reshape(grouped) * entry[0][None])
                + (ub_ref[lo:hi, :].reshape(grouped) + ab_ref[lo:hi, :].reshape(grouped) * entry[1][None]))
        ys.append(hsum.reshape(seq_len, cw_cols) * gelu_y[lo:hi, :])
    y = jnp.concatenate(ys, axis=0) if nseq > 1 else ys[0]

    y = _per_sequence(pout_ref[...], y.astype(BF16), seq_len)
    acc_ref[...] += _mm(y, wo_ref[...])

    @pl.when(c == pl.num_programs(1) - 1)
    def _():
        o_ref[...] = x_ref[...] + _gate(m_ref, 1) * acc_ref[...]


def _lru_gate_weights(gate_w):
    nd, ng, nb, bw, _ = gate_w.shape
    per = LRU_GATE_CHUNK // bw
    nc = nb // per
    w = (0.5 * gate_w).astype(BF16).reshape(nd * ng, nc, per, bw, bw)
    rows = []
    for n in range(per):
        blk = jnp.transpose(w[:, :, n], (1, 2, 0, 3))
        blk = jnp.pad(blk, ((0, 0), (0, 0), (0, 0), (n * bw, (per - 1 - n) * bw)))
        rows.append(blk.reshape(nc, bw, nd * ng * per * bw))
    return jnp.concatenate(rows, axis=1)


def _lru(x, m, row_of_tile, norm_g, layer, w_in, conv_w, conv_b, gate_w, gate_b, lam, w_out,
         seq_len, h0, emit_state):
    t, d = x.shape
    r = w_out.shape[0]
    tm, cb, gc = TOKEN_TILE, LRU_CHUNK, LRU_GATE_CHUNK
    assert t % tm == 0 and tm % seq_len == 0 and r % cb == 0 and seq_len % SUBLANES == 0
    assert cb % gc == 0
    nc = r // cb
    nseq = tm // seq_len
    has_h0 = h0 is not None
    in_specs = [
        pl.BlockSpec((tm, d), lambda i, c: (i, 0)),
        pl.BlockSpec((None, None, 3 * N_SUB, d), lambda i, c: (layer, row_of_tile(i), 0, 0)),
        pl.BlockSpec((None, N_SUB, d), lambda i, c: (layer, 0, 0)),
        pl.BlockSpec((d, cb), lambda i, c: (0, c)),
        pl.BlockSpec((d, cb), lambda i, c: (0, nc + c)),
        pl.BlockSpec((conv_w.shape[0], cb), lambda i, c: (0, c)),
        pl.BlockSpec((1, cb), lambda i, c: (0, c)),
        pl.BlockSpec((cb // gc, gc, 4 * gc), lambda i, c: (c, 0, 0)),
        pl.BlockSpec((4, cb), lambda i, c: (0, c)),
        pl.BlockSpec((2, cb), lambda i, c: (0, c)),
        pl.BlockSpec((cb, d), lambda i, c: (c, 0)),
        pl.BlockSpec((seq_len, seq_len), lambda i, c: (0, 0), pipeline_mode=pl.Buffered(1)),
        pl.BlockSpec((seq_len, seq_len), lambda i, c: (0, 0), pipeline_mode=pl.Buffered(1)),
    ]
    perm = _time_permutation(seq_len)
    args = [x, m, norm_g, w_in, w_in, conv_w, conv_b.reshape(1, r), _lru_gate_weights(gate_w),
            gate_b.reshape(4, r), lam, w_out, jnp.asarray(perm, BF16), jnp.asarray(perm.T, BF16)]
    if has_h0:
        in_specs.append(pl.BlockSpec((nseq, 2, cb), lambda i, c: (i, 0, c)))
        args.append(h0)
    out_specs = [pl.BlockSpec((tm, d), lambda i, c: (i, 0))]
    out_shape = [jax.ShapeDtypeStruct((t, d), F32)]
    if emit_state:
        out_specs.append(pl.BlockSpec((nseq, 2, cb), lambda i, c: (i, 0, c)))
        out_shape.append(jax.ShapeDtypeStruct((t // seq_len, 2, r), F32))
    outs = pl.pallas_call(
        functools.partial(_lru_kernel, seq_len=seq_len, has_h0=has_h0, emit_state=emit_state,
                          conv_left=(conv_w.shape[0] - 1) // 2),
        grid=(t // tm, nc),
        in_specs=in_specs,
        out_specs=out_specs,
        out_shape=out_shape,
        scratch_shapes=[pltpu.VMEM((tm, d), BF16), pltpu.VMEM((tm, d), F32)]
        + [pltpu.VMEM((tm, cb), F32)] * 4,
        compiler_params=_params("arbitrary", "arbitrary"),
        name="rglru",
    )(*args)
    return outs if emit_state else (outs[0], None)


def _rope(x, cos, sin_signed, lane):
    hd = x.shape[1]
    partner = jnp.where((lane & 32) == 0, pltpu.roll(x, hd - 32, 1), pltpu.roll(x, 32, 1))
    return x * cos + partner * sin_signed


def _attn_kernel(*refs, seq_len, q_block, past_len, rope, emit_kv, group, kvh):
    refs = list(refs)
    x_ref, m_ref, g_ref, wq_ref, wk_ref, wv_ref, qg_ref, kg_ref, wo_ref = refs[:9]
    pos = 9
    if rope:
        cos_ref, sin_ref = refs[pos:pos + 2]
        pos += 2
    if past_len:
        ck_ref, cv_ref = refs[pos:pos + 2]
        pos += 2
    o_ref = refs[pos]
    pos += 1
    if emit_kv:
        kn_ref, vn_ref = refs[pos:pos + 2]
        pos += 2
    h_ref, acc_ref, q_s, k_s, v_s, o_s = refs[pos:]

    gi = pl.program_id(1)
    tm = x_ref.shape[0]
    hd = k_s.shape[1]
    nqb = seq_len // q_block
    nchunks = (tm // seq_len) * nqb
    nk = past_len + seq_len
    scale = hd ** -0.5

    @pl.when(gi == 0)
    def _():
        h_ref[...] = _sub_in(x_ref[...], m_ref, g_ref, 1).astype(BF16)
        acc_ref[...] = jnp.zeros_like(acc_ref)

    h = h_ref[...]
    q = _mm(h, wq_ref[...])
    k = _rms(_mm(h, wk_ref[...])) * kg_ref[...]
    v = _mm(h, wv_ref[...])
    if emit_kv:
        kn_ref[pl.ds(gi, tm, stride=kvh), :] = k
        vn_ref[pl.ds(gi, tm, stride=kvh), :] = v
    if rope:
        lane = lax.broadcasted_iota(jnp.int32, (tm, hd), 1)
        cos, sin = cos_ref[...], sin_ref[...]
        k = _rope(k, cos, sin, lane)
    k_s[past_len:past_len + tm, :] = k.astype(BF16)
    v_s[past_len:past_len + tm, :] = v.astype(BF16)
    if past_len:
        k_s[0:past_len, :] = ck_ref[pl.ds(gi, past_len, stride=kvh), :].astype(BF16)
        v_s[0:past_len, :] = cv_ref[pl.ds(gi, past_len, stride=kvh), :].astype(BF16)
    for j in range(group):
        qh = _rms(q[:, j * hd:(j + 1) * hd]) * qg_ref[...]
        if rope:
            qh = _rope(qh, cos, sin, lane)
        q_s[:, j * hd:(j + 1) * hd] = qh.astype(BF16)

    def chunk(ci, carry):
        r0 = pl.multiple_of(ci * q_block, q_block)
        if past_len:
            keys, vals = k_s[...], v_s[...]
        else:
            koff = pl.multiple_of((ci // nqb) * seq_len, seq_len)
            keys, vals = k_s[pl.ds(koff, nk), :], v_s[pl.ds(koff, nk), :]
        qc = jnp.concatenate([q_s[pl.ds(r0, q_block), j * hd:(j + 1) * hd] for j in range(group)],
                             axis=0)
        raw = lax.dot_general(qc, keys, (((1,), (1,)), ((), ())), preferred_element_type=F32)
        e = jnp.exp2((raw - jnp.max(raw, axis=-1, keepdims=True)) * (scale * LOG2_E))
        inv = 1.0 / jnp.sum(e, axis=-1, keepdims=True)
        oc = jnp.dot(e.astype(BF16), vals, preferred_element_type=F32) * inv
        for j in range(group):
            o_s[pl.ds(r0, q_block), j * hd:(j + 1) * hd] = oc[j * q_block:(j + 1) * q_block, :].astype(BF16)
        return carry

    lax.fori_loop(0, nchunks, chunk, 0, unroll=min(nchunks, 4))
    acc_ref[...] += _mm(o_s[...], wo_ref[...])

    @pl.when(gi == pl.num_programs(1) - 1)
    def _():
        o_ref[...] = x_ref[...] + _gate(m_ref, 1) * acc_ref[...]


def _rope_tables(n_tok, hd):
    rows = n_tok // GRID_W
    r_idx = jnp.broadcast_to(jnp.arange(rows)[:, None], (rows, GRID_W)).reshape(n_tok).astype(F32)
    c_idx = jnp.broadcast_to(jnp.arange(GRID_W)[None, :], (rows, GRID_W)).reshape(n_tok).astype(F32)
    n_freq = hd // 4
    inv = ROPE_THETA ** (-jnp.arange(n_freq, dtype=F32) / n_freq)
    ang = jnp.stack([r_idx[:, None] * inv, c_idx[:, None] * inv], axis=1)
    cos, sin = jnp.cos(ang), jnp.sin(ang)
    cos_full = jnp.concatenate([cos, cos], axis=-1).reshape(n_tok, hd)
    sin_signed = jnp.concatenate([-sin, sin], axis=-1).reshape(n_tok, hd)
    return cos_full, sin_signed


def _attn(x, m, row_of_tile, norm_g, layer, w_qkv, q_g, k_g, w_o, seq_len, q_block,
          cache_k=None, cache_v=None, rope=False, emit_kv=False):
    t, d = x.shape
    hd, kvh = HEAD_DIM, N_KV_HEADS
    n_heads = w_o.shape[0] // hd
    group = n_heads // kvh
    gw = group * hd
    tm = TOKEN_TILE
    assert t % tm == 0 and tm % seq_len == 0 and seq_len % q_block == 0
    past_len = 0 if cache_k is None else cache_k.shape[1] // kvh
    assert past_len == 0 or tm == seq_len
    in_specs = [
        pl.BlockSpec((tm, d), lambda i, g: (i, 0)),
        pl.BlockSpec((None, None, 3 * N_SUB, d), lambda i, g: (layer, row_of_tile(i), 0, 0)),
        pl.BlockSpec((None, N_SUB, d), lambda i, g: (layer, 0, 0)),
        pl.BlockSpec((d, gw), lambda i, g: (0, g)),
        pl.BlockSpec((d, hd), lambda i, g: (0, n_heads + g)),
        pl.BlockSpec((d, hd), lambda i, g: (0, n_heads + kvh + g)),
        pl.BlockSpec((1, hd), lambda i, g: (0, 0)),
        pl.BlockSpec((1, hd), lambda i, g: (0, 0)),
        pl.BlockSpec((gw, d), lambda i, g: (g, 0)),
    ]
    args = [x, m, norm_g, w_qkv, w_qkv, w_qkv, q_g.reshape(1, hd), k_g.reshape(1, hd), w_o]
    if rope:
        assert tm == seq_len
        cos, sin = _rope_tables(seq_len, hd)
        in_specs += [pl.BlockSpec((tm, hd), lambda i, g: (0, 0))] * 2
        args += [cos, sin]
    if past_len:
        in_specs += [pl.BlockSpec((None, past_len * kvh, hd), lambda i, g: (i, 0, 0))] * 2
        args += [cache_k, cache_v]
    out_specs = [pl.BlockSpec((tm, d), lambda i, g: (i, 0))]
    out_shape = [jax.ShapeDtypeStruct((t, d), F32)]
    if emit_kv:
        out_specs += [pl.BlockSpec((tm * kvh, hd), lambda i, g: (i, 0))] * 2
        out_shape += [jax.ShapeDtypeStruct((t * kvh, hd), F32)] * 2
    nkeys = past_len + tm
    outs = pl.pallas_call(
        functools.partial(_attn_kernel, seq_len=seq_len, q_block=q_block, past_len=past_len,
                          rope=rope, emit_kv=emit_kv, group=group, kvh=kvh),
        grid=(t // tm, kvh),
        in_specs=in_specs,
        out_specs=out_specs,
        out_shape=out_shape,
        scratch_shapes=[pltpu.VMEM((tm, d), BF16), pltpu.VMEM((tm, d), F32),
                        pltpu.VMEM((tm, gw), BF16), pltpu.VMEM((nkeys, hd), BF16),
                        pltpu.VMEM((nkeys, hd), BF16), pltpu.VMEM((tm, gw), BF16)],
        compiler_params=_params("arbitrary", "arbitrary"),
        name="gqa",
    )(*args)
    return outs


def kernel(x_prompt, x_sample, c, state_lru, cache_k, cache_v, c_ctx, mod_w, mod_b, norm_g,
           ffn_w_gu, ffn_w_down, lru_w_in, lru_conv_w, lru_conv_b, lru_gate_w, lru_gate_b,
           lru_lambda, lru_w_out, att_w_qkv, att_q_g, att_k_g, att_w_o, final_g):
    b, s, d = x_prompt.shape
    db, ds, _ = x_sample.shape
    depth = mod_w.shape[0]
    n_mixers = 2
    assert ds % TOKEN_TILE == 0 and 1 + db <= SUBLANES

    xp = x_prompt.reshape(b * s, d)
    xs = x_sample.reshape(db * ds, d)
    cvecs = jnp.concatenate([c_ctx[None], c, jnp.zeros((SUBLANES - 1 - db, d), F32)], axis=0)
    m = _modulation(cvecs, mod_w, mod_b).reshape(depth, SUBLANES, 3 * N_SUB, d)

    tiles_per_sample = ds // TOKEN_TILE
    prompt_row = lambda i: 0
    sample_row = lambda i: 1 + i // tiles_per_sample
    prompt_tiles = (b * s) // TOKEN_TILE
    both_row = lambda i: jnp.where(i < prompt_tiles, 0, 1 + (i - prompt_tiles) // tiles_per_sample)

    new_states, new_k, new_v = [], [], []
    for layer in range(depth):
        j = layer // n_mixers
        last = layer == depth - 1
        xp, xs = _ffn(xp, xs, m, both_row, norm_g, ffn_w_gu, ffn_w_down, layer, 0, 0)
        if layer % n_mixers == 0:
            lru_p = (lru_w_in[j], lru_conv_w[j], lru_conv_b[j], lru_gate_w[j], lru_gate_b[j],
                     lru_lambda[j], lru_w_out[j])
            xp, st = _lru(xp, m, prompt_row, norm_g, layer, *lru_p, seq_len=s, h0=None,
                          emit_state=True)
            new_states.append(st)
            xs, _ = _lru(xs, m, sample_row, norm_g, layer, *lru_p, seq_len=ds,
                         h0=state_lru[:, j], emit_state=False)
        else:
            att_p = (att_w_qkv[j], att_q_g[j], att_k_g[j], att_w_o[j])
            xp, kp, vp = _attn(xp, m, prompt_row, norm_g, layer, *att_p, seq_len=s, q_block=s,
                               emit_kv=True)
            new_k.append(kp.reshape(b, s, N_KV_HEADS, HEAD_DIM))
            new_v.append(vp.reshape(b, s, N_KV_HEADS, HEAD_DIM))
            past = cache_k.shape[2]
            ck = cache_k[:, j].reshape(db, past * N_KV_HEADS, HEAD_DIM)
            cv = cache_v[:, j].reshape(db, past * N_KV_HEADS, HEAD_DIM)
            (xs,) = _attn(xs, m, sample_row, norm_g, layer, *att_p, seq_len=ds, q_block=128,
                          cache_k=ck, cache_v=cv, rope=True)
        fg = final_g if last else None
        xp, xs = _ffn(xp, xs, m, both_row, norm_g, ffn_w_gu, ffn_w_down, layer, 1, 2, final_g=fg)

    y_prompt = xp.reshape(b, s, d)
    y_sample = xs.reshape(db, ds, d)
    return (y_prompt, y_sample, jnp.stack(new_states, axis=1), jnp.stack(new_k, axis=1),
            jnp.stack(new_v, axis=1))
```

```python
import functools

import jax
import jax.numpy as jnp
import numpy as np
from jax import lax
from jax.experimental import pallas as pl
from jax.experimental.pallas import tpu as pltpu

F32 = jnp.float32
BF16 = jnp.bfloat16

EPS = 1e-6
LRU_C = 8.0
LOG2_E = 1.4426950408889634
GRID_W = 64
ROPE_THETA = 10000.0
N_SUB = 3
HEAD_DIM = 128
N_KV_HEADS = 2
LRU_BLOCKS = 16

V7X_VMEM_LIMIT_BYTES = 56 * 1024 * 1024
SUBLANES = 8
TOKEN_TILE = 1024
FF_CHUNK = 256
FFN_ROW_BLOCK = 256
FFN_STAGE_SLOTS = 3
LRU_CHUNK = 512
LRU_GATE_CHUNK = 256
MOD_CHUNK = 1536


def _params(*semantics):
    return pltpu.CompilerParams(dimension_semantics=semantics,
                                vmem_limit_bytes=V7X_VMEM_LIMIT_BYTES)


def _mm(a_bf16, w_f32):
    return jnp.dot(a_bf16, w_f32.astype(BF16), preferred_element_type=F32)


def _rms(x):
    return x * lax.rsqrt(jnp.mean(x * x, axis=-1, keepdims=True) + EPS)


def _sub_in(x, m_ref, g_ref, sidx):
    shift = m_ref[3 * sidx:3 * sidx + 1, :]
    scale = m_ref[3 * sidx + 1:3 * sidx + 2, :]
    return (_rms(x) * g_ref[sidx:sidx + 1, :]) * (1.0 + scale) + shift


def _gate(m_ref, sidx):
    return m_ref[3 * sidx + 2:3 * sidx + 3, :]


def _mod_kernel(cv_ref, w_ref, b_ref, o_ref):
    cv = cv_ref[...]
    act = (cv * jax.nn.sigmoid(cv)).astype(BF16)
    o_ref[...] = _mm(act, w_ref[...]) + b_ref[...]


def _modulation(cvecs, mod_w, mod_b):
    depth, d, n = mod_w.shape
    rows = cvecs.shape[0]
    tn = MOD_CHUNK
    assert n % tn == 0
    return pl.pallas_call(
        _mod_kernel,
        grid=(depth, n // tn),
        in_specs=[
            pl.BlockSpec((rows, d), lambda l, j: (0, 0)),
            pl.BlockSpec((None, d, tn), lambda l, j: (l, 0, j)),
            pl.BlockSpec((None, 1, tn), lambda l, j: (l, 0, j)),
        ],
        out_specs=pl.BlockSpec((None, rows, tn), lambda l, j: (l, 0, j)),
        out_shape=jax.ShapeDtypeStruct((depth, rows, n), F32),
        compiler_params=_params("arbitrary", "arbitrary"),
        name="modulation",
    )(cvecs, mod_w, mod_b.reshape(depth, 1, n))


def _ffn_kernel(*refs, layer, s, sidx, final, n_first):
    if final:
        (m_ref, g_ref, fg_ref, xa_hbm, xb_hbm, wgu_hbm, wd_hbm, oa_hbm, ob_hbm,
         wgu_res, wd_res, act_ref, h0_ref, act0_ref, gu_buf, wd_buf, xbuf, obuf,
         w_sem, xsem, osem) = refs
    else:
        (m_ref, g_ref, xa_hbm, xb_hbm, wgu_hbm, wd_hbm, oa_hbm, ob_hbm,
         wgu_res, wd_res, act_ref, h0_ref, act0_ref, gu_buf, wd_buf, xbuf, obuf,
         w_sem, xsem, osem) = refs
    i = pl.program_id(0)
    n = pl.num_programs(0)
    tm = xbuf.shape[1]
    _, d, tf = wgu_res.shape
    f = wd_res.shape[0]
    nk = f // tf
    rb = act_ref.shape[0]
    nslot = gu_buf.shape[0]
    ahead = nslot - 1
    slot = lax.rem(i, 2)
    other = 1 - slot

    def stage_copies(k, sl):
        gcols = pl.ds(pl.multiple_of(k * tf, tf), tf)
        ucols = pl.ds(pl.multiple_of(f + k * tf, tf), tf)
        return (
            pltpu.make_async_copy(wgu_hbm.at[layer, s, :, gcols], gu_buf.at[sl, 0], w_sem.at[sl, 0]),
            pltpu.make_async_copy(wgu_hbm.at[layer, s, :, ucols], gu_buf.at[sl, 1], w_sem.at[sl, 1]),
            pltpu.make_async_copy(wd_hbm.at[layer, s, gcols, :], wd_buf.at[sl], w_sem.at[sl, 2]),
        )

    def tile_rows(hbm, tile):
        return hbm.at[pl.ds(pl.multiple_of(tile * tm, tm), tm), :]

    def x_copy(which, tile, sl):
        return pltpu.make_async_copy(tile_rows((xa_hbm, xb_hbm)[which], tile), xbuf.at[sl],
                                     xsem.at[sl])

    def o_copy(which, tile, sl):
        return pltpu.make_async_copy(obuf.at[sl], tile_rows((oa_hbm, ob_hbm)[which], tile),
                                     osem.at[sl])

    def start_by_stream(make, tile, sl):
        @pl.when(tile < n_first)
        def _():
            make(0, tile, sl).start()

        @pl.when(tile >= n_first)
        def _():
            make(1, tile - n_first, sl).start()

    @pl.when(i == 0)
    def _():
        x_copy(0, 0, 0).start()
        for k in range(ahead):
            for cp in stage_copies(k, k):
                cp.start()

    @pl.when(i + 1 < n)
    def _():
        start_by_stream(x_copy, i + 1, other)

    x_copy(0, 0, slot).wait()

    @pl.when(i >= 2)
    def _():
        o_copy(0, 0, slot).wait()

    def swiglu(h, wg, wu):
        gt = jnp.dot(h, wg, preferred_element_type=F32)
        up = jnp.dot(h, wu, preferred_element_type=F32)
        return ((gt * jax.nn.sigmoid(gt)) * up).astype(BF16)

    def down_and_store(x, rows):
        ff = jnp.dot(act_ref[...], wd_res[...], preferred_element_type=F32)
        y = x + (0.5 * _gate(m_ref, sidx)) * ff
        if final:
            y = _rms(y) * fg_ref[...]
        obuf[slot, rows, :] = y

    def row_block(r, carry):
        rows = pl.ds(pl.multiple_of(r * rb, rb), rb)
        x = xbuf[slot, rows, :]
        h = _sub_in(x, m_ref, g_ref, sidx).astype(BF16)
        for k in range(nk):
            act_ref[:, k * tf:(k + 1) * tf] = swiglu(h, wgu_res[k], wgu_res[nk + k])
        down_and_store(x, rows)
        return carry

    def first_tile_chunk(k, carry):
        sl = lax.rem(k, nslot)
        for cp in stage_copies(k, sl):
            cp.wait()

        @pl.when(k + ahead < nk)
        def _():
            for cp in stage_copies(k + ahead, lax.rem(k + ahead, nslot)):
                cp.start()

        wg = gu_buf[sl, 0].astype(BF16)
        wu = gu_buf[sl, 1].astype(BF16)
        wgu_res[k] = wg
        wgu_res[nk + k] = wu
        wd_res[pl.ds(pl.multiple_of(k * tf, tf), tf), :] = wd_buf[sl].astype(BF16)
        act0_ref[k] = swiglu(h0_ref[...], wg, wu)
        return carry

    def first_tile_down(r, carry):
        rows = pl.ds(pl.multiple_of(r * rb, rb), rb)
        for k in range(nk):
            act_ref[:, k * tf:(k + 1) * tf] = act0_ref[k, rows, :]
        down_and_store(xbuf[slot, rows, :], rows)
        return carry

    @pl.when(i == 0)
    def _():
        h0_ref[...] = _sub_in(xbuf[slot], m_ref, g_ref, sidx).astype(BF16)
        lax.fori_loop(0, nk, first_tile_chunk, 0)
        lax.fori_loop(0, tm // rb, first_tile_down, 0, unroll=2)

    @pl.when(i > 0)
    def _():
        lax.fori_loop(0, tm // rb, row_block, 0, unroll=2)

    start_by_stream(o_copy, i, slot)

    @pl.when(i == n - 1)
    def _():
        o_copy(0, 0, other).wait()
        o_copy(0, 0, slot).wait()


def _ffn(xa, xb, m, row_of_tile, norm_g, w_gu, w_down, layer, s, sidx, final_g=None):
    (ta, d), tb = xa.shape, xb.shape[0]
    f = w_down.shape[2]
    tm, tf, rb = TOKEN_TILE, FF_CHUNK, FFN_ROW_BLOCK
    assert ta % tm == 0 and tb % tm == 0 and f % tf == 0 and (ta + tb) // tm >= 2
    assert tm % rb == 0 and f // tf >= FFN_STAGE_SLOTS
    final = final_g is not None
    in_specs = [
        pl.BlockSpec((None, None, 3 * N_SUB, d), lambda i: (layer, row_of_tile(i), 0, 0)),
        pl.BlockSpec((None, N_SUB, d), lambda i: (layer, 0, 0)),
    ]
    args = [m, norm_g]
    if final:
        in_specs.append(pl.BlockSpec((1, d), lambda i: (0, 0)))
        args.append(final_g.reshape(1, d))
    in_specs += [pl.BlockSpec(memory_space=pl.ANY)] * 4
    args += [xa, xb, w_gu, w_down]
    return pl.pallas_call(
        functools.partial(_ffn_kernel, layer=layer, s=s, sidx=sidx, final=final,
                          n_first=ta // tm),
        grid=((ta + tb) // tm,),
        in_specs=in_specs,
        out_specs=[pl.BlockSpec(memory_space=pl.ANY)] * 2,
        out_shape=[jax.ShapeDtypeStruct((ta, d), F32), jax.ShapeDtypeStruct((tb, d), F32)],
        scratch_shapes=[
            pltpu.VMEM((2 * f // tf, d, tf), BF16), pltpu.VMEM((f, d), BF16),
            pltpu.VMEM((rb, f), BF16),
            pltpu.VMEM((tm, d), BF16), pltpu.VMEM((f // tf, tm, tf), BF16),
            pltpu.VMEM((FFN_STAGE_SLOTS, 2, d, tf), F32),
            pltpu.VMEM((FFN_STAGE_SLOTS, tf, d), F32),
            pltpu.VMEM((2, tm, d), F32), pltpu.VMEM((2, tm, d), F32),
            pltpu.SemaphoreType.DMA((FFN_STAGE_SLOTS, 3)),
            pltpu.SemaphoreType.DMA((2,)),
            pltpu.SemaphoreType.DMA((2,)),
        ],
        compiler_params=_params("arbitrary"),
        name="ffn",
    )(*args)


def _gelu_tanh(x):
    c = np.float32(np.sqrt(2.0 / np.pi))
    return x * (0.5 * (1.0 + jnp.tanh(c * (x + 0.044715 * (x * x * x)))))


def _time_permutation(seq_len):
    segs = seq_len // SUBLANES
    p = np.arange(seq_len)
    j, s = p // SUBLANES, p % SUBLANES
    mat = np.zeros((seq_len, seq_len), np.float32)
    mat[p, s * segs + j] = 1.0
    return mat


def _per_sequence(mat, rows, seq_len):
    parts = [jnp.dot(mat, rows[lo:lo + seq_len, :], preferred_element_type=F32).astype(BF16)
             for lo in range(0, rows.shape[0], seq_len)]
    return jnp.concatenate(parts, axis=0) if len(parts) > 1 else parts[0]


def _lru_kernel(*refs, seq_len, has_h0, emit_state, conv_left):
    refs = list(refs)
    (x_ref, m_ref, g_ref, wx_ref, wy_ref, cw_ref, cb_ref, gw_ref, gb_ref, lam_ref,
     wo_ref, pin_ref, pout_ref) = refs[:13]
    pos = 13
    h0_ref = None
    if has_h0:
        h0_ref = refs[pos]
        pos += 1
    o_ref = refs[pos]
    pos += 1
    st_ref = None
    if emit_state:
        st_ref = refs[pos]
        pos += 1
    h_ref, acc_ref, af_ref, uf_ref, ab_ref, ub_ref = refs[pos:]

    c = pl.program_id(1)
    tm, cw_cols = af_ref.shape
    nseq = tm // seq_len
    segs = seq_len // SUBLANES

    @pl.when(c == 0)
    def _():
        h = _sub_in(x_ref[...], m_ref, g_ref, 1).astype(BF16)
        h_ref[...] = _per_sequence(pin_ref[...], h, seq_len)
        acc_ref[...] = jnp.zeros_like(acc_ref)

    h = h_ref[...]
    xb = _mm(h, wx_ref[...])
    yb = _mm(h, wy_ref[...])

    sub = lax.broadcasted_iota(jnp.int32, (SUBLANES, cw_cols), 0)

    def next_segment(grp):
        return jnp.where(sub < SUBLANES - 1, pltpu.roll(grp, SUBLANES - 1, 0), 0.0)

    def prev_segment(grp):
        return jnp.where(sub > 0, pltpu.roll(grp, 1, 0), 0.0)

    def tap(seq, off):
        n = abs(off) * SUBLANES
        if off == 0:
            return seq
        if off > 0:
            edge = [next_segment(seq[g * SUBLANES:(g + 1) * SUBLANES, :]) for g in range(off)]
            return jnp.concatenate([seq[n:, :]] + edge, axis=0)
        start = seq_len - n
        edge = [prev_segment(seq[start + g * SUBLANES:start + (g + 1) * SUBLANES, :])
                for g in range(-off)]
        return jnp.concatenate(edge + [seq[:start, :]], axis=0)

    cw = cw_ref[...]
    assert cw.shape[0] - 1 < segs
    xcs = []
    for sq in range(nseq):
        seq = xb[sq * seq_len:(sq + 1) * seq_len, :]
        acc = cb_ref[...]
        for k in range(cw.shape[0]):
            acc = acc + tap(seq, k - conv_left) * cw[k:k + 1, :]
        xcs.append(acc)
    xc = jnp.concatenate(xcs, axis=0) if nseq > 1 else xcs[0]

    gb = gb_ref[...]
    lam = lam_ref[...]
    gc = gw_ref.shape[1]
    for ch in range(cw_cols // gc):
        cols = slice(ch * gc, (ch + 1) * gc)
        xcc = xc[:, cols]
        gl = jnp.dot(xcc.astype(BF16), gw_ref[ch], preferred_element_type=F32)
        for d, (a_ref, u_ref) in enumerate(((af_ref, uf_ref), (ab_ref, ub_ref))):
            tr = jnp.tanh(gl[:, (2 * d) * gc:(2 * d + 1) * gc] + 0.5 * gb[2 * d:2 * d + 1, cols])
            ig = 0.5 * jnp.tanh(gl[:, (2 * d + 1) * gc:(2 * d + 2) * gc]
                                + 0.5 * gb[2 * d + 1:2 * d + 2, cols]) + 0.5
            lm = lam[d:d + 1, cols]
            log_sig = jnp.minimum(lm, 0.0) - jnp.log1p(jnp.exp(-jnp.abs(lm)))
            c2 = (0.5 * LRU_C * LOG2_E) * log_sig
            a = jnp.exp2(c2 * tr + c2)
            v = 1.0 - a * a
            u = jnp.where(v > 0.0, v * lax.rsqrt(v), 0.0) * (ig * xcc)
            a_ref[:, cols] = a
            u_ref[:, cols] = u

    zero = jnp.zeros((SUBLANES, cw_cols), F32)
    one = jnp.ones((SUBLANES, cw_cols), F32)

    def local_scan(j, carry):
        out = []
        for sq in range(nseq):
            for dd, (a_ref, u_ref) in enumerate(((af_ref, uf_ref), (ab_ref, ub_ref))):
                grp = j if dd == 0 else segs - 1 - j
                rows = pl.ds(pl.multiple_of(sq * seq_len + grp * SUBLANES, SUBLANES), SUBLANES)
                hl, pc = carry[2 * (2 * sq + dd)], carry[2 * (2 * sq + dd) + 1]
                a = a_ref[rows, :]
                hl = a * hl + u_ref[rows, :]
                pc = a * pc
                u_ref[rows, :] = hl
                a_ref[rows, :] = pc
                out += [hl, pc]
        return tuple(out)

    totals = lax.fori_loop(0, segs, local_scan, (zero, one) * (2 * nseq), unroll=True)

    gelu_y = _gelu_tanh(yb)
    ys = []
    for sq in range(nseq):
        entry = []
        for dd in range(2):
            hl, pc = totals[2 * (2 * sq + dd)], totals[2 * (2 * sq + dd) + 1]
            state = h0_ref[sq, dd:dd + 1, :] if has_h0 else jnp.zeros((1, cw_cols), F32)
            rows = [None] * SUBLANES
            order = range(SUBLANES) if dd == 0 else range(SUBLANES - 1, -1, -1)
            for s in order:
                rows[s] = state
                state = pc[s:s + 1, :] * state + hl[s:s + 1, :]
            entry.append(jnp.concatenate(rows, axis=0))
            if emit_state:
                st_ref[sq, dd:dd + 1, :] = state
        lo, hi = sq * seq_len, (sq + 1) * seq_len
        grouped = (segs, SUBLANES, cw_cols)
        hsum = ((uf_ref[lo:hi, :].reshape(grouped) + af_ref[lo:hi, :].reshape(grouped) * entry[0][None])
                + (ub_ref[lo:hi, :].reshape(grouped) + ab_ref[lo:hi, :].reshape(grouped) * entry[1][None]))
        ys.append(hsum.reshape(seq_len, cw_cols) * gelu_y[lo:hi, :])
    y = jnp.concatenate(ys, axis=0) if nseq > 1 else ys[0]

    y = _per_sequence(pout_ref[...], y.astype(BF16), seq_len)
    acc_ref[...] += _mm(y, wo_ref[...])

    @pl.when(c == pl.num_programs(1) - 1)
    def _():
        o_ref[...] = x_ref[...] + _gate(m_ref, 1) * acc_ref[...]


def _lru_gate_weights(gate_w):
    nd, ng, nb, bw, _ = gate_w.shape
    per = LRU_GATE_CHUNK // bw
    nc = nb // per
    w = (0.5 * gate_w).astype(BF16).reshape(nd * ng, nc, per, bw, bw)
    rows = []
    for n in range(per):
        blk = jnp.transpose(w[:, :, n], (1, 2, 0, 3))
        blk = jnp.pad(blk, ((0, 0), (0, 0), (0, 0), (n * bw, (per - 1 - n) * bw)))
        rows.append(blk.reshape(nc, bw, nd * ng * per * bw))
    return jnp.concatenate(rows, axis=1)


def _lru(x, m, row_of_tile, norm_g, layer, w_in, conv_w, conv_b, gate_w, gate_b, lam, w_out,
         seq_len, h0, emit_state):
    t, d = x.shape
    r = w_out.shape[0]
    tm, cb, gc = TOKEN_TILE, LRU_CHUNK, LRU_GATE_CHUNK
    assert t % tm == 0 and tm % seq_len == 0 and r % cb == 0 and seq_len % SUBLANES == 0
    assert cb % gc == 0
    nc = r // cb
    nseq = tm // seq_len
    has_h0 = h0 is not None
    in_specs = [
        pl.BlockSpec((tm, d), lambda i, c: (i, 0)),
        pl.BlockSpec((None, None, 3 * N_SUB, d), lambda i, c: (layer, row_of_tile(i), 0, 0)),
        pl.BlockSpec((None, N_SUB, d), lambda i, c: (layer, 0, 0)),
        pl.BlockSpec((d, cb), lambda i, c: (0, c)),
        pl.BlockSpec((d, cb), lambda i, c: (0, nc + c)),
        pl.BlockSpec((conv_w.shape[0], cb), lambda i, c: (0, c)),
        pl.BlockSpec((1, cb), lambda i, c: (0, c)),
        pl.BlockSpec((cb // gc, gc, 4 * gc), lambda i, c: (c, 0, 0)),
        pl.BlockSpec((4, cb), lambda i, c: (0, c)),
        pl.BlockSpec((2, cb), lambda i, c: (0, c)),
        pl.BlockSpec((cb, d), lambda i, c: (c, 0)),
        pl.BlockSpec((seq_len, seq_len), lambda i, c: (0, 0), pipeline_mode=pl.Buffered(1)),
        pl.BlockSpec((seq_len, seq_len), lambda i, c: (0, 0), pipeline_mode=pl.Buffered(1)),
    ]
    perm = _time_permutation(seq_len)
    args = [x, m, norm_g, w_in, w_in, conv_w, conv_b.reshape(1, r), _lru_gate_weights(gate_w),
            gate_b.reshape(4, r), lam, w_out, jnp.asarray(perm, BF16), jnp.asarray(perm.T, BF16)]
    if has_h0:
        in_specs.append(pl.BlockSpec((nseq, 2, cb), lambda i, c: (i, 0, c)))
        args.append(h0)
    out_specs = [pl.BlockSpec((tm, d), lambda i, c: (i, 0))]
    out_shape = [jax.ShapeDtypeStruct((t, d), F32)]
    if emit_state:
        out_specs.append(pl.BlockSpec((nseq, 2, cb), lambda i, c: (i, 0, c)))
        out_shape.append(jax.ShapeDtypeStruct((t // seq_len, 2, r), F32))
    outs = pl.pallas_call(
        functools.partial(_lru_kernel, seq_len=seq_len, has_h0=has_h0, emit_state=emit_state,
                          conv_left=(conv_w.shape[0] - 1) // 2),
        grid=(t // tm, nc),
        in_specs=in_specs,
        out_specs=out_specs,
        out_shape=out_shape,
        scratch_shapes=[pltpu.VMEM((tm, d), BF16), pltpu.VMEM((tm, d), F32)]
        + [pltpu.VMEM((tm, cb), F32)] * 4,
        compiler_params=_params("arbitrary", "arbitrary"),
        name="rglru",
    )(*args)
    return outs if emit_state else (outs[0], None)


def _rope(x, cos, sin_signed, lane):
    hd = x.shape[1]
    partner = jnp.where((lane & 32) == 0, pltpu.roll(x, hd - 32, 1), pltpu.roll(x, 32, 1))
    return x * cos + partner * sin_signed


def _attn_kernel(*refs, seq_len, q_block, past_len, rope, emit_kv, group, kvh, fused):
    refs = list(refs)
    x_ref, m_ref, g_ref, wq_ref, wk_ref, wv_ref, qg_ref, kg_ref, wo_ref = refs[:9]
    pos = 9
    if rope:
        cos_ref, sin_ref = refs[pos:pos + 2]
        pos += 2
    if past_len:
        ck_ref, cv_ref = refs[pos:pos + 2]
        pos += 2
    o_ref = refs[pos]
    pos += 1
    if emit_kv:
        kn_ref, vn_ref = refs[pos:pos + 2]
        pos += 2
    if fused:
        q_s, k_s, v_s, o_s = refs[pos:]
    else:
        h_ref, acc_ref, q_s, k_s, v_s, o_s = refs[pos:]

    tm = x_ref.shape[0]
    hd = k_s.shape[-1]
    gw = group * hd
    nqb = seq_len // q_block
    nchunks = (tm // seq_len) * nqb
    nk = past_len + seq_len
    scale = hd ** -0.5

    def group_out(h, g, slot, cols):
        if fused:
            qs, ks, vs, os_ = q_s.at[slot], k_s.at[slot], v_s.at[slot], o_s.at[slot]
            wq, wk = wq_ref[:, cols * gw:(cols + 1) * gw], wk_ref[:, cols * hd:(cols + 1) * hd]
            wv, wo = wv_ref[:, cols * hd:(cols + 1) * hd], wo_ref[cols * gw:(cols + 1) * gw, :]
        else:
            qs, ks, vs, os_ = q_s, k_s, v_s, o_s
            wq, wk, wv, wo = wq_ref[...], wk_ref[...], wv_ref[...], wo_ref[...]
        q = _mm(h, wq)
        k = _rms(_mm(h, wk)) * kg_ref[...]
        v = _mm(h, wv)
        if emit_kv:
            kn_ref[pl.ds(g, tm, stride=kvh), :] = k
            vn_ref[pl.ds(g, tm, stride=kvh), :] = v
        if rope:
            lane = lax.broadcasted_iota(jnp.int32, (tm, hd), 1)
            cos, sin = cos_ref[...], sin_ref[...]
            k = _rope(k, cos, sin, lane)
        ks[past_len:past_len + tm, :] = k.astype(BF16)
        vs[past_len:past_len + tm, :] = v.astype(BF16)
        if past_len:
            ks[0:past_len, :] = ck_ref[pl.ds(g, past_len, stride=kvh), :].astype(BF16)
            vs[0:past_len, :] = cv_ref[pl.ds(g, past_len, stride=kvh), :].astype(BF16)
        for j in range(group):
            qh = _rms(q[:, j * hd:(j + 1) * hd]) * qg_ref[...]
            if rope:
                qh = _rope(qh, cos, sin, lane)
            qs[:, j * hd:(j + 1) * hd] = qh.astype(BF16)

        def chunk(ci, carry):
            r0 = pl.multiple_of(ci * q_block, q_block)
            if past_len:
                keys, vals = ks[...], vs[...]
            else:
                koff = pl.multiple_of((ci // nqb) * seq_len, seq_len)
                keys, vals = ks[pl.ds(koff, nk), :], vs[pl.ds(koff, nk), :]
            qc = jnp.concatenate(
                [qs[pl.ds(r0, q_block), j * hd:(j + 1) * hd] for j in range(group)], axis=0)
            raw = lax.dot_general(qc, keys, (((1,), (1,)), ((), ())), preferred_element_type=F32)
            e = jnp.exp2((raw - jnp.max(raw, axis=-1, keepdims=True)) * (scale * LOG2_E))
            inv = 1.0 / jnp.sum(e, axis=-1, keepdims=True)
            oc = jnp.dot(e.astype(BF16), vals, preferred_element_type=F32) * inv
            for j in range(group):
                os_[pl.ds(r0, q_block), j * hd:(j + 1) * hd] = (
                    oc[j * q_block:(j + 1) * q_block, :].astype(BF16))
            return carry

        lax.fori_loop(0, nchunks, chunk, 0, unroll=min(nchunks, 4))
        return _mm(os_[...], wo)

    if fused:
        x = x_ref[...]
        h = _sub_in(x, m_ref, g_ref, 1).astype(BF16)
        acc = group_out(h, 0, 0, 0)
        for g in range(1, kvh):
            acc = acc + group_out(h, g, g, g)
        o_ref[...] = x + _gate(m_ref, 1) * acc
        return

    gi = pl.program_id(1)

    @pl.when(gi == 0)
    def _():
        h_ref[...] = _sub_in(x_ref[...], m_ref, g_ref, 1).astype(BF16)
        acc_ref[...] = jnp.zeros_like(acc_ref)

    out = group_out(h_ref[...], gi, 0, 0)
    acc_ref[...] += out

    @pl.when(gi == pl.num_programs(1) - 1)
    def _():
        o_ref[...] = x_ref[...] + _gate(m_ref, 1) * acc_ref[...]


def _rope_tables(n_tok, hd):
    rows = n_tok // GRID_W
    r_idx = jnp.broadcast_to(jnp.arange(rows)[:, None], (rows, GRID_W)).reshape(n_tok).astype(F32)
    c_idx = jnp.broadcast_to(jnp.arange(GRID_W)[None, :], (rows, GRID_W)).reshape(n_tok).astype(F32)
    n_freq = hd // 4
    inv = ROPE_THETA ** (-jnp.arange(n_freq, dtype=F32) / n_freq)
    ang = jnp.stack([r_idx[:, None] * inv, c_idx[:, None] * inv], axis=1)
    cos, sin = jnp.cos(ang), jnp.sin(ang)
    cos_full = jnp.concatenate([cos, cos], axis=-1).reshape(n_tok, hd)
    sin_signed = jnp.concatenate([-sin, sin], axis=-1).reshape(n_tok, hd)
    return cos_full, sin_signed


def _attn(x, m, row_of_tile, norm_g, layer, w_qkv, q_g, k_g, w_o, seq_len, q_block,
          cache_k=None, cache_v=None, rope=False, emit_kv=False, fused=False):
    t, d = x.shape
    hd, kvh = HEAD_DIM, N_KV_HEADS
    n_heads = w_o.shape[0] // hd
    group = n_heads // kvh
    gw = group * hd
    tm = TOKEN_TILE
    assert t % tm == 0 and tm % seq_len == 0 and seq_len % q_block == 0
    past_len = 0 if cache_k is None else cache_k.shape[1] // kvh
    assert past_len == 0 or tm == seq_len
    gs = kvh if fused else 1
    lead = (kvh,) if fused else ()
    in_specs = [
        pl.BlockSpec((tm, d), lambda i, g: (i, 0)),
        pl.BlockSpec((None, None, 3 * N_SUB, d), lambda i, g: (layer, row_of_tile(i), 0, 0)),
        pl.BlockSpec((None, N_SUB, d), lambda i, g: (layer, 0, 0)),
        pl.BlockSpec((d, gs * gw), lambda i, g: (0, g)),
        pl.BlockSpec((d, gs * hd), lambda i, g: (0, n_heads // gs + g)),
        pl.BlockSpec((d, gs * hd), lambda i, g: (0, (n_heads + kvh) // gs + g)),
        pl.BlockSpec((1, hd), lambda i, g: (0, 0)),
        pl.BlockSpec((1, hd), lambda i, g: (0, 0)),
        pl.BlockSpec((gs * gw, d), lambda i, g: (g, 0)),
    ]
    args = [x, m, norm_g, w_qkv, w_qkv, w_qkv, q_g.reshape(1, hd), k_g.reshape(1, hd), w_o]
    if rope:
        assert tm == seq_len
        cos, sin = _rope_tables(seq_len, hd)
        in_specs += [pl.BlockSpec((tm, hd), lambda i, g: (0, 0))] * 2
        args += [cos, sin]
    if past_len:
        in_specs += [pl.BlockSpec((None, past_len * kvh, hd), lambda i, g: (i, 0, 0))] * 2
        args += [cache_k, cache_v]
    out_specs = [pl.BlockSpec((tm, d), lambda i, g: (i, 0))]
    out_shape = [jax.ShapeDtypeStruct((t, d), F32)]
    if emit_kv:
        out_specs += [pl.BlockSpec((tm * kvh, hd), lambda i, g: (i, 0))] * 2
        out_shape += [jax.ShapeDtypeStruct((t * kvh, hd), F32)] * 2
    nkeys = past_len + tm
    outs = pl.pallas_call(
        functools.partial(_attn_kernel, seq_len=seq_len, q_block=q_block, past_len=past_len,
                          rope=rope, emit_kv=emit_kv, group=group, kvh=kvh, fused=fused),
        grid=(t // tm, kvh // gs),
        in_specs=in_specs,
        out_specs=out_specs,
        out_shape=out_shape,
        scratch_shapes=([] if fused else [pltpu.VMEM((tm, d), BF16), pltpu.VMEM((tm, d), F32)])
        + [pltpu.VMEM(lead + (tm, gw), BF16), pltpu.VMEM(lead + (nkeys, hd), BF16),
           pltpu.VMEM(lead + (nkeys, hd), BF16), pltpu.VMEM(lead + (tm, gw), BF16)],
        compiler_params=_params("arbitrary", "arbitrary"),
        name="gqa",
    )(*args)
    return outs


def kernel(x_prompt, x_sample, c, state_lru, cache_k, cache_v, c_ctx, mod_w, mod_b, norm_g,
           ffn_w_gu, ffn_w_down, lru_w_in, lru_conv_w, lru_conv_b, lru_gate_w, lru_gate_b,
           lru_lambda, lru_w_out, att_w_qkv, att_q_g, att_k_g, att_w_o, final_g):
    b, s, d = x_prompt.shape
    db, ds, _ = x_sample.shape
    depth = mod_w.shape[0]
    n_mixers = 2
    assert ds % TOKEN_TILE == 0 and 1 + db <= SUBLANES

    xp = x_prompt.reshape(b * s, d)
    xs = x_sample.reshape(db * ds, d)
    cvecs = jnp.concatenate([c_ctx[None], c, jnp.zeros((SUBLANES - 1 - db, d), F32)], axis=0)
    m = _modulation(cvecs, mod_w, mod_b).reshape(depth, SUBLANES, 3 * N_SUB, d)

    tiles_per_sample = ds // TOKEN_TILE
    prompt_row = lambda i: 0
    sample_row = lambda i: 1 + i // tiles_per_sample
    prompt_tiles = (b * s) // TOKEN_TILE
    both_row = lambda i: jnp.where(i < prompt_tiles, 0, 1 + (i - prompt_tiles) // tiles_per_sample)

    new_states, new_k, new_v = [], [], []
    for layer in range(depth):
        j = layer // n_mixers
        last = layer == depth - 1
        xp, xs = _ffn(xp, xs, m, both_row, norm_g, ffn_w_gu, ffn_w_down, layer, 0, 0)
        if layer % n_mixers == 0:
            lru_p = (lru_w_in[j], lru_conv_w[j], lru_conv_b[j], lru_gate_w[j], lru_gate_b[j],
                     lru_lambda[j], lru_w_out[j])
            xp, st = _lru(xp, m, prompt_row, norm_g, layer, *lru_p, seq_len=s, h0=None,
                          emit_state=True)
            new_states.append(st)
            xs, _ = _lru(xs, m, sample_row, norm_g, layer, *lru_p, seq_len=ds,
                         h0=state_lru[:, j], emit_state=False)
        else:
            att_p = (att_w_qkv[j], att_q_g[j], att_k_g[j], att_w_o[j])
            xp, kp, vp = _attn(xp, m, prompt_row, norm_g, layer, *att_p, seq_len=s, q_block=s,
                               emit_kv=True, fused=True)
            new_k.append(kp.reshape(b, s, N_KV_HEADS, HEAD_DIM))
            new_v.append(vp.reshape(b, s, N_KV_HEADS, HEAD_DIM))
            past = cache_k.shape[2]
            ck = cache_k[:, j].reshape(db, past * N_KV_HEADS, HEAD_DIM)
            cv = cache_v[:, j].reshape(db, past * N_KV_HEADS, HEAD_DIM)
            (xs,) = _attn(xs, m, sample_row, norm_g, layer, *att_p, seq_len=ds, q_block=128,
                          cache_k=ck, cache_v=cv, rope=True)
        fg = final_g if last else None
        xp, xs = _ffn(xp, xs, m, both_row, norm_g, ffn_w_gu, ffn_w_down, layer, 1, 2, final_g=fg)

    y_prompt = xp.reshape(b, s, d)
    y_sample = xs.reshape(db, ds, d)
    return (y_prompt, y_sample, jnp.stack(new_states, axis=1), jnp.stack(new_k, axis=1),
            jnp.stack(new_v, axis=1))
```

```python
import functools

import jax
import jax.numpy as jnp
import numpy as np
from jax import lax
from jax.experimental import pallas as pl
from jax.experimental.pallas import tpu as pltpu

F32 = jnp.float32
BF16 = jnp.bfloat16

EPS = 1e-6
LRU_C = 8.0
LOG2_E = 1.4426950408889634
GRID_W = 64
ROPE_THETA = 10000.0
N_SUB = 3
HEAD_DIM = 128
N_KV_HEADS = 2
LRU_BLOCKS = 16

V7X_VMEM_LIMIT_BYTES = 56 * 1024 * 1024
SUBLANES = 8
TOKEN_TILE = 1024
FF_CHUNK = 256
FFN_ROW_BLOCK = 256
FFN_STAGE_SLOTS = 3
LRU_CHUNK = 512
LRU_GATE_CHUNK = 256
MOD_CHUNK = 1536


def _params(*semantics):
    return pltpu.CompilerParams(dimension_semantics=semantics,
                                vmem_limit_bytes=V7X_VMEM_LIMIT_BYTES)


def _mm(a_bf16, w_f32):
    return jnp.dot(a_bf16, w_f32.astype(BF16), preferred_element_type=F32)


def _rms(x):
    return x * lax.rsqrt(jnp.mean(x * x, axis=-1, keepdims=True) + EPS)


def _sub_in(x, m_ref, g_ref, sidx):
    shift = m_ref[3 * sidx:3 * sidx + 1, :]
    scale = m_ref[3 * sidx + 1:3 * sidx + 2, :]
    return (_rms(x) * g_ref[sidx:sidx + 1, :]) * (1.0 + scale) + shift


def _gate(m_ref, sidx):
    return m_ref[3 * sidx + 2:3 * sidx + 3, :]


def _mod_kernel(cv_ref, w_ref, b_ref, o_ref):
    cv = cv_ref[...]
    act = (cv * jax.nn.sigmoid(cv)).astype(BF16)
    o_ref[...] = _mm(act, w_ref[...]) + b_ref[...]


def _modulation(cvecs, mod_w, mod_b):
    depth, d, n = mod_w.shape
    rows = cvecs.shape[0]
    tn = MOD_CHUNK
    assert n % tn == 0
    return pl.pallas_call(
        _mod_kernel,
        grid=(depth, n // tn),
        in_specs=[
            pl.BlockSpec((rows, d), lambda l, j: (0, 0)),
            pl.BlockSpec((None, d, tn), lambda l, j: (l, 0, j)),
            pl.BlockSpec((None, 1, tn), lambda l, j: (l, 0, j)),
        ],
        out_specs=pl.BlockSpec((None, rows, tn), lambda l, j: (l, 0, j)),
        out_shape=jax.ShapeDtypeStruct((depth, rows, n), F32),
        compiler_params=_params("arbitrary", "arbitrary"),
        name="modulation",
    )(cvecs, mod_w, mod_b.reshape(depth, 1, n))


def _ffn_kernel(*refs, layer, s, sidx, final, n_first):
    if final:
        (m_ref, g_ref, fg_ref, xa_hbm, xb_hbm, wgu_hbm, wd_hbm, oa_hbm, ob_hbm,
         wgu_res, wd_res, act_ref, h0_ref, act0_ref, gu_buf, wd_buf, xbuf, obuf,
         w_sem, xsem, osem) = refs
    else:
        (m_ref, g_ref, xa_hbm, xb_hbm, wgu_hbm, wd_hbm, oa_hbm, ob_hbm,
         wgu_res, wd_res, act_ref, h0_ref, act0_ref, gu_buf, wd_buf, xbuf, obuf,
         w_sem, xsem, osem) = refs
    i = pl.program_id(0)
    n = pl.num_programs(0)
    tm = xbuf.shape[1]
    _, d, tf = wgu_res.shape
    f = wd_res.shape[0]
    nk = f // tf
    rb = act_ref.shape[0]
    nslot = gu_buf.shape[0]
    ahead = nslot - 1
    slot = lax.rem(i, 2)
    other = 1 - slot

    def stage_copies(k, sl):
        gcols = pl.ds(pl.multiple_of(k * tf, tf), tf)
        ucols = pl.ds(pl.multiple_of(f + k * tf, tf), tf)
        return (
            pltpu.make_async_copy(wgu_hbm.at[layer, s, :, gcols], gu_buf.at[sl, 0], w_sem.at[sl, 0]),
            pltpu.make_async_copy(wgu_hbm.at[layer, s, :, ucols], gu_buf.at[sl, 1], w_sem.at[sl, 1]),
            pltpu.make_async_copy(wd_hbm.at[layer, s, gcols, :], wd_buf.at[sl], w_sem.at[sl, 2]),
        )

    def tile_rows(hbm, tile):
        return hbm.at[pl.ds(pl.multiple_of(tile * tm, tm), tm), :]

    def x_copy(which, tile, sl):
        return pltpu.make_async_copy(tile_rows((xa_hbm, xb_hbm)[which], tile), xbuf.at[sl],
                                     xsem.at[sl])

    def o_copy(which, tile, sl):
        return pltpu.make_async_copy(obuf.at[sl], tile_rows((oa_hbm, ob_hbm)[which], tile),
                                     osem.at[sl])

    def start_by_stream(make, tile, sl):
        @pl.when(tile < n_first)
        def _():
            make(0, tile, sl).start()

        @pl.when(tile >= n_first)
        def _():
            make(1, tile - n_first, sl).start()

    @pl.when(i == 0)
    def _():
        x_copy(0, 0, 0).start()
        for k in range(ahead):
            for cp in stage_copies(k, k):
                cp.start()

    @pl.when(i + 1 < n)
    def _():
        start_by_stream(x_copy, i + 1, other)

    x_copy(0, 0, slot).wait()

    @pl.when(i >= 2)
    def _():
        o_copy(0, 0, slot).wait()

    def swiglu(h, wg, wu):
        gt = jnp.dot(h, wg, preferred_element_type=F32)
        up = jnp.dot(h, wu, preferred_element_type=F32)
        return ((gt * jax.nn.sigmoid(gt)) * up).astype(BF16)

    def down_and_store(x, rows):
        ff = jnp.dot(act_ref[...], wd_res[...], preferred_element_type=F32)
        y = x + (0.5 * _gate(m_ref, sidx)) * ff
        if final:
            y = _rms(y) * fg_ref[...]
        obuf[slot, rows, :] = y

    def row_block(r, carry):
        rows = pl.ds(pl.multiple_of(r * rb, rb), rb)
        x = xbuf[slot, rows, :]
        h = _sub_in(x, m_ref, g_ref, sidx).astype(BF16)
        for k in range(nk):
            act_ref[:, k * tf:(k + 1) * tf] = swiglu(h, wgu_res[k], wgu_res[nk + k])
        down_and_store(x, rows)
        return carry

    def first_tile_chunk(k, carry):
        sl = lax.rem(k, nslot)
        for cp in stage_copies(k, sl):
            cp.wait()

        @pl.when(k + ahead < nk)
        def _():
            for cp in stage_copies(k + ahead, lax.rem(k + ahead, nslot)):
                cp.start()

        wg = gu_buf[sl, 0].astype(BF16)
        wu = gu_buf[sl, 1].astype(BF16)
        wgu_res[k] = wg
        wgu_res[nk + k] = wu
        wd_res[pl.ds(pl.multiple_of(k * tf, tf), tf), :] = wd_buf[sl].astype(BF16)
        act0_ref[k] = swiglu(h0_ref[...], wg, wu)
        return carry

    def first_tile_down(r, carry):
        rows = pl.ds(pl.multiple_of(r * rb, rb), rb)
        for k in range(nk):
            act_ref[:, k * tf:(k + 1) * tf] = act0_ref[k, rows, :]
        down_and_store(xbuf[slot, rows, :], rows)
        return carry

    @pl.when(i == 0)
    def _():
        h0_ref[...] = _sub_in(xbuf[slot], m_ref, g_ref, sidx).astype(BF16)
        lax.fori_loop(0, nk, first_tile_chunk, 0)
        lax.fori_loop(0, tm // rb, first_tile_down, 0, unroll=2)

    @pl.when(i > 0)
    def _():
        lax.fori_loop(0, tm // rb, row_block, 0, unroll=2)

    start_by_stream(o_copy, i, slot)

    @pl.when(i == n - 1)
    def _():
        o_copy(0, 0, other).wait()
        o_copy(0, 0, slot).wait()


def _ffn(xa, xb, m, row_of_tile, norm_g, w_gu, w_down, layer, s, sidx, final_g=None):
    (ta, d), tb = xa.shape, xb.shape[0]
    f = w_down.shape[2]
    tm, tf, rb = TOKEN_TILE, FF_CHUNK, FFN_ROW_BLOCK
    assert ta % tm == 0 and tb % tm == 0 and f % tf == 0 and (ta + tb) // tm >= 2
    assert tm % rb == 0 and f // tf >= FFN_STAGE_SLOTS
    final = final_g is not None
    in_specs = [
        pl.BlockSpec((None, None, 3 * N_SUB, d), lambda i: (layer, row_of_tile(i), 0, 0)),
        pl.BlockSpec((None, N_SUB, d), lambda i: (layer, 0, 0)),
    ]
    args = [m, norm_g]
    if final:
        in_specs.append(pl.BlockSpec((1, d), lambda i: (0, 0)))
        args.append(final_g.reshape(1, d))
    in_specs += [pl.BlockSpec(memory_space=pl.ANY)] * 4
    args += [xa, xb, w_gu, w_down]
    return pl.pallas_call(
        functools.partial(_ffn_kernel, layer=layer, s=s, sidx=sidx, final=final,
                          n_first=ta // tm),
        grid=((ta + tb) // tm,),
        in_specs=in_specs,
        out_specs=[pl.BlockSpec(memory_space=pl.ANY)] * 2,
        out_shape=[jax.ShapeDtypeStruct((ta, d), F32), jax.ShapeDtypeStruct((tb, d), F32)],
        scratch_shapes=[
            pltpu.VMEM((2 * f // tf, d, tf), BF16), pltpu.VMEM((f, d), BF16),
            pltpu.VMEM((rb, f), BF16),
            pltpu.VMEM((tm, d), BF16), pltpu.VMEM((f // tf, tm, tf), BF16),
            pltpu.VMEM((FFN_STAGE_SLOTS, 2, d, tf), F32),
            pltpu.VMEM((FFN_STAGE_SLOTS, tf, d), F32),
            pltpu.VMEM((2, tm, d), F32), pltpu.VMEM((2, tm, d), F32),
            pltpu.SemaphoreType.DMA((FFN_STAGE_SLOTS, 3)),
            pltpu.SemaphoreType.DMA((2,)),
            pltpu.SemaphoreType.DMA((2,)),
        ],
        compiler_params=_params("arbitrary"),
        name="ffn",
    )(*args)


def _gelu_tanh(x):
    c = np.float32(np.sqrt(2.0 / np.pi))
    return x * (0.5 * (1.0 + jnp.tanh(c * (x + 0.044715 * (x * x * x)))))


def _time_permutation(seq_len):
    segs = seq_len // SUBLANES
    p = np.arange(seq_len)
    j, s = p // SUBLANES, p % SUBLANES
    mat = np.zeros((seq_len, seq_len), np.float32)
    mat[p, s * segs + j] = 1.0
    return mat


def _per_sequence(mat, rows, seq_len):
    parts = [jnp.dot(mat, rows[lo:lo + seq_len, :], preferred_element_type=F32).astype(BF16)
             for lo in range(0, rows.shape[0], seq_len)]
    return jnp.concatenate(parts, axis=0) if len(parts) > 1 else parts[0]


def _lru_kernel(*refs, seq_len, has_h0, emit_state, conv_left, single_step):
    refs = list(refs)
    (x_ref, m_ref, g_ref, wx_ref, wy_ref, cw_ref, cb_ref, gw_ref, gb_ref, lam_ref,
     wo_ref, pin_ref, pout_ref) = refs[:13]
    pos = 13
    h0_ref = None
    if has_h0:
        h0_ref = refs[pos]
        pos += 1
    o_ref = refs[pos]
    pos += 1
    st_ref = None
    if emit_state:
        st_ref = refs[pos]
        pos += 1
    if single_step:
        af_ref, uf_ref, ab_ref, ub_ref = refs[pos:]
    else:
        h_ref, acc_ref, af_ref, uf_ref, ab_ref, ub_ref = refs[pos:]

    tm, cw_cols = af_ref.shape
    nseq = tm // seq_len
    segs = seq_len // SUBLANES

    def permuted_input():
        h = _sub_in(x_ref[...], m_ref, g_ref, 1).astype(BF16)
        return _per_sequence(pin_ref[...], h, seq_len)

    if single_step:
        h = permuted_input()
    else:
        c = pl.program_id(1)

        @pl.when(c == 0)
        def _():
            h_ref[...] = permuted_input()
            acc_ref[...] = jnp.zeros_like(acc_ref)

        h = h_ref[...]
    xb = _mm(h, wx_ref[...])
    yb = _mm(h, wy_ref[...])

    sub = lax.broadcasted_iota(jnp.int32, (SUBLANES, cw_cols), 0)

    def next_segment(grp):
        return jnp.where(sub < SUBLANES - 1, pltpu.roll(grp, SUBLANES - 1, 0), 0.0)

    def prev_segment(grp):
        return jnp.where(sub > 0, pltpu.roll(grp, 1, 0), 0.0)

    def tap(seq, off):
        n = abs(off) * SUBLANES
        if off == 0:
            return seq
        if off > 0:
            edge = [next_segment(seq[g * SUBLANES:(g + 1) * SUBLANES, :]) for g in range(off)]
            return jnp.concatenate([seq[n:, :]] + edge, axis=0)
        start = seq_len - n
        edge = [prev_segment(seq[start + g * SUBLANES:start + (g + 1) * SUBLANES, :])
                for g in range(-off)]
        return jnp.concatenate(edge + [seq[:start, :]], axis=0)

    cw = cw_ref[...]
    assert cw.shape[0] - 1 < segs
    xcs = []
    for sq in range(nseq):
        seq = xb[sq * seq_len:(sq + 1) * seq_len, :]
        acc = cb_ref[...]
        for k in range(cw.shape[0]):
            acc = acc + tap(seq, k - conv_left) * cw[k:k + 1, :]
        xcs.append(acc)
    xc = jnp.concatenate(xcs, axis=0) if nseq > 1 else xcs[0]

    gb = gb_ref[...]
    lam = lam_ref[...]
    gc = gw_ref.shape[1]
    for ch in range(cw_cols // gc):
        cols = slice(ch * gc, (ch + 1) * gc)
        xcc = xc[:, cols]
        gl = jnp.dot(xcc.astype(BF16), gw_ref[ch], preferred_element_type=F32)
        for d, (a_ref, u_ref) in enumerate(((af_ref, uf_ref), (ab_ref, ub_ref))):
            tr = jnp.tanh(gl[:, (2 * d) * gc:(2 * d + 1) * gc] + 0.5 * gb[2 * d:2 * d + 1, cols])
            ig = 0.5 * jnp.tanh(gl[:, (2 * d + 1) * gc:(2 * d + 2) * gc]
                                + 0.5 * gb[2 * d + 1:2 * d + 2, cols]) + 0.5
            lm = lam[d:d + 1, cols]
            log_sig = jnp.minimum(lm, 0.0) - jnp.log1p(jnp.exp(-jnp.abs(lm)))
            c2 = (0.5 * LRU_C * LOG2_E) * log_sig
            a = jnp.exp2(c2 * tr + c2)
            v = 1.0 - a * a
            u = jnp.where(v > 0.0, v * lax.rsqrt(v), 0.0) * (ig * xcc)
            a_ref[:, cols] = a
            u_ref[:, cols] = u

    zero = jnp.zeros((SUBLANES, cw_cols), F32)
    one = jnp.ones((SUBLANES, cw_cols), F32)

    def local_scan(j, carry):
        out = []
        for sq in range(nseq):
            for dd, (a_ref, u_ref) in enumerate(((af_ref, uf_ref), (ab_ref, ub_ref))):
                grp = j if dd == 0 else segs - 1 - j
                rows = pl.ds(pl.multiple_of(sq * seq_len + grp * SUBLANES, SUBLANES), SUBLANES)
                hl, pc = carry[2 * (2 * sq + dd)], carry[2 * (2 * sq + dd) + 1]
                a = a_ref[rows, :]
                hl = a * hl + u_ref[rows, :]
                pc = a * pc
                u_ref[rows, :] = hl
                a_ref[rows, :] = pc
                out += [hl, pc]
        return tuple(out)

    totals = lax.fori_loop(0, segs, local_scan, (zero, one) * (2 * nseq), unroll=True)

    gelu_y = _gelu_tanh(yb)
    ys = []
    for sq in range(nseq):
        entry = []
        for dd in range(2):
            hl, pc = totals[2 * (2 * sq + dd)], totals[2 * (2 * sq + dd) + 1]
            state = h0_ref[sq, dd:dd + 1, :] if has_h0 else jnp.zeros((1, cw_cols), F32)
            rows = [None] * SUBLANES
            order = range(SUBLANES) if dd == 0 else range(SUBLANES - 1, -1, -1)
            for s in order:
                rows[s] = state
                state = pc[s:s + 1, :] * state + hl[s:s + 1, :]
            entry.append(jnp.concatenate(rows, axis=0))
            if emit_state:
                st_ref[sq, dd:dd + 1, :] = state
        lo, hi = sq * seq_len, (sq + 1) * seq_len
        grouped = (segs, SUBLANES, cw_cols)
        hsum = ((uf_ref[lo:hi, :].reshape(grouped) + af_ref[lo:hi, :].reshape(grouped) * entry[0][None])
                + (ub_ref[lo:hi, :].reshape(grouped) + ab_ref[lo:hi, :].reshape(grouped) * entry[1][None]))
        ys.append(hsum.reshape(seq_len, cw_cols) * gelu_y[lo:hi, :])
    y = jnp.concatenate(ys, axis=0) if nseq > 1 else ys[0]

    y = _per_sequence(pout_ref[...], y.astype(BF16), seq_len)
    out = _mm(y, wo_ref[...])
    if single_step:
        o_ref[...] = x_ref[...] + _gate(m_ref, 1) * out
        return
    acc_ref[...] += out

    @pl.when(c == pl.num_programs(1) - 1)
    def _():
        o_ref[...] = x_ref[...] + _gate(m_ref, 1) * acc_ref[...]


def _lru_gate_weights(gate_w):
    nd, ng, nb, bw, _ = gate_w.shape
    per = LRU_GATE_CHUNK // bw
    nc = nb // per
    w = (0.5 * gate_w).astype(BF16).reshape(nd * ng, nc, per, bw, bw)
    rows = []
    for n in range(per):
        blk = jnp.transpose(w[:, :, n], (1, 2, 0, 3))
        blk = jnp.pad(blk, ((0, 0), (0, 0), (0, 0), (n * bw, (per - 1 - n) * bw)))
        rows.append(blk.reshape(nc, bw, nd * ng * per * bw))
    return jnp.concatenate(rows, axis=1)


def _lru(x, m, row_of_tile, norm_g, layer, w_in, conv_w, conv_b, gate_w, gate_b, lam, w_out,
         seq_len, h0, emit_state, single_step=False):
    t, d = x.shape
    r = w_out.shape[0]
    tm, gc = TOKEN_TILE, LRU_GATE_CHUNK
    cb = r if single_step else LRU_CHUNK
    assert t % tm == 0 and tm % seq_len == 0 and r % cb == 0 and seq_len % SUBLANES == 0
    assert cb % gc == 0
    nc = r // cb
    nseq = tm // seq_len
    has_h0 = h0 is not None
    resident = dict(pipeline_mode=pl.Buffered(1)) if single_step else {}
    in_specs = [
        pl.BlockSpec((tm, d), lambda i, c: (i, 0)),
        pl.BlockSpec((None, None, 3 * N_SUB, d), lambda i, c: (layer, row_of_tile(i), 0, 0)),
        pl.BlockSpec((None, N_SUB, d), lambda i, c: (layer, 0, 0)),
        pl.BlockSpec((d, cb), lambda i, c: (0, c), **resident),
        pl.BlockSpec((d, cb), lambda i, c: (0, nc + c), **resident),
        pl.BlockSpec((conv_w.shape[0], cb), lambda i, c: (0, c)),
        pl.BlockSpec((1, cb), lambda i, c: (0, c)),
        pl.BlockSpec((cb // gc, gc, 4 * gc), lambda i, c: (c, 0, 0), **resident),
        pl.BlockSpec((4, cb), lambda i, c: (0, c)),
        pl.BlockSpec((2, cb), lambda i, c: (0, c)),
        pl.BlockSpec((cb, d), lambda i, c: (c, 0), **resident),
        pl.BlockSpec((seq_len, seq_len), lambda i, c: (0, 0), pipeline_mode=pl.Buffered(1)),
        pl.BlockSpec((seq_len, seq_len), lambda i, c: (0, 0), pipeline_mode=pl.Buffered(1)),
    ]
    perm = _time_permutation(seq_len)
    args = [x, m, norm_g, w_in, w_in, conv_w, conv_b.reshape(1, r), _lru_gate_weights(gate_w),
            gate_b.reshape(4, r), lam, w_out, jnp.asarray(perm, BF16), jnp.asarray(perm.T, BF16)]
    if has_h0:
        in_specs.append(pl.BlockSpec((nseq, 2, cb), lambda i, c: (i, 0, c)))
        args.append(h0)
    out_specs = [pl.BlockSpec((tm, d), lambda i, c: (i, 0))]
    out_shape = [jax.ShapeDtypeStruct((t, d), F32)]
    if emit_state:
        out_specs.append(pl.BlockSpec((nseq, 2, cb), lambda i, c: (i, 0, c)))
        out_shape.append(jax.ShapeDtypeStruct((t // seq_len, 2, r), F32))
    outs = pl.pallas_call(
        functools.partial(_lru_kernel, seq_len=seq_len, has_h0=has_h0, emit_state=emit_state,
                          conv_left=(conv_w.shape[0] - 1) // 2, single_step=single_step),
        grid=(t // tm, nc),
        in_specs=in_specs,
        out_specs=out_specs,
        out_shape=out_shape,
        scratch_shapes=([] if single_step
                        else [pltpu.VMEM((tm, d), BF16), pltpu.VMEM((tm, d), F32)])
        + [pltpu.VMEM((tm, cb), F32)] * 4,
        compiler_params=_params("arbitrary", "arbitrary"),
        name="rglru",
    )(*args)
    return outs if emit_state else (outs[0], None)


def _rope(x, cos, sin_signed, lane):
    hd = x.shape[1]
    partner = jnp.where((lane & 32) == 0, pltpu.roll(x, hd - 32, 1), pltpu.roll(x, 32, 1))
    return x * cos + partner * sin_signed


def _attn_kernel(*refs, seq_len, q_block, past_len, rope, emit_kv, group, kvh, fused):
    refs = list(refs)
    x_ref, m_ref, g_ref, wq_ref, wk_ref, wv_ref, qg_ref, kg_ref, wo_ref = refs[:9]
    pos = 9
    if rope:
        cos_ref, sin_ref = refs[pos:pos + 2]
        pos += 2
    if past_len:
        ck_ref, cv_ref = refs[pos:pos + 2]
        pos += 2
    o_ref = refs[pos]
    pos += 1
    if emit_kv:
        kn_ref, vn_ref = refs[pos:pos + 2]
        pos += 2
    if fused:
        q_s, k_s, v_s, o_s = refs[pos:]
    else:
        h_ref, acc_ref, q_s, k_s, v_s, o_s = refs[pos:]

    tm = x_ref.shape[0]
    hd = k_s.shape[-1]
    gw = group * hd
    nqb = seq_len // q_block
    nchunks = (tm // seq_len) * nqb
    nk = past_len + seq_len
    scale = hd ** -0.5

    def group_out(h, g, slot, cols):
        if fused:
            qs, ks, vs, os_ = q_s.at[slot], k_s.at[slot], v_s.at[slot], o_s.at[slot]
            wq, wk = wq_ref[:, cols * gw:(cols + 1) * gw], wk_ref[:, cols * hd:(cols + 1) * hd]
            wv, wo = wv_ref[:, cols * hd:(cols + 1) * hd], wo_ref[cols * gw:(cols + 1) * gw, :]
        else:
            qs, ks, vs, os_ = q_s, k_s, v_s, o_s
            wq, wk, wv, wo = wq_ref[...], wk_ref[...], wv_ref[...], wo_ref[...]
        q = _mm(h, wq)
        k = _rms(_mm(h, wk)) * kg_ref[...]
        v = _mm(h, wv)
        if emit_kv:
            kn_ref[pl.ds(g, tm, stride=kvh), :] = k
            vn_ref[pl.ds(g, tm, stride=kvh), :] = v
        if rope:
            lane = lax.broadcasted_iota(jnp.int32, (tm, hd), 1)
            cos, sin = cos_ref[...], sin_ref[...]
            k = _rope(k, cos, sin, lane)
        ks[past_len:past_len + tm, :] = k.astype(BF16)
        vs[past_len:past_len + tm, :] = v.astype(BF16)
        if past_len:
            ks[0:past_len, :] = ck_ref[pl.ds(g, past_len, stride=kvh), :].astype(BF16)
            vs[0:past_len, :] = cv_ref[pl.ds(g, past_len, stride=kvh), :].astype(BF16)
        for j in range(group):
            qh = _rms(q[:, j * hd:(j + 1) * hd]) * qg_ref[...]
            if rope:
                qh = _rope(qh, cos, sin, lane)
            qs[:, j * hd:(j + 1) * hd] = qh.astype(BF16)

        def chunk(ci, carry):
            r0 = pl.multiple_of(ci * q_block, q_block)
            if past_len:
                keys, vals = ks[...], vs[...]
            else:
                koff = pl.multiple_of((ci // nqb) * seq_len, seq_len)
                keys, vals = ks[pl.ds(koff, nk), :], vs[pl.ds(koff, nk), :]
            qc = jnp.concatenate(
                [qs[pl.ds(r0, q_block), j * hd:(j + 1) * hd] for j in range(group)], axis=0)
            raw = lax.dot_general(qc, keys, (((1,), (1,)), ((), ())), preferred_element_type=F32)
            e = jnp.exp2((raw - jnp.max(raw, axis=-1, keepdims=True)) * (scale * LOG2_E))
            inv = 1.0 / jnp.sum(e, axis=-1, keepdims=True)
            oc = jnp.dot(e.astype(BF16), vals, preferred_element_type=F32) * inv
            for j in range(group):
                os_[pl.ds(r0, q_block), j * hd:(j + 1) * hd] = (
                    oc[j * q_block:(j + 1) * q_block, :].astype(BF16))
            return carry

        lax.fori_loop(0, nchunks, chunk, 0, unroll=min(nchunks, 4))
        return _mm(os_[...], wo)

    if fused:
        x = x_ref[...]
        h = _sub_in(x, m_ref, g_ref, 1).astype(BF16)
        acc = group_out(h, 0, 0, 0)
        for g in range(1, kvh):
            acc = acc + group_out(h, g, g, g)
        o_ref[...] = x + _gate(m_ref, 1) * acc
        return

    gi = pl.program_id(1)

    @pl.when(gi == 0)
    def _():
        h_ref[...] = _sub_in(x_ref[...], m_ref, g_ref, 1).astype(BF16)
        acc_ref[...] = jnp.zeros_like(acc_ref)

    out = group_out(h_ref[...], gi, 0, 0)
    acc_ref[...] += out

    @pl.when(gi == pl.num_programs(1) - 1)
    def _():
        o_ref[...] = x_ref[...] + _gate(m_ref, 1) * acc_ref[...]


def _rope_tables(n_tok, hd):
    rows = n_tok // GRID_W
    r_idx = jnp.broadcast_to(jnp.arange(rows)[:, None], (rows, GRID_W)).reshape(n_tok).astype(F32)
    c_idx = jnp.broadcast_to(jnp.arange(GRID_W)[None, :], (rows, GRID_W)).reshape(n_tok).astype(F32)
    n_freq = hd // 4
    inv = ROPE_THETA ** (-jnp.arange(n_freq, dtype=F32) / n_freq)
    ang = jnp.stack([r_idx[:, None] * inv, c_idx[:, None] * inv], axis=1)
    cos, sin = jnp.cos(ang), jnp.sin(ang)
    cos_full = jnp.concatenate([cos, cos], axis=-1).reshape(n_tok, hd)
    sin_signed = jnp.concatenate([-sin, sin], axis=-1).reshape(n_tok, hd)
    return cos_full, sin_signed


def _attn(x, m, row_of_tile, norm_g, layer, w_qkv, q_g, k_g, w_o, seq_len, q_block,
          cache_k=None, cache_v=None, rope=False, emit_kv=False, fused=False):
    t, d = x.shape
    hd, kvh = HEAD_DIM, N_KV_HEADS
    n_heads = w_o.shape[0] // hd
    group = n_heads // kvh
    gw = group * hd
    tm = TOKEN_TILE
    assert t % tm == 0 and tm % seq_len == 0 and seq_len % q_block == 0
    past_len = 0 if cache_k is None else cache_k.shape[1] // kvh
    assert past_len == 0 or tm == seq_len
    gs = kvh if fused else 1
    lead = (kvh,) if fused else ()
    in_specs = [
        pl.BlockSpec((tm, d), lambda i, g: (i, 0)),
        pl.BlockSpec((None, None, 3 * N_SUB, d), lambda i, g: (layer, row_of_tile(i), 0, 0)),
        pl.BlockSpec((None, N_SUB, d), lambda i, g: (layer, 0, 0)),
        pl.BlockSpec((d, gs * gw), lambda i, g: (0, g)),
        pl.BlockSpec((d, gs * hd), lambda i, g: (0, n_heads // gs + g)),
        pl.BlockSpec((d, gs * hd), lambda i, g: (0, (n_heads + kvh) // gs + g)),
        pl.BlockSpec((1, hd), lambda i, g: (0, 0)),
        pl.BlockSpec((1, hd), lambda i, g: (0, 0)),
        pl.BlockSpec((gs * gw, d), lambda i, g: (g, 0)),
    ]
    args = [x, m, norm_g, w_qkv, w_qkv, w_qkv, q_g.reshape(1, hd), k_g.reshape(1, hd), w_o]
    if rope:
        assert tm == seq_len
        cos, sin = _rope_tables(seq_len, hd)
        in_specs += [pl.BlockSpec((tm, hd), lambda i, g: (0, 0))] * 2
        args += [cos, sin]
    if past_len:
        in_specs += [pl.BlockSpec((None, past_len * kvh, hd), lambda i, g: (i, 0, 0))] * 2
        args += [cache_k, cache_v]
    out_specs = [pl.BlockSpec((tm, d), lambda i, g: (i, 0))]
    out_shape = [jax.ShapeDtypeStruct((t, d), F32)]
    if emit_kv:
        out_specs += [pl.BlockSpec((tm * kvh, hd), lambda i, g: (i, 0))] * 2
        out_shape += [jax.ShapeDtypeStruct((t * kvh, hd), F32)] * 2
    nkeys = past_len + tm
    outs = pl.pallas_call(
        functools.partial(_attn_kernel, seq_len=seq_len, q_block=q_block, past_len=past_len,
                          rope=rope, emit_kv=emit_kv, group=group, kvh=kvh, fused=fused),
        grid=(t // tm, kvh // gs),
        in_specs=in_specs,
        out_specs=out_specs,
        out_shape=out_shape,
        scratch_shapes=([] if fused else [pltpu.VMEM((tm, d), BF16), pltpu.VMEM((tm, d), F32)])
        + [pltpu.VMEM(lead + (tm, gw), BF16), pltpu.VMEM(lead + (nkeys, hd), BF16),
           pltpu.VMEM(lead + (nkeys, hd), BF16), pltpu.VMEM(lead + (tm, gw), BF16)],
        compiler_params=_params("arbitrary", "arbitrary"),
        name="gqa",
    )(*args)
    return outs


def kernel(x_prompt, x_sample, c, state_lru, cache_k, cache_v, c_ctx, mod_w, mod_b, norm_g,
           ffn_w_gu, ffn_w_down, lru_w_in, lru_conv_w, lru_conv_b, lru_gate_w, lru_gate_b,
           lru_lambda, lru_w_out, att_w_qkv, att_q_g, att_k_g, att_w_o, final_g):
    b, s, d = x_prompt.shape
    db, ds, _ = x_sample.shape
    depth = mod_w.shape[0]
    n_mixers = 2
    assert ds % TOKEN_TILE == 0 and 1 + db <= SUBLANES

    xp = x_prompt.reshape(b * s, d)
    xs = x_sample.reshape(db * ds, d)
    cvecs = jnp.concatenate([c_ctx[None], c, jnp.zeros((SUBLANES - 1 - db, d), F32)], axis=0)
    m = _modulation(cvecs, mod_w, mod_b).reshape(depth, SUBLANES, 3 * N_SUB, d)

    tiles_per_sample = ds // TOKEN_TILE
    prompt_row = lambda i: 0
    sample_row = lambda i: 1 + i // tiles_per_sample
    prompt_tiles = (b * s) // TOKEN_TILE
    both_row = lambda i: jnp.where(i < prompt_tiles, 0, 1 + (i - prompt_tiles) // tiles_per_sample)

    new_states, new_k, new_v = [], [], []
    for layer in range(depth):
        j = layer // n_mixers
        last = layer == depth - 1
        xp, xs = _ffn(xp, xs, m, both_row, norm_g, ffn_w_gu, ffn_w_down, layer, 0, 0)
        if layer % n_mixers == 0:
            lru_p = (lru_w_in[j], lru_conv_w[j], lru_conv_b[j], lru_gate_w[j], lru_gate_b[j],
                     lru_lambda[j], lru_w_out[j])
            xp, st = _lru(xp, m, prompt_row, norm_g, layer, *lru_p, seq_len=s, h0=None,
                          emit_state=True, single_step=True)
            new_states.append(st)
            xs, _ = _lru(xs, m, sample_row, norm_g, layer, *lru_p, seq_len=ds,
                         h0=state_lru[:, j], emit_state=False)
        else:
            att_p = (att_w_qkv[j], att_q_g[j], att_k_g[j], att_w_o[j])
            xp, kp, vp = _attn(xp, m, prompt_row, norm_g, layer, *att_p, seq_len=s, q_block=s,
                               emit_kv=True, fused=True)
            new_k.append(kp.reshape(b, s, N_KV_HEADS, HEAD_DIM))
            new_v.append(vp.reshape(b, s, N_KV_HEADS, HEAD_DIM))
            past = cache_k.shape[2]
            ck = cache_k[:, j].reshape(db, past * N_KV_HEADS, HEAD_DIM)
            cv = cache_v[:, j].reshape(db, past * N_KV_HEADS, HEAD_DIM)
            (xs,) = _attn(xs, m, sample_row, norm_g, layer, *att_p, seq_len=ds, q_block=128,
                          cache_k=ck, cache_v=cv, rope=True)
        fg = final_g if last else None
        xp, xs = _ffn(xp, xs, m, both_row, norm_g, ffn_w_gu, ffn_w_down, layer, 1, 2, final_g=fg)

    y_prompt = xp.reshape(b, s, d)
    y_sample = xs.reshape(db, ds, d)
    return (y_prompt, y_sample, jnp.stack(new_states, axis=1), jnp.stack(new_k, axis=1),
            jnp.stack(new_v, axis=1))
```

```python
import functools

import jax
import jax.numpy as jnp
import numpy as np
from jax import lax
from jax.experimental import pallas as pl
from jax.experimental.pallas import tpu as pltpu

F32 = jnp.float32
BF16 = jnp.bfloat16

EPS = 1e-6
LRU_C = 8.0
LOG2_E = 1.4426950408889634
GRID_W = 64
ROPE_THETA = 10000.0
N_SUB = 3
HEAD_DIM = 128
N_KV_HEADS = 2

V7X_VMEM_LIMIT_BYTES = 56 * 1024 * 1024
SUBLANES = 8
TOKEN_TILE = 1024
FF_CHUNK = 256
FFN_ROW_BLOCK = 256
FFN_STAGE_SLOTS = 3
LRU_CHUNK = 512
LRU_GATE_CHUNK = 256
MOD_CHUNK = 1536


def _params(*semantics):
    return pltpu.CompilerParams(dimension_semantics=semantics,
                                vmem_limit_bytes=V7X_VMEM_LIMIT_BYTES)


def _mm(a_bf16, w_f32):
    return jnp.dot(a_bf16, w_f32.astype(BF16), preferred_element_type=F32)


def _rms(x):
    return x * lax.rsqrt(jnp.mean(x * x, axis=-1, keepdims=True) + EPS)


def _sub_in(x, m_ref, g_ref, sidx):
    shift = m_ref[3 * sidx:3 * sidx + 1, :]
    scale = m_ref[3 * sidx + 1:3 * sidx + 2, :]
    return (_rms(x) * g_ref[sidx:sidx + 1, :]) * (1.0 + scale) + shift


def _gate(m_ref, sidx):
    return m_ref[3 * sidx + 2:3 * sidx + 3, :]


def _mod_kernel(cv_ref, w_ref, b_ref, o_ref):
    cv = cv_ref[...]
    act = (cv * jax.nn.sigmoid(cv)).astype(BF16)
    o_ref[...] = _mm(act, w_ref[...]) + b_ref[...]


def _modulation(cvecs, mod_w, mod_b):
    depth, d, n = mod_w.shape
    rows = cvecs.shape[0]
    tn = MOD_CHUNK
    assert n % tn == 0
    return pl.pallas_call(
        _mod_kernel,
        grid=(depth, n // tn),
        in_specs=[
            pl.BlockSpec((rows, d), lambda l, j: (0, 0)),
            pl.BlockSpec((None, d, tn), lambda l, j: (l, 0, j)),
            pl.BlockSpec((None, 1, tn), lambda l, j: (l, 0, j)),
        ],
        out_specs=pl.BlockSpec((None, rows, tn), lambda l, j: (l, 0, j)),
        out_shape=jax.ShapeDtypeStruct((depth, rows, n), F32),
        compiler_params=_params("arbitrary", "arbitrary"),
        name="modulation",
    )(cvecs, mod_w, mod_b.reshape(depth, 1, n))


def _ffn_kernel(*refs, layer, s, sidx, final, n_first):
    if final:
        (m_ref, g_ref, fg_ref, xa_hbm, xb_hbm, wgu_hbm, wd_hbm, oa_hbm, ob_hbm,
         wgu_res, wd_res, act_ref, h0_ref, act0_ref, gu_buf, wd_buf, xbuf, obuf,
         w_sem, xsem, osem) = refs
    else:
        (m_ref, g_ref, xa_hbm, xb_hbm, wgu_hbm, wd_hbm, oa_hbm, ob_hbm,
         wgu_res, wd_res, act_ref, h0_ref, act0_ref, gu_buf, wd_buf, xbuf, obuf,
         w_sem, xsem, osem) = refs
    i = pl.program_id(0)
    n = pl.num_programs(0)
    tm = xbuf.shape[1]
    _, d, tf = wgu_res.shape
    f = wd_res.shape[0]
    nk = f // tf
    rb = act_ref.shape[0]
    nslot = gu_buf.shape[0]
    ahead = nslot - 1
    slot = lax.rem(i, 2)
    other = 1 - slot

    def stage_copies(k, sl):
        gcols = pl.ds(pl.multiple_of(k * tf, tf), tf)
        ucols = pl.ds(pl.multiple_of(f + k * tf, tf), tf)
        return (
            pltpu.make_async_copy(wgu_hbm.at[layer, s, :, gcols], gu_buf.at[sl, 0], w_sem.at[sl, 0]),
            pltpu.make_async_copy(wgu_hbm.at[layer, s, :, ucols], gu_buf.at[sl, 1], w_sem.at[sl, 1]),
            pltpu.make_async_copy(wd_hbm.at[layer, s, gcols, :], wd_buf.at[sl], w_sem.at[sl, 2]),
        )

    def tile_rows(hbm, tile):
        return hbm.at[pl.ds(pl.multiple_of(tile * tm, tm), tm), :]

    def x_copy(which, tile, sl):
        return pltpu.make_async_copy(tile_rows((xa_hbm, xb_hbm)[which], tile), xbuf.at[sl],
                                     xsem.at[sl])

    def o_copy(which, tile, sl):
        return pltpu.make_async_copy(obuf.at[sl], tile_rows((oa_hbm, ob_hbm)[which], tile),
                                     osem.at[sl])

    def start_by_stream(make, tile, sl):
        @pl.when(tile < n_first)
        def _():
            make(0, tile, sl).start()

        @pl.when(tile >= n_first)
        def _():
            make(1, tile - n_first, sl).start()

    @pl.when(i == 0)
    def _():
        x_copy(0, 0, 0).start()
        for k in range(ahead):
            for cp in stage_copies(k, k):
                cp.start()

    @pl.when(i + 1 < n)
    def _():
        start_by_stream(x_copy, i + 1, other)

    x_copy(0, 0, slot).wait()

    @pl.when(i >= 2)
    def _():
        o_copy(0, 0, slot).wait()

    def swiglu(h, wg, wu):
        gt = jnp.dot(h, wg, preferred_element_type=F32)
        up = jnp.dot(h, wu, preferred_element_type=F32)
        return ((gt * jax.nn.sigmoid(gt)) * up).astype(BF16)

    def down_and_store(x, rows):
        ff = jnp.dot(act_ref[...], wd_res[...], preferred_element_type=F32)
        y = x + (0.5 * _gate(m_ref, sidx)) * ff
        if final:
            y = _rms(y) * fg_ref[...]
        obuf[slot, rows, :] = y

    def row_block(r, carry):
        rows = pl.ds(pl.multiple_of(r * rb, rb), rb)
        x = xbuf[slot, rows, :]
        h = _sub_in(x, m_ref, g_ref, sidx).astype(BF16)
        for k in range(nk):
            act_ref[:, k * tf:(k + 1) * tf] = swiglu(h, wgu_res[k], wgu_res[nk + k])
        down_and_store(x, rows)
        return carry

    def first_tile_chunk(k, carry):
        sl = lax.rem(k, nslot)
        for cp in stage_copies(k, sl):
            cp.wait()

        @pl.when(k + ahead < nk)
        def _():
            for cp in stage_copies(k + ahead, lax.rem(k + ahead, nslot)):
                cp.start()

        wg = gu_buf[sl, 0].astype(BF16)
        wu = gu_buf[sl, 1].astype(BF16)
        wgu_res[k] = wg
        wgu_res[nk + k] = wu
        wd_res[pl.ds(pl.multiple_of(k * tf, tf), tf), :] = wd_buf[sl].astype(BF16)
        act0_ref[k] = swiglu(h0_ref[...], wg, wu)
        return carry

    def first_tile_down(r, carry):
        rows = pl.ds(pl.multiple_of(r * rb, rb), rb)
        for k in range(nk):
            act_ref[:, k * tf:(k + 1) * tf] = act0_ref[k, rows, :]
        down_and_store(xbuf[slot, rows, :], rows)
        return carry

    @pl.when(i == 0)
    def _():
        h0_ref[...] = _sub_in(xbuf[slot], m_ref, g_ref, sidx).astype(BF16)
        lax.fori_loop(0, nk, first_tile_chunk, 0)
        lax.fori_loop(0, tm // rb, first_tile_down, 0, unroll=2)

    @pl.when(i > 0)
    def _():
        lax.fori_loop(0, tm // rb, row_block, 0, unroll=2)

    start_by_stream(o_copy, i, slot)

    @pl.when(i == n - 1)
    def _():
        o_copy(0, 0, other).wait()
        o_copy(0, 0, slot).wait()


def _ffn(xa, xb, m, row_of_tile, norm_g, w_gu, w_down, layer, s, sidx, final_g=None):
    (ta, d), tb = xa.shape, xb.shape[0]
    f = w_down.shape[2]
    tm, tf, rb = TOKEN_TILE, FF_CHUNK, FFN_ROW_BLOCK
    assert ta % tm == 0 and tb % tm == 0 and f % tf == 0 and (ta + tb) // tm >= 2
    assert tm % rb == 0 and f // tf >= FFN_STAGE_SLOTS
    final = final_g is not None
    in_specs = [
        pl.BlockSpec((None, None, 3 * N_SUB, d), lambda i: (layer, row_of_tile(i), 0, 0)),
        pl.BlockSpec((None, N_SUB, d), lambda i: (layer, 0, 0)),
    ]
    args = [m, norm_g]
    if final:
        in_specs.append(pl.BlockSpec((1, d), lambda i: (0, 0)))
        args.append(final_g.reshape(1, d))
    in_specs += [pl.BlockSpec(memory_space=pl.ANY)] * 4
    args += [xa, xb, w_gu, w_down]
    return pl.pallas_call(
        functools.partial(_ffn_kernel, layer=layer, s=s, sidx=sidx, final=final,
                          n_first=ta // tm),
        grid=((ta + tb) // tm,),
        in_specs=in_specs,
        out_specs=[pl.BlockSpec(memory_space=pl.ANY)] * 2,
        out_shape=[jax.ShapeDtypeStruct((ta, d), F32), jax.ShapeDtypeStruct((tb, d), F32)],
        scratch_shapes=[
            pltpu.VMEM((2 * f // tf, d, tf), BF16), pltpu.VMEM((f, d), BF16),
            pltpu.VMEM((rb, f), BF16),
            pltpu.VMEM((tm, d), BF16), pltpu.VMEM((f // tf, tm, tf), BF16),
            pltpu.VMEM((FFN_STAGE_SLOTS, 2, d, tf), F32),
            pltpu.VMEM((FFN_STAGE_SLOTS, tf, d), F32),
            pltpu.VMEM((2, tm, d), F32), pltpu.VMEM((2, tm, d), F32),
            pltpu.SemaphoreType.DMA((FFN_STAGE_SLOTS, 3)),
            pltpu.SemaphoreType.DMA((2,)),
            pltpu.SemaphoreType.DMA((2,)),
        ],
        compiler_params=_params("arbitrary"),
        name="ffn",
    )(*args)


def _gelu_tanh(x):
    c = np.float32(np.sqrt(2.0 / np.pi))
    return x * (0.5 * (1.0 + jnp.tanh(c * (x + 0.044715 * (x * x * x)))))


_SWAP_BLOCK = SUBLANES * SUBLANES
_SWAP_SLAB = 4 * _SWAP_BLOCK


def _block_swap_matrix():
    p = np.arange(_SWAP_SLAB)
    blk, a, b = p // _SWAP_BLOCK, (p % _SWAP_BLOCK) // SUBLANES, p % SUBLANES
    mat = np.zeros((_SWAP_SLAB, _SWAP_SLAB), np.float32)
    mat[p, blk * _SWAP_BLOCK + b * SUBLANES + a] = 1.0
    return mat


def _swap_blocks(q, rows):
    parts = [jnp.dot(q, rows[lo:lo + _SWAP_SLAB, :], preferred_element_type=F32)
             for lo in range(0, rows.shape[0], _SWAP_SLAB)]
    return jnp.concatenate(parts, axis=0) if len(parts) > 1 else parts[0]


def _regroup(rows, seq_len, to_permuted):
    segs = seq_len // SUBLANES
    pieces = []
    for lo in range(0, rows.shape[0], seq_len):
        outer, inner = (segs // SUBLANES, SUBLANES) if to_permuted else (SUBLANES, segs // SUBLANES)
        for o in range(outer):
            for i in range(inner):
                jh, s = (o, i) if to_permuted else (i, o)
                src = s * segs + jh * SUBLANES if to_permuted else jh * _SWAP_BLOCK + s * SUBLANES
                pieces.append(rows[lo + src:lo + src + SUBLANES, :])
    return jnp.concatenate(pieces, axis=0)


def _to_permuted_time(q, h_f32, seq_len):
    return _swap_blocks(q, _regroup(h_f32, seq_len, True).astype(BF16)).astype(BF16)


def _to_natural_time(q, y_bf16, seq_len):
    return _regroup(_swap_blocks(q, y_bf16), seq_len, False).astype(BF16)


def _lru_kernel(*refs, seq_len, has_h0, emit_state, conv_left, single_step):
    refs = list(refs)
    (x_ref, m_ref, g_ref, wx_ref, wy_ref, cw_ref, cb_ref, gw_ref, gb_ref, lam_ref,
     wo_ref, swap_ref) = refs[:12]
    pos = 12
    h0_ref = None
    if has_h0:
        h0_ref = refs[pos]
        pos += 1
    o_ref = refs[pos]
    pos += 1
    st_ref = None
    if emit_state:
        st_ref = refs[pos]
        pos += 1
    if single_step:
        af_ref, uf_ref, ab_ref, ub_ref = refs[pos:]
    else:
        h_ref, acc_ref, af_ref, uf_ref, ab_ref, ub_ref = refs[pos:]

    tm, cw_cols = af_ref.shape
    nseq = tm // seq_len
    segs = seq_len // SUBLANES

    def permuted_input():
        return _to_permuted_time(swap_ref[...], _sub_in(x_ref[...], m_ref, g_ref, 1), seq_len)

    if single_step:
        h = permuted_input()
    else:
        c = pl.program_id(1)

        @pl.when(c == 0)
        def _():
            h_ref[...] = permuted_input()
            acc_ref[...] = jnp.zeros_like(acc_ref)

        h = h_ref[...]
    xb = _mm(h, wx_ref[...])
    yb = _mm(h, wy_ref[...])

    sub = lax.broadcasted_iota(jnp.int32, (SUBLANES, cw_cols), 0)

    def next_segment(grp):
        return jnp.where(sub < SUBLANES - 1, pltpu.roll(grp, SUBLANES - 1, 0), 0.0)

    def prev_segment(grp):
        return jnp.where(sub > 0, pltpu.roll(grp, 1, 0), 0.0)

    def tap(seq, off):
        n = abs(off) * SUBLANES
        if off == 0:
            return seq
        if off > 0:
            edge = [next_segment(seq[g * SUBLANES:(g + 1) * SUBLANES, :]) for g in range(off)]
            return jnp.concatenate([seq[n:, :]] + edge, axis=0)
        start = seq_len - n
        edge = [prev_segment(seq[start + g * SUBLANES:start + (g + 1) * SUBLANES, :])
                for g in range(-off)]
        return jnp.concatenate(edge + [seq[:start, :]], axis=0)

    cw = cw_ref[...]
    assert cw.shape[0] - 1 < segs
    xcs = []
    for sq in range(nseq):
        seq = xb[sq * seq_len:(sq + 1) * seq_len, :]
        acc = cb_ref[...]
        for k in range(cw.shape[0]):
            acc = acc + tap(seq, k - conv_left) * cw[k:k + 1, :]
        xcs.append(acc)
    xc = jnp.concatenate(xcs, axis=0) if nseq > 1 else xcs[0]

    gb = gb_ref[...]
    lam = lam_ref[...]
    gc = gw_ref.shape[1]
    for ch in range(cw_cols // gc):
        cols = slice(ch * gc, (ch + 1) * gc)
        xcc = xc[:, cols]
        gl = jnp.dot(xcc.astype(BF16), gw_ref[ch], preferred_element_type=F32)
        for d, (a_ref, u_ref) in enumerate(((af_ref, uf_ref), (ab_ref, ub_ref))):
            tr = jnp.tanh(gl[:, (2 * d) * gc:(2 * d + 1) * gc] + 0.5 * gb[2 * d:2 * d + 1, cols])
            ig = 0.5 * jnp.tanh(gl[:, (2 * d + 1) * gc:(2 * d + 2) * gc]
                                + 0.5 * gb[2 * d + 1:2 * d + 2, cols]) + 0.5
            lm = lam[d:d + 1, cols]
            log_sig = jnp.minimum(lm, 0.0) - jnp.log1p(jnp.exp(-jnp.abs(lm)))
            c2 = (0.5 * LRU_C * LOG2_E) * log_sig
            a = jnp.exp2(c2 * tr + c2)
            v = 1.0 - a * a
            u = jnp.where(v > 0.0, v * lax.rsqrt(v), 0.0) * (ig * xcc)
            a_ref[:, cols] = a
            u_ref[:, cols] = u

    zero = jnp.zeros((SUBLANES, cw_cols), F32)
    one = jnp.ones((SUBLANES, cw_cols), F32)

    def local_scan(j, carry):
        out = []
        for sq in range(nseq):
            for dd, (a_ref, u_ref) in enumerate(((af_ref, uf_ref), (ab_ref, ub_ref))):
                grp = j if dd == 0 else segs - 1 - j
                rows = pl.ds(pl.multiple_of(sq * seq_len + grp * SUBLANES, SUBLANES), SUBLANES)
                hl, pc = carry[2 * (2 * sq + dd)], carry[2 * (2 * sq + dd) + 1]
                a = a_ref[rows, :]
                hl = a * hl + u_ref[rows, :]
                pc = a * pc
                u_ref[rows, :] = hl
                a_ref[rows, :] = pc
                out += [hl, pc]
        return tuple(out)

    totals = lax.fori_loop(0, segs, local_scan, (zero, one) * (2 * nseq), unroll=True)

    gelu_y = _gelu_tanh(yb)
    ys = []
    for sq in range(nseq):
        entry = []
        for dd in range(2):
            hl, pc = totals[2 * (2 * sq + dd)], totals[2 * (2 * sq + dd) + 1]
            state = h0_ref[sq, dd:dd + 1, :] if has_h0 else jnp.zeros((1, cw_cols), F32)
            rows = [None] * SUBLANES
            order = range(SUBLANES) if dd == 0 else range(SUBLANES - 1, -1, -1)
            for s in order:
                rows[s] = state
                state = pc[s:s + 1, :] * state + hl[s:s + 1, :]
            entry.append(jnp.concatenate(rows, axis=0))
            if emit_state:
                st_ref[sq, dd:dd + 1, :] = state
        lo, hi = sq * seq_len, (sq + 1) * seq_len
        grouped = (segs, SUBLANES, cw_cols)
        hsum = ((uf_ref[lo:hi, :].reshape(grouped) + af_ref[lo:hi, :].reshape(grouped) * entry[0][None])
                + (ub_ref[lo:hi, :].reshape(grouped) + ab_ref[lo:hi, :].reshape(grouped) * entry[1][None]))
        ys.append(hsum.reshape(seq_len, cw_cols) * gelu_y[lo:hi, :])
    y = jnp.concatenate(ys, axis=0) if nseq > 1 else ys[0]

    y = _to_natural_time(swap_ref[...], y.astype(BF16), seq_len)
    out = _mm(y, wo_ref[...])
    if single_step:
        o_ref[...] = x_ref[...] + _gate(m_ref, 1) * out
        return
    acc_ref[...] += out

    @pl.when(c == pl.num_programs(1) - 1)
    def _():
        o_ref[...] = x_ref[...] + _gate(m_ref, 1) * acc_ref[...]


def _lru_gate_weights(gate_w):
    nd, ng, nb, bw, _ = gate_w.shape
    per = LRU_GATE_CHUNK // bw
    nc = nb // per
    w = (0.5 * gate_w).astype(BF16).reshape(nd * ng, nc, per, bw, bw)
    rows = []
    for n in range(per):
        blk = jnp.transpose(w[:, :, n], (1, 2, 0, 3))
        blk = jnp.pad(blk, ((0, 0), (0, 0), (0, 0), (n * bw, (per - 1 - n) * bw)))
        rows.append(blk.reshape(nc, bw, nd * ng * per * bw))
    return jnp.concatenate(rows, axis=1)


def _lru(x, m, row_of_tile, norm_g, layer, w_in, conv_w, conv_b, gate_w, gate_b, lam, w_out,
         seq_len, h0, emit_state, single_step=False):
    t, d = x.shape
    r = w_out.shape[0]
    tm, gc = TOKEN_TILE, LRU_GATE_CHUNK
    cb = r if single_step else LRU_CHUNK
    assert t % tm == 0 and tm % seq_len == 0 and r % cb == 0 and cb % gc == 0
    assert seq_len % _SWAP_BLOCK == 0 and tm % _SWAP_SLAB == 0
    nc = r // cb
    nseq = tm // seq_len
    has_h0 = h0 is not None
    resident = dict(pipeline_mode=pl.Buffered(1)) if single_step else {}
    in_specs = [
        pl.BlockSpec((tm, d), lambda i, c: (i, 0)),
        pl.BlockSpec((None, None, 3 * N_SUB, d), lambda i, c: (layer, row_of_tile(i), 0, 0)),
        pl.BlockSpec((None, N_SUB, d), lambda i, c: (layer, 0, 0)),
        pl.BlockSpec((d, cb), lambda i, c: (0, c), **resident),
        pl.BlockSpec((d, cb), lambda i, c: (0, nc + c), **resident),
        pl.BlockSpec((conv_w.shape[0], cb), lambda i, c: (0, c)),
        pl.BlockSpec((1, cb), lambda i, c: (0, c)),
        pl.BlockSpec((cb // gc, gc, 4 * gc), lambda i, c: (c, 0, 0), **resident),
        pl.BlockSpec((4, cb), lambda i, c: (0, c)),
        pl.BlockSpec((2, cb), lambda i, c: (0, c)),
        pl.BlockSpec((cb, d), lambda i, c: (c, 0), **resident),
        pl.BlockSpec((_SWAP_SLAB, _SWAP_SLAB), lambda i, c: (0, 0)),
    ]
    args = [x, m, norm_g, w_in, w_in, conv_w, conv_b.reshape(1, r), _lru_gate_weights(gate_w),
            gate_b.reshape(4, r), lam, w_out, jnp.asarray(_block_swap_matrix(), BF16)]
    if has_h0:
        in_specs.append(pl.BlockSpec((nseq, 2, cb), lambda i, c: (i, 0, c)))
        args.append(h0)
    out_specs = [pl.BlockSpec((tm, d), lambda i, c: (i, 0))]
    out_shape = [jax.ShapeDtypeStruct((t, d), F32)]
    if emit_state:
        out_specs.append(pl.BlockSpec((nseq, 2, cb), lambda i, c: (i, 0, c)))
        out_shape.append(jax.ShapeDtypeStruct((t // seq_len, 2, r), F32))
    outs = pl.pallas_call(
        functools.partial(_lru_kernel, seq_len=seq_len, has_h0=has_h0, emit_state=emit_state,
                          conv_left=(conv_w.shape[0] - 1) // 2, single_step=single_step),
        grid=(t // tm, nc),
        in_specs=in_specs,
        out_specs=out_specs,
        out_shape=out_shape,
        scratch_shapes=([] if single_step
                        else [pltpu.VMEM((tm, d), BF16), pltpu.VMEM((tm, d), F32)])
        + [pltpu.VMEM((tm, cb), F32)] * 4,
        compiler_params=_params("arbitrary", "arbitrary"),
        name="rglru",
    )(*args)
    return outs if emit_state else (outs[0], None)


def _rope(x, cos, sin_signed, lane):
    hd = x.shape[1]
    partner = jnp.where((lane & 32) == 0, pltpu.roll(x, hd - 32, 1), pltpu.roll(x, 32, 1))
    return x * cos + partner * sin_signed


def _attn_kernel(*refs, seq_len, q_block, past_len, rope, emit_kv, group, kvh, fused):
    refs = list(refs)
    x_ref, m_ref, g_ref, wq_ref, wk_ref, wv_ref, qg_ref, kg_ref, wo_ref = refs[:9]
    pos = 9
    if rope:
        cos_ref, sin_ref = refs[pos:pos + 2]
        pos += 2
    if past_len:
        ck_ref, cv_ref = refs[pos:pos + 2]
        pos += 2
    o_ref = refs[pos]
    pos += 1
    if emit_kv:
        kn_ref, vn_ref = refs[pos:pos + 2]
        pos += 2
    if fused:
        q_s, k_s, v_s, o_s = refs[pos:]
    else:
        h_ref, acc_ref, q_s, k_s, v_s, o_s = refs[pos:]

    tm = x_ref.shape[0]
    hd = k_s.shape[-1]
    gw = group * hd
    nqb = seq_len // q_block
    nchunks = (tm // seq_len) * nqb
    nk = past_len + seq_len
    scale = hd ** -0.5

    def group_out(h, g, slot, cols):
        if fused:
            qs, ks, vs, os_ = q_s.at[slot], k_s.at[slot], v_s.at[slot], o_s.at[slot]
            wq, wk = wq_ref[:, cols * gw:(cols + 1) * gw], wk_ref[:, cols * hd:(cols + 1) * hd]
            wv, wo = wv_ref[:, cols * hd:(cols + 1) * hd], wo_ref[cols * gw:(cols + 1) * gw, :]
        else:
            qs, ks, vs, os_ = q_s, k_s, v_s, o_s
            wq, wk, wv, wo = wq_ref[...], wk_ref[...], wv_ref[...], wo_ref[...]
        q = _mm(h, wq)
        k = _rms(_mm(h, wk)) * kg_ref[...]
        v = _mm(h, wv)
        if emit_kv:
            kn_ref[pl.ds(g, tm, stride=kvh), :] = k
            vn_ref[pl.ds(g, tm, stride=kvh), :] = v
        if rope:
            lane = lax.broadcasted_iota(jnp.int32, (tm, hd), 1)
            cos, sin = cos_ref[...], sin_ref[...]
            k = _rope(k, cos, sin, lane)
        ks[past_len:past_len + tm, :] = k.astype(BF16)
        vs[past_len:past_len + tm, :] = v.astype(BF16)
        if past_len:
            ks[0:past_len, :] = ck_ref[pl.ds(g, past_len, stride=kvh), :].astype(BF16)
            vs[0:past_len, :] = cv_ref[pl.ds(g, past_len, stride=kvh), :].astype(BF16)
        for j in range(group):
            qh = _rms(q[:, j * hd:(j + 1) * hd]) * qg_ref[...]
            if rope:
                qh = _rope(qh, cos, sin, lane)
            qs[:, j * hd:(j + 1) * hd] = qh.astype(BF16)

        def chunk(ci, carry):
            r0 = pl.multiple_of(ci * q_block, q_block)
            if past_len:
                keys, vals = ks[...], vs[...]
            else:
                koff = pl.multiple_of((ci // nqb) * seq_len, seq_len)
                keys, vals = ks[pl.ds(koff, nk), :], vs[pl.ds(koff, nk), :]
            qc = jnp.concatenate(
                [qs[pl.ds(r0, q_block), j * hd:(j + 1) * hd] for j in range(group)], axis=0)
            raw = lax.dot_general(qc, keys, (((1,), (1,)), ((), ())), preferred_element_type=F32)
            e = jnp.exp2((raw - jnp.max(raw, axis=-1, keepdims=True)) * (scale * LOG2_E))
            inv = 1.0 / jnp.sum(e, axis=-1, keepdims=True)
            oc = jnp.dot(e.astype(BF16), vals, preferred_element_type=F32) * inv
            for j in range(group):
                os_[pl.ds(r0, q_block), j * hd:(j + 1) * hd] = (
                    oc[j * q_block:(j + 1) * q_block, :].astype(BF16))
            return carry

        lax.fori_loop(0, nchunks, chunk, 0, unroll=min(nchunks, 4))
        return _mm(os_[...], wo)

    if fused:
        x = x_ref[...]
        h = _sub_in(x, m_ref, g_ref, 1).astype(BF16)
        acc = group_out(h, 0, 0, 0)
        for g in range(1, kvh):
            acc = acc + group_out(h, g, g, g)
        o_ref[...] = x + _gate(m_ref, 1) * acc
        return

    gi = pl.program_id(1)

    @pl.when(gi == 0)
    def _():
        h_ref[...] = _sub_in(x_ref[...], m_ref, g_ref, 1).astype(BF16)
        acc_ref[...] = jnp.zeros_like(acc_ref)

    out = group_out(h_ref[...], gi, 0, 0)
    acc_ref[...] += out

    @pl.when(gi == pl.num_programs(1) - 1)
    def _():
        o_ref[...] = x_ref[...] + _gate(m_ref, 1) * acc_ref[...]


def _rope_tables(n_tok, hd):
    rows = n_tok // GRID_W
    r_idx = jnp.broadcast_to(jnp.arange(rows)[:, None], (rows, GRID_W)).reshape(n_tok).astype(F32)
    c_idx = jnp.broadcast_to(jnp.arange(GRID_W)[None, :], (rows, GRID_W)).reshape(n_tok).astype(F32)
    n_freq = hd // 4
    inv = ROPE_THETA ** (-jnp.arange(n_freq, dtype=F32) / n_freq)
    ang = jnp.stack([r_idx[:, None] * inv, c_idx[:, None] * inv], axis=1)
    cos, sin = jnp.cos(ang), jnp.sin(ang)
    cos_full = jnp.concatenate([cos, cos], axis=-1).reshape(n_tok, hd)
    sin_signed = jnp.concatenate([-sin, sin], axis=-1).reshape(n_tok, hd)
    return cos_full, sin_signed


def _attn(x, m, row_of_tile, norm_g, layer, w_qkv, q_g, k_g, w_o, seq_len, q_block,
          cache_k=None, cache_v=None, rope=False, emit_kv=False, fused=False):
    t, d = x.shape
    hd, kvh = HEAD_DIM, N_KV_HEADS
    n_heads = w_o.shape[0] // hd
    group = n_heads // kvh
    gw = group * hd
    tm = TOKEN_TILE
    assert t % tm == 0 and tm % seq_len == 0 and seq_len % q_block == 0
    past_len = 0 if cache_k is None else cache_k.shape[1] // kvh
    assert past_len == 0 or tm == seq_len
    gs = kvh if fused else 1
    lead = (kvh,) if fused else ()
    in_specs = [
        pl.BlockSpec((tm, d), lambda i, g: (i, 0)),
        pl.BlockSpec((None, None, 3 * N_SUB, d), lambda i, g: (layer, row_of_tile(i), 0, 0)),
        pl.BlockSpec((None, N_SUB, d), lambda i, g: (layer, 0, 0)),
        pl.BlockSpec((d, gs * gw), lambda i, g: (0, g)),
        pl.BlockSpec((d, gs * hd), lambda i, g: (0, n_heads // gs + g)),
        pl.BlockSpec((d, gs * hd), lambda i, g: (0, (n_heads + kvh) // gs + g)),
        pl.BlockSpec((1, hd), lambda i, g: (0, 0)),
        pl.BlockSpec((1, hd), lambda i, g: (0, 0)),
        pl.BlockSpec((gs * gw, d), lambda i, g: (g, 0)),
    ]
    args = [x, m, norm_g, w_qkv, w_qkv, w_qkv, q_g.reshape(1, hd), k_g.reshape(1, hd), w_o]
    if rope:
        assert tm == seq_len
        cos, sin = _rope_tables(seq_len, hd)
        in_specs += [pl.BlockSpec((tm, hd), lambda i, g: (0, 0))] * 2
        args += [cos, sin]
    if past_len:
        in_specs += [pl.BlockSpec((None, past_len * kvh, hd), lambda i, g: (i, 0, 0))] * 2
        args += [cache_k, cache_v]
    out_specs = [pl.BlockSpec((tm, d), lambda i, g: (i, 0))]
    out_shape = [jax.ShapeDtypeStruct((t, d), F32)]
    if emit_kv:
        out_specs += [pl.BlockSpec((tm * kvh, hd), lambda i, g: (i, 0))] * 2
        out_shape += [jax.ShapeDtypeStruct((t * kvh, hd), F32)] * 2
    nkeys = past_len + tm
    outs = pl.pallas_call(
        functools.partial(_attn_kernel, seq_len=seq_len, q_block=q_block, past_len=past_len,
                          rope=rope, emit_kv=emit_kv, group=group, kvh=kvh, fused=fused),
        grid=(t // tm, kvh // gs),
        in_specs=in_specs,
        out_specs=out_specs,
        out_shape=out_shape,
        scratch_shapes=([] if fused else [pltpu.VMEM((tm, d), BF16), pltpu.VMEM((tm, d), F32)])
        + [pltpu.VMEM(lead + (tm, gw), BF16), pltpu.VMEM(lead + (nkeys, hd), BF16),
           pltpu.VMEM(lead + (nkeys, hd), BF16), pltpu.VMEM(lead + (tm, gw), BF16)],
        compiler_params=_params("arbitrary", "arbitrary"),
        name="gqa",
    )(*args)
    return outs


def kernel(x_prompt, x_sample, c, state_lru, cache_k, cache_v, c_ctx, mod_w, mod_b, norm_g,
           ffn_w_gu, ffn_w_down, lru_w_in, lru_conv_w, lru_conv_b, lru_gate_w, lru_gate_b,
           lru_lambda, lru_w_out, att_w_qkv, att_q_g, att_k_g, att_w_o, final_g):
    b, s, d = x_prompt.shape
    db, ds, _ = x_sample.shape
    depth = mod_w.shape[0]
    n_mixers = 2
    assert ds % TOKEN_TILE == 0 and 1 + db <= SUBLANES

    xp = x_prompt.reshape(b * s, d)
    xs = x_sample.reshape(db * ds, d)
    cvecs = jnp.concatenate([c_ctx[None], c, jnp.zeros((SUBLANES - 1 - db, d), F32)], axis=0)
    m = _modulation(cvecs, mod_w, mod_b).reshape(depth, SUBLANES, 3 * N_SUB, d)

    tiles_per_sample = ds // TOKEN_TILE
    prompt_row = lambda i: 0
    sample_row = lambda i: 1 + i // tiles_per_sample
    prompt_tiles = (b * s) // TOKEN_TILE
    both_row = lambda i: jnp.where(i < prompt_tiles, 0, 1 + (i - prompt_tiles) // tiles_per_sample)

    new_states, new_k, new_v = [], [], []
    for layer in range(depth):
        j = layer // n_mixers
        last = layer == depth - 1
        xp, xs = _ffn(xp, xs, m, both_row, norm_g, ffn_w_gu, ffn_w_down, layer, 0, 0)
        if layer % n_mixers == 0:
            lru_p = (lru_w_in[j], lru_conv_w[j], lru_conv_b[j], lru_gate_w[j], lru_gate_b[j],
                     lru_lambda[j], lru_w_out[j])
            xp, st = _lru(xp, m, prompt_row, norm_g, layer, *lru_p, seq_len=s, h0=None,
                          emit_state=True, single_step=True)
            new_states.append(st)
            xs, _ = _lru(xs, m, sample_row, norm_g, layer, *lru_p, seq_len=ds,
                         h0=state_lru[:, j], emit_state=False, single_step=True)
        else:
            att_p = (att_w_qkv[j], att_q_g[j], att_k_g[j], att_w_o[j])
            xp, kp, vp = _attn(xp, m, prompt_row, norm_g, layer, *att_p, seq_len=s, q_block=s,
                               emit_kv=True, fused=True)
            new_k.append(kp.reshape(b, s, N_KV_HEADS, HEAD_DIM))
            new_v.append(vp.reshape(b, s, N_KV_HEADS, HEAD_DIM))
            past = cache_k.shape[2]
            ck = cache_k[:, j].reshape(db, past * N_KV_HEADS, HEAD_DIM)
            cv = cache_v[:, j].reshape(db, past * N_KV_HEADS, HEAD_DIM)
            (xs,) = _attn(xs, m, sample_row, norm_g, layer, *att_p, seq_len=ds, q_block=128,
                          cache_k=ck, cache_v=cv, rope=True)
        fg = final_g if last else None
        xp, xs = _ffn(xp, xs, m, both_row, norm_g, ffn_w_gu, ffn_w_down, layer, 1, 2, final_g=fg)

    y_prompt = xp.reshape(b, s, d)
    y_sample = xs.reshape(db, ds, d)
    return (y_prompt, y_sample, jnp.stack(new_states, axis=1), jnp.stack(new_k, axis=1),
            jnp.stack(new_v, axis=1))
```

```python
import functools

import jax
import jax.numpy as jnp
import numpy as np
from jax import lax
from jax.experimental import pallas as pl
from jax.experimental.pallas import tpu as pltpu

F32 = jnp.float32
BF16 = jnp.bfloat16

EPS = 1e-6
LRU_C = 8.0
LOG2_E = 1.4426950408889634
GRID_W = 64
ROPE_THETA = 10000.0
N_SUB = 3
HEAD_DIM = 128
N_KV_HEADS = 2

V7X_VMEM_LIMIT_BYTES = 56 * 1024 * 1024
SUBLANES = 8
TOKEN_TILE = 1024
FF_CHUNK = 256
FFN_ROW_BLOCK = 256
FFN_STAGE_SLOTS = 3
LRU_CHUNK = 512
LRU_GATE_CHUNK = 256
MOD_CHUNK = 1536


def _params(*semantics):
    return pltpu.CompilerParams(dimension_semantics=semantics,
                                vmem_limit_bytes=V7X_VMEM_LIMIT_BYTES)


def _mm(a_bf16, w_f32):
    return jnp.dot(a_bf16, w_f32.astype(BF16), preferred_element_type=F32)


def _rms(x):
    return x * lax.rsqrt(jnp.mean(x * x, axis=-1, keepdims=True) + EPS)


def _sub_in(x, m_ref, g_ref, sidx):
    shift = m_ref[3 * sidx:3 * sidx + 1, :]
    scale = m_ref[3 * sidx + 1:3 * sidx + 2, :]
    return (_rms(x) * g_ref[sidx:sidx + 1, :]) * (1.0 + scale) + shift


def _gate(m_ref, sidx):
    return m_ref[3 * sidx + 2:3 * sidx + 3, :]


def _mod_kernel(cv_ref, w_ref, b_ref, o_ref):
    cv = cv_ref[...]
    act = (cv * jax.nn.sigmoid(cv)).astype(BF16)
    o_ref[...] = _mm(act, w_ref[...]) + b_ref[...]


def _modulation(cvecs, mod_w, mod_b):
    depth, d, n = mod_w.shape
    rows = cvecs.shape[0]
    tn = MOD_CHUNK
    assert n % tn == 0
    return pl.pallas_call(
        _mod_kernel,
        grid=(depth, n // tn),
        in_specs=[
            pl.BlockSpec((rows, d), lambda l, j: (0, 0)),
            pl.BlockSpec((None, d, tn), lambda l, j: (l, 0, j)),
            pl.BlockSpec((None, 1, tn), lambda l, j: (l, 0, j)),
        ],
        out_specs=pl.BlockSpec((None, rows, tn), lambda l, j: (l, 0, j)),
        out_shape=jax.ShapeDtypeStruct((depth, rows, n), F32),
        compiler_params=_params("arbitrary", "arbitrary"),
        name="modulation",
    )(cvecs, mod_w, mod_b.reshape(depth, 1, n))


def _ffn_kernel(*refs, layer, s, sidx, final, n_first):
    if final:
        (m_ref, g_ref, fg_ref, xa_hbm, xb_hbm, wgu_hbm, wd_hbm, oa_hbm, ob_hbm,
         wgu_res, wd_res, act_ref, h0_ref, act0_ref, gu_buf, wd_buf, xbuf, obuf,
         w_sem, xsem, osem) = refs
    else:
        (m_ref, g_ref, xa_hbm, xb_hbm, wgu_hbm, wd_hbm, oa_hbm, ob_hbm,
         wgu_res, wd_res, act_ref, h0_ref, act0_ref, gu_buf, wd_buf, xbuf, obuf,
         w_sem, xsem, osem) = refs
    i = pl.program_id(0)
    n = pl.num_programs(0)
    tm = xbuf.shape[1]
    _, d, tf = wgu_res.shape
    f = wd_res.shape[0]
    nk = f // tf
    rb = act_ref.shape[0]
    nslot = gu_buf.shape[0]
    ahead = nslot - 1
    slot = lax.rem(i, 2)
    other = 1 - slot

    def stage_copies(k, sl):
        gcols = pl.ds(pl.multiple_of(k * tf, tf), tf)
        ucols = pl.ds(pl.multiple_of(f + k * tf, tf), tf)
        return (
            pltpu.make_async_copy(wgu_hbm.at[layer, s, :, gcols], gu_buf.at[sl, 0], w_sem.at[sl, 0]),
            pltpu.make_async_copy(wgu_hbm.at[layer, s, :, ucols], gu_buf.at[sl, 1], w_sem.at[sl, 1]),
            pltpu.make_async_copy(wd_hbm.at[layer, s, gcols, :], wd_buf.at[sl], w_sem.at[sl, 2]),
        )

    def tile_rows(hbm, tile):
        return hbm.at[pl.ds(pl.multiple_of(tile * tm, tm), tm), :]

    def x_copy(which, tile, sl):
        return pltpu.make_async_copy(tile_rows((xa_hbm, xb_hbm)[which], tile), xbuf.at[sl],
                                     xsem.at[sl])

    def o_copy(which, tile, sl):
        return pltpu.make_async_copy(obuf.at[sl], tile_rows((oa_hbm, ob_hbm)[which], tile),
                                     osem.at[sl])

    def start_by_stream(make, tile, sl):
        @pl.when(tile < n_first)
        def _():
            make(0, tile, sl).start()

        @pl.when(tile >= n_first)
        def _():
            make(1, tile - n_first, sl).start()

    @pl.when(i == 0)
    def _():
        x_copy(0, 0, 0).start()
        for k in range(ahead):
            for cp in stage_copies(k, k):
                cp.start()

    @pl.when(i + 1 < n)
    def _():
        start_by_stream(x_copy, i + 1, other)

    x_copy(0, 0, slot).wait()

    @pl.when(i >= 2)
    def _():
        o_copy(0, 0, slot).wait()

    def swiglu(h, wg, wu):
        gt = jnp.dot(h, wg, preferred_element_type=F32)
        up = jnp.dot(h, wu, preferred_element_type=F32)
        return ((gt * jax.nn.sigmoid(gt)) * up).astype(BF16)

    def down_and_store(x, rows):
        ff = jnp.dot(act_ref[...], wd_res[...], preferred_element_type=F32)
        y = x + (0.5 * _gate(m_ref, sidx)) * ff
        if final:
            y = _rms(y) * fg_ref[...]
        obuf[slot, rows, :] = y

    def row_block(r, carry):
        rows = pl.ds(pl.multiple_of(r * rb, rb), rb)
        x = xbuf[slot, rows, :]
        h = _sub_in(x, m_ref, g_ref, sidx).astype(BF16)
        for k in range(nk):
            act_ref[:, k * tf:(k + 1) * tf] = swiglu(h, wgu_res[k], wgu_res[nk + k])
        down_and_store(x, rows)
        return carry

    def first_tile_chunk(k, carry):
        sl = lax.rem(k, nslot)
        for cp in stage_copies(k, sl):
            cp.wait()

        @pl.when(k + ahead < nk)
        def _():
            for cp in stage_copies(k + ahead, lax.rem(k + ahead, nslot)):
                cp.start()

        wg = gu_buf[sl, 0].astype(BF16)
        wu = gu_buf[sl, 1].astype(BF16)
        wgu_res[k] = wg
        wgu_res[nk + k] = wu
        wd_res[pl.ds(pl.multiple_of(k * tf, tf), tf), :] = wd_buf[sl].astype(BF16)
        act0_ref[k] = swiglu(h0_ref[...], wg, wu)
        return carry

    def first_tile_down(r, carry):
        rows = pl.ds(pl.multiple_of(r * rb, rb), rb)
        for k in range(nk):
            act_ref[:, k * tf:(k + 1) * tf] = act0_ref[k, rows, :]
        down_and_store(xbuf[slot, rows, :], rows)
        return carry

    @pl.when(i == 0)
    def _():
        h0_ref[...] = _sub_in(xbuf[slot], m_ref, g_ref, sidx).astype(BF16)
        lax.fori_loop(0, nk, first_tile_chunk, 0)
        lax.fori_loop(0, tm // rb, first_tile_down, 0, unroll=2)

    @pl.when(i > 0)
    def _():
        lax.fori_loop(0, tm // rb, row_block, 0, unroll=2)

    start_by_stream(o_copy, i, slot)

    @pl.when(i == n - 1)
    def _():
        o_copy(0, 0, other).wait()
        o_copy(0, 0, slot).wait()


def _ffn(xa, xb, m, row_of_tile, norm_g, w_gu, w_down, layer, s, sidx, final_g=None):
    (ta, d), tb = xa.shape, xb.shape[0]
    f = w_down.shape[2]
    tm, tf, rb = TOKEN_TILE, FF_CHUNK, FFN_ROW_BLOCK
    assert ta % tm == 0 and tb % tm == 0 and f % tf == 0 and (ta + tb) // tm >= 2
    assert tm % rb == 0 and f // tf >= FFN_STAGE_SLOTS
    final = final_g is not None
    in_specs = [
        pl.BlockSpec((None, None, 3 * N_SUB, d), lambda i: (layer, row_of_tile(i), 0, 0)),
        pl.BlockSpec((None, N_SUB, d), lambda i: (layer, 0, 0)),
    ]
    args = [m, norm_g]
    if final:
        in_specs.append(pl.BlockSpec((1, d), lambda i: (0, 0)))
        args.append(final_g.reshape(1, d))
    in_specs += [pl.BlockSpec(memory_space=pl.ANY)] * 4
    args += [xa, xb, w_gu, w_down]
    return pl.pallas_call(
        functools.partial(_ffn_kernel, layer=layer, s=s, sidx=sidx, final=final,
                          n_first=ta // tm),
        grid=((ta + tb) // tm,),
        in_specs=in_specs,
        out_specs=[pl.BlockSpec(memory_space=pl.ANY)] * 2,
        out_shape=[jax.ShapeDtypeStruct((ta, d), F32), jax.ShapeDtypeStruct((tb, d), F32)],
        scratch_shapes=[
            pltpu.VMEM((2 * f // tf, d, tf), BF16), pltpu.VMEM((f, d), BF16),
            pltpu.VMEM((rb, f), BF16),
            pltpu.VMEM((tm, d), BF16), pltpu.VMEM((f // tf, tm, tf), BF16),
            pltpu.VMEM((FFN_STAGE_SLOTS, 2, d, tf), F32),
            pltpu.VMEM((FFN_STAGE_SLOTS, tf, d), F32),
            pltpu.VMEM((2, tm, d), F32), pltpu.VMEM((2, tm, d), F32),
            pltpu.SemaphoreType.DMA((FFN_STAGE_SLOTS, 3)),
            pltpu.SemaphoreType.DMA((2,)),
            pltpu.SemaphoreType.DMA((2,)),
        ],
        compiler_params=_params("arbitrary"),
        name="ffn",
    )(*args)


def _gelu_tanh(x):
    c = np.float32(np.sqrt(2.0 / np.pi))
    return x * (0.5 * (1.0 + jnp.tanh(c * (x + 0.044715 * (x * x * x)))))


_SWAP_BLOCK = SUBLANES * SUBLANES
_SWAP_SLAB = 4 * _SWAP_BLOCK


def _block_swap_matrix():
    p = np.arange(_SWAP_SLAB)
    blk, a, b = p // _SWAP_BLOCK, (p % _SWAP_BLOCK) // SUBLANES, p % SUBLANES
    mat = np.zeros((_SWAP_SLAB, _SWAP_SLAB), np.float32)
    mat[p, blk * _SWAP_BLOCK + b * SUBLANES + a] = 1.0
    return mat


def _swap_blocks(q, rows):
    parts = [jnp.dot(q, rows[lo:lo + _SWAP_SLAB, :], preferred_element_type=F32)
             for lo in range(0, rows.shape[0], _SWAP_SLAB)]
    return jnp.concatenate(parts, axis=0) if len(parts) > 1 else parts[0]


def _regroup(rows, seq_len, to_permuted):
    segs = seq_len // SUBLANES
    pieces = []
    for lo in range(0, rows.shape[0], seq_len):
        outer, inner = (segs // SUBLANES, SUBLANES) if to_permuted else (SUBLANES, segs // SUBLANES)
        for o in range(outer):
            for i in range(inner):
                jh, s = (o, i) if to_permuted else (i, o)
                src = s * segs + jh * SUBLANES if to_permuted else jh * _SWAP_BLOCK + s * SUBLANES
                pieces.append(rows[lo + src:lo + src + SUBLANES, :])
    return jnp.concatenate(pieces, axis=0)


def _to_permuted_time(q, h_f32, seq_len):
    return _swap_blocks(q, _regroup(h_f32, seq_len, True).astype(BF16)).astype(BF16)


def _to_natural_time(q, y_bf16, seq_len):
    return _regroup(_swap_blocks(q, y_bf16), seq_len, False).astype(BF16)


def _lru_kernel(*refs, seq_len, has_h0, emit_state, conv_left, single_step):
    refs = list(refs)
    (x_ref, m_ref, g_ref, wx_ref, wy_ref, cw_ref, cb_ref, gw_ref, gb_ref, lam_ref,
     wo_ref, swap_ref) = refs[:12]
    pos = 12
    h0_ref = None
    if has_h0:
        h0_ref = refs[pos]
        pos += 1
    o_ref = refs[pos]
    pos += 1
    st_ref = None
    if emit_state:
        st_ref = refs[pos]
        pos += 1
    if single_step:
        af_ref, uf_ref, ab_ref, ub_ref = refs[pos:]
    else:
        h_ref, acc_ref, af_ref, uf_ref, ab_ref, ub_ref = refs[pos:]

    tm, cw_cols = af_ref.shape
    nseq = tm // seq_len
    segs = seq_len // SUBLANES

    def permuted_input():
        return _to_permuted_time(swap_ref[...], _sub_in(x_ref[...], m_ref, g_ref, 1), seq_len)

    if single_step:
        h = permuted_input()
    else:
        c = pl.program_id(1)

        @pl.when(c == 0)
        def _():
            h_ref[...] = permuted_input()
            acc_ref[...] = jnp.zeros_like(acc_ref)

        h = h_ref[...]
    xb = _mm(h, wx_ref[...])
    yb = _mm(h, wy_ref[...])

    sub = lax.broadcasted_iota(jnp.int32, (SUBLANES, cw_cols), 0)

    def next_segment(grp):
        return jnp.where(sub < SUBLANES - 1, pltpu.roll(grp, SUBLANES - 1, 0), 0.0)

    def prev_segment(grp):
        return jnp.where(sub > 0, pltpu.roll(grp, 1, 0), 0.0)

    def tap(seq, off):
        n = abs(off) * SUBLANES
        if off == 0:
            return seq
        if off > 0:
            edge = [next_segment(seq[g * SUBLANES:(g + 1) * SUBLANES, :]) for g in range(off)]
            return jnp.concatenate([seq[n:, :]] + edge, axis=0)
        start = seq_len - n
        edge = [prev_segment(seq[start + g * SUBLANES:start + (g + 1) * SUBLANES, :])
                for g in range(-off)]
        return jnp.concatenate(edge + [seq[:start, :]], axis=0)

    cw = cw_ref[...]
    assert cw.shape[0] - 1 < segs
    xcs = []
    for sq in range(nseq):
        seq = xb[sq * seq_len:(sq + 1) * seq_len, :]
        acc = cb_ref[...]
        for k in range(cw.shape[0]):
            acc = acc + tap(seq, k - conv_left) * cw[k:k + 1, :]
        xcs.append(acc)
    xc = jnp.concatenate(xcs, axis=0) if nseq > 1 else xcs[0]

    gb = gb_ref[...]
    lam = lam_ref[...]
    gc = gw_ref.shape[1]
    for ch in range(cw_cols // gc):
        cols = slice(ch * gc, (ch + 1) * gc)
        xcc = xc[:, cols]
        gl = jnp.dot(xcc.astype(BF16), gw_ref[ch], preferred_element_type=F32)
        for d, (a_ref, u_ref) in enumerate(((af_ref, uf_ref), (ab_ref, ub_ref))):
            tr = jnp.tanh(gl[:, (2 * d) * gc:(2 * d + 1) * gc] + 0.5 * gb[2 * d:2 * d + 1, cols])
            ig = 0.5 * jnp.tanh(gl[:, (2 * d + 1) * gc:(2 * d + 2) * gc]
                                + 0.5 * gb[2 * d + 1:2 * d + 2, cols]) + 0.5
            lm = lam[d:d + 1, cols]
            log_sig = jnp.minimum(lm, 0.0) - jnp.log1p(jnp.exp(-jnp.abs(lm)))
            c2 = (0.5 * LRU_C * LOG2_E) * log_sig
            a = jnp.exp2(c2 * tr + c2)
            v = 1.0 - a * a
            u = jnp.where(v > 0.0, v * lax.rsqrt(v), 0.0) * (ig * xcc)
            a_ref[:, cols] = a
            u_ref[:, cols] = u

    zero = jnp.zeros((SUBLANES, cw_cols), F32)
    one = jnp.ones((SUBLANES, cw_cols), F32)

    def local_scan(j, carry):
        out = []
        for sq in range(nseq):
            for dd, (a_ref, u_ref) in enumerate(((af_ref, uf_ref), (ab_ref, ub_ref))):
                grp = j if dd == 0 else segs - 1 - j
                rows = pl.ds(pl.multiple_of(sq * seq_len + grp * SUBLANES, SUBLANES), SUBLANES)
                hl, pc = carry[2 * (2 * sq + dd)], carry[2 * (2 * sq + dd) + 1]
                a = a_ref[rows, :]
                hl = a * hl + u_ref[rows, :]
                pc = a * pc
                u_ref[rows, :] = hl
                a_ref[rows, :] = pc
                out += [hl, pc]
        return tuple(out)

    totals = lax.fori_loop(0, segs, local_scan, (zero, one) * (2 * nseq), unroll=True)

    gelu_y = _gelu_tanh(yb)
    ys = []
    for sq in range(nseq):
        entry = []
        for dd in range(2):
            hl, pc = totals[2 * (2 * sq + dd)], totals[2 * (2 * sq + dd) + 1]
            state = h0_ref[sq, dd:dd + 1, :] if has_h0 else jnp.zeros((1, cw_cols), F32)
            rows = [None] * SUBLANES
            order = range(SUBLANES) if dd == 0 else range(SUBLANES - 1, -1, -1)
            for s in order:
                rows[s] = state
                state = pc[s:s + 1, :] * state + hl[s:s + 1, :]
            entry.append(jnp.concatenate(rows, axis=0))
            if emit_state:
                st_ref[sq, dd:dd + 1, :] = state
        lo, hi = sq * seq_len, (sq + 1) * seq_len
        grouped = (segs, SUBLANES, cw_cols)
        hsum = ((uf_ref[lo:hi, :].reshape(grouped) + af_ref[lo:hi, :].reshape(grouped) * entry[0][None])
                + (ub_ref[lo:hi, :].reshape(grouped) + ab_ref[lo:hi, :].reshape(grouped) * entry[1][None]))
        ys.append(hsum.reshape(seq_len, cw_cols) * gelu_y[lo:hi, :])
    y = jnp.concatenate(ys, axis=0) if nseq > 1 else ys[0]

    y = _to_natural_time(swap_ref[...], y.astype(BF16), seq_len)
    out = _mm(y, wo_ref[...])
    if single_step:
        o_ref[...] = x_ref[...] + _gate(m_ref, 1) * out
        return
    acc_ref[...] += out

    @pl.when(c == pl.num_programs(1) - 1)
    def _():
        o_ref[...] = x_ref[...] + _gate(m_ref, 1) * acc_ref[...]


def _lru_gate_weights(gate_w):
    nd, ng, nb, bw, _ = gate_w.shape
    per = LRU_GATE_CHUNK // bw
    nc = nb // per
    w = (0.5 * gate_w).astype(BF16).reshape(nd * ng, nc, per, bw, bw)
    rows = []
    for n in range(per):
        blk = jnp.transpose(w[:, :, n], (1, 2, 0, 3))
        blk = jnp.pad(blk, ((0, 0), (0, 0), (0, 0), (n * bw, (per - 1 - n) * bw)))
        rows.append(blk.reshape(nc, bw, nd * ng * per * bw))
    return jnp.concatenate(rows, axis=1)


def _lru(x, m, row_of_tile, norm_g, layer, w_in, conv_w, conv_b, gate_w, gate_b, lam, w_out,
         seq_len, h0, emit_state, single_step=False):
    t, d = x.shape
    r = w_out.shape[0]
    tm, gc = TOKEN_TILE, LRU_GATE_CHUNK
    cb = r if single_step else LRU_CHUNK
    assert t % tm == 0 and tm % seq_len == 0 and r % cb == 0 and cb % gc == 0
    assert seq_len % _SWAP_BLOCK == 0 and tm % _SWAP_SLAB == 0
    nc = r // cb
    nseq = tm // seq_len
    has_h0 = h0 is not None
    resident = dict(pipeline_mode=pl.Buffered(1)) if single_step else {}
    in_specs = [
        pl.BlockSpec((tm, d), lambda i, c: (i, 0)),
        pl.BlockSpec((None, None, 3 * N_SUB, d), lambda i, c: (layer, row_of_tile(i), 0, 0)),
        pl.BlockSpec((None, N_SUB, d), lambda i, c: (layer, 0, 0)),
        pl.BlockSpec((d, cb), lambda i, c: (0, c), **resident),
        pl.BlockSpec((d, cb), lambda i, c: (0, nc + c), **resident),
        pl.BlockSpec((conv_w.shape[0], cb), lambda i, c: (0, c)),
        pl.BlockSpec((1, cb), lambda i, c: (0, c)),
        pl.BlockSpec((cb // gc, gc, 4 * gc), lambda i, c: (c, 0, 0), **resident),
        pl.BlockSpec((4, cb), lambda i, c: (0, c)),
        pl.BlockSpec((2, cb), lambda i, c: (0, c)),
        pl.BlockSpec((cb, d), lambda i, c: (c, 0), **resident),
        pl.BlockSpec((_SWAP_SLAB, _SWAP_SLAB), lambda i, c: (0, 0)),
    ]
    args = [x, m, norm_g, w_in, w_in, conv_w, conv_b.reshape(1, r), _lru_gate_weights(gate_w),
            gate_b.reshape(4, r), lam, w_out, jnp.asarray(_block_swap_matrix(), BF16)]
    if has_h0:
        in_specs.append(pl.BlockSpec((nseq, 2, cb), lambda i, c: (i, 0, c)))
        args.append(h0)
    out_specs = [pl.BlockSpec((tm, d), lambda i, c: (i, 0))]
    out_shape = [jax.ShapeDtypeStruct((t, d), F32)]
    if emit_state:
        out_specs.append(pl.BlockSpec((nseq, 2, cb), lambda i, c: (i, 0, c)))
        out_shape.append(jax.ShapeDtypeStruct((t // seq_len, 2, r), F32))
    outs = pl.pallas_call(
        functools.partial(_lru_kernel, seq_len=seq_len, has_h0=has_h0, emit_state=emit_state,
                          conv_left=(conv_w.shape[0] - 1) // 2, single_step=single_step),
        grid=(t // tm, nc),
        in_specs=in_specs,
        out_specs=out_specs,
        out_shape=out_shape,
        scratch_shapes=([] if single_step
                        else [pltpu.VMEM((tm, d), BF16), pltpu.VMEM((tm, d), F32)])
        + [pltpu.VMEM((tm, cb), F32)] * 4,
        compiler_params=_params("arbitrary", "arbitrary"),
        name="rglru",
    )(*args)
    return outs if emit_state else (outs[0], None)


def _rope(x, cos, sin_signed, lane):
    hd = x.shape[1]
    partner = jnp.where((lane & 32) == 0, pltpu.roll(x, hd - 32, 1), pltpu.roll(x, 32, 1))
    return x * cos + partner * sin_signed


def _attn_kernel(*refs, seq_len, q_block, past_len, rope, emit_kv, group, kvh, fused):
    refs = list(refs)
    x_ref, m_ref, g_ref, wq_ref, wk_ref, wv_ref, qg_ref, kg_ref, wo_ref = refs[:9]
    pos = 9
    if rope:
        cos_ref, sin_ref = refs[pos:pos + 2]
        pos += 2
    if past_len:
        ck_ref, cv_ref = refs[pos:pos + 2]
        pos += 2
    o_ref = refs[pos]
    pos += 1
    if emit_kv:
        kn_ref, vn_ref = refs[pos:pos + 2]
        pos += 2
    if fused:
        q_s, k_s, v_s, o_s = refs[pos:]
    else:
        h_ref, acc_ref, q_s, k_s, v_s, o_s = refs[pos:]

    tm = x_ref.shape[0]
    hd = k_s.shape[-1]
    gw = group * hd
    nqb = seq_len // q_block
    nchunks = (tm // seq_len) * nqb
    nk = past_len + seq_len
    scale = hd ** -0.5

    def group_out(h, g, slot, cols, kv_all=None):
        if fused:
            qs, ks, vs, os_ = q_s.at[slot], k_s.at[slot], v_s.at[slot], o_s.at[slot]
            wq, wo = wq_ref[:, cols * gw:(cols + 1) * gw], wo_ref[cols * gw:(cols + 1) * gw, :]
        else:
            qs, ks, vs, os_ = q_s, k_s, v_s, o_s
            wq, wo = wq_ref[...], wo_ref[...]
        q = _mm(h, wq)
        if kv_all is None:
            k_raw, v = _mm(h, wk_ref[...]), _mm(h, wv_ref[...])
        else:
            k_raw, v = (a[:, cols * hd:(cols + 1) * hd] for a in kv_all)
        k = _rms(k_raw) * kg_ref[...]
        if emit_kv:
            kn_ref[pl.ds(g, tm, stride=kvh), :] = k
            vn_ref[pl.ds(g, tm, stride=kvh), :] = v
        if rope:
            lane = lax.broadcasted_iota(jnp.int32, (tm, hd), 1)
            cos, sin = cos_ref[...], sin_ref[...]
            k = _rope(k, cos, sin, lane)
        ks[past_len:past_len + tm, :] = k.astype(BF16)
        vs[past_len:past_len + tm, :] = v.astype(BF16)
        if past_len:
            ks[0:past_len, :] = ck_ref[pl.ds(g, past_len, stride=kvh), :].astype(BF16)
            vs[0:past_len, :] = cv_ref[pl.ds(g, past_len, stride=kvh), :].astype(BF16)
        for j in range(group):
            qh = _rms(q[:, j * hd:(j + 1) * hd]) * qg_ref[...]
            if rope:
                qh = _rope(qh, cos, sin, lane)
            qs[:, j * hd:(j + 1) * hd] = qh.astype(BF16)

        def chunk(ci, carry):
            r0 = pl.multiple_of(ci * q_block, q_block)
            if past_len:
                keys, vals = ks[...], vs[...]
            else:
                koff = pl.multiple_of((ci // nqb) * seq_len, seq_len)
                keys, vals = ks[pl.ds(koff, nk), :], vs[pl.ds(koff, nk), :]
            qc = jnp.concatenate(
                [qs[pl.ds(r0, q_block), j * hd:(j + 1) * hd] for j in range(group)], axis=0)
            raw = lax.dot_general(qc, keys, (((1,), (1,)), ((), ())), preferred_element_type=F32)
            e = jnp.exp2((raw - jnp.max(raw, axis=-1, keepdims=True)) * (scale * LOG2_E))
            inv = 1.0 / jnp.sum(e, axis=-1, keepdims=True)
            oc = jnp.dot(e.astype(BF16), vals, preferred_element_type=F32) * inv
            for j in range(group):
                os_[pl.ds(r0, q_block), j * hd:(j + 1) * hd] = (
                    oc[j * q_block:(j + 1) * q_block, :].astype(BF16))
            return carry

        lax.fori_loop(0, nchunks, chunk, 0, unroll=min(nchunks, 4))
        return _mm(os_[...], wo)

    if fused:
        x = x_ref[...]
        h = _sub_in(x, m_ref, g_ref, 1).astype(BF16)
        kv_all = (_mm(h, wk_ref[...]), _mm(h, wv_ref[...]))
        acc = group_out(h, 0, 0, 0, kv_all)
        for g in range(1, kvh):
            acc = acc + group_out(h, g, g, g, kv_all)
        o_ref[...] = x + _gate(m_ref, 1) * acc
        return

    gi = pl.program_id(1)

    @pl.when(gi == 0)
    def _():
        h_ref[...] = _sub_in(x_ref[...], m_ref, g_ref, 1).astype(BF16)
        acc_ref[...] = jnp.zeros_like(acc_ref)

    out = group_out(h_ref[...], gi, 0, 0)
    acc_ref[...] += out

    @pl.when(gi == pl.num_programs(1) - 1)
    def _():
        o_ref[...] = x_ref[...] + _gate(m_ref, 1) * acc_ref[...]


def _rope_tables(n_tok, hd):
    rows = n_tok // GRID_W
    r_idx = jnp.broadcast_to(jnp.arange(rows)[:, None], (rows, GRID_W)).reshape(n_tok).astype(F32)
    c_idx = jnp.broadcast_to(jnp.arange(GRID_W)[None, :], (rows, GRID_W)).reshape(n_tok).astype(F32)
    n_freq = hd // 4
    inv = ROPE_THETA ** (-jnp.arange(n_freq, dtype=F32) / n_freq)
    ang = jnp.stack([r_idx[:, None] * inv, c_idx[:, None] * inv], axis=1)
    cos, sin = jnp.cos(ang), jnp.sin(ang)
    cos_full = jnp.concatenate([cos, cos], axis=-1).reshape(n_tok, hd)
    sin_signed = jnp.concatenate([-sin, sin], axis=-1).reshape(n_tok, hd)
    return cos_full, sin_signed


def _attn(x, m, row_of_tile, norm_g, layer, w_qkv, q_g, k_g, w_o, seq_len, q_block,
          cache_k=None, cache_v=None, rope=False, emit_kv=False, fused=False):
    t, d = x.shape
    hd, kvh = HEAD_DIM, N_KV_HEADS
    n_heads = w_o.shape[0] // hd
    group = n_heads // kvh
    gw = group * hd
    tm = TOKEN_TILE
    assert t % tm == 0 and tm % seq_len == 0 and seq_len % q_block == 0
    past_len = 0 if cache_k is None else cache_k.shape[1] // kvh
    assert past_len == 0 or tm == seq_len
    gs = kvh if fused else 1
    lead = (kvh,) if fused else ()
    in_specs = [
        pl.BlockSpec((tm, d), lambda i, g: (i, 0)),
        pl.BlockSpec((None, None, 3 * N_SUB, d), lambda i, g: (layer, row_of_tile(i), 0, 0)),
        pl.BlockSpec((None, N_SUB, d), lambda i, g: (layer, 0, 0)),
        pl.BlockSpec((d, gs * gw), lambda i, g: (0, g)),
        pl.BlockSpec((d, gs * hd), lambda i, g: (0, n_heads // gs + g)),
        pl.BlockSpec((d, gs * hd), lambda i, g: (0, (n_heads + kvh) // gs + g)),
        pl.BlockSpec((1, hd), lambda i, g: (0, 0)),
        pl.BlockSpec((1, hd), lambda i, g: (0, 0)),
        pl.BlockSpec((gs * gw, d), lambda i, g: (g, 0)),
    ]
    args = [x, m, norm_g, w_qkv, w_qkv, w_qkv, q_g.reshape(1, hd), k_g.reshape(1, hd), w_o]
    if rope:
        assert tm == seq_len
        cos, sin = _rope_tables(seq_len, hd)
        in_specs += [pl.BlockSpec((tm, hd), lambda i, g: (0, 0))] * 2
        args += [cos, sin]
    if past_len:
        in_specs += [pl.BlockSpec((None, past_len * kvh, hd), lambda i, g: (i, 0, 0))] * 2
        args += [cache_k, cache_v]
    out_specs = [pl.BlockSpec((tm, d), lambda i, g: (i, 0))]
    out_shape = [jax.ShapeDtypeStruct((t, d), F32)]
    if emit_kv:
        out_specs += [pl.BlockSpec((tm * kvh, hd), lambda i, g: (i, 0))] * 2
        out_shape += [jax.ShapeDtypeStruct((t * kvh, hd), F32)] * 2
    nkeys = past_len + tm
    outs = pl.pallas_call(
        functools.partial(_attn_kernel, seq_len=seq_len, q_block=q_block, past_len=past_len,
                          rope=rope, emit_kv=emit_kv, group=group, kvh=kvh, fused=fused),
        grid=(t // tm, kvh // gs),
        in_specs=in_specs,
        out_specs=out_specs,
        out_shape=out_shape,
        scratch_shapes=([] if fused else [pltpu.VMEM((tm, d), BF16), pltpu.VMEM((tm, d), F32)])
        + [pltpu.VMEM(lead + (tm, gw), BF16), pltpu.VMEM(lead + (nkeys, hd), BF16),
           pltpu.VMEM(lead + (nkeys, hd), BF16), pltpu.VMEM(lead + (tm, gw), BF16)],
        compiler_params=_params("arbitrary", "arbitrary"),
        name="gqa",
    )(*args)
    return outs


def kernel(x_prompt, x_sample, c, state_lru, cache_k, cache_v, c_ctx, mod_w, mod_b, norm_g,
           ffn_w_gu, ffn_w_down, lru_w_in, lru_conv_w, lru_conv_b, lru_gate_w, lru_gate_b,
           lru_lambda, lru_w_out, att_w_qkv, att_q_g, att_k_g, att_w_o, final_g):
    b, s, d = x_prompt.shape
    db, ds, _ = x_sample.shape
    depth = mod_w.shape[0]
    n_mixers = 2
    assert ds % TOKEN_TILE == 0 and 1 + db <= SUBLANES

    xp = x_prompt.reshape(b * s, d)
    xs = x_sample.reshape(db * ds, d)
    cvecs = jnp.concatenate([c_ctx[None], c, jnp.zeros((SUBLANES - 1 - db, d), F32)], axis=0)
    m = _modulation(cvecs, mod_w, mod_b).reshape(depth, SUBLANES, 3 * N_SUB, d)

    tiles_per_sample = ds // TOKEN_TILE
    prompt_row = lambda i: 0
    sample_row = lambda i: 1 + i // tiles_per_sample
    prompt_tiles = (b * s) // TOKEN_TILE
    both_row = lambda i: jnp.where(i < prompt_tiles, 0, 1 + (i - prompt_tiles) // tiles_per_sample)

    new_states, new_k, new_v = [], [], []
    for layer in range(depth):
        j = layer // n_mixers
        last = layer == depth - 1
        xp, xs = _ffn(xp, xs, m, both_row, norm_g, ffn_w_gu, ffn_w_down, layer, 0, 0)
        if layer % n_mixers == 0:
            lru_p = (lru_w_in[j], lru_conv_w[j], lru_conv_b[j], lru_gate_w[j], lru_gate_b[j],
                     lru_lambda[j], lru_w_out[j])
            xp, st = _lru(xp, m, prompt_row, norm_g, layer, *lru_p, seq_len=s, h0=None,
                          emit_state=True, single_step=True)
            new_states.append(st)
            xs, _ = _lru(xs, m, sample_row, norm_g, layer, *lru_p, seq_len=ds,
                         h0=state_lru[:, j], emit_state=False, single_step=True)
        else:
            att_p = (att_w_qkv[j], att_q_g[j], att_k_g[j], att_w_o[j])
            xp, kp, vp = _attn(xp, m, prompt_row, norm_g, layer, *att_p, seq_len=s, q_block=s,
                               emit_kv=True, fused=True)
            new_k.append(kp.reshape(b, s, N_KV_HEADS, HEAD_DIM))
            new_v.append(vp.reshape(b, s, N_KV_HEADS, HEAD_DIM))
            past = cache_k.shape[2]
            ck = cache_k[:, j].reshape(db, past * N_KV_HEADS, HEAD_DIM)
            cv = cache_v[:, j].reshape(db, past * N_KV_HEADS, HEAD_DIM)
            (xs,) = _attn(xs, m, sample_row, norm_g, layer, *att_p, seq_len=ds, q_block=128,
                          cache_k=ck, cache_v=cv, rope=True)
        fg = final_g if last else None
        xp, xs = _ffn(xp, xs, m, both_row, norm_g, ffn_w_gu, ffn_w_down, layer, 1, 2, final_g=fg)

    y_prompt = xp.reshape(b, s, d)
    y_sample = xs.reshape(db, ds, d)
    return (y_prompt, y_sample, jnp.stack(new_states, axis=1), jnp.stack(new_k, axis=1),
            jnp.stack(new_v, axis=1))
```

```python
import functools

import jax
import jax.numpy as jnp
import numpy as np
from jax import lax
from jax.experimental import pallas as pl
from jax.experimental.pallas import tpu as pltpu

F32 = jnp.float32
BF16 = jnp.bfloat16

EPS = 1e-6
LRU_C = 8.0
LOG2_E = 1.4426950408889634
GRID_W = 64
ROPE_THETA = 10000.0
N_SUB = 3
HEAD_DIM = 128
N_KV_HEADS = 2

V7X_VMEM_LIMIT_BYTES = 56 * 1024 * 1024
SUBLANES = 8
TOKEN_TILE = 1024
FF_CHUNK = 256
FFN_ROW_BLOCK = 256
FFN_STAGE_SLOTS = 3
LRU_CHUNK = 512
LRU_GATE_CHUNK = 256
MOD_CHUNK = 1536


def _params(*semantics):
    return pltpu.CompilerParams(dimension_semantics=semantics,
                                vmem_limit_bytes=V7X_VMEM_LIMIT_BYTES)


def _mm(a_bf16, w_f32):
    return jnp.dot(a_bf16, w_f32.astype(BF16), preferred_element_type=F32)


def _rms(x):
    return x * lax.rsqrt(jnp.mean(x * x, axis=-1, keepdims=True) + EPS)


def _sub_in(x, m_ref, g_ref, sidx):
    shift = m_ref[3 * sidx:3 * sidx + 1, :]
    scale = m_ref[3 * sidx + 1:3 * sidx + 2, :]
    return (_rms(x) * g_ref[sidx:sidx + 1, :]) * (1.0 + scale) + shift


def _gate(m_ref, sidx):
    return m_ref[3 * sidx + 2:3 * sidx + 3, :]


def _mod_kernel(cv_ref, w_ref, b_ref, o_ref):
    cv = cv_ref[...]
    act = (cv * jax.nn.sigmoid(cv)).astype(BF16)
    o_ref[...] = _mm(act, w_ref[...]) + b_ref[...]


def _modulation(cvecs, mod_w, mod_b):
    depth, d, n = mod_w.shape
    rows = cvecs.shape[0]
    tn = MOD_CHUNK
    assert n % tn == 0
    return pl.pallas_call(
        _mod_kernel,
        grid=(depth, n // tn),
        in_specs=[
            pl.BlockSpec((rows, d), lambda l, j: (0, 0)),
            pl.BlockSpec((None, d, tn), lambda l, j: (l, 0, j)),
            pl.BlockSpec((None, 1, tn), lambda l, j: (l, 0, j)),
        ],
        out_specs=pl.BlockSpec((None, rows, tn), lambda l, j: (l, 0, j)),
        out_shape=jax.ShapeDtypeStruct((depth, rows, n), F32),
        compiler_params=_params("arbitrary", "arbitrary"),
        name="modulation",
    )(cvecs, mod_w, mod_b.reshape(depth, 1, n))


def _ffn_kernel(*refs, layer, s, sidx, final, n_first):
    if final:
        (m_ref, g_ref, fg_ref, xa_hbm, xb_hbm, wgu_hbm, wd_hbm, oa_hbm, ob_hbm,
         wgu_res, wd_res, act_ref, h0_ref, act0_ref, gu_buf, wd_buf, xbuf, obuf,
         w_sem, xsem, osem) = refs
    else:
        (m_ref, g_ref, xa_hbm, xb_hbm, wgu_hbm, wd_hbm, oa_hbm, ob_hbm,
         wgu_res, wd_res, act_ref, h0_ref, act0_ref, gu_buf, wd_buf, xbuf, obuf,
         w_sem, xsem, osem) = refs
    i = pl.program_id(0)
    n = pl.num_programs(0)
    tm = xbuf.shape[1]
    _, d, tf = wgu_res.shape
    f = wd_res.shape[0]
    nk = f // tf
    rb = act_ref.shape[0]
    nslot = gu_buf.shape[0]
    ahead = nslot - 1
    slot = lax.rem(i, 2)
    other = 1 - slot

    def stage_copies(k, sl):
        gcols = pl.ds(pl.multiple_of(k * tf, tf), tf)
        ucols = pl.ds(pl.multiple_of(f + k * tf, tf), tf)
        return (
            pltpu.make_async_copy(wgu_hbm.at[layer, s, :, gcols], gu_buf.at[sl, 0], w_sem.at[sl, 0]),
            pltpu.make_async_copy(wgu_hbm.at[layer, s, :, ucols], gu_buf.at[sl, 1], w_sem.at[sl, 1]),
            pltpu.make_async_copy(wd_hbm.at[layer, s, gcols, :], wd_buf.at[sl], w_sem.at[sl, 2]),
        )

    def tile_rows(hbm, tile):
        return hbm.at[pl.ds(pl.multiple_of(tile * tm, tm), tm), :]

    def x_copy(which, tile, sl):
        return pltpu.make_async_copy(tile_rows((xa_hbm, xb_hbm)[which], tile), xbuf.at[sl],
                                     xsem.at[sl])

    def o_copy(which, tile, sl):
        return pltpu.make_async_copy(obuf.at[sl], tile_rows((oa_hbm, ob_hbm)[which], tile),
                                     osem.at[sl])

    def start_by_stream(make, tile, sl):
        @pl.when(tile < n_first)
        def _():
            make(0, tile, sl).start()

        @pl.when(tile >= n_first)
        def _():
            make(1, tile - n_first, sl).start()

    @pl.when(i == 0)
    def _():
        x_copy(0, 0, 0).start()
        for k in range(ahead):
            for cp in stage_copies(k, k):
                cp.start()

    @pl.when(i + 1 < n)
    def _():
        start_by_stream(x_copy, i + 1, other)

    x_copy(0, 0, slot).wait()

    @pl.when(i >= 2)
    def _():
        o_copy(0, 0, slot).wait()

    def swiglu(h, wg, wu):
        gt = jnp.dot(h, wg, preferred_element_type=F32)
        up = jnp.dot(h, wu, preferred_element_type=F32)
        return ((gt * jax.nn.sigmoid(gt)) * up).astype(BF16)

    def down_and_store(x, rows):
        ff = jnp.dot(act_ref[...], wd_res[...], preferred_element_type=F32)
        y = x + (0.5 * _gate(m_ref, sidx)) * ff
        if final:
            y = _rms(y) * fg_ref[...]
        obuf[slot, rows, :] = y

    def row_block(r, carry):
        rows = pl.ds(pl.multiple_of(r * rb, rb), rb)
        x = xbuf[slot, rows, :]
        h = _sub_in(x, m_ref, g_ref, sidx).astype(BF16)
        for k in range(nk):
            act_ref[:, k * tf:(k + 1) * tf] = swiglu(h, wgu_res[k], wgu_res[nk + k])
        down_and_store(x, rows)
        return carry

    def first_tile_chunk(k, carry):
        sl = lax.rem(k, nslot)
        for cp in stage_copies(k, sl):
            cp.wait()

        @pl.when(k + ahead < nk)
        def _():
            for cp in stage_copies(k + ahead, lax.rem(k + ahead, nslot)):
                cp.start()

        wg = gu_buf[sl, 0].astype(BF16)
        wu = gu_buf[sl, 1].astype(BF16)
        wgu_res[k] = wg
        wgu_res[nk + k] = wu
        wd_res[pl.ds(pl.multiple_of(k * tf, tf), tf), :] = wd_buf[sl].astype(BF16)
        act0_ref[k] = swiglu(h0_ref[...], wg, wu)
        return carry

    def first_tile_down(r, carry):
        rows = pl.ds(pl.multiple_of(r * rb, rb), rb)
        for k in range(nk):
            act_ref[:, k * tf:(k + 1) * tf] = act0_ref[k, rows, :]
        down_and_store(xbuf[slot, rows, :], rows)
        return carry

    @pl.when(i == 0)
    def _():
        h0_ref[...] = _sub_in(xbuf[slot], m_ref, g_ref, sidx).astype(BF16)
        lax.fori_loop(0, nk, first_tile_chunk, 0, unroll=2)
        lax.fori_loop(0, tm // rb, first_tile_down, 0, unroll=2)

    @pl.when(i > 0)
    def _():
        lax.fori_loop(0, tm // rb, row_block, 0, unroll=2)

    start_by_stream(o_copy, i, slot)

    @pl.when(i == n - 1)
    def _():
        o_copy(0, 0, other).wait()
        o_copy(0, 0, slot).wait()


def _ffn(xa, xb, m, row_of_tile, norm_g, w_gu, w_down, layer, s, sidx, final_g=None):
    (ta, d), tb = xa.shape, xb.shape[0]
    f = w_down.shape[2]
    tm, tf, rb = TOKEN_TILE, FF_CHUNK, FFN_ROW_BLOCK
    assert ta % tm == 0 and tb % tm == 0 and f % tf == 0 and (ta + tb) // tm >= 2
    assert tm % rb == 0 and f // tf >= FFN_STAGE_SLOTS
    final = final_g is not None
    in_specs = [
        pl.BlockSpec((None, None, 3 * N_SUB, d), lambda i: (layer, row_of_tile(i), 0, 0)),
        pl.BlockSpec((None, N_SUB, d), lambda i: (layer, 0, 0)),
    ]
    args = [m, norm_g]
    if final:
        in_specs.append(pl.BlockSpec((1, d), lambda i: (0, 0)))
        args.append(final_g.reshape(1, d))
    in_specs += [pl.BlockSpec(memory_space=pl.ANY)] * 4
    args += [xa, xb, w_gu, w_down]
    return pl.pallas_call(
        functools.partial(_ffn_kernel, layer=layer, s=s, sidx=sidx, final=final,
                          n_first=ta // tm),
        grid=((ta + tb) // tm,),
        in_specs=in_specs,
        out_specs=[pl.BlockSpec(memory_space=pl.ANY)] * 2,
        out_shape=[jax.ShapeDtypeStruct((ta, d), F32), jax.ShapeDtypeStruct((tb, d), F32)],
        scratch_shapes=[
            pltpu.VMEM((2 * f // tf, d, tf), BF16), pltpu.VMEM((f, d), BF16),
            pltpu.VMEM((rb, f), BF16),
            pltpu.VMEM((tm, d), BF16), pltpu.VMEM((f // tf, tm, tf), BF16),
            pltpu.VMEM((FFN_STAGE_SLOTS, 2, d, tf), F32),
            pltpu.VMEM((FFN_STAGE_SLOTS, tf, d), F32),
            pltpu.VMEM((2, tm, d), F32), pltpu.VMEM((2, tm, d), F32),
            pltpu.SemaphoreType.DMA((FFN_STAGE_SLOTS, 3)),
            pltpu.SemaphoreType.DMA((2,)),
            pltpu.SemaphoreType.DMA((2,)),
        ],
        compiler_params=_params("arbitrary"),
        name="ffn",
    )(*args)


def _gelu_tanh(x):
    c = np.float32(np.sqrt(2.0 / np.pi))
    return x * (0.5 * (1.0 + jnp.tanh(c * (x + 0.044715 * (x * x * x)))))


_SWAP_BLOCK = SUBLANES * SUBLANES
_SWAP_SLAB = 4 * _SWAP_BLOCK


def _block_swap_matrix():
    p = np.arange(_SWAP_SLAB)
    blk, a, b = p // _SWAP_BLOCK, (p % _SWAP_BLOCK) // SUBLANES, p % SUBLANES
    mat = np.zeros((_SWAP_SLAB, _SWAP_SLAB), np.float32)
    mat[p, blk * _SWAP_BLOCK + b * SUBLANES + a] = 1.0
    return mat


def _swap_blocks(q, rows):
    parts = [jnp.dot(q, rows[lo:lo + _SWAP_SLAB, :], preferred_element_type=F32)
             for lo in range(0, rows.shape[0], _SWAP_SLAB)]
    return jnp.concatenate(parts, axis=0) if len(parts) > 1 else parts[0]


def _regroup(rows, seq_len, to_permuted):
    segs = seq_len // SUBLANES
    pieces = []
    for lo in range(0, rows.shape[0], seq_len):
        outer, inner = (segs // SUBLANES, SUBLANES) if to_permuted else (SUBLANES, segs // SUBLANES)
        for o in range(outer):
            for i in range(inner):
                jh, s = (o, i) if to_permuted else (i, o)
                src = s * segs + jh * SUBLANES if to_permuted else jh * _SWAP_BLOCK + s * SUBLANES
                pieces.append(rows[lo + src:lo + src + SUBLANES, :])
    return jnp.concatenate(pieces, axis=0)


def _to_permuted_time(q, h_f32, seq_len):
    return _swap_blocks(q, _regroup(h_f32, seq_len, True).astype(BF16)).astype(BF16)


def _to_natural_time(q, y_bf16, seq_len):
    return _regroup(_swap_blocks(q, y_bf16), seq_len, False).astype(BF16)


def _lru_kernel(*refs, seq_len, has_h0, emit_state, conv_left, single_step):
    refs = list(refs)
    (x_ref, m_ref, g_ref, wx_ref, wy_ref, cw_ref, cb_ref, gw_ref, gb_ref, lam_ref,
     wo_ref, swap_ref) = refs[:12]
    pos = 12
    h0_ref = None
    if has_h0:
        h0_ref = refs[pos]
        pos += 1
    o_ref = refs[pos]
    pos += 1
    st_ref = None
    if emit_state:
        st_ref = refs[pos]
        pos += 1
    if single_step:
        af_ref, uf_ref, ab_ref, ub_ref = refs[pos:]
    else:
        h_ref, acc_ref, af_ref, uf_ref, ab_ref, ub_ref = refs[pos:]

    tm, cw_cols = af_ref.shape
    nseq = tm // seq_len
    segs = seq_len // SUBLANES

    def permuted_input():
        return _to_permuted_time(swap_ref[...], _sub_in(x_ref[...], m_ref, g_ref, 1), seq_len)

    if single_step:
        h = permuted_input()
    else:
        c = pl.program_id(1)

        @pl.when(c == 0)
        def _():
            h_ref[...] = permuted_input()
            acc_ref[...] = jnp.zeros_like(acc_ref)

        h = h_ref[...]
    xb = _mm(h, wx_ref[...])
    yb = _mm(h, wy_ref[...])

    sub = lax.broadcasted_iota(jnp.int32, (SUBLANES, cw_cols), 0)

    def next_segment(grp):
        return jnp.where(sub < SUBLANES - 1, pltpu.roll(grp, SUBLANES - 1, 0), 0.0)

    def prev_segment(grp):
        return jnp.where(sub > 0, pltpu.roll(grp, 1, 0), 0.0)

    def tap(seq, off):
        n = abs(off) * SUBLANES
        if off == 0:
            return seq
        if off > 0:
            edge = [next_segment(seq[g * SUBLANES:(g + 1) * SUBLANES, :]) for g in range(off)]
            return jnp.concatenate([seq[n:, :]] + edge, axis=0)
        start = seq_len - n
        edge = [prev_segment(seq[start + g * SUBLANES:start + (g + 1) * SUBLANES, :])
                for g in range(-off)]
        return jnp.concatenate(edge + [seq[:start, :]], axis=0)

    cw = cw_ref[...]
    assert cw.shape[0] - 1 < segs
    xcs = []
    for sq in range(nseq):
        seq = xb[sq * seq_len:(sq + 1) * seq_len, :]
        acc = cb_ref[...]
        for k in range(cw.shape[0]):
            acc = acc + tap(seq, k - conv_left) * cw[k:k + 1, :]
        xcs.append(acc)
    xc = jnp.concatenate(xcs, axis=0) if nseq > 1 else xcs[0]

    gb = gb_ref[...]
    lam = lam_ref[...]
    gc = gw_ref.shape[1]
    for ch in range(cw_cols // gc):
        cols = slice(ch * gc, (ch + 1) * gc)
        xcc = xc[:, cols]
        gl = jnp.dot(xcc.astype(BF16), gw_ref[ch], preferred_element_type=F32)
        for d, (a_ref, u_ref) in enumerate(((af_ref, uf_ref), (ab_ref, ub_ref))):
            tr = jnp.tanh(gl[:, (2 * d) * gc:(2 * d + 1) * gc] + 0.5 * gb[2 * d:2 * d + 1, cols])
            ig = 0.5 * jnp.tanh(gl[:, (2 * d + 1) * gc:(2 * d + 2) * gc]
                                + 0.5 * gb[2 * d + 1:2 * d + 2, cols]) + 0.5
            lm = lam[d:d + 1, cols]
            log_sig = jnp.minimum(lm, 0.0) - jnp.log1p(jnp.exp(-jnp.abs(lm)))
            c2 = (0.5 * LRU_C * LOG2_E) * log_sig
            a = jnp.exp2(c2 * tr + c2)
            v = 1.0 - a * a
            u = jnp.where(v > 0.0, v * lax.rsqrt(v), 0.0) * (ig * xcc)
            a_ref[:, cols] = a
            u_ref[:, cols] = u

    zero = jnp.zeros((SUBLANES, cw_cols), F32)
    one = jnp.ones((SUBLANES, cw_cols), F32)

    def local_scan(j, carry):
        out = []
        for sq in range(nseq):
            for dd, (a_ref, u_ref) in enumerate(((af_ref, uf_ref), (ab_ref, ub_ref))):
                grp = j if dd == 0 else segs - 1 - j
                rows = pl.ds(pl.multiple_of(sq * seq_len + grp * SUBLANES, SUBLANES), SUBLANES)
                hl, pc = carry[2 * (2 * sq + dd)], carry[2 * (2 * sq + dd) + 1]
                a = a_ref[rows, :]
                hl = a * hl + u_ref[rows, :]
                pc = a * pc
                u_ref[rows, :] = hl
                a_ref[rows, :] = pc
                out += [hl, pc]
        return tuple(out)

    totals = lax.fori_loop(0, segs, local_scan, (zero, one) * (2 * nseq), unroll=True)

    gelu_y = _gelu_tanh(yb)
    ys = []
    for sq in range(nseq):
        entry = []
        for dd in range(2):
            hl, pc = totals[2 * (2 * sq + dd)], totals[2 * (2 * sq + dd) + 1]
            state = h0_ref[sq, dd:dd + 1, :] if has_h0 else jnp.zeros((1, cw_cols), F32)
            rows = [None] * SUBLANES
            order = range(SUBLANES) if dd == 0 else range(SUBLANES - 1, -1, -1)
            for s in order:
                rows[s] = state
                state = pc[s:s + 1, :] * state + hl[s:s + 1, :]
            entry.append(jnp.concatenate(rows, axis=0))
            if emit_state:
                st_ref[sq, dd:dd + 1, :] = state
        lo, hi = sq * seq_len, (sq + 1) * seq_len
        grouped = (segs, SUBLANES, cw_cols)
        hsum = ((uf_ref[lo:hi, :].reshape(grouped) + af_ref[lo:hi, :].reshape(grouped) * entry[0][None])
                + (ub_ref[lo:hi, :].reshape(grouped) + ab_ref[lo:hi, :].reshape(grouped) * entry[1][None]))
        ys.append(hsum.reshape(seq_len, cw_cols) * gelu_y[lo:hi, :])
    y = jnp.concatenate(ys, axis=0) if nseq > 1 else ys[0]

    y = _to_natural_time(swap_ref[...], y.astype(BF16), seq_len)
    out = _mm(y, wo_ref[...])
    if single_step:
        o_ref[...] = x_ref[...] + _gate(m_ref, 1) * out
        return
    acc_ref[...] += out

    @pl.when(c == pl.num_programs(1) - 1)
    def _():
        o_ref[...] = x_ref[...] + _gate(m_ref, 1) * acc_ref[...]


def _lru_gate_weights(gate_w):
    nd, ng, nb, bw, _ = gate_w.shape
    per = LRU_GATE_CHUNK // bw
    nc = nb // per
    w = (0.5 * gate_w).astype(BF16).reshape(nd * ng, nc, per, bw, bw)
    rows = []
    for n in range(per):
        blk = jnp.transpose(w[:, :, n], (1, 2, 0, 3))
        blk = jnp.pad(blk, ((0, 0), (0, 0), (0, 0), (n * bw, (per - 1 - n) * bw)))
        rows.append(blk.reshape(nc, bw, nd * ng * per * bw))
    return jnp.concatenate(rows, axis=1)


def _lru(x, m, row_of_tile, norm_g, layer, w_in, conv_w, conv_b, gate_w, gate_b, lam, w_out,
         seq_len, h0, emit_state, single_step=False):
    t, d = x.shape
    r = w_out.shape[0]
    tm, gc = TOKEN_TILE, LRU_GATE_CHUNK
    cb = r if single_step else LRU_CHUNK
    assert t % tm == 0 and tm % seq_len == 0 and r % cb == 0 and cb % gc == 0
    assert seq_len % _SWAP_BLOCK == 0 and tm % _SWAP_SLAB == 0
    nc = r // cb
    nseq = tm // seq_len
    has_h0 = h0 is not None
    resident = dict(pipeline_mode=pl.Buffered(1)) if single_step else {}
    in_specs = [
        pl.BlockSpec((tm, d), lambda i, c: (i, 0)),
        pl.BlockSpec((None, None, 3 * N_SUB, d), lambda i, c: (layer, row_of_tile(i), 0, 0)),
        pl.BlockSpec((None, N_SUB, d), lambda i, c: (layer, 0, 0)),
        pl.BlockSpec((d, cb), lambda i, c: (0, c), **resident),
        pl.BlockSpec((d, cb), lambda i, c: (0, nc + c), **resident),
        pl.BlockSpec((conv_w.shape[0], cb), lambda i, c: (0, c)),
        pl.BlockSpec((1, cb), lambda i, c: (0, c)),
        pl.BlockSpec((cb // gc, gc, 4 * gc), lambda i, c: (c, 0, 0), **resident),
        pl.BlockSpec((4, cb), lambda i, c: (0, c)),
        pl.BlockSpec((2, cb), lambda i, c: (0, c)),
        pl.BlockSpec((cb, d), lambda i, c: (c, 0), **resident),
        pl.BlockSpec((_SWAP_SLAB, _SWAP_SLAB), lambda i, c: (0, 0)),
    ]
    args = [x, m, norm_g, w_in, w_in, conv_w, conv_b.reshape(1, r), _lru_gate_weights(gate_w),
            gate_b.reshape(4, r), lam, w_out, jnp.asarray(_block_swap_matrix(), BF16)]
    if has_h0:
        in_specs.append(pl.BlockSpec((nseq, 2, cb), lambda i, c: (i, 0, c)))
        args.append(h0)
    out_specs = [pl.BlockSpec((tm, d), lambda i, c: (i, 0))]
    out_shape = [jax.ShapeDtypeStruct((t, d), F32)]
    if emit_state:
        out_specs.append(pl.BlockSpec((nseq, 2, cb), lambda i, c: (i, 0, c)))
        out_shape.append(jax.ShapeDtypeStruct((t // seq_len, 2, r), F32))
    outs = pl.pallas_call(
        functools.partial(_lru_kernel, seq_len=seq_len, has_h0=has_h0, emit_state=emit_state,
                          conv_left=(conv_w.shape[0] - 1) // 2, single_step=single_step),
        grid=(t // tm, nc),
        in_specs=in_specs,
        out_specs=out_specs,
        out_shape=out_shape,
        scratch_shapes=([] if single_step
                        else [pltpu.VMEM((tm, d), BF16), pltpu.VMEM((tm, d), F32)])
        + [pltpu.VMEM((tm, cb), F32)] * 4,
        compiler_params=_params("arbitrary", "arbitrary"),
        name="rglru",
    )(*args)
    return outs if emit_state else (outs[0], None)


def _rope(x, cos, sin_signed, lane):
    hd = x.shape[1]
    partner = jnp.where((lane & 32) == 0, pltpu.roll(x, hd - 32, 1), pltpu.roll(x, 32, 1))
    return x * cos + partner * sin_signed


def _attn_kernel(*refs, seq_len, q_block, past_len, rope, emit_kv, group, kvh, fused):
    refs = list(refs)
    x_ref, m_ref, g_ref, wq_ref, wk_ref, wv_ref, qg_ref, kg_ref, wo_ref = refs[:9]
    pos = 9
    if rope:
        cos_ref, sin_ref = refs[pos:pos + 2]
        pos += 2
    if past_len:
        ck_ref, cv_ref = refs[pos:pos + 2]
        pos += 2
    o_ref = refs[pos]
    pos += 1
    if emit_kv:
        kn_ref, vn_ref = refs[pos:pos + 2]
        pos += 2
    if fused:
        q_s, k_s, v_s, o_s = refs[pos:]
    else:
        h_ref, acc_ref, q_s, k_s, v_s, o_s = refs[pos:]

    tm = x_ref.shape[0]
    hd = k_s.shape[-1]
    gw = group * hd
    nqb = seq_len // q_block
    nchunks = (tm // seq_len) * nqb
    nk = past_len + seq_len
    scale = hd ** -0.5

    def group_out(h, g, slot, cols, kv_all=None):
        if fused:
            qs, ks, vs, os_ = q_s.at[slot], k_s.at[slot], v_s.at[slot], o_s.at[slot]
            wq, wo = wq_ref[:, cols * gw:(cols + 1) * gw], wo_ref[cols * gw:(cols + 1) * gw, :]
        else:
            qs, ks, vs, os_ = q_s, k_s, v_s, o_s
            wq, wo = wq_ref[...], wo_ref[...]
        q = _mm(h, wq)
        if kv_all is None:
            k_raw, v = _mm(h, wk_ref[...]), _mm(h, wv_ref[...])
        else:
            k_raw, v = (a[:, cols * hd:(cols + 1) * hd] for a in kv_all)
        k = _rms(k_raw) * kg_ref[...]
        if emit_kv:
            kn_ref[pl.ds(g, tm, stride=kvh), :] = k
            vn_ref[pl.ds(g, tm, stride=kvh), :] = v
        if rope:
            lane = lax.broadcasted_iota(jnp.int32, (tm, hd), 1)
            cos, sin = cos_ref[...], sin_ref[...]
            k = _rope(k, cos, sin, lane)
        ks[past_len:past_len + tm, :] = k.astype(BF16)
        vs[past_len:past_len + tm, :] = v.astype(BF16)
        if past_len:
            ks[0:past_len, :] = ck_ref[pl.ds(g, past_len, stride=kvh), :].astype(BF16)
            vs[0:past_len, :] = cv_ref[pl.ds(g, past_len, stride=kvh), :].astype(BF16)
        for j in range(group):
            qh = _rms(q[:, j * hd:(j + 1) * hd]) * qg_ref[...]
            if rope:
                qh = _rope(qh, cos, sin, lane)
            qs[:, j * hd:(j + 1) * hd] = qh.astype(BF16)

        def chunk(ci, carry):
            r0 = pl.multiple_of(ci * q_block, q_block)
            if past_len:
                keys, vals = ks[...], vs[...]
            else:
                koff = pl.multiple_of((ci // nqb) * seq_len, seq_len)
                keys, vals = ks[pl.ds(koff, nk), :], vs[pl.ds(koff, nk), :]
            qc = jnp.concatenate(
                [qs[pl.ds(r0, q_block), j * hd:(j + 1) * hd] for j in range(group)], axis=0)
            raw = lax.dot_general(qc, keys, (((1,), (1,)), ((), ())), preferred_element_type=F32)
            e = jnp.exp2((raw - jnp.max(raw, axis=-1, keepdims=True)) * (scale * LOG2_E))
            inv = 1.0 / jnp.sum(e, axis=-1, keepdims=True)
            oc = jnp.dot(e.astype(BF16), vals, preferred_element_type=F32) * inv
            for j in range(group):
                os_[pl.ds(r0, q_block), j * hd:(j + 1) * hd] = (
                    oc[j * q_block:(j + 1) * q_block, :].astype(BF16))
            return carry

        lax.fori_loop(0, nchunks, chunk, 0, unroll=min(nchunks, 4))
        return _mm(os_[...], wo)

    if fused:
        x = x_ref[...]
        h = _sub_in(x, m_ref, g_ref, 1).astype(BF16)
        kv_all = (_mm(h, wk_ref[...]), _mm(h, wv_ref[...]))
        acc = group_out(h, 0, 0, 0, kv_all)
        for g in range(1, kvh):
            acc = acc + group_out(h, g, g, g, kv_all)
        o_ref[...] = x + _gate(m_ref, 1) * acc
        return

    gi = pl.program_id(1)

    @pl.when(gi == 0)
    def _():
        h_ref[...] = _sub_in(x_ref[...], m_ref, g_ref, 1).astype(BF16)
        acc_ref[...] = jnp.zeros_like(acc_ref)

    out = group_out(h_ref[...], gi, 0, 0)
    acc_ref[...] += out

    @pl.when(gi == pl.num_programs(1) - 1)
    def _():
        o_ref[...] = x_ref[...] + _gate(m_ref, 1) * acc_ref[...]


def _rope_tables(n_tok, hd):
    rows = n_tok // GRID_W
    r_idx = jnp.broadcast_to(jnp.arange(rows)[:, None], (rows, GRID_W)).reshape(n_tok).astype(F32)
    c_idx = jnp.broadcast_to(jnp.arange(GRID_W)[None, :], (rows, GRID_W)).reshape(n_tok).astype(F32)
    n_freq = hd // 4
    inv = ROPE_THETA ** (-jnp.arange(n_freq, dtype=F32) / n_freq)
    ang = jnp.stack([r_idx[:, None] * inv, c_idx[:, None] * inv], axis=1)
    cos, sin = jnp.cos(ang), jnp.sin(ang)
    cos_full = jnp.concatenate([cos, cos], axis=-1).reshape(n_tok, hd)
    sin_signed = jnp.concatenate([-sin, sin], axis=-1).reshape(n_tok, hd)
    return cos_full, sin_signed


def _attn(x, m, row_of_tile, norm_g, layer, w_qkv, q_g, k_g, w_o, seq_len, q_block,
          cache_k=None, cache_v=None, rope=False, emit_kv=False, fused=False):
    t, d = x.shape
    hd, kvh = HEAD_DIM, N_KV_HEADS
    n_heads = w_o.shape[0] // hd
    group = n_heads // kvh
    gw = group * hd
    tm = TOKEN_TILE
    assert t % tm == 0 and tm % seq_len == 0 and seq_len % q_block == 0
    past_len = 0 if cache_k is None else cache_k.shape[1] // kvh
    assert past_len == 0 or tm == seq_len
    gs = kvh if fused else 1
    lead = (kvh,) if fused else ()
    in_specs = [
        pl.BlockSpec((tm, d), lambda i, g: (i, 0)),
        pl.BlockSpec((None, None, 3 * N_SUB, d), lambda i, g: (layer, row_of_tile(i), 0, 0)),
        pl.BlockSpec((None, N_SUB, d), lambda i, g: (layer, 0, 0)),
        pl.BlockSpec((d, gs * gw), lambda i, g: (0, g)),
        pl.BlockSpec((d, gs * hd), lambda i, g: (0, n_heads // gs + g)),
        pl.BlockSpec((d, gs * hd), lambda i, g: (0, (n_heads + kvh) // gs + g)),
        pl.BlockSpec((1, hd), lambda i, g: (0, 0)),
        pl.BlockSpec((1, hd), lambda i, g: (0, 0)),
        pl.BlockSpec((gs * gw, d), lambda i, g: (g, 0)),
    ]
    args = [x, m, norm_g, w_qkv, w_qkv, w_qkv, q_g.reshape(1, hd), k_g.reshape(1, hd), w_o]
    if rope:
        assert tm == seq_len
        cos, sin = _rope_tables(seq_len, hd)
        in_specs += [pl.BlockSpec((tm, hd), lambda i, g: (0, 0))] * 2
        args += [cos, sin]
    if past_len:
        in_specs += [pl.BlockSpec((None, past_len * kvh, hd), lambda i, g: (i, 0, 0))] * 2
        args += [cache_k, cache_v]
    out_specs = [pl.BlockSpec((tm, d), lambda i, g: (i, 0))]
    out_shape = [jax.ShapeDtypeStruct((t, d), F32)]
    if emit_kv:
        out_specs += [pl.BlockSpec((tm * kvh, hd), lambda i, g: (i, 0))] * 2
        out_shape += [jax.ShapeDtypeStruct((t * kvh, hd), F32)] * 2
    nkeys = past_len + tm
    outs = pl.pallas_call(
        functools.partial(_attn_kernel, seq_len=seq_len, q_block=q_block, past_len=past_len,
                          rope=rope, emit_kv=emit_kv, group=group, kvh=kvh, fused=fused),
        grid=(t // tm, kvh // gs),
        in_specs=in_specs,
        out_specs=out_specs,
        out_shape=out_shape,
        scratch_shapes=([] if fused else [pltpu.VMEM((tm, d), BF16), pltpu.VMEM((tm, d), F32)])
        + [pltpu.VMEM(lead + (tm, gw), BF16), pltpu.VMEM(lead + (nkeys, hd), BF16),
           pltpu.VMEM(lead + (nkeys, hd), BF16), pltpu.VMEM(lead + (tm, gw), BF16)],
        compiler_params=_params("arbitrary", "arbitrary"),
        name="gqa",
    )(*args)
    return outs


def kernel(x_prompt, x_sample, c, state_lru, cache_k, cache_v, c_ctx, mod_w, mod_b, norm_g,
           ffn_w_gu, ffn_w_down, lru_w_in, lru_conv_w, lru_conv_b, lru_gate_w, lru_gate_b,
           lru_lambda, lru_w_out, att_w_qkv, att_q_g, att_k_g, att_w_o, final_g):
    b, s, d = x_prompt.shape
    db, ds, _ = x_sample.shape
    depth = mod_w.shape[0]
    n_mixers = 2
    assert ds % TOKEN_TILE == 0 and 1 + db <= SUBLANES

    xp = x_prompt.reshape(b * s, d)
    xs = x_sample.reshape(db * ds, d)
    cvecs = jnp.concatenate([c_ctx[None], c, jnp.zeros((SUBLANES - 1 - db, d), F32)], axis=0)
    m = _modulation(cvecs, mod_w, mod_b).reshape(depth, SUBLANES, 3 * N_SUB, d)

    tiles_per_sample = ds // TOKEN_TILE
    prompt_row = lambda i: 0
    sample_row = lambda i: 1 + i // tiles_per_sample
    prompt_tiles = (b * s) // TOKEN_TILE
    both_row = lambda i: jnp.where(i < prompt_tiles, 0, 1 + (i - prompt_tiles) // tiles_per_sample)

    new_states, new_k, new_v = [], [], []
    for layer in range(depth):
        j = layer // n_mixers
        last = layer == depth - 1
        xp, xs = _ffn(xp, xs, m, both_row, norm_g, ffn_w_gu, ffn_w_down, layer, 0, 0)
        if layer % n_mixers == 0:
            lru_p = (lru_w_in[j], lru_conv_w[j], lru_conv_b[j], lru_gate_w[j], lru_gate_b[j],
                     lru_lambda[j], lru_w_out[j])
            xp, st = _lru(xp, m, prompt_row, norm_g, layer, *lru_p, seq_len=s, h0=None,
                          emit_state=True, single_step=True)
            new_states.append(st)
            xs, _ = _lru(xs, m, sample_row, norm_g, layer, *lru_p, seq_len=ds,
                         h0=state_lru[:, j], emit_state=False, single_step=True)
        else:
            att_p = (att_w_qkv[j], att_q_g[j], att_k_g[j], att_w_o[j])
            xp, kp, vp = _attn(xp, m, prompt_row, norm_g, layer, *att_p, seq_len=s, q_block=s,
                               emit_kv=True, fused=True)
            new_k.append(kp.reshape(b, s, N_KV_HEADS, HEAD_DIM))
            new_v.append(vp.reshape(b, s, N_KV_HEADS, HEAD_DIM))
            past = cache_k.shape[2]
            ck = cache_k[:, j].reshape(db, past * N_KV_HEADS, HEAD_DIM)
            cv = cache_v[:, j].reshape(db, past * N_KV_HEADS, HEAD_DIM)
            (xs,) = _attn(xs, m, sample_row, norm_g, layer, *att_p, seq_len=ds, q_block=128,
                          cache_k=ck, cache_v=cv, rope=True)
        fg = final_g if last else None
        xp, xs = _ffn(xp, xs, m, both_row, norm_g, ffn_w_gu, ffn_w_down, layer, 1, 2, final_g=fg)

    y_prompt = xp.reshape(b, s, d)
    y_sample = xs.reshape(db, ds, d)
    return (y_prompt, y_sample, jnp.stack(new_states, axis=1), jnp.stack(new_k, axis=1),
            jnp.stack(new_v, axis=1))
```

```python
import functools

import jax
import jax.numpy as jnp
import numpy as np
from jax import lax
from jax.experimental import pallas as pl
from jax.experimental.pallas import tpu as pltpu

F32 = jnp.float32
BF16 = jnp.bfloat16

EPS = 1e-6
LRU_C = 8.0
LOG2_E = 1.4426950408889634
GRID_W = 64
ROPE_THETA = 10000.0
N_SUB = 3
HEAD_DIM = 128
N_KV_HEADS = 2

V7X_VMEM_LIMIT_BYTES = 56 * 1024 * 1024
SUBLANES = 8
TOKEN_TILE = 1024
FF_CHUNK = 256
FFN_ROW_BLOCK = 256
FFN_STAGE_SLOTS = 3
LRU_CHUNK = 512
LRU_GATE_CHUNK = 256
MOD_CHUNK = 768
MOD_SLOTS = 4


def _params(*semantics):
    return pltpu.CompilerParams(dimension_semantics=semantics,
                                vmem_limit_bytes=V7X_VMEM_LIMIT_BYTES)


def _mm(a_bf16, w_f32):
    return jnp.dot(a_bf16, w_f32.astype(BF16), preferred_element_type=F32)


def _rms(x):
    return x * lax.rsqrt(jnp.mean(x * x, axis=-1, keepdims=True) + EPS)


def _sub_in(x, m_ref, g_ref, sidx):
    shift = m_ref[3 * sidx:3 * sidx + 1, :]
    scale = m_ref[3 * sidx + 1:3 * sidx + 2, :]
    return (_rms(x) * g_ref[sidx:sidx + 1, :]) * (1.0 + scale) + shift


def _gate(m_ref, sidx):
    return m_ref[3 * sidx + 2:3 * sidx + 3, :]


def _mod_kernel(cv_ref, b_ref, w_hbm, o_ref, wbuf, sem):
    depth, _, n = o_ref.shape
    nslot, _, tn = wbuf.shape
    per_layer = n // tn
    count = depth * per_layer

    def chunk_copy(q):
        l, j = divmod(q, per_layer)
        return pltpu.make_async_copy(w_hbm.at[l, :, pl.ds(j * tn, tn)], wbuf.at[q % nslot],
                                     sem.at[q % nslot])

    for q in range(min(nslot - 1, count)):
        chunk_copy(q).start()
    cv = cv_ref[...]
    act = (cv * jax.nn.sigmoid(cv)).astype(BF16)
    for q in range(count):
        chunk_copy(q).wait()
        if q + nslot - 1 < count:
            chunk_copy(q + nslot - 1).start()
        l, j = divmod(q, per_layer)
        cols = slice(j * tn, (j + 1) * tn)
        o_ref[l, :, cols] = _mm(act, wbuf[q % nslot]) + b_ref[l, :, cols]


def _modulation(cvecs, mod_w, mod_b):
    depth, d, n = mod_w.shape
    rows = cvecs.shape[0]
    tn = MOD_CHUNK
    assert n % tn == 0
    return pl.pallas_call(
        _mod_kernel,
        in_specs=[
            pl.BlockSpec(memory_space=pltpu.VMEM),
            pl.BlockSpec(memory_space=pltpu.VMEM),
            pl.BlockSpec(memory_space=pl.ANY),
        ],
        out_specs=pl.BlockSpec(memory_space=pltpu.VMEM),
        out_shape=jax.ShapeDtypeStruct((depth, rows, n), F32),
        scratch_shapes=[pltpu.VMEM((MOD_SLOTS, d, tn), F32), pltpu.SemaphoreType.DMA((MOD_SLOTS,))],
        compiler_params=pltpu.CompilerParams(vmem_limit_bytes=V7X_VMEM_LIMIT_BYTES),
        name="modulation",
    )(cvecs, mod_b.reshape(depth, 1, n), mod_w)


def _ffn_kernel(*refs, layer, s, sidx, final, n_first):
    if final:
        (m_ref, g_ref, fg_ref, xa_hbm, xb_hbm, wgu_hbm, wd_hbm, oa_hbm, ob_hbm,
         wgu_res, wd_res, act_ref, h0_ref, act0_ref, gu_buf, wd_buf, xbuf, obuf,
         w_sem, xsem, osem) = refs
    else:
        (m_ref, g_ref, xa_hbm, xb_hbm, wgu_hbm, wd_hbm, oa_hbm, ob_hbm,
         wgu_res, wd_res, act_ref, h0_ref, act0_ref, gu_buf, wd_buf, xbuf, obuf,
         w_sem, xsem, osem) = refs
    i = pl.program_id(0)
    n = pl.num_programs(0)
    tm = xbuf.shape[1]
    _, d, tf = wgu_res.shape
    f = wd_res.shape[0]
    nk = f // tf
    rb = act_ref.shape[0]
    nslot = gu_buf.shape[0]
    ahead = nslot - 1
    slot = lax.rem(i, 2)
    other = 1 - slot

    def stage_copies(k, sl):
        gcols = pl.ds(pl.multiple_of(k * tf, tf), tf)
        ucols = pl.ds(pl.multiple_of(f + k * tf, tf), tf)
        return (
            pltpu.make_async_copy(wgu_hbm.at[layer, s, :, gcols], gu_buf.at[sl, 0], w_sem.at[sl, 0]),
            pltpu.make_async_copy(wgu_hbm.at[layer, s, :, ucols], gu_buf.at[sl, 1], w_sem.at[sl, 1]),
            pltpu.make_async_copy(wd_hbm.at[layer, s, gcols, :], wd_buf.at[sl], w_sem.at[sl, 2]),
        )

    def tile_rows(hbm, tile):
        return hbm.at[pl.ds(pl.multiple_of(tile * tm, tm), tm), :]

    def x_copy(which, tile, sl):
        return pltpu.make_async_copy(tile_rows((xa_hbm, xb_hbm)[which], tile), xbuf.at[sl],
                                     xsem.at[sl])

    def o_copy(which, tile, sl):
        return pltpu.make_async_copy(obuf.at[sl], tile_rows((oa_hbm, ob_hbm)[which], tile),
                                     osem.at[sl])

    def start_by_stream(make, tile, sl):
        @pl.when(tile < n_first)
        def _():
            make(0, tile, sl).start()

        @pl.when(tile >= n_first)
        def _():
            make(1, tile - n_first, sl).start()

    @pl.when(i == 0)
    def _():
        x_copy(0, 0, 0).start()
        for k in range(ahead):
            for cp in stage_copies(k, k):
                cp.start()

    @pl.when(i + 1 < n)
    def _():
        start_by_stream(x_copy, i + 1, other)

    x_copy(0, 0, slot).wait()

    @pl.when(i >= 2)
    def _():
        o_copy(0, 0, slot).wait()

    def swiglu(h, wg, wu):
        gt = jnp.dot(h, wg, preferred_element_type=F32)
        up = jnp.dot(h, wu, preferred_element_type=F32)
        return ((gt * jax.nn.sigmoid(gt)) * up).astype(BF16)

    def down_and_store(x, rows):
        ff = jnp.dot(act_ref[...], wd_res[...], preferred_element_type=F32)
        y = x + (0.5 * _gate(m_ref, sidx)) * ff
        if final:
            y = _rms(y) * fg_ref[...]
        obuf[slot, rows, :] = y

    def row_block(r, carry):
        rows = pl.ds(pl.multiple_of(r * rb, rb), rb)
        x = xbuf[slot, rows, :]
        h = _sub_in(x, m_ref, g_ref, sidx).astype(BF16)
        for k in range(nk):
            act_ref[:, k * tf:(k + 1) * tf] = swiglu(h, wgu_res[k], wgu_res[nk + k])
        down_and_store(x, rows)
        return carry

    def first_tile_chunk(k, carry):
        sl = lax.rem(k, nslot)
        for cp in stage_copies(k, sl):
            cp.wait()

        @pl.when(k + ahead < nk)
        def _():
            for cp in stage_copies(k + ahead, lax.rem(k + ahead, nslot)):
                cp.start()

        wg = gu_buf[sl, 0].astype(BF16)
        wu = gu_buf[sl, 1].astype(BF16)
        wgu_res[k] = wg
        wgu_res[nk + k] = wu
        wd_res[pl.ds(pl.multiple_of(k * tf, tf), tf), :] = wd_buf[sl].astype(BF16)
        act0_ref[k] = swiglu(h0_ref[...], wg, wu)
        return carry

    def first_tile_down(r, carry):
        rows = pl.ds(pl.multiple_of(r * rb, rb), rb)
        for k in range(nk):
            act_ref[:, k * tf:(k + 1) * tf] = act0_ref[k, rows, :]
        down_and_store(xbuf[slot, rows, :], rows)
        return carry

    @pl.when(i == 0)
    def _():
        h0_ref[...] = _sub_in(xbuf[slot], m_ref, g_ref, sidx).astype(BF16)
        lax.fori_loop(0, nk, first_tile_chunk, 0, unroll=2)
        lax.fori_loop(0, tm // rb, first_tile_down, 0, unroll=2)

    @pl.when(i > 0)
    def _():
        lax.fori_loop(0, tm // rb, row_block, 0, unroll=2)

    start_by_stream(o_copy, i, slot)

    @pl.when(i == n - 1)
    def _():
        o_copy(0, 0, other).wait()
        o_copy(0, 0, slot).wait()


def _ffn(xa, xb, m, row_of_tile, norm_g, w_gu, w_down, layer, s, sidx, final_g=None):
    (ta, d), tb = xa.shape, xb.shape[0]
    f = w_down.shape[2]
    tm, tf, rb = TOKEN_TILE, FF_CHUNK, FFN_ROW_BLOCK
    assert ta % tm == 0 and tb % tm == 0 and f % tf == 0 and (ta + tb) // tm >= 2
    assert tm % rb == 0 and f // tf >= FFN_STAGE_SLOTS
    final = final_g is not None
    in_specs = [
        pl.BlockSpec((None, None, 3 * N_SUB, d), lambda i: (layer, row_of_tile(i), 0, 0)),
        pl.BlockSpec((None, N_SUB, d), lambda i: (layer, 0, 0)),
    ]
    args = [m, norm_g]
    if final:
        in_specs.append(pl.BlockSpec((1, d), lambda i: (0, 0)))
        args.append(final_g.reshape(1, d))
    in_specs += [pl.BlockSpec(memory_space=pl.ANY)] * 4
    args += [xa, xb, w_gu, w_down]
    return pl.pallas_call(
        functools.partial(_ffn_kernel, layer=layer, s=s, sidx=sidx, final=final,
                          n_first=ta // tm),
        grid=((ta + tb) // tm,),
        in_specs=in_specs,
        out_specs=[pl.BlockSpec(memory_space=pl.ANY)] * 2,
        out_shape=[jax.ShapeDtypeStruct((ta, d), F32), jax.ShapeDtypeStruct((tb, d), F32)],
        scratch_shapes=[
            pltpu.VMEM((2 * f // tf, d, tf), BF16), pltpu.VMEM((f, d), BF16),
            pltpu.VMEM((rb, f), BF16),
            pltpu.VMEM((tm, d), BF16), pltpu.VMEM((f // tf, tm, tf), BF16),
            pltpu.VMEM((FFN_STAGE_SLOTS, 2, d, tf), F32),
            pltpu.VMEM((FFN_STAGE_SLOTS, tf, d), F32),
            pltpu.VMEM((2, tm, d), F32), pltpu.VMEM((2, tm, d), F32),
            pltpu.SemaphoreType.DMA((FFN_STAGE_SLOTS, 3)),
            pltpu.SemaphoreType.DMA((2,)),
            pltpu.SemaphoreType.DMA((2,)),
        ],
        compiler_params=_params("arbitrary"),
        name="ffn",
    )(*args)


def _gelu_tanh(x):
    c = np.float32(np.sqrt(2.0 / np.pi))
    return x * (0.5 * (1.0 + jnp.tanh(c * (x + 0.044715 * (x * x * x)))))


_SWAP_BLOCK = SUBLANES * SUBLANES
_SWAP_SLAB = 4 * _SWAP_BLOCK


def _block_swap_matrix():
    p = np.arange(_SWAP_SLAB)
    blk, a, b = p // _SWAP_BLOCK, (p % _SWAP_BLOCK) // SUBLANES, p % SUBLANES
    mat = np.zeros((_SWAP_SLAB, _SWAP_SLAB), np.float32)
    mat[p, blk * _SWAP_BLOCK + b * SUBLANES + a] = 1.0
    return mat


def _swap_blocks(q, rows):
    parts = [jnp.dot(q, rows[lo:lo + _SWAP_SLAB, :], preferred_element_type=F32)
             for lo in range(0, rows.shape[0], _SWAP_SLAB)]
    return jnp.concatenate(parts, axis=0) if len(parts) > 1 else parts[0]


def _regroup(rows, seq_len, to_permuted):
    segs = seq_len // SUBLANES
    pieces = []
    for lo in range(0, rows.shape[0], seq_len):
        outer, inner = (segs // SUBLANES, SUBLANES) if to_permuted else (SUBLANES, segs // SUBLANES)
        for o in range(outer):
            for i in range(inner):
                jh, s = (o, i) if to_permuted else (i, o)
                src = s * segs + jh * SUBLANES if to_permuted else jh * _SWAP_BLOCK + s * SUBLANES
                pieces.append(rows[lo + src:lo + src + SUBLANES, :])
    return jnp.concatenate(pieces, axis=0)


def _to_permuted_time(q, h_f32, seq_len):
    return _swap_blocks(q, _regroup(h_f32, seq_len, True).astype(BF16)).astype(BF16)


def _to_natural_time(q, y_bf16, seq_len):
    return _regroup(_swap_blocks(q, y_bf16), seq_len, False).astype(BF16)


def _lru_kernel(*refs, seq_len, has_h0, emit_state, conv_left, single_step):
    refs = list(refs)
    (x_ref, m_ref, g_ref, wx_ref, wy_ref, cw_ref, cb_ref, gw_ref, gb_ref, lam_ref,
     wo_ref, swap_ref) = refs[:12]
    pos = 12
    h0_ref = None
    if has_h0:
        h0_ref = refs[pos]
        pos += 1
    o_ref = refs[pos]
    pos += 1
    st_ref = None
    if emit_state:
        st_ref = refs[pos]
        pos += 1
    if single_step:
        af_ref, uf_ref, ab_ref, ub_ref = refs[pos:]
    else:
        h_ref, acc_ref, af_ref, uf_ref, ab_ref, ub_ref = refs[pos:]

    tm, cw_cols = af_ref.shape
    nseq = tm // seq_len
    segs = seq_len // SUBLANES

    def permuted_input():
        return _to_permuted_time(swap_ref[...], _sub_in(x_ref[...], m_ref, g_ref, 1), seq_len)

    if single_step:
        h = permuted_input()
    else:
        c = pl.program_id(1)

        @pl.when(c == 0)
        def _():
            h_ref[...] = permuted_input()
            acc_ref[...] = jnp.zeros_like(acc_ref)

        h = h_ref[...]
    xb = _mm(h, wx_ref[...])
    yb = _mm(h, wy_ref[...])

    sub = lax.broadcasted_iota(jnp.int32, (SUBLANES, cw_cols), 0)

    def next_segment(grp):
        return jnp.where(sub < SUBLANES - 1, pltpu.roll(grp, SUBLANES - 1, 0), 0.0)

    def prev_segment(grp):
        return jnp.where(sub > 0, pltpu.roll(grp, 1, 0), 0.0)

    def tap(seq, off):
        n = abs(off) * SUBLANES
        if off == 0:
            return seq
        if off > 0:
            edge = [next_segment(seq[g * SUBLANES:(g + 1) * SUBLANES, :]) for g in range(off)]
            return jnp.concatenate([seq[n:, :]] + edge, axis=0)
        start = seq_len - n
        edge = [prev_segment(seq[start + g * SUBLANES:start + (g + 1) * SUBLANES, :])
                for g in range(-off)]
        return jnp.concatenate(edge + [seq[:start, :]], axis=0)

    cw = cw_ref[...]
    assert cw.shape[0] - 1 < segs
    xcs = []
    for sq in range(nseq):
        seq = xb[sq * seq_len:(sq + 1) * seq_len, :]
        acc = cb_ref[...]
        for k in range(cw.shape[0]):
            acc = acc + tap(seq, k - conv_left) * cw[k:k + 1, :]
        xcs.append(acc)
    xc = jnp.concatenate(xcs, axis=0) if nseq > 1 else xcs[0]

    gb = gb_ref[...]
    lam = lam_ref[...]
    gc = gw_ref.shape[1]
    for ch in range(cw_cols // gc):
        cols = slice(ch * gc, (ch + 1) * gc)
        xcc = xc[:, cols]
        gl = jnp.dot(xcc.astype(BF16), gw_ref[ch], preferred_element_type=F32)
        for d, (a_ref, u_ref) in enumerate(((af_ref, uf_ref), (ab_ref, ub_ref))):
            tr = jnp.tanh(gl[:, (2 * d) * gc:(2 * d + 1) * gc] + 0.5 * gb[2 * d:2 * d + 1, cols])
            ig = 0.5 * jnp.tanh(gl[:, (2 * d + 1) * gc:(2 * d + 2) * gc]
                                + 0.5 * gb[2 * d + 1:2 * d + 2, cols]) + 0.5
            lm = lam[d:d + 1, cols]
            log_sig = jnp.minimum(lm, 0.0) - jnp.log1p(jnp.exp(-jnp.abs(lm)))
            c2 = (0.5 * LRU_C * LOG2_E) * log_sig
            a = jnp.exp2(c2 * tr + c2)
            v = 1.0 - a * a
            u = jnp.where(v > 0.0, v * lax.rsqrt(v), 0.0) * (ig * xcc)
            a_ref[:, cols] = a
            u_ref[:, cols] = u

    zero = jnp.zeros((SUBLANES, cw_cols), F32)
    one = jnp.ones((SUBLANES, cw_cols), F32)

    def local_scan(j, carry):
        out = []
        for sq in range(nseq):
            for dd, (a_ref, u_ref) in enumerate(((af_ref, uf_ref), (ab_ref, ub_ref))):
                grp = j if dd == 0 else segs - 1 - j
                rows = pl.ds(pl.multiple_of(sq * seq_len + grp * SUBLANES, SUBLANES), SUBLANES)
                hl, pc = carry[2 * (2 * sq + dd)], carry[2 * (2 * sq + dd) + 1]
                a = a_ref[rows, :]
                hl = a * hl + u_ref[rows, :]
                pc = a * pc
                u_ref[rows, :] = hl
                a_ref[rows, :] = pc
                out += [hl, pc]
        return tuple(out)

    totals = lax.fori_loop(0, segs, local_scan, (zero, one) * (2 * nseq), unroll=True)

    gelu_y = _gelu_tanh(yb)
    ys = []
    for sq in range(nseq):
        entry = []
        for dd in range(2):
            hl, pc = totals[2 * (2 * sq + dd)], totals[2 * (2 * sq + dd) + 1]
            state = h0_ref[sq, dd:dd + 1, :] if has_h0 else jnp.zeros((1, cw_cols), F32)
            rows = [None] * SUBLANES
            order = range(SUBLANES) if dd == 0 else range(SUBLANES - 1, -1, -1)
            for s in order:
                rows[s] = state
                state = pc[s:s + 1, :] * state + hl[s:s + 1, :]
            entry.append(jnp.concatenate(rows, axis=0))
            if emit_state:
                st_ref[sq, dd:dd + 1, :] = state
        lo, hi = sq * seq_len, (sq + 1) * seq_len
        grouped = (segs, SUBLANES, cw_cols)
        hsum = ((uf_ref[lo:hi, :].reshape(grouped) + af_ref[lo:hi, :].reshape(grouped) * entry[0][None])
                + (ub_ref[lo:hi, :].reshape(grouped) + ab_ref[lo:hi, :].reshape(grouped) * entry[1][None]))
        ys.append(hsum.reshape(seq_len, cw_cols) * gelu_y[lo:hi, :])
    y = jnp.concatenate(ys, axis=0) if nseq > 1 else ys[0]

    y = _to_natural_time(swap_ref[...], y.astype(BF16), seq_len)
    out = _mm(y, wo_ref[...])
    if single_step:
        o_ref[...] = x_ref[...] + _gate(m_ref, 1) * out
        return
    acc_ref[...] += out

    @pl.when(c == pl.num_programs(1) - 1)
    def _():
        o_ref[...] = x_ref[...] + _gate(m_ref, 1) * acc_ref[...]


def _lru_gate_weights(gate_w):
    nd, ng, nb, bw, _ = gate_w.shape
    per = LRU_GATE_CHUNK // bw
    nc = nb // per
    w = (0.5 * gate_w).astype(BF16).reshape(nd * ng, nc, per, bw, bw)
    rows = []
    for n in range(per):
        blk = jnp.transpose(w[:, :, n], (1, 2, 0, 3))
        blk = jnp.pad(blk, ((0, 0), (0, 0), (0, 0), (n * bw, (per - 1 - n) * bw)))
        rows.append(blk.reshape(nc, bw, nd * ng * per * bw))
    return jnp.concatenate(rows, axis=1)


def _lru(x, m, row_of_tile, norm_g, layer, w_in, conv_w, conv_b, gate_w, gate_b, lam, w_out,
         seq_len, h0, emit_state, single_step=False):
    t, d = x.shape
    r = w_out.shape[0]
    tm, gc = TOKEN_TILE, LRU_GATE_CHUNK
    cb = r if single_step else LRU_CHUNK
    assert t % tm == 0 and tm % seq_len == 0 and r % cb == 0 and cb % gc == 0
    assert seq_len % _SWAP_BLOCK == 0 and tm % _SWAP_SLAB == 0
    nc = r // cb
    nseq = tm // seq_len
    has_h0 = h0 is not None
    resident = dict(pipeline_mode=pl.Buffered(1)) if single_step else {}
    in_specs = [
        pl.BlockSpec((tm, d), lambda i, c: (i, 0)),
        pl.BlockSpec((None, None, 3 * N_SUB, d), lambda i, c: (layer, row_of_tile(i), 0, 0)),
        pl.BlockSpec((None, N_SUB, d), lambda i, c: (layer, 0, 0)),
        pl.BlockSpec((d, cb), lambda i, c: (0, c), **resident),
        pl.BlockSpec((d, cb), lambda i, c: (0, nc + c), **resident),
        pl.BlockSpec((conv_w.shape[0], cb), lambda i, c: (0, c)),
        pl.BlockSpec((1, cb), lambda i, c: (0, c)),
        pl.BlockSpec((cb // gc, gc, 4 * gc), lambda i, c: (c, 0, 0), **resident),
        pl.BlockSpec((4, cb), lambda i, c: (0, c)),
        pl.BlockSpec((2, cb), lambda i, c: (0, c)),
        pl.BlockSpec((cb, d), lambda i, c: (c, 0), **resident),
        pl.BlockSpec((_SWAP_SLAB, _SWAP_SLAB), lambda i, c: (0, 0)),
    ]
    args = [x, m, norm_g, w_in, w_in, conv_w, conv_b.reshape(1, r), _lru_gate_weights(gate_w),
            gate_b.reshape(4, r), lam, w_out, jnp.asarray(_block_swap_matrix(), BF16)]
    if has_h0:
        in_specs.append(pl.BlockSpec((nseq, 2, cb), lambda i, c: (i, 0, c)))
        args.append(h0)
    out_specs = [pl.BlockSpec((tm, d), lambda i, c: (i, 0))]
    out_shape = [jax.ShapeDtypeStruct((t, d), F32)]
    if emit_state:
        out_specs.append(pl.BlockSpec((nseq, 2, cb), lambda i, c: (i, 0, c)))
        out_shape.append(jax.ShapeDtypeStruct((t // seq_len, 2, r), F32))
    outs = pl.pallas_call(
        functools.partial(_lru_kernel, seq_len=seq_len, has_h0=has_h0, emit_state=emit_state,
                          conv_left=(conv_w.shape[0] - 1) // 2, single_step=single_step),
        grid=(t // tm, nc),
        in_specs=in_specs,
        out_specs=out_specs,
        out_shape=out_shape,
        scratch_shapes=([] if single_step
                        else [pltpu.VMEM((tm, d), BF16), pltpu.VMEM((tm, d), F32)])
        + [pltpu.VMEM((tm, cb), F32)] * 4,
        compiler_params=_params("arbitrary", "arbitrary"),
        name="rglru",
    )(*args)
    return outs if emit_state else (outs[0], None)


def _rope(x, cos, sin_signed, lane):
    hd = x.shape[1]
    partner = jnp.where((lane & 32) == 0, pltpu.roll(x, hd - 32, 1), pltpu.roll(x, 32, 1))
    return x * cos + partner * sin_signed


def _attn_kernel(*refs, seq_len, q_block, past_len, rope, emit_kv, group, kvh, fused):
    refs = list(refs)
    x_ref, m_ref, g_ref, wq_ref, wk_ref, wv_ref, qg_ref, kg_ref, wo_ref = refs[:9]
    pos = 9
    if rope:
        cos_ref, sin_ref = refs[pos:pos + 2]
        pos += 2
    if past_len:
        ck_ref, cv_ref = refs[pos:pos + 2]
        pos += 2
    o_ref = refs[pos]
    pos += 1
    if emit_kv:
        kn_ref, vn_ref = refs[pos:pos + 2]
        pos += 2
    if fused:
        q_s, k_s, v_s, o_s = refs[pos:]
    else:
        h_ref, acc_ref, q_s, k_s, v_s, o_s = refs[pos:]

    tm = x_ref.shape[0]
    hd = k_s.shape[-1]
    gw = group * hd
    nqb = seq_len // q_block
    nchunks = (tm // seq_len) * nqb
    nk = past_len + seq_len
    scale = hd ** -0.5

    def group_out(h, g, slot, cols, kv_all=None):
        if fused:
            qs, ks, vs, os_ = q_s.at[slot], k_s.at[slot], v_s.at[slot], o_s.at[slot]
            wq, wo = wq_ref[:, cols * gw:(cols + 1) * gw], wo_ref[cols * gw:(cols + 1) * gw, :]
        else:
            qs, ks, vs, os_ = q_s, k_s, v_s, o_s
            wq, wo = wq_ref[...], wo_ref[...]
        q = _mm(h, wq)
        if kv_all is None:
            k_raw, v = _mm(h, wk_ref[...]), _mm(h, wv_ref[...])
        else:
            k_raw, v = (a[:, cols * hd:(cols + 1) * hd] for a in kv_all)
        k = _rms(k_raw) * kg_ref[...]
        if emit_kv:
            kn_ref[pl.ds(g, tm, stride=kvh), :] = k
            vn_ref[pl.ds(g, tm, stride=kvh), :] = v
        if rope:
            lane = lax.broadcasted_iota(jnp.int32, (tm, hd), 1)
            cos, sin = cos_ref[...], sin_ref[...]
            k = _rope(k, cos, sin, lane)
        ks[past_len:past_len + tm, :] = k.astype(BF16)
        vs[past_len:past_len + tm, :] = v.astype(BF16)
        if past_len:
            ks[0:past_len, :] = ck_ref[pl.ds(g, past_len, stride=kvh), :].astype(BF16)
            vs[0:past_len, :] = cv_ref[pl.ds(g, past_len, stride=kvh), :].astype(BF16)
        for j in range(group):
            qh = _rms(q[:, j * hd:(j + 1) * hd]) * qg_ref[...]
            if rope:
                qh = _rope(qh, cos, sin, lane)
            qs[:, j * hd:(j + 1) * hd] = qh.astype(BF16)

        def chunk(ci, carry):
            r0 = pl.multiple_of(ci * q_block, q_block)
            if past_len:
                keys, vals = ks[...], vs[...]
            else:
                koff = pl.multiple_of((ci // nqb) * seq_len, seq_len)
                keys, vals = ks[pl.ds(koff, nk), :], vs[pl.ds(koff, nk), :]
            qc = jnp.concatenate(
                [qs[pl.ds(r0, q_block), j * hd:(j + 1) * hd] for j in range(group)], axis=0)
            raw = lax.dot_general(qc, keys, (((1,), (1,)), ((), ())), preferred_element_type=F32)
            e = jnp.exp2((raw - jnp.max(raw, axis=-1, keepdims=True)) * (scale * LOG2_E))
            inv = 1.0 / jnp.sum(e, axis=-1, keepdims=True)
            oc = jnp.dot(e.astype(BF16), vals, preferred_element_type=F32) * inv
            for j in range(group):
                os_[pl.ds(r0, q_block), j * hd:(j + 1) * hd] = (
                    oc[j * q_block:(j + 1) * q_block, :].astype(BF16))
            return carry

        lax.fori_loop(0, nchunks, chunk, 0, unroll=min(nchunks, 4))
        return _mm(os_[...], wo)

    if fused:
        x = x_ref[...]
        h = _sub_in(x, m_ref, g_ref, 1).astype(BF16)
        kv_all = (_mm(h, wk_ref[...]), _mm(h, wv_ref[...]))
        acc = group_out(h, 0, 0, 0, kv_all)
        for g in range(1, kvh):
            acc = acc + group_out(h, g, g, g, kv_all)
        o_ref[...] = x + _gate(m_ref, 1) * acc
        return

    gi = pl.program_id(1)

    @pl.when(gi == 0)
    def _():
        h_ref[...] = _sub_in(x_ref[...], m_ref, g_ref, 1).astype(BF16)
        acc_ref[...] = jnp.zeros_like(acc_ref)

    out = group_out(h_ref[...], gi, 0, 0)
    acc_ref[...] += out

    @pl.when(gi == pl.num_programs(1) - 1)
    def _():
        o_ref[...] = x_ref[...] + _gate(m_ref, 1) * acc_ref[...]


def _rope_tables(n_tok, hd):
    rows = n_tok // GRID_W
    r_idx = jnp.broadcast_to(jnp.arange(rows)[:, None], (rows, GRID_W)).reshape(n_tok).astype(F32)
    c_idx = jnp.broadcast_to(jnp.arange(GRID_W)[None, :], (rows, GRID_W)).reshape(n_tok).astype(F32)
    n_freq = hd // 4
    inv = ROPE_THETA ** (-jnp.arange(n_freq, dtype=F32) / n_freq)
    ang = jnp.stack([r_idx[:, None] * inv, c_idx[:, None] * inv], axis=1)
    cos, sin = jnp.cos(ang), jnp.sin(ang)
    cos_full = jnp.concatenate([cos, cos], axis=-1).reshape(n_tok, hd)
    sin_signed = jnp.concatenate([-sin, sin], axis=-1).reshape(n_tok, hd)
    return cos_full, sin_signed


def _attn(x, m, row_of_tile, norm_g, layer, w_qkv, q_g, k_g, w_o, seq_len, q_block,
          cache_k=None, cache_v=None, rope=False, emit_kv=False, fused=False):
    t, d = x.shape
    hd, kvh = HEAD_DIM, N_KV_HEADS
    n_heads = w_o.shape[0] // hd
    group = n_heads // kvh
    gw = group * hd
    tm = TOKEN_TILE
    assert t % tm == 0 and tm % seq_len == 0 and seq_len % q_block == 0
    past_len = 0 if cache_k is None else cache_k.shape[1] // kvh
    assert past_len == 0 or tm == seq_len
    gs = kvh if fused else 1
    lead = (kvh,) if fused else ()
    in_specs = [
        pl.BlockSpec((tm, d), lambda i, g: (i, 0)),
        pl.BlockSpec((None, None, 3 * N_SUB, d), lambda i, g: (layer, row_of_tile(i), 0, 0)),
        pl.BlockSpec((None, N_SUB, d), lambda i, g: (layer, 0, 0)),
        pl.BlockSpec((d, gs * gw), lambda i, g: (0, g)),
        pl.BlockSpec((d, gs * hd), lambda i, g: (0, n_heads // gs + g)),
        pl.BlockSpec((d, gs * hd), lambda i, g: (0, (n_heads + kvh) // gs + g)),
        pl.BlockSpec((1, hd), lambda i, g: (0, 0)),
        pl.BlockSpec((1, hd), lambda i, g: (0, 0)),
        pl.BlockSpec((gs * gw, d), lambda i, g: (g, 0)),
    ]
    args = [x, m, norm_g, w_qkv, w_qkv, w_qkv, q_g.reshape(1, hd), k_g.reshape(1, hd), w_o]
    if rope:
        assert tm == seq_len
        cos, sin = _rope_tables(seq_len, hd)
        in_specs += [pl.BlockSpec((tm, hd), lambda i, g: (0, 0))] * 2
        args += [cos, sin]
    if past_len:
        in_specs += [pl.BlockSpec((None, past_len * kvh, hd), lambda i, g: (i, 0, 0))] * 2
        args += [cache_k, cache_v]
    out_specs = [pl.BlockSpec((tm, d), lambda i, g: (i, 0))]
    out_shape = [jax.ShapeDtypeStruct((t, d), F32)]
    if emit_kv:
        out_specs += [pl.BlockSpec((tm * kvh, hd), lambda i, g: (i, 0))] * 2
        out_shape += [jax.ShapeDtypeStruct((t * kvh, hd), F32)] * 2
    nkeys = past_len + tm
    outs = pl.pallas_call(
        functools.partial(_attn_kernel, seq_len=seq_len, q_block=q_block, past_len=past_len,
                          rope=rope, emit_kv=emit_kv, group=group, kvh=kvh, fused=fused),
        grid=(t // tm, kvh // gs),
        in_specs=in_specs,
        out_specs=out_specs,
        out_shape=out_shape,
        scratch_shapes=([] if fused else [pltpu.VMEM((tm, d), BF16), pltpu.VMEM((tm, d), F32)])
        + [pltpu.VMEM(lead + (tm, gw), BF16), pltpu.VMEM(lead + (nkeys, hd), BF16),
           pltpu.VMEM(lead + (nkeys, hd), BF16), pltpu.VMEM(lead + (tm, gw), BF16)],
        compiler_params=_params("arbitrary", "arbitrary"),
        name="gqa",
    )(*args)
    return outs


def kernel(x_prompt, x_sample, c, state_lru, cache_k, cache_v, c_ctx, mod_w, mod_b, norm_g,
           ffn_w_gu, ffn_w_down, lru_w_in, lru_conv_w, lru_conv_b, lru_gate_w, lru_gate_b,
           lru_lambda, lru_w_out, att_w_qkv, att_q_g, att_k_g, att_w_o, final_g):
    b, s, d = x_prompt.shape
    db, ds, _ = x_sample.shape
    depth = mod_w.shape[0]
    n_mixers = 2
    assert ds % TOKEN_TILE == 0 and 1 + db <= SUBLANES

    xp = x_prompt.reshape(b * s, d)
    xs = x_sample.reshape(db * ds, d)
    cvecs = jnp.concatenate([c_ctx[None], c, jnp.zeros((SUBLANES - 1 - db, d), F32)], axis=0)
    m = _modulation(cvecs, mod_w, mod_b).reshape(depth, SUBLANES, 3 * N_SUB, d)

    tiles_per_sample = ds // TOKEN_TILE
    prompt_row = lambda i: 0
    sample_row = lambda i: 1 + i // tiles_per_sample
    prompt_tiles = (b * s) // TOKEN_TILE
    both_row = lambda i: jnp.where(i < prompt_tiles, 0, 1 + (i - prompt_tiles) // tiles_per_sample)

    new_states, new_k, new_v = [], [], []
    for layer in range(depth):
        j = layer // n_mixers
        last = layer == depth - 1
        xp, xs = _ffn(xp, xs, m, both_row, norm_g, ffn_w_gu, ffn_w_down, layer, 0, 0)
        if layer % n_mixers == 0:
            lru_p = (lru_w_in[j], lru_conv_w[j], lru_conv_b[j], lru_gate_w[j], lru_gate_b[j],
                     lru_lambda[j], lru_w_out[j])
            xp, st = _lru(xp, m, prompt_row, norm_g, layer, *lru_p, seq_len=s, h0=None,
                          emit_state=True, single_step=True)
            new_states.append(st)
            xs, _ = _lru(xs, m, sample_row, norm_g, layer, *lru_p, seq_len=ds,
                         h0=state_lru[:, j], emit_state=False, single_step=True)
        else:
            att_p = (att_w_qkv[j], att_q_g[j], att_k_g[j], att_w_o[j])
            xp, kp, vp = _attn(xp, m, prompt_row, norm_g, layer, *att_p, seq_len=s, q_block=s,
                               emit_kv=True, fused=True)
            new_k.append(kp.reshape(b, s, N_KV_HEADS, HEAD_DIM))
            new_v.append(vp.reshape(b, s, N_KV_HEADS, HEAD_DIM))
            past = cache_k.shape[2]
            ck = cache_k[:, j].reshape(db, past * N_KV_HEADS, HEAD_DIM)
            cv = cache_v[:, j].reshape(db, past * N_KV_HEADS, HEAD_DIM)
            (xs,) = _attn(xs, m, sample_row, norm_g, layer, *att_p, seq_len=ds, q_block=128,
                          cache_k=ck, cache_v=cv, rope=True)
        fg = final_g if last else None
        xp, xs = _ffn(xp, xs, m, both_row, norm_g, ffn_w_gu, ffn_w_down, layer, 1, 2, final_g=fg)

    y_prompt = xp.reshape(b, s, d)
    y_sample = xs.reshape(db, ds, d)
    return (y_prompt, y_sample, jnp.stack(new_states, axis=1), jnp.stack(new_k, axis=1),
            jnp.stack(new_v, axis=1))
```

```python
import functools

import jax
import jax.numpy as jnp
import numpy as np
from jax import lax
from jax.experimental import pallas as pl
from jax.experimental.pallas import tpu as pltpu

F32 = jnp.float32
BF16 = jnp.bfloat16

EPS = 1e-6
LRU_C = 8.0
LOG2_E = 1.4426950408889634
GRID_W = 64
ROPE_THETA = 10000.0
N_SUB = 3
HEAD_DIM = 128
N_KV_HEADS = 2

V7X_VMEM_LIMIT_BYTES = 56 * 1024 * 1024
SUBLANES = 8
TOKEN_TILE = 1024
FF_CHUNK = 256
FFN_ROW_BLOCK = 256
FFN_STAGE_SLOTS = 3
LRU_CHUNK = 512
LRU_GATE_CHUNK = 256
MOD_CHUNK = 1536


def _params(*semantics):
    return pltpu.CompilerParams(dimension_semantics=semantics,
                                vmem_limit_bytes=V7X_VMEM_LIMIT_BYTES)


def _mm(a_bf16, w_f32):
    return jnp.dot(a_bf16, w_f32.astype(BF16), preferred_element_type=F32)


def _rms(x):
    return x * lax.rsqrt(jnp.mean(x * x, axis=-1, keepdims=True) + EPS)


def _sub_in(x, m_ref, g_ref, sidx):
    shift = m_ref[3 * sidx:3 * sidx + 1, :]
    scale = m_ref[3 * sidx + 1:3 * sidx + 2, :]
    return (_rms(x) * g_ref[sidx:sidx + 1, :]) * (1.0 + scale) + shift


def _gate(m_ref, sidx):
    return m_ref[3 * sidx + 2:3 * sidx + 3, :]


def _mod_kernel(cv_ref, w_ref, b_ref, o_ref):
    cv = cv_ref[...]
    act = (cv * jax.nn.sigmoid(cv)).astype(BF16)
    o_ref[...] = _mm(act, w_ref[...]) + b_ref[...]


def _modulation(cvecs, mod_w, mod_b):
    depth, d, n = mod_w.shape
    rows = cvecs.shape[0]
    tn = MOD_CHUNK
    assert n % tn == 0
    return pl.pallas_call(
        _mod_kernel,
        grid=(depth, n // tn),
        in_specs=[
            pl.BlockSpec((rows, d), lambda l, j: (0, 0)),
            pl.BlockSpec((None, d, tn), lambda l, j: (l, 0, j)),
            pl.BlockSpec((None, 1, tn), lambda l, j: (l, 0, j)),
        ],
        out_specs=pl.BlockSpec((None, rows, tn), lambda l, j: (l, 0, j)),
        out_shape=jax.ShapeDtypeStruct((depth, rows, n), F32),
        compiler_params=_params("arbitrary", "arbitrary"),
        name="modulation",
    )(cvecs, mod_w, mod_b.reshape(depth, 1, n))


def _ffn_kernel(*refs, layer, s, sidx, final, n_first):
    if final:
        (m_ref, g_ref, fg_ref, xa_hbm, xb_hbm, wgu_hbm, wd_hbm, oa_hbm, ob_hbm,
         wgu_res, wd_res, act_ref, h0_ref, act0_ref, gu_buf, wd_buf, xbuf, obuf,
         w_sem, xsem, osem) = refs
    else:
        (m_ref, g_ref, xa_hbm, xb_hbm, wgu_hbm, wd_hbm, oa_hbm, ob_hbm,
         wgu_res, wd_res, act_ref, h0_ref, act0_ref, gu_buf, wd_buf, xbuf, obuf,
         w_sem, xsem, osem) = refs
    i = pl.program_id(0)
    n = pl.num_programs(0)
    tm = xbuf.shape[1]
    _, d, tf = wgu_res.shape
    f = wd_res.shape[0]
    nk = f // tf
    rb = act_ref.shape[0]
    nslot = gu_buf.shape[0]
    ahead = nslot - 1
    slot = lax.rem(i, 2)
    other = 1 - slot

    def stage_copies(k, sl):
        gcols = pl.ds(pl.multiple_of(k * tf, tf), tf)
        ucols = pl.ds(pl.multiple_of(f + k * tf, tf), tf)
        return (
            pltpu.make_async_copy(wgu_hbm.at[layer, s, :, gcols], gu_buf.at[sl, 0], w_sem.at[sl, 0]),
            pltpu.make_async_copy(wgu_hbm.at[layer, s, :, ucols], gu_buf.at[sl, 1], w_sem.at[sl, 1]),
            pltpu.make_async_copy(wd_hbm.at[layer, s, gcols, :], wd_buf.at[sl], w_sem.at[sl, 2]),
        )

    def tile_rows(hbm, tile):
        return hbm.at[pl.ds(pl.multiple_of(tile * tm, tm), tm), :]

    def x_copy(which, tile, sl):
        return pltpu.make_async_copy(tile_rows((xa_hbm, xb_hbm)[which], tile), xbuf.at[sl],
                                     xsem.at[sl])

    def o_copy(which, tile, sl):
        return pltpu.make_async_copy(obuf.at[sl], tile_rows((oa_hbm, ob_hbm)[which], tile),
                                     osem.at[sl])

    def start_by_stream(make, tile, sl):
        @pl.when(tile < n_first)
        def _():
            make(0, tile, sl).start()

        @pl.when(tile >= n_first)
        def _():
            make(1, tile - n_first, sl).start()

    @pl.when(i == 0)
    def _():
        x_copy(0, 0, 0).start()
        for k in range(ahead):
            for cp in stage_copies(k, k):
                cp.start()

    @pl.when(i + 1 < n)
    def _():
        start_by_stream(x_copy, i + 1, other)

    x_copy(0, 0, slot).wait()

    @pl.when(i >= 2)
    def _():
        o_copy(0, 0, slot).wait()

    def swiglu(h, wg, wu):
        gt = jnp.dot(h, wg, preferred_element_type=F32)
        up = jnp.dot(h, wu, preferred_element_type=F32)
        return ((gt * jax.nn.sigmoid(gt)) * up).astype(BF16)

    def down_and_store(x, rows):
        ff = jnp.dot(act_ref[...], wd_res[...], preferred_element_type=F32)
        y = x + (0.5 * _gate(m_ref, sidx)) * ff
        if final:
            y = _rms(y) * fg_ref[...]
        obuf[slot, rows, :] = y

    def row_block(r, carry):
        rows = pl.ds(pl.multiple_of(r * rb, rb), rb)
        x = xbuf[slot, rows, :]
        h = _sub_in(x, m_ref, g_ref, sidx).astype(BF16)
        for k in range(nk):
            act_ref[:, k * tf:(k + 1) * tf] = swiglu(h, wgu_res[k], wgu_res[nk + k])
        down_and_store(x, rows)
        return carry

    def first_tile_chunk(k, carry):
        sl = lax.rem(k, nslot)
        for cp in stage_copies(k, sl):
            cp.wait()

        @pl.when(k + ahead < nk)
        def _():
            for cp in stage_copies(k + ahead, lax.rem(k + ahead, nslot)):
                cp.start()

        wg = gu_buf[sl, 0].astype(BF16)
        wu = gu_buf[sl, 1].astype(BF16)
        wgu_res[k] = wg
        wgu_res[nk + k] = wu
        wd_res[pl.ds(pl.multiple_of(k * tf, tf), tf), :] = wd_buf[sl].astype(BF16)
        act0_ref[k] = swiglu(h0_ref[...], wg, wu)
        return carry

    def first_tile_down(r, carry):
        rows = pl.ds(pl.multiple_of(r * rb, rb), rb)
        for k in range(nk):
            act_ref[:, k * tf:(k + 1) * tf] = act0_ref[k, rows, :]
        down_and_store(xbuf[slot, rows, :], rows)
        return carry

    @pl.when(i == 0)
    def _():
        h0_ref[...] = _sub_in(xbuf[slot], m_ref, g_ref, sidx).astype(BF16)
        lax.fori_loop(0, nk, first_tile_chunk, 0, unroll=2)
        lax.fori_loop(0, tm // rb, first_tile_down, 0, unroll=2)

    @pl.when(i > 0)
    def _():
        lax.fori_loop(0, tm // rb, row_block, 0, unroll=2)

    start_by_stream(o_copy, i, slot)

    @pl.when(i == n - 1)
    def _():
        o_copy(0, 0, other).wait()
        o_copy(0, 0, slot).wait()


def _ffn(xa, xb, m, row_of_tile, norm_g, w_gu, w_down, layer, s, sidx, final_g=None):
    (ta, d), tb = xa.shape, xb.shape[0]
    f = w_down.shape[2]
    tm, tf, rb = TOKEN_TILE, FF_CHUNK, FFN_ROW_BLOCK
    assert ta % tm == 0 and tb % tm == 0 and f % tf == 0 and (ta + tb) // tm >= 2
    assert tm % rb == 0 and f // tf >= FFN_STAGE_SLOTS
    final = final_g is not None
    in_specs = [
        pl.BlockSpec((None, None, 3 * N_SUB, d), lambda i: (layer, row_of_tile(i), 0, 0)),
        pl.BlockSpec((None, N_SUB, d), lambda i: (layer, 0, 0)),
    ]
    args = [m, norm_g]
    if final:
        in_specs.append(pl.BlockSpec((1, d), lambda i: (0, 0)))
        args.append(final_g.reshape(1, d))
    in_specs += [pl.BlockSpec(memory_space=pl.ANY)] * 4
    args += [xa, xb, w_gu, w_down]
    return pl.pallas_call(
        functools.partial(_ffn_kernel, layer=layer, s=s, sidx=sidx, final=final,
                          n_first=ta // tm),
        grid=((ta + tb) // tm,),
        in_specs=in_specs,
        out_specs=[pl.BlockSpec(memory_space=pl.ANY)] * 2,
        out_shape=[jax.ShapeDtypeStruct((ta, d), F32), jax.ShapeDtypeStruct((tb, d), F32)],
        scratch_shapes=[
            pltpu.VMEM((2 * f // tf, d, tf), BF16), pltpu.VMEM((f, d), BF16),
            pltpu.VMEM((rb, f), BF16),
            pltpu.VMEM((tm, d), BF16), pltpu.VMEM((f // tf, tm, tf), BF16),
            pltpu.VMEM((FFN_STAGE_SLOTS, 2, d, tf), F32),
            pltpu.VMEM((FFN_STAGE_SLOTS, tf, d), F32),
            pltpu.VMEM((2, tm, d), F32), pltpu.VMEM((2, tm, d), F32),
            pltpu.SemaphoreType.DMA((FFN_STAGE_SLOTS, 3)),
            pltpu.SemaphoreType.DMA((2,)),
            pltpu.SemaphoreType.DMA((2,)),
        ],
        compiler_params=_params("arbitrary"),
        name="ffn",
    )(*args)


def _gelu_tanh(x):
    c = np.float32(np.sqrt(2.0 / np.pi))
    return x * (0.5 * (1.0 + jnp.tanh(c * (x + 0.044715 * (x * x * x)))))


_SWAP_BLOCK = SUBLANES * SUBLANES
_SWAP_SLAB = 4 * _SWAP_BLOCK


def _block_swap_matrix():
    p = np.arange(_SWAP_SLAB)
    blk, a, b = p // _SWAP_BLOCK, (p % _SWAP_BLOCK) // SUBLANES, p % SUBLANES
    mat = np.zeros((_SWAP_SLAB, _SWAP_SLAB), np.float32)
    mat[p, blk * _SWAP_BLOCK + b * SUBLANES + a] = 1.0
    return mat


def _swap_blocks(q, rows):
    parts = [jnp.dot(q, rows[lo:lo + _SWAP_SLAB, :], preferred_element_type=F32)
             for lo in range(0, rows.shape[0], _SWAP_SLAB)]
    return jnp.concatenate(parts, axis=0) if len(parts) > 1 else parts[0]


def _regroup(rows, seq_len, to_permuted):
    segs = seq_len // SUBLANES
    pieces = []
    for lo in range(0, rows.shape[0], seq_len):
        outer, inner = (segs // SUBLANES, SUBLANES) if to_permuted else (SUBLANES, segs // SUBLANES)
        for o in range(outer):
            for i in range(inner):
                jh, s = (o, i) if to_permuted else (i, o)
                src = s * segs + jh * SUBLANES if to_permuted else jh * _SWAP_BLOCK + s * SUBLANES
                pieces.append(rows[lo + src:lo + src + SUBLANES, :])
    return jnp.concatenate(pieces, axis=0)


def _to_permuted_time(q, h_f32, seq_len):
    return _swap_blocks(q, _regroup(h_f32, seq_len, True).astype(BF16)).astype(BF16)


def _to_natural_time(q, y_bf16, seq_len):
    return _regroup(_swap_blocks(q, y_bf16), seq_len, False).astype(BF16)


def _lru_kernel(*refs, seq_len, has_h0, emit_state, conv_left, single_step):
    refs = list(refs)
    (x_ref, m_ref, g_ref, wx_ref, wy_ref, cw_ref, cb_ref, gw_ref, gb_ref, lam_ref,
     wo_ref, swap_ref) = refs[:12]
    pos = 12
    h0_ref = None
    if has_h0:
        h0_ref = refs[pos]
        pos += 1
    o_ref = refs[pos]
    pos += 1
    st_ref = None
    if emit_state:
        st_ref = refs[pos]
        pos += 1
    if single_step:
        af_ref, uf_ref, ab_ref, ub_ref = refs[pos:]
    else:
        h_ref, acc_ref, af_ref, uf_ref, ab_ref, ub_ref = refs[pos:]

    tm, cw_cols = af_ref.shape
    nseq = tm // seq_len
    segs = seq_len // SUBLANES

    def permuted_input():
        return _to_permuted_time(swap_ref[...], _sub_in(x_ref[...], m_ref, g_ref, 1), seq_len)

    if single_step:
        h = permuted_input()
    else:
        c = pl.program_id(1)

        @pl.when(c == 0)
        def _():
            h_ref[...] = permuted_input()
            acc_ref[...] = jnp.zeros_like(acc_ref)

        h = h_ref[...]
    xb = _mm(h, wx_ref[...])
    yb = _mm(h, wy_ref[...])

    sub = lax.broadcasted_iota(jnp.int32, (SUBLANES, cw_cols), 0)

    def next_segment(grp):
        return jnp.where(sub < SUBLANES - 1, pltpu.roll(grp, SUBLANES - 1, 0), 0.0)

    def prev_segment(grp):
        return jnp.where(sub > 0, pltpu.roll(grp, 1, 0), 0.0)

    def tap(seq, off):
        n = abs(off) * SUBLANES
        if off == 0:
            return seq
        if off > 0:
            edge = [next_segment(seq[g * SUBLANES:(g + 1) * SUBLANES, :]) for g in range(off)]
            return jnp.concatenate([seq[n:, :]] + edge, axis=0)
        start = seq_len - n
        edge = [prev_segment(seq[start + g * SUBLANES:start + (g + 1) * SUBLANES, :])
                for g in range(-off)]
        return jnp.concatenate(edge + [seq[:start, :]], axis=0)

    cw = cw_ref[...]
    assert cw.shape[0] - 1 < segs
    xcs = []
    for sq in range(nseq):
        seq = xb[sq * seq_len:(sq + 1) * seq_len, :]
        acc = cb_ref[...]
        for k in range(cw.shape[0]):
            acc = acc + tap(seq, k - conv_left) * cw[k:k + 1, :]
        xcs.append(acc)
    xc = jnp.concatenate(xcs, axis=0) if nseq > 1 else xcs[0]

    gb = gb_ref[...]
    lam = lam_ref[...]
    gc = gw_ref.shape[1]
    for ch in range(cw_cols // gc):
        cols = slice(ch * gc, (ch + 1) * gc)
        xcc = xc[:, cols]
        gl = jnp.dot(xcc.astype(BF16), gw_ref[ch], preferred_element_type=F32)
        for d, (a_ref, u_ref) in enumerate(((af_ref, uf_ref), (ab_ref, ub_ref))):
            tr = jnp.tanh(gl[:, (2 * d) * gc:(2 * d + 1) * gc] + 0.5 * gb[2 * d:2 * d + 1, cols])
            ig = 0.5 * jnp.tanh(gl[:, (2 * d + 1) * gc:(2 * d + 2) * gc]
                                + 0.5 * gb[2 * d + 1:2 * d + 2, cols]) + 0.5
            lm = lam[d:d + 1, cols]
            log_sig = jnp.minimum(lm, 0.0) - jnp.log1p(jnp.exp(-jnp.abs(lm)))
            c2 = (0.5 * LRU_C * LOG2_E) * log_sig
            a = jnp.exp2(c2 * tr + c2)
            v = 1.0 - a * a
            u = jnp.where(v > 0.0, v * lax.rsqrt(v), 0.0) * (ig * xcc)
            a_ref[:, cols] = a
            u_ref[:, cols] = u

    zero = jnp.zeros((SUBLANES, cw_cols), F32)
    one = jnp.ones((SUBLANES, cw_cols), F32)

    def local_scan(j, carry):
        out = []
        for sq in range(nseq):
            for dd, (a_ref, u_ref) in enumerate(((af_ref, uf_ref), (ab_ref, ub_ref))):
                grp = j if dd == 0 else segs - 1 - j
                rows = pl.ds(pl.multiple_of(sq * seq_len + grp * SUBLANES, SUBLANES), SUBLANES)
                hl, pc = carry[2 * (2 * sq + dd)], carry[2 * (2 * sq + dd) + 1]
                a = a_ref[rows, :]
                hl = a * hl + u_ref[rows, :]
                pc = a * pc
                u_ref[rows, :] = hl
                a_ref[rows, :] = pc
                out += [hl, pc]
        return tuple(out)

    totals = lax.fori_loop(0, segs, local_scan, (zero, one) * (2 * nseq), unroll=True)

    gelu_y = _gelu_tanh(yb)
    ys = []
    for sq in range(nseq):
        entry = []
        for dd in range(2):
            hl, pc = totals[2 * (2 * sq + dd)], totals[2 * (2 * sq + dd) + 1]
            state = h0_ref[sq, dd:dd + 1, :] if has_h0 else jnp.zeros((1, cw_cols), F32)
            rows = [None] * SUBLANES
            order = range(SUBLANES) if dd == 0 else range(SUBLANES - 1, -1, -1)
            for s in order:
                rows[s] = state
                state = pc[s:s + 1, :] * state + hl[s:s + 1, :]
            entry.append(jnp.concatenate(rows, axis=0))
            if emit_state:
                st_ref[sq, dd:dd + 1, :] = state
        lo, hi = sq * seq_len, (sq + 1) * seq_len
        grouped = (segs, SUBLANES, cw_cols)
        hsum = ((uf_ref[lo:hi, :].reshape(grouped) + af_ref[lo:hi, :].reshape(grouped) * entry[0][None])
                + (ub_ref[lo:hi, :].reshape(grouped) + ab_ref[lo:hi, :].reshape(grouped) * entry[1][None]))
        ys.append(hsum.reshape(seq_len, cw_cols) * gelu_y[lo:hi, :])
    y = jnp.concatenate(ys, axis=0) if nseq > 1 else ys[0]

    y = _to_natural_time(swap_ref[...], y.astype(BF16), seq_len)
    out = _mm(y, wo_ref[...])
    if single_step:
        o_ref[...] = x_ref[...] + _gate(m_ref, 1) * out
        return
    acc_ref[...] += out

    @pl.when(c == pl.num_programs(1) - 1)
    def _():
        o_ref[...] = x_ref[...] + _gate(m_ref, 1) * acc_ref[...]


def _lru_gate_weights(gate_w):
    nd, ng, nb, bw, _ = gate_w.shape
    per = LRU_GATE_CHUNK // bw
    nc = nb // per
    w = (0.5 * gate_w).astype(BF16).reshape(nd * ng, nc, per, bw, bw)
    rows = []
    for n in range(per):
        blk = jnp.transpose(w[:, :, n], (1, 2, 0, 3))
        blk = jnp.pad(blk, ((0, 0), (0, 0), (0, 0), (n * bw, (per - 1 - n) * bw)))
        rows.append(blk.reshape(nc, bw, nd * ng * per * bw))
    return jnp.concatenate(rows, axis=1)


def _lru(x, m, row_of_tile, norm_g, layer, w_in, conv_w, conv_b, gate_w, gate_b, lam, w_out,
         seq_len, h0, emit_state, single_step=False):
    t, d = x.shape
    r = w_out.shape[0]
    tm, gc = TOKEN_TILE, LRU_GATE_CHUNK
    cb = r if single_step else LRU_CHUNK
    assert t % tm == 0 and tm % seq_len == 0 and r % cb == 0 and cb % gc == 0
    assert seq_len % _SWAP_BLOCK == 0 and tm % _SWAP_SLAB == 0
    nc = r // cb
    nseq = tm // seq_len
    has_h0 = h0 is not None
    resident = dict(pipeline_mode=pl.Buffered(1)) if single_step else {}
    in_specs = [
        pl.BlockSpec((tm, d), lambda i, c: (i, 0)),
        pl.BlockSpec((None, None, 3 * N_SUB, d), lambda i, c: (layer, row_of_tile(i), 0, 0)),
        pl.BlockSpec((None, N_SUB, d), lambda i, c: (layer, 0, 0)),
        pl.BlockSpec((d, cb), lambda i, c: (0, c), **resident),
        pl.BlockSpec((d, cb), lambda i, c: (0, nc + c), **resident),
        pl.BlockSpec((conv_w.shape[0], cb), lambda i, c: (0, c)),
        pl.BlockSpec((1, cb), lambda i, c: (0, c)),
        pl.BlockSpec((cb // gc, gc, 4 * gc), lambda i, c: (c, 0, 0), **resident),
        pl.BlockSpec((4, cb), lambda i, c: (0, c)),
        pl.BlockSpec((2, cb), lambda i, c: (0, c)),
        pl.BlockSpec((cb, d), lambda i, c: (c, 0), **resident),
        pl.BlockSpec((_SWAP_SLAB, _SWAP_SLAB), lambda i, c: (0, 0)),
    ]
    args = [x, m, norm_g, w_in, w_in, conv_w, conv_b.reshape(1, r), _lru_gate_weights(gate_w),
            gate_b.reshape(4, r), lam, w_out, jnp.asarray(_block_swap_matrix(), BF16)]
    if has_h0:
        in_specs.append(pl.BlockSpec((nseq, 2, cb), lambda i, c: (i, 0, c)))
        args.append(h0)
    out_specs = [pl.BlockSpec((tm, d), lambda i, c: (i, 0))]
    out_shape = [jax.ShapeDtypeStruct((t, d), F32)]
    if emit_state:
        out_specs.append(pl.BlockSpec((nseq, 2, cb), lambda i, c: (i, 0, c)))
        out_shape.append(jax.ShapeDtypeStruct((t // seq_len, 2, r), F32))
    outs = pl.pallas_call(
        functools.partial(_lru_kernel, seq_len=seq_len, has_h0=has_h0, emit_state=emit_state,
                          conv_left=(conv_w.shape[0] - 1) // 2, single_step=single_step),
        grid=(t // tm, nc),
        in_specs=in_specs,
        out_specs=out_specs,
        out_shape=out_shape,
        scratch_shapes=([] if single_step
                        else [pltpu.VMEM((tm, d), BF16), pltpu.VMEM((tm, d), F32)])
        + [pltpu.VMEM((tm, cb), F32)] * 4,
        compiler_params=_params("arbitrary", "arbitrary"),
        name="rglru",
    )(*args)
    return outs if emit_state else (outs[0], None)


def _rope(x, cos, sin_signed, lane):
    hd = x.shape[1]
    partner = jnp.where((lane & 32) == 0, pltpu.roll(x, hd - 32, 1), pltpu.roll(x, 32, 1))
    return x * cos + partner * sin_signed


def _attn_kernel(*refs, seq_len, q_block, past_len, rope, emit_kv, group, kvh, fused):
    refs = list(refs)
    x_ref, m_ref, g_ref, wq_ref, wk_ref, wv_ref, qg_ref, kg_ref, wo_ref = refs[:9]
    pos = 9
    if rope:
        cos_ref, sin_ref = refs[pos:pos + 2]
        pos += 2
    if past_len:
        ck_ref, cv_ref = refs[pos:pos + 2]
        pos += 2
    o_ref = refs[pos]
    pos += 1
    if emit_kv:
        kn_ref, vn_ref = refs[pos:pos + 2]
        pos += 2
    if fused:
        q_s, k_s, v_s, o_s = refs[pos:]
    else:
        h_ref, acc_ref, q_s, k_s, v_s, o_s = refs[pos:]

    tm = x_ref.shape[0]
    hd = k_s.shape[-1]
    gw = group * hd
    nqb = seq_len // q_block
    nchunks = (tm // seq_len) * nqb
    nk = past_len + seq_len
    scale = hd ** -0.5

    def group_out(h, g, slot, cols, kv_all=None):
        if fused:
            qs, ks, vs, os_ = q_s.at[slot], k_s.at[slot], v_s.at[slot], o_s.at[slot]
            wq, wo = wq_ref[:, cols * gw:(cols + 1) * gw], wo_ref[cols * gw:(cols + 1) * gw, :]
        else:
            qs, ks, vs, os_ = q_s, k_s, v_s, o_s
            wq, wo = wq_ref[...], wo_ref[...]
        q = _mm(h, wq)
        if kv_all is None:
            k_raw, v = _mm(h, wk_ref[...]), _mm(h, wv_ref[...])
        else:
            k_raw, v = (a[:, cols * hd:(cols + 1) * hd] for a in kv_all)
        k = _rms(k_raw) * kg_ref[...]
        if emit_kv:
            kn_ref[pl.ds(g, tm, stride=kvh), :] = k
            vn_ref[pl.ds(g, tm, stride=kvh), :] = v
        if rope:
            lane = lax.broadcasted_iota(jnp.int32, (tm, hd), 1)
            cos, sin = cos_ref[...], sin_ref[...]
            k = _rope(k, cos, sin, lane)
        ks[past_len:past_len + tm, :] = k.astype(BF16)
        vs[past_len:past_len + tm, :] = v.astype(BF16)
        if past_len:
            ks[0:past_len, :] = ck_ref[pl.ds(g, past_len, stride=kvh), :].astype(BF16)
            vs[0:past_len, :] = cv_ref[pl.ds(g, past_len, stride=kvh), :].astype(BF16)
        for j in range(group):
            qh = _rms(q[:, j * hd:(j + 1) * hd]) * qg_ref[...]
            if rope:
                qh = _rope(qh, cos, sin, lane)
            qs[:, j * hd:(j + 1) * hd] = qh.astype(BF16)

        def chunk(ci, carry):
            r0 = pl.multiple_of(ci * q_block, q_block)
            if past_len:
                keys, vals = ks[...], vs[...]
            else:
                koff = pl.multiple_of((ci // nqb) * seq_len, seq_len)
                keys, vals = ks[pl.ds(koff, nk), :], vs[pl.ds(koff, nk), :]
            qc = jnp.concatenate(
                [qs[pl.ds(r0, q_block), j * hd:(j + 1) * hd] for j in range(group)], axis=0)
            raw = lax.dot_general(qc, keys, (((1,), (1,)), ((), ())), preferred_element_type=F32)
            e = jnp.exp2((raw - jnp.max(raw, axis=-1, keepdims=True)) * (scale * LOG2_E))
            inv = 1.0 / jnp.sum(e, axis=-1, keepdims=True)
            oc = jnp.dot(e.astype(BF16), vals, preferred_element_type=F32) * inv
            for j in range(group):
                os_[pl.ds(r0, q_block), j * hd:(j + 1) * hd] = (
                    oc[j * q_block:(j + 1) * q_block, :].astype(BF16))
            return carry

        lax.fori_loop(0, nchunks, chunk, 0, unroll=min(nchunks, 4))
        return _mm(os_[...], wo)

    if fused:
        x = x_ref[...]
        h = _sub_in(x, m_ref, g_ref, 1).astype(BF16)
        kv_all = (_mm(h, wk_ref[...]), _mm(h, wv_ref[...]))
        acc = group_out(h, 0, 0, 0, kv_all)
        for g in range(1, kvh):
            acc = acc + group_out(h, g, g, g, kv_all)
        o_ref[...] = x + _gate(m_ref, 1) * acc
        return

    gi = pl.program_id(1)

    @pl.when(gi == 0)
    def _():
        h_ref[...] = _sub_in(x_ref[...], m_ref, g_ref, 1).astype(BF16)
        acc_ref[...] = jnp.zeros_like(acc_ref)

    out = group_out(h_ref[...], gi, 0, 0)
    acc_ref[...] += out

    @pl.when(gi == pl.num_programs(1) - 1)
    def _():
        o_ref[...] = x_ref[...] + _gate(m_ref, 1) * acc_ref[...]


def _rope_tables(n_tok, hd):
    rows = n_tok // GRID_W
    r_idx = np.repeat(np.arange(rows), GRID_W).astype(np.float64)
    c_idx = np.tile(np.arange(GRID_W), rows).astype(np.float64)
    n_freq = hd // 4
    inv = ROPE_THETA ** (-np.arange(n_freq, dtype=np.float64) / n_freq)
    ang = np.stack([r_idx[:, None] * inv, c_idx[:, None] * inv], axis=1)
    cos, sin = np.cos(ang), np.sin(ang)
    cos_full = np.concatenate([cos, cos], axis=-1).reshape(n_tok, hd)
    sin_signed = np.concatenate([-sin, sin], axis=-1).reshape(n_tok, hd)
    return jnp.asarray(cos_full, F32), jnp.asarray(sin_signed, F32)


def _attn(x, m, row_of_tile, norm_g, layer, w_qkv, q_g, k_g, w_o, seq_len, q_block,
          cache_k=None, cache_v=None, rope=False, emit_kv=False, fused=False):
    t, d = x.shape
    hd, kvh = HEAD_DIM, N_KV_HEADS
    n_heads = w_o.shape[0] // hd
    group = n_heads // kvh
    gw = group * hd
    tm = TOKEN_TILE
    assert t % tm == 0 and tm % seq_len == 0 and seq_len % q_block == 0
    past_len = 0 if cache_k is None else cache_k.shape[1] // kvh
    assert past_len == 0 or tm == seq_len
    gs = kvh if fused else 1
    lead = (kvh,) if fused else ()
    in_specs = [
        pl.BlockSpec((tm, d), lambda i, g: (i, 0)),
        pl.BlockSpec((None, None, 3 * N_SUB, d), lambda i, g: (layer, row_of_tile(i), 0, 0)),
        pl.BlockSpec((None, N_SUB, d), lambda i, g: (layer, 0, 0)),
        pl.BlockSpec((d, gs * gw), lambda i, g: (0, g)),
        pl.BlockSpec((d, gs * hd), lambda i, g: (0, n_heads // gs + g)),
        pl.BlockSpec((d, gs * hd), lambda i, g: (0, (n_heads + kvh) // gs + g)),
        pl.BlockSpec((1, hd), lambda i, g: (0, 0)),
        pl.BlockSpec((1, hd), lambda i, g: (0, 0)),
        pl.BlockSpec((gs * gw, d), lambda i, g: (g, 0)),
    ]
    args = [x, m, norm_g, w_qkv, w_qkv, w_qkv, q_g.reshape(1, hd), k_g.reshape(1, hd), w_o]
    if rope:
        assert tm == seq_len
        cos, sin = _rope_tables(seq_len, hd)
        in_specs += [pl.BlockSpec((tm, hd), lambda i, g: (0, 0))] * 2
        args += [cos, sin]
    if past_len:
        in_specs += [pl.BlockSpec((None, past_len * kvh, hd), lambda i, g: (i, 0, 0))] * 2
        args += [cache_k, cache_v]
    out_specs = [pl.BlockSpec((tm, d), lambda i, g: (i, 0))]
    out_shape = [jax.ShapeDtypeStruct((t, d), F32)]
    if emit_kv:
        out_specs += [pl.BlockSpec((tm * kvh, hd), lambda i, g: (i, 0))] * 2
        out_shape += [jax.ShapeDtypeStruct((t * kvh, hd), F32)] * 2
    nkeys = past_len + tm
    outs = pl.pallas_call(
        functools.partial(_attn_kernel, seq_len=seq_len, q_block=q_block, past_len=past_len,
                          rope=rope, emit_kv=emit_kv, group=group, kvh=kvh, fused=fused),
        grid=(t // tm, kvh // gs),
        in_specs=in_specs,
        out_specs=out_specs,
        out_shape=out_shape,
        scratch_shapes=([] if fused else [pltpu.VMEM((tm, d), BF16), pltpu.VMEM((tm, d), F32)])
        + [pltpu.VMEM(lead + (tm, gw), BF16), pltpu.VMEM(lead + (nkeys, hd), BF16),
           pltpu.VMEM(lead + (nkeys, hd), BF16), pltpu.VMEM(lead + (tm, gw), BF16)],
        compiler_params=_params("arbitrary", "arbitrary"),
        name="gqa",
    )(*args)
    return outs


def kernel(x_prompt, x_sample, c, state_lru, cache_k, cache_v, c_ctx, mod_w, mod_b, norm_g,
           ffn_w_gu, ffn_w_down, lru_w_in, lru_conv_w, lru_conv_b, lru_gate_w, lru_gate_b,
           lru_lambda, lru_w_out, att_w_qkv, att_q_g, att_k_g, att_w_o, final_g):
    b, s, d = x_prompt.shape
    db, ds, _ = x_sample.shape
    depth = mod_w.shape[0]
    n_mixers = 2
    assert ds % TOKEN_TILE == 0 and 1 + db <= SUBLANES

    xp = x_prompt.reshape(b * s, d)
    xs = x_sample.reshape(db * ds, d)
    cvecs = jnp.concatenate([c_ctx[None], c, jnp.zeros((SUBLANES - 1 - db, d), F32)], axis=0)
    m = _modulation(cvecs, mod_w, mod_b).reshape(depth, SUBLANES, 3 * N_SUB, d)

    tiles_per_sample = ds // TOKEN_TILE
    prompt_row = lambda i: 0
    sample_row = lambda i: 1 + i // tiles_per_sample
    prompt_tiles = (b * s) // TOKEN_TILE
    both_row = lambda i: jnp.where(i < prompt_tiles, 0, 1 + (i - prompt_tiles) // tiles_per_sample)

    new_states, new_k, new_v = [], [], []
    for layer in range(depth):
        j = layer // n_mixers
        last = layer == depth - 1
        xp, xs = _ffn(xp, xs, m, both_row, norm_g, ffn_w_gu, ffn_w_down, layer, 0, 0)
        if layer % n_mixers == 0:
            lru_p = (lru_w_in[j], lru_conv_w[j], lru_conv_b[j], lru_gate_w[j], lru_gate_b[j],
                     lru_lambda[j], lru_w_out[j])
            xp, st = _lru(xp, m, prompt_row, norm_g, layer, *lru_p, seq_len=s, h0=None,
                          emit_state=True, single_step=True)
            new_states.append(st)
            xs, _ = _lru(xs, m, sample_row, norm_g, layer, *lru_p, seq_len=ds,
                         h0=state_lru[:, j], emit_state=False, single_step=True)
        else:
            att_p = (att_w_qkv[j], att_q_g[j], att_k_g[j], att_w_o[j])
            xp, kp, vp = _attn(xp, m, prompt_row, norm_g, layer, *att_p, seq_len=s, q_block=s,
                               emit_kv=True, fused=True)
            new_k.append(kp.reshape(b, s, N_KV_HEADS, HEAD_DIM))
            new_v.append(vp.reshape(b, s, N_KV_HEADS, HEAD_DIM))
            past = cache_k.shape[2]
            ck = cache_k[:, j].reshape(db, past * N_KV_HEADS, HEAD_DIM)
            cv = cache_v[:, j].reshape(db, past * N_KV_HEADS, HEAD_DIM)
            (xs,) = _attn(xs, m, sample_row, norm_g, layer, *att_p, seq_len=ds, q_block=128,
                          cache_k=ck, cache_v=cv, rope=True)
        fg = final_g if last else None
        xp, xs = _ffn(xp, xs, m, both_row, norm_g, ffn_w_gu, ffn_w_down, layer, 1, 2, final_g=fg)

    y_prompt = xp.reshape(b, s, d)
    y_sample = xs.reshape(db, ds, d)
    return (y_prompt, y_sample, jnp.stack(new_states, axis=1), jnp.stack(new_k, axis=1),
            jnp.stack(new_v, axis=1))
```

```python
import functools

import jax
import jax.numpy as jnp
import numpy as np
from jax import lax
from jax.experimental import pallas as pl
from jax.experimental.pallas import tpu as pltpu

F32 = jnp.float32
BF16 = jnp.bfloat16

EPS = 1e-6
LRU_C = 8.0
LOG2_E = 1.4426950408889634
GRID_W = 64
ROPE_THETA = 10000.0
N_SUB = 3
HEAD_DIM = 128
N_KV_HEADS = 2

V7X_VMEM_LIMIT_BYTES = 56 * 1024 * 1024
SUBLANES = 8
TOKEN_TILE = 1024
FF_CHUNK = 256
FFN_ROW_BLOCK = 256
FFN_STAGE_SLOTS = 3
LRU_CHUNK = 512
LRU_GATE_CHUNK = 256
MOD_CHUNK = 1536


def _params(*semantics):
    return pltpu.CompilerParams(dimension_semantics=semantics,
                                vmem_limit_bytes=V7X_VMEM_LIMIT_BYTES)


def _mm(a_bf16, w_f32):
    return jnp.dot(a_bf16, w_f32.astype(BF16), preferred_element_type=F32)


def _rms(x):
    return x * lax.rsqrt(jnp.mean(x * x, axis=-1, keepdims=True) + EPS)


def _sub_in(x, m_ref, g_ref, sidx):
    shift = m_ref[3 * sidx:3 * sidx + 1, :]
    scale = m_ref[3 * sidx + 1:3 * sidx + 2, :]
    return (_rms(x) * g_ref[sidx:sidx + 1, :]) * (1.0 + scale) + shift


def _gate(m_ref, sidx):
    return m_ref[3 * sidx + 2:3 * sidx + 3, :]


def _mod_kernel(cv_ref, w_ref, b_ref, o_ref):
    cv = cv_ref[...]
    act = (cv * jax.nn.sigmoid(cv)).astype(BF16)
    o_ref[...] = _mm(act, w_ref[...]) + b_ref[pl.ds(pl.program_id(0), 1), :]


def _modulation(cvecs, mod_w, mod_b):
    depth, d, n = mod_w.shape
    rows = cvecs.shape[0]
    tn = MOD_CHUNK
    assert n % tn == 0
    return pl.pallas_call(
        _mod_kernel,
        grid=(depth, n // tn),
        in_specs=[
            pl.BlockSpec((rows, d), lambda l, j: (0, 0)),
            pl.BlockSpec((None, d, tn), lambda l, j: (l, 0, j)),
            pl.BlockSpec((depth, tn), lambda l, j: (0, j)),
        ],
        out_specs=pl.BlockSpec((None, rows, tn), lambda l, j: (l, 0, j)),
        out_shape=jax.ShapeDtypeStruct((depth, rows, n), F32),
        compiler_params=_params("arbitrary", "arbitrary"),
        name="modulation",
    )(cvecs, mod_w, mod_b)


def _ffn_kernel(*refs, layer, s, sidx, final, n_first):
    if final:
        (m_ref, g_ref, fg_ref, xa_hbm, xb_hbm, wgu_hbm, wd_hbm, oa_hbm, ob_hbm,
         wgu_res, wd_res, act_ref, h0_ref, act0_ref, gu_buf, wd_buf, xbuf, obuf,
         w_sem, xsem, osem) = refs
    else:
        (m_ref, g_ref, xa_hbm, xb_hbm, wgu_hbm, wd_hbm, oa_hbm, ob_hbm,
         wgu_res, wd_res, act_ref, h0_ref, act0_ref, gu_buf, wd_buf, xbuf, obuf,
         w_sem, xsem, osem) = refs
    i = pl.program_id(0)
    n = pl.num_programs(0)
    tm = xbuf.shape[1]
    _, d, tf = wgu_res.shape
    f = wd_res.shape[0]
    nk = f // tf
    rb = act_ref.shape[0]
    nslot = gu_buf.shape[0]
    ahead = nslot - 1
    slot = lax.rem(i, 2)
    other = 1 - slot

    def stage_copies(k, sl):
        gcols = pl.ds(pl.multiple_of(k * tf, tf), tf)
        ucols = pl.ds(pl.multiple_of(f + k * tf, tf), tf)
        return (
            pltpu.make_async_copy(wgu_hbm.at[layer, s, :, gcols], gu_buf.at[sl, 0], w_sem.at[sl, 0]),
            pltpu.make_async_copy(wgu_hbm.at[layer, s, :, ucols], gu_buf.at[sl, 1], w_sem.at[sl, 1]),
            pltpu.make_async_copy(wd_hbm.at[layer, s, gcols, :], wd_buf.at[sl], w_sem.at[sl, 2]),
        )

    def tile_rows(hbm, tile):
        return hbm.at[pl.ds(pl.multiple_of(tile * tm, tm), tm), :]

    def x_copy(which, tile, sl):
        return pltpu.make_async_copy(tile_rows((xa_hbm, xb_hbm)[which], tile), xbuf.at[sl],
                                     xsem.at[sl])

    def o_copy(which, tile, sl):
        return pltpu.make_async_copy(obuf.at[sl], tile_rows((oa_hbm, ob_hbm)[which], tile),
                                     osem.at[sl])

    def start_by_stream(make, tile, sl):
        @pl.when(tile < n_first)
        def _():
            make(0, tile, sl).start()

        @pl.when(tile >= n_first)
        def _():
            make(1, tile - n_first, sl).start()

    @pl.when(i == 0)
    def _():
        x_copy(0, 0, 0).start()
        for k in range(ahead):
            for cp in stage_copies(k, k):
                cp.start()

    @pl.when(i + 1 < n)
    def _():
        start_by_stream(x_copy, i + 1, other)

    x_copy(0, 0, slot).wait()

    @pl.when(i >= 2)
    def _():
        o_copy(0, 0, slot).wait()

    def swiglu(h, wg, wu):
        gt = jnp.dot(h, wg, preferred_element_type=F32)
        up = jnp.dot(h, wu, preferred_element_type=F32)
        return ((gt * jax.nn.sigmoid(gt)) * up).astype(BF16)

    def down_and_store(x, rows):
        ff = jnp.dot(act_ref[...], wd_res[...], preferred_element_type=F32)
        y = x + (0.5 * _gate(m_ref, sidx)) * ff
        if final:
            y = _rms(y) * fg_ref[...]
        obuf[slot, rows, :] = y

    def row_block(r, carry):
        rows = pl.ds(pl.multiple_of(r * rb, rb), rb)
        x = xbuf[slot, rows, :]
        h = _sub_in(x, m_ref, g_ref, sidx).astype(BF16)
        for k in range(nk):
            act_ref[:, k * tf:(k + 1) * tf] = swiglu(h, wgu_res[k], wgu_res[nk + k])
        down_and_store(x, rows)
        return carry

    def first_tile_chunk(k, carry):
        sl = lax.rem(k, nslot)
        for cp in stage_copies(k, sl):
            cp.wait()

        @pl.when(k + ahead < nk)
        def _():
            for cp in stage_copies(k + ahead, lax.rem(k + ahead, nslot)):
                cp.start()

        wg = gu_buf[sl, 0].astype(BF16)
        wu = gu_buf[sl, 1].astype(BF16)
        wgu_res[k] = wg
        wgu_res[nk + k] = wu
        wd_res[pl.ds(pl.multiple_of(k * tf, tf), tf), :] = wd_buf[sl].astype(BF16)
        act0_ref[k] = swiglu(h0_ref[...], wg, wu)
        return carry

    def first_tile_down(r, carry):
        rows = pl.ds(pl.multiple_of(r * rb, rb), rb)
        for k in range(nk):
            act_ref[:, k * tf:(k + 1) * tf] = act0_ref[k, rows, :]
        down_and_store(xbuf[slot, rows, :], rows)
        return carry

    @pl.when(i == 0)
    def _():
        h0_ref[...] = _sub_in(xbuf[slot], m_ref, g_ref, sidx).astype(BF16)
        lax.fori_loop(0, nk, first_tile_chunk, 0, unroll=2)
        lax.fori_loop(0, tm // rb, first_tile_down, 0, unroll=2)

    @pl.when(i > 0)
    def _():
        lax.fori_loop(0, tm // rb, row_block, 0, unroll=2)

    start_by_stream(o_copy, i, slot)

    @pl.when(i == n - 1)
    def _():
        o_copy(0, 0, other).wait()
        o_copy(0, 0, slot).wait()


def _ffn(xa, xb, m, row_of_tile, norm_g, w_gu, w_down, layer, s, sidx, final_g=None):
    (ta, d), tb = xa.shape, xb.shape[0]
    f = w_down.shape[2]
    tm, tf, rb = TOKEN_TILE, FF_CHUNK, FFN_ROW_BLOCK
    assert ta % tm == 0 and tb % tm == 0 and f % tf == 0 and (ta + tb) // tm >= 2
    assert tm % rb == 0 and f // tf >= FFN_STAGE_SLOTS
    final = final_g is not None
    in_specs = [
        pl.BlockSpec((None, None, 3 * N_SUB, d), lambda i: (layer, row_of_tile(i), 0, 0)),
        pl.BlockSpec((None, N_SUB, d), lambda i: (layer, 0, 0)),
    ]
    args = [m, norm_g]
    if final:
        in_specs.append(pl.BlockSpec((1, d), lambda i: (0, 0)))
        args.append(final_g.reshape(1, d))
    in_specs += [pl.BlockSpec(memory_space=pl.ANY)] * 4
    args += [xa, xb, w_gu, w_down]
    return pl.pallas_call(
        functools.partial(_ffn_kernel, layer=layer, s=s, sidx=sidx, final=final,
                          n_first=ta // tm),
        grid=((ta + tb) // tm,),
        in_specs=in_specs,
        out_specs=[pl.BlockSpec(memory_space=pl.ANY)] * 2,
        out_shape=[jax.ShapeDtypeStruct((ta, d), F32), jax.ShapeDtypeStruct((tb, d), F32)],
        scratch_shapes=[
            pltpu.VMEM((2 * f // tf, d, tf), BF16), pltpu.VMEM((f, d), BF16),
            pltpu.VMEM((rb, f), BF16),
            pltpu.VMEM((tm, d), BF16), pltpu.VMEM((f // tf, tm, tf), BF16),
            pltpu.VMEM((FFN_STAGE_SLOTS, 2, d, tf), F32),
            pltpu.VMEM((FFN_STAGE_SLOTS, tf, d), F32),
            pltpu.VMEM((2, tm, d), F32), pltpu.VMEM((2, tm, d), F32),
            pltpu.SemaphoreType.DMA((FFN_STAGE_SLOTS, 3)),
            pltpu.SemaphoreType.DMA((2,)),
            pltpu.SemaphoreType.DMA((2,)),
        ],
        compiler_params=_params("arbitrary"),
        name="ffn",
    )(*args)


def _gelu_tanh(x):
    c = np.float32(np.sqrt(2.0 / np.pi))
    return x * (0.5 * (1.0 + jnp.tanh(c * (x + 0.044715 * (x * x * x)))))


_SWAP_BLOCK = SUBLANES * SUBLANES
_SWAP_SLAB = 4 * _SWAP_BLOCK


def _block_swap_matrix():
    p = np.arange(_SWAP_SLAB)
    blk, a, b = p // _SWAP_BLOCK, (p % _SWAP_BLOCK) // SUBLANES, p % SUBLANES
    mat = np.zeros((_SWAP_SLAB, _SWAP_SLAB), np.float32)
    mat[p, blk * _SWAP_BLOCK + b * SUBLANES + a] = 1.0
    return mat


def _swap_blocks(q, rows):
    parts = [jnp.dot(q, rows[lo:lo + _SWAP_SLAB, :], preferred_element_type=F32)
             for lo in range(0, rows.shape[0], _SWAP_SLAB)]
    return jnp.concatenate(parts, axis=0) if len(parts) > 1 else parts[0]


def _regroup(rows, seq_len, to_permuted):
    segs = seq_len // SUBLANES
    pieces = []
    for lo in range(0, rows.shape[0], seq_len):
        outer, inner = (segs // SUBLANES, SUBLANES) if to_permuted else (SUBLANES, segs // SUBLANES)
        for o in range(outer):
            for i in range(inner):
                jh, s = (o, i) if to_permuted else (i, o)
                src = s * segs + jh * SUBLANES if to_permuted else jh * _SWAP_BLOCK + s * SUBLANES
                pieces.append(rows[lo + src:lo + src + SUBLANES, :])
    return jnp.concatenate(pieces, axis=0)


def _to_permuted_time(q, h_f32, seq_len):
    return _swap_blocks(q, _regroup(h_f32, seq_len, True).astype(BF16)).astype(BF16)


def _to_natural_time(q, y_bf16, seq_len):
    return _regroup(_swap_blocks(q, y_bf16), seq_len, False).astype(BF16)


def _lru_kernel(*refs, seq_len, has_h0, emit_state, conv_left, single_step):
    refs = list(refs)
    (x_ref, m_ref, g_ref, wx_ref, wy_ref, cw_ref, cb_ref, gw_ref, gb_ref, lam_ref,
     wo_ref, swap_ref) = refs[:12]
    pos = 12
    h0_ref = None
    if has_h0:
        h0_ref = refs[pos]
        pos += 1
    o_ref = refs[pos]
    pos += 1
    st_ref = None
    if emit_state:
        st_ref = refs[pos]
        pos += 1
    if single_step:
        af_ref, uf_ref, ab_ref, ub_ref = refs[pos:]
    else:
        h_ref, acc_ref, af_ref, uf_ref, ab_ref, ub_ref = refs[pos:]

    tm, cw_cols = af_ref.shape
    nseq = tm // seq_len
    segs = seq_len // SUBLANES

    def permuted_input():
        return _to_permuted_time(swap_ref[...], _sub_in(x_ref[...], m_ref, g_ref, 1), seq_len)

    if single_step:
        h = permuted_input()
    else:
        c = pl.program_id(1)

        @pl.when(c == 0)
        def _():
            h_ref[...] = permuted_input()
            acc_ref[...] = jnp.zeros_like(acc_ref)

        h = h_ref[...]
    xb = _mm(h, wx_ref[...])
    yb = _mm(h, wy_ref[...])

    sub = lax.broadcasted_iota(jnp.int32, (SUBLANES, cw_cols), 0)

    def next_segment(grp):
        return jnp.where(sub < SUBLANES - 1, pltpu.roll(grp, SUBLANES - 1, 0), 0.0)

    def prev_segment(grp):
        return jnp.where(sub > 0, pltpu.roll(grp, 1, 0), 0.0)

    def tap(seq, off):
        n = abs(off) * SUBLANES
        if off == 0:
            return seq
        if off > 0:
            edge = [next_segment(seq[g * SUBLANES:(g + 1) * SUBLANES, :]) for g in range(off)]
            return jnp.concatenate([seq[n:, :]] + edge, axis=0)
        start = seq_len - n
        edge = [prev_segment(seq[start + g * SUBLANES:start + (g + 1) * SUBLANES, :])
                for g in range(-off)]
        return jnp.concatenate(edge + [seq[:start, :]], axis=0)

    cw = cw_ref[...]
    assert cw.shape[0] - 1 < segs
    xcs = []
    for sq in range(nseq):
        seq = xb[sq * seq_len:(sq + 1) * seq_len, :]
        acc = cb_ref[...]
        for k in range(cw.shape[0]):
            acc = acc + tap(seq, k - conv_left) * cw[k:k + 1, :]
        xcs.append(acc)
    xc = jnp.concatenate(xcs, axis=0) if nseq > 1 else xcs[0]

    gb = gb_ref[...]
    lam = lam_ref[...]
    gc = gw_ref.shape[1]
    for ch in range(cw_cols // gc):
        cols = slice(ch * gc, (ch + 1) * gc)
        xcc = xc[:, cols]
        gl = jnp.dot(xcc.astype(BF16), gw_ref[ch], preferred_element_type=F32)
        for d, (a_ref, u_ref) in enumerate(((af_ref, uf_ref), (ab_ref, ub_ref))):
            tr = jnp.tanh(gl[:, (2 * d) * gc:(2 * d + 1) * gc] + 0.5 * gb[d, 0:1, cols])
            ig = 0.5 * jnp.tanh(gl[:, (2 * d + 1) * gc:(2 * d + 2) * gc]
                                + 0.5 * gb[d, 1:2, cols]) + 0.5
            lm = lam[d:d + 1, cols]
            log_sig = jnp.minimum(lm, 0.0) - jnp.log1p(jnp.exp(-jnp.abs(lm)))
            c2 = (0.5 * LRU_C * LOG2_E) * log_sig
            a = jnp.exp2(c2 * tr + c2)
            v = 1.0 - a * a
            u = jnp.where(v > 0.0, v * lax.rsqrt(v), 0.0) * (ig * xcc)
            a_ref[:, cols] = a
            u_ref[:, cols] = u

    zero = jnp.zeros((SUBLANES, cw_cols), F32)
    one = jnp.ones((SUBLANES, cw_cols), F32)

    def local_scan(j, carry):
        out = []
        for sq in range(nseq):
            for dd, (a_ref, u_ref) in enumerate(((af_ref, uf_ref), (ab_ref, ub_ref))):
                grp = j if dd == 0 else segs - 1 - j
                rows = pl.ds(pl.multiple_of(sq * seq_len + grp * SUBLANES, SUBLANES), SUBLANES)
                hl, pc = carry[2 * (2 * sq + dd)], carry[2 * (2 * sq + dd) + 1]
                a = a_ref[rows, :]
                hl = a * hl + u_ref[rows, :]
                pc = a * pc
                u_ref[rows, :] = hl
                a_ref[rows, :] = pc
                out += [hl, pc]
        return tuple(out)

    totals = lax.fori_loop(0, segs, local_scan, (zero, one) * (2 * nseq), unroll=True)

    gelu_y = _gelu_tanh(yb)
    ys = []
    for sq in range(nseq):
        entry = []
        for dd in range(2):
            hl, pc = totals[2 * (2 * sq + dd)], totals[2 * (2 * sq + dd) + 1]
            state = h0_ref[sq, dd:dd + 1, :] if has_h0 else jnp.zeros((1, cw_cols), F32)
            rows = [None] * SUBLANES
            order = range(SUBLANES) if dd == 0 else range(SUBLANES - 1, -1, -1)
            for s in order:
                rows[s] = state
                state = pc[s:s + 1, :] * state + hl[s:s + 1, :]
            entry.append(jnp.concatenate(rows, axis=0))
            if emit_state:
                st_ref[sq, dd:dd + 1, :] = state
        lo, hi = sq * seq_len, (sq + 1) * seq_len
        grouped = (segs, SUBLANES, cw_cols)
        hsum = ((uf_ref[lo:hi, :].reshape(grouped) + af_ref[lo:hi, :].reshape(grouped) * entry[0][None])
                + (ub_ref[lo:hi, :].reshape(grouped) + ab_ref[lo:hi, :].reshape(grouped) * entry[1][None]))
        ys.append(hsum.reshape(seq_len, cw_cols) * gelu_y[lo:hi, :])
    y = jnp.concatenate(ys, axis=0) if nseq > 1 else ys[0]

    y = _to_natural_time(swap_ref[...], y.astype(BF16), seq_len)
    out = _mm(y, wo_ref[...])
    if single_step:
        o_ref[...] = x_ref[...] + _gate(m_ref, 1) * out
        return
    acc_ref[...] += out

    @pl.when(c == pl.num_programs(1) - 1)
    def _():
        o_ref[...] = x_ref[...] + _gate(m_ref, 1) * acc_ref[...]


def _lru_gate_weights(gate_w):
    nd, ng, nb, bw, _ = gate_w.shape
    per = LRU_GATE_CHUNK // bw
    nc = nb // per
    w = (0.5 * gate_w).astype(BF16).reshape(nd * ng, nc, per, bw, bw)
    rows = []
    for n in range(per):
        blk = jnp.transpose(w[:, :, n], (1, 2, 0, 3))
        blk = jnp.pad(blk, ((0, 0), (0, 0), (0, 0), (n * bw, (per - 1 - n) * bw)))
        rows.append(blk.reshape(nc, bw, nd * ng * per * bw))
    return jnp.concatenate(rows, axis=1)


def _lru(x, m, row_of_tile, norm_g, layer, w_in, conv_w, conv_b, gate_w, gate_b, lam, w_out,
         seq_len, h0, emit_state, single_step=False):
    t, d = x.shape
    r = w_out.shape[0]
    tm, gc = TOKEN_TILE, LRU_GATE_CHUNK
    cb = r if single_step else LRU_CHUNK
    assert t % tm == 0 and tm % seq_len == 0 and r % cb == 0 and cb % gc == 0
    assert seq_len % _SWAP_BLOCK == 0 and tm % _SWAP_SLAB == 0
    nc = r // cb
    nseq = tm // seq_len
    has_h0 = h0 is not None
    resident = dict(pipeline_mode=pl.Buffered(1)) if single_step else {}
    in_specs = [
        pl.BlockSpec((tm, d), lambda i, c: (i, 0)),
        pl.BlockSpec((None, None, 3 * N_SUB, d), lambda i, c: (layer, row_of_tile(i), 0, 0)),
        pl.BlockSpec((None, N_SUB, d), lambda i, c: (layer, 0, 0)),
        pl.BlockSpec((d, cb), lambda i, c: (0, c), **resident),
        pl.BlockSpec((d, cb), lambda i, c: (0, nc + c), **resident),
        pl.BlockSpec((conv_w.shape[0], cb), lambda i, c: (0, c)),
        pl.BlockSpec((1, cb), lambda i, c: (0, c)),
        pl.BlockSpec((cb // gc, gc, 4 * gc), lambda i, c: (c, 0, 0), **resident),
        pl.BlockSpec((2, 2, cb), lambda i, c: (0, 0, c)),
        pl.BlockSpec((2, cb), lambda i, c: (0, c)),
        pl.BlockSpec((cb, d), lambda i, c: (c, 0), **resident),
        pl.BlockSpec((_SWAP_SLAB, _SWAP_SLAB), lambda i, c: (0, 0)),
    ]
    args = [x, m, norm_g, w_in, w_in, conv_w, conv_b.reshape(1, r), _lru_gate_weights(gate_w),
            gate_b, lam, w_out, jnp.asarray(_block_swap_matrix(), BF16)]
    if has_h0:
        in_specs.append(pl.BlockSpec((nseq, 2, cb), lambda i, c: (i, 0, c)))
        args.append(h0)
    out_specs = [pl.BlockSpec((tm, d), lambda i, c: (i, 0))]
    out_shape = [jax.ShapeDtypeStruct((t, d), F32)]
    if emit_state:
        out_specs.append(pl.BlockSpec((nseq, 2, cb), lambda i, c: (i, 0, c)))
        out_shape.append(jax.ShapeDtypeStruct((t // seq_len, 2, r), F32))
    outs = pl.pallas_call(
        functools.partial(_lru_kernel, seq_len=seq_len, has_h0=has_h0, emit_state=emit_state,
                          conv_left=(conv_w.shape[0] - 1) // 2, single_step=single_step),
        grid=(t // tm, nc),
        in_specs=in_specs,
        out_specs=out_specs,
        out_shape=out_shape,
        scratch_shapes=([] if single_step
                        else [pltpu.VMEM((tm, d), BF16), pltpu.VMEM((tm, d), F32)])
        + [pltpu.VMEM((tm, cb), F32)] * 4,
        compiler_params=_params("arbitrary", "arbitrary"),
        name="rglru",
    )(*args)
    return outs if emit_state else (outs[0], None)


def _rope(x, cos, sin_signed, lane):
    hd = x.shape[1]
    partner = jnp.where((lane & 32) == 0, pltpu.roll(x, hd - 32, 1), pltpu.roll(x, 32, 1))
    return x * cos + partner * sin_signed


def _attn_kernel(*refs, seq_len, q_block, past_len, rope, emit_kv, group, kvh, fused):
    refs = list(refs)
    x_ref, m_ref, g_ref, wq_ref, wk_ref, wv_ref, qg_ref, kg_ref, wo_ref = refs[:9]
    pos = 9
    if rope:
        cos_ref, sin_ref = refs[pos:pos + 2]
        pos += 2
    if past_len:
        ck_ref, cv_ref = refs[pos:pos + 2]
        pos += 2
    o_ref = refs[pos]
    pos += 1
    if emit_kv:
        kn_ref, vn_ref = refs[pos:pos + 2]
        pos += 2
    if fused:
        q_s, k_s, v_s, o_s = refs[pos:]
    else:
        h_ref, acc_ref, q_s, k_s, v_s, o_s = refs[pos:]

    tm = x_ref.shape[0]
    hd = k_s.shape[-1]
    gw = group * hd
    nqb = seq_len // q_block
    nchunks = (tm // seq_len) * nqb
    nk = past_len + seq_len
    scale = hd ** -0.5

    def group_out(h, g, slot, cols, kv_all=None):
        if fused:
            qs, ks, vs, os_ = q_s.at[slot], k_s.at[slot], v_s.at[slot], o_s.at[slot]
            wq, wo = wq_ref[:, cols * gw:(cols + 1) * gw], wo_ref[cols * gw:(cols + 1) * gw, :]
        else:
            qs, ks, vs, os_ = q_s, k_s, v_s, o_s
            wq, wo = wq_ref[...], wo_ref[...]
        q = _mm(h, wq)
        if kv_all is None:
            k_raw, v = _mm(h, wk_ref[...]), _mm(h, wv_ref[...])
        else:
            k_raw, v = (a[:, cols * hd:(cols + 1) * hd] for a in kv_all)
        k = _rms(k_raw) * kg_ref[...]
        if emit_kv:
            kn_ref[pl.ds(g, tm, stride=kvh), :] = k
            vn_ref[pl.ds(g, tm, stride=kvh), :] = v
        if rope:
            lane = lax.broadcasted_iota(jnp.int32, (tm, hd), 1)
            cos, sin = cos_ref[...], sin_ref[...]
            k = _rope(k, cos, sin, lane)
        ks[past_len:past_len + tm, :] = k.astype(BF16)
        vs[past_len:past_len + tm, :] = v.astype(BF16)
        if past_len:
            ks[0:past_len, :] = ck_ref[pl.ds(g, past_len, stride=kvh), :].astype(BF16)
            vs[0:past_len, :] = cv_ref[pl.ds(g, past_len, stride=kvh), :].astype(BF16)
        for j in range(group):
            qh = _rms(q[:, j * hd:(j + 1) * hd]) * qg_ref[...]
            if rope:
                qh = _rope(qh, cos, sin, lane)
            qs[:, j * hd:(j + 1) * hd] = qh.astype(BF16)

        def chunk(ci, carry):
            r0 = pl.multiple_of(ci * q_block, q_block)
            if past_len:
                keys, vals = ks[...], vs[...]
            else:
                koff = pl.multiple_of((ci // nqb) * seq_len, seq_len)
                keys, vals = ks[pl.ds(koff, nk), :], vs[pl.ds(koff, nk), :]
            qc = jnp.concatenate(
                [qs[pl.ds(r0, q_block), j * hd:(j + 1) * hd] for j in range(group)], axis=0)
            raw = lax.dot_general(qc, keys, (((1,), (1,)), ((), ())), preferred_element_type=F32)
            e = jnp.exp2((raw - jnp.max(raw, axis=-1, keepdims=True)) * (scale * LOG2_E))
            inv = 1.0 / jnp.sum(e, axis=-1, keepdims=True)
            oc = jnp.dot(e.astype(BF16), vals, preferred_element_type=F32) * inv
            for j in range(group):
                os_[pl.ds(r0, q_block), j * hd:(j + 1) * hd] = (
                    oc[j * q_block:(j + 1) * q_block, :].astype(BF16))
            return carry

        lax.fori_loop(0, nchunks, chunk, 0, unroll=min(nchunks, 4))
        return _mm(os_[...], wo)

    if fused:
        x = x_ref[...]
        h = _sub_in(x, m_ref, g_ref, 1).astype(BF16)
        kv_all = (_mm(h, wk_ref[...]), _mm(h, wv_ref[...]))
        acc = group_out(h, 0, 0, 0, kv_all)
        for g in range(1, kvh):
            acc = acc + group_out(h, g, g, g, kv_all)
        o_ref[...] = x + _gate(m_ref, 1) * acc
        return

    gi = pl.program_id(1)

    @pl.when(gi == 0)
    def _():
        h_ref[...] = _sub_in(x_ref[...], m_ref, g_ref, 1).astype(BF16)
        acc_ref[...] = jnp.zeros_like(acc_ref)

    out = group_out(h_ref[...], gi, 0, 0)
    acc_ref[...] += out

    @pl.when(gi == pl.num_programs(1) - 1)
    def _():
        o_ref[...] = x_ref[...] + _gate(m_ref, 1) * acc_ref[...]


def _rope_tables(n_tok, hd):
    rows = n_tok // GRID_W
    r_idx = np.repeat(np.arange(rows), GRID_W).astype(np.float64)
    c_idx = np.tile(np.arange(GRID_W), rows).astype(np.float64)
    n_freq = hd // 4
    inv = ROPE_THETA ** (-np.arange(n_freq, dtype=np.float64) / n_freq)
    ang = np.stack([r_idx[:, None] * inv, c_idx[:, None] * inv], axis=1)
    cos, sin = np.cos(ang), np.sin(ang)
    cos_full = np.concatenate([cos, cos], axis=-1).reshape(n_tok, hd)
    sin_signed = np.concatenate([-sin, sin], axis=-1).reshape(n_tok, hd)
    return jnp.asarray(cos_full, F32), jnp.asarray(sin_signed, F32)


def _attn(x, m, row_of_tile, norm_g, layer, w_qkv, q_g, k_g, w_o, seq_len, q_block,
          cache_k=None, cache_v=None, rope=False, emit_kv=False, fused=False):
    t, d = x.shape
    hd, kvh = HEAD_DIM, N_KV_HEADS
    n_heads = w_o.shape[0] // hd
    group = n_heads // kvh
    gw = group * hd
    tm = TOKEN_TILE
    assert t % tm == 0 and tm % seq_len == 0 and seq_len % q_block == 0
    past_len = 0 if cache_k is None else cache_k.shape[1] // kvh
    assert past_len == 0 or tm == seq_len
    gs = kvh if fused else 1
    lead = (kvh,) if fused else ()
    in_specs = [
        pl.BlockSpec((tm, d), lambda i, g: (i, 0)),
        pl.BlockSpec((None, None, 3 * N_SUB, d), lambda i, g: (layer, row_of_tile(i), 0, 0)),
        pl.BlockSpec((None, N_SUB, d), lambda i, g: (layer, 0, 0)),
        pl.BlockSpec((d, gs * gw), lambda i, g: (0, g)),
        pl.BlockSpec((d, gs * hd), lambda i, g: (0, n_heads // gs + g)),
        pl.BlockSpec((d, gs * hd), lambda i, g: (0, (n_heads + kvh) // gs + g)),
        pl.BlockSpec((1, hd), lambda i, g: (0, 0)),
        pl.BlockSpec((1, hd), lambda i, g: (0, 0)),
        pl.BlockSpec((gs * gw, d), lambda i, g: (g, 0)),
    ]
    args = [x, m, norm_g, w_qkv, w_qkv, w_qkv, q_g.reshape(1, hd), k_g.reshape(1, hd), w_o]
    if rope:
        assert tm == seq_len
        cos, sin = _rope_tables(seq_len, hd)
        in_specs += [pl.BlockSpec((tm, hd), lambda i, g: (0, 0))] * 2
        args += [cos, sin]
    if past_len:
        in_specs += [pl.BlockSpec((None, past_len * kvh, hd), lambda i, g: (i, 0, 0))] * 2
        args += [cache_k, cache_v]
    out_specs = [pl.BlockSpec((tm, d), lambda i, g: (i, 0))]
    out_shape = [jax.ShapeDtypeStruct((t, d), F32)]
    if emit_kv:
        out_specs += [pl.BlockSpec((tm * kvh, hd), lambda i, g: (i, 0))] * 2
        out_shape += [jax.ShapeDtypeStruct((t * kvh, hd), F32)] * 2
    nkeys = past_len + tm
    outs = pl.pallas_call(
        functools.partial(_attn_kernel, seq_len=seq_len, q_block=q_block, past_len=past_len,
                          rope=rope, emit_kv=emit_kv, group=group, kvh=kvh, fused=fused),
        grid=(t // tm, kvh // gs),
        in_specs=in_specs,
        out_specs=out_specs,
        out_shape=out_shape,
        scratch_shapes=([] if fused else [pltpu.VMEM((tm, d), BF16), pltpu.VMEM((tm, d), F32)])
        + [pltpu.VMEM(lead + (tm, gw), BF16), pltpu.VMEM(lead + (nkeys, hd), BF16),
           pltpu.VMEM(lead + (nkeys, hd), BF16), pltpu.VMEM(lead + (tm, gw), BF16)],
        compiler_params=_params("arbitrary", "arbitrary"),
        name="gqa",
    )(*args)
    return outs


def kernel(x_prompt, x_sample, c, state_lru, cache_k, cache_v, c_ctx, mod_w, mod_b, norm_g,
           ffn_w_gu, ffn_w_down, lru_w_in, lru_conv_w, lru_conv_b, lru_gate_w, lru_gate_b,
           lru_lambda, lru_w_out, att_w_qkv, att_q_g, att_k_g, att_w_o, final_g):
    b, s, d = x_prompt.shape
    db, ds, _ = x_sample.shape
    depth = mod_w.shape[0]
    n_mixers = 2
    assert ds % TOKEN_TILE == 0 and 1 + db <= SUBLANES

    xp = x_prompt.reshape(b * s, d)
    xs = x_sample.reshape(db * ds, d)
    cvecs = jnp.concatenate([c_ctx[None], c, jnp.zeros((SUBLANES - 1 - db, d), F32)], axis=0)
    m = _modulation(cvecs, mod_w, mod_b).reshape(depth, SUBLANES, 3 * N_SUB, d)

    tiles_per_sample = ds // TOKEN_TILE
    prompt_row = lambda i: 0
    sample_row = lambda i: 1 + i // tiles_per_sample
    prompt_tiles = (b * s) // TOKEN_TILE
    both_row = lambda i: jnp.where(i < prompt_tiles, 0, 1 + (i - prompt_tiles) // tiles_per_sample)

    new_states, new_k, new_v = [], [], []
    for layer in range(depth):
        j = layer // n_mixers
        last = layer == depth - 1
        xp, xs = _ffn(xp, xs, m, both_row, norm_g, ffn_w_gu, ffn_w_down, layer, 0, 0)
        if layer % n_mixers == 0:
            lru_p = (lru_w_in[j], lru_conv_w[j], lru_conv_b[j], lru_gate_w[j], lru_gate_b[j],
                     lru_lambda[j], lru_w_out[j])
            xp, st = _lru(xp, m, prompt_row, norm_g, layer, *lru_p, seq_len=s, h0=None,
                          emit_state=True, single_step=True)
            new_states.append(st)
            xs, _ = _lru(xs, m, sample_row, norm_g, layer, *lru_p, seq_len=ds,
                         h0=state_lru[:, j], emit_state=False, single_step=True)
        else:
            att_p = (att_w_qkv[j], att_q_g[j], att_k_g[j], att_w_o[j])
            xp, kp, vp = _attn(xp, m, prompt_row, norm_g, layer, *att_p, seq_len=s, q_block=s,
                               emit_kv=True, fused=True)
            new_k.append(kp.reshape(b, s, N_KV_HEADS, HEAD_DIM))
            new_v.append(vp.reshape(b, s, N_KV_HEADS, HEAD_DIM))
            past = cache_k.shape[2]
            ck = cache_k[:, j].reshape(db, past * N_KV_HEADS, HEAD_DIM)
            cv = cache_v[:, j].reshape(db, past * N_KV_HEADS, HEAD_DIM)
            (xs,) = _attn(xs, m, sample_row, norm_g, layer, *att_p, seq_len=ds, q_block=128,
                          cache_k=ck, cache_v=cv, rope=True)
        fg = final_g if last else None
        xp, xs = _ffn(xp, xs, m, both_row, norm_g, ffn_w_gu, ffn_w_down, layer, 1, 2, final_g=fg)

    y_prompt = xp.reshape(b, s, d)
    y_sample = xs.reshape(db, ds, d)
    return (y_prompt, y_sample, jnp.stack(new_states, axis=1), jnp.stack(new_k, axis=1),
            jnp.stack(new_v, axis=1))
```
